```python
import math
import jax
import jax.numpy as jnp
from jax import lax
import numpy as np

D_MODEL = 1024
BATCH = 2
SEQ = 8192
DEPTH = 2

GRID_W = 64
CTX_LEN = 256
EPS = 1e-6
M_HEADS = 4
M_DH = D_MODEL // M_HEADS
M_WIDTH = M_HEADS * M_DH
M_CHUNK = 128
A_HEADS = 8
A_DH = D_MODEL // (2 * A_HEADS)
A_WIDTH = A_HEADS * 2 * A_DH
Q_BLOCK = 128
ROPE_BASE = 10000.0
H_WIDTH = D_MODEL
H_EMB = 33
H_FFN = 64
H_DECAY_MIN_PCT = 0.3
H_DECAY_MAX_PCT = 1.5
SHORT_CONV = 3
N_EXPERTS = 32
N_GROUPS = 4
TOPK_GROUPS = 1
TOP_K = 2
D_EXPERT = 512
MOE_BLOCK = 128
IN_SPLITS = (M_WIDTH, M_WIDTH, M_WIDTH, M_WIDTH, 4 * M_HEADS,
             A_WIDTH, A_WIDTH, A_WIDTH,
             H_WIDTH, H_WIDTH, H_WIDTH,
             3 * D_MODEL)
D_IN = sum(IN_SPLITS)

kernel_name = 'hybrid_mlstm_diffattn_hyena_moe_dit'


def _rmsnorm(x, g):
    xf = x.astype(jnp.float32)
    y = xf * lax.rsqrt(jnp.mean(xf * xf, axis=-1, keepdims=True) + EPS)
    return (y * g.astype(jnp.float32)).astype(x.dtype)


def _modulate(xn, shift, scale):
    return xn * (1 + scale) + shift


def _short_conv(u, w, b):
    L = u.shape[1]
    pad = SHORT_CONV // 2
    up = jnp.pad(u, ((0, 0), (pad, pad), (0, 0)))
    y = b
    for j in range(SHORT_CONV):
        y = y + up[:, j:j + L] * w[j]
    return y


def _heads(a, n_heads):
    B, L, W = a.shape
    return a.reshape(B, L, n_heads, W // n_heads).transpose(0, 2, 1, 3)


def _flip(a):
    return jnp.flip(a, axis=2)


def _axial_rope(L, dh):
    rows = L // GRID_W
    row = jnp.broadcast_to(jnp.arange(rows)[:, None], (rows, GRID_W)).reshape(-1).astype(jnp.float32)
    col = jnp.broadcast_to(jnp.arange(GRID_W)[None, :], (rows, GRID_W)).reshape(-1).astype(jnp.float32)
    nf = dh // 4
    inv = ROPE_BASE ** (-jnp.arange(nf, dtype=jnp.float32) / nf)
    ang_r = row[:, None] * inv
    ang_c = col[:, None] * inv
    ang = jnp.concatenate([ang_r, ang_r, ang_c, ang_c], axis=-1)
    return jnp.cos(ang), jnp.sin(ang)


def _rot_half(h):
    h1, h2 = jnp.split(h, 2, axis=-1)
    return jnp.concatenate([-h2, h1], axis=-1)


def _apply_rope(x, cos, sin):
    xr, xc = jnp.split(x, 2, axis=-1)
    rot = jnp.concatenate([_rot_half(xr), _rot_half(xc)], axis=-1)
    return (x.astype(jnp.float32) * cos + rot.astype(jnp.float32) * sin).astype(x.dtype)


def _mlstm_scan(q, k, v, li, lf, state):
    B, H, L, dk = q.shape
    dv = v.shape[-1]
    nc = L // M_CHUNK

    def chunks(a):
        return jnp.moveaxis(a.reshape((B, H, nc, M_CHUNK) + a.shape[3:]), 2, 0)

    mask = jnp.tril(jnp.ones((M_CHUNK, M_CHUNK), dtype=bool))

    def step(carry, inp):
        C, n, m = carry
        qc, kc, vc, lic, lfc = inp
        b = jnp.cumsum(lfc, axis=-1)
        Dm = jnp.where(mask, b[..., :, None] - b[..., None, :] + lic[..., None, :], -jnp.inf)
        inter = b + m[..., None]
        mt = jnp.maximum(inter, jnp.max(Dm, axis=-1))
        S = jnp.einsum('bhtd,bhsd->bhts', qc, kc) * jnp.exp(Dm - mt[..., None])
        wi = jnp.exp(inter - mt)
        num = jnp.einsum('bhts,bhsv->bhtv', S, vc) + wi[..., None] * jnp.einsum('bhtk,bhkv->bhtv', qc, C)
        den = jnp.sum(S, axis=-1) + wi * jnp.einsum('bhtk,bhk->bht', qc, n)
        h = num / jnp.maximum(jnp.abs(den), jnp.exp(-mt))[..., None]
        bQ = b[..., -1]
        ws = bQ[..., None] - b + lic
        m_new = jnp.maximum(bQ + m, jnp.max(ws, axis=-1))
        decay = jnp.exp(bQ + m - m_new)
        kw = kc * jnp.exp(ws - m_new[..., None])[..., None]
        C_new = decay[..., None, None] * C + jnp.einsum('bhsk,bhsv->bhkv', kw, vc)
        n_new = decay[..., None] * n + jnp.sum(kw, axis=2)
        return (C_new, n_new, m_new), h

    state, hs = lax.scan(step, state, (chunks(q), chunks(k), chunks(v), chunks(li), chunks(lf)))
    h = jnp.moveaxis(hs, 0, 2).reshape(B, H, L, dv)
    return h, state


def _mlstm_inputs(mq, mk, mv, mg, conv_w, conv_b, gate_b):
    qk = jax.nn.silu(_short_conv(jnp.concatenate([mq, mk], axis=-1), conv_w, conv_b))
    q, k = jnp.split(qk, 2, axis=-1)
    q = _heads(q, M_HEADS).astype(jnp.float32)
    k = _heads(k, M_HEADS).astype(jnp.float32) * (M_DH ** -0.5)
    v = _heads(mv, M_HEADS).astype(jnp.float32)
    B, L, _ = mg.shape
    g = (mg.astype(jnp.float32) + gate_b.astype(jnp.float32)).reshape(B, L, 4, M_HEADS).transpose(2, 0, 3, 1)
    return q, k, v, g[0], jax.nn.log_sigmoid(g[1]), g[2], jax.nn.log_sigmoid(g[3])


def _mlstm_out(h, o, g):
    B, H, L, dv = h.shape
    y = _rmsnorm(h.transpose(0, 2, 1, 3), g).reshape(B, L, H * dv)
    return (y * jax.nn.sigmoid(o.astype(jnp.float32))).astype(o.dtype)


def _mlstm_branch(parts_l, parts_c, conv_w, conv_b, gate_b, norm_g, need_ctx):
    mq_l, mk_l, mv_l, mo_l, mg_l = parts_l
    mq_c, mk_c, mv_c, mo_c, mg_c = parts_c
    ql, kl, vl, ilf, flf, ilb, flb = _mlstm_inputs(mq_l, mk_l, mv_l, mg_l, conv_w, conv_b, gate_b)
    qc, kc, vc, icf, fcf, icb, fcb = _mlstm_inputs(mq_c, mk_c, mv_c, mg_c, conv_w, conv_b, gate_b)
    B = ql.shape[0]
    zero = (jnp.zeros((B, M_HEADS, M_DH, M_DH), jnp.float32),
            jnp.zeros((B, M_HEADS, M_DH), jnp.float32),
            jnp.zeros((B, M_HEADS), jnp.float32))
    hcf, s_f = _mlstm_scan(qc, kc, vc, icf, fcf, zero)
    hcb, s_b = _mlstm_scan(_flip(qc), _flip(kc), _flip(vc), _flip(icb), _flip(fcb), zero)
    hlf, _ = _mlstm_scan(ql, kl, vl, ilf, flf, s_f)
    hlb, _ = _mlstm_scan(_flip(ql), _flip(kl), _flip(vl), _flip(ilb), _flip(flb), s_b)
    yl = _mlstm_out(hlf + _flip(hlb), mo_l, norm_g)
    yc = _mlstm_out(hcf + _flip(hcb), mo_c, norm_g) if need_ctx else None
    return yl, yc


def _diff_inputs(aq, ak, av, qg, kg):
    B, L, _ = aq.shape
    q = _rmsnorm(aq.reshape(B, L, A_HEADS, 2, A_DH), qg).transpose(0, 2, 3, 1, 4)
    k = _rmsnorm(ak.reshape(B, L, A_HEADS, 2, A_DH), kg).transpose(0, 2, 3, 1, 4)
    v = _heads(av, A_HEADS)
    return q, k, v


def _diff_attend(q, k, v, lam):
    s = jnp.einsum('bhcqd,bhckd->bhcqk', q, k).astype(jnp.float32) * (A_DH ** -0.5)
    p = jax.nn.softmax(s, axis=-1)
    w = p[:, :, 0] - lam * p[:, :, 1]
    return jnp.einsum('bhqk,bhkd->bhqd', w.astype(v.dtype), v)


def _diff_out(o, g, lam_init):
    B, H, L, dv = o.shape
    return (_rmsnorm(o.transpose(0, 2, 1, 3), g) * (1.0 - lam_init)).reshape(B, L, H * dv)


def _diff_branch(parts_l, parts_c, qg, kg, lam_p, sub_g, lam_init, cos, sin, need_ctx):
    ql, kl, vl = _diff_inputs(parts_l[0], parts_l[1], parts_l[2], qg, kg)
    qc, kc, vc = _diff_inputs(parts_c[0], parts_c[1], parts_c[2], qg, kg)
    ql = _apply_rope(ql, cos, sin)
    kl = _apply_rope(kl, cos, sin)
    lp = lam_p.astype(jnp.float32)
    lam = jnp.exp(jnp.sum(lp[0] * lp[1])) - jnp.exp(jnp.sum(lp[2] * lp[3])) + lam_init
    k_all = jnp.concatenate([kc, kl], axis=3)
    v_all = jnp.concatenate([vc, vl], axis=2)
    B, H, _, L, dh = ql.shape
    nb = L // Q_BLOCK
    qb = jnp.moveaxis(ql.reshape(B, H, 2, nb, Q_BLOCK, dh), 3, 0)
    ob = lax.map(lambda qq: _diff_attend(qq, k_all, v_all, lam), qb)
    ol = jnp.moveaxis(ob, 0, 2).reshape(B, H, L, 2 * dh)
    yl = _diff_out(ol, sub_g, lam_init)
    yc = _diff_out(_diff_attend(qc, kc, vc, lam), sub_g, lam_init) if need_ctx else None
    return yl, yc


def _hyena_filters(L, w1, b1, w2, b2, w3, freq, delta):
    t = jnp.linspace(0.0, 1.0, L, dtype=jnp.float32)[:, None]
    bands = (H_EMB - 1) // 2
    w = 2.0 * math.pi * jnp.arange(L, dtype=jnp.float32)[:, None] / L
    f = jnp.linspace(1e-4, bands - 1, bands, dtype=jnp.float32)[None, :]
    z = jnp.concatenate([t, jnp.cos(f * w), -jnp.sin(f * w)], axis=-1)
    hdn = jnp.sin(freq * (z @ w1 + b1))
    hdn = jnp.sin(freq * (hdn @ w2 + b2))
    h = ((hdn @ w3) * jnp.exp(-t * jnp.abs(delta))).astype(jnp.float32)
    h = h.reshape(L, 2, H_WIDTH)
    h = h * lax.rsqrt(jnp.sum(h * h, axis=(0, 1), keepdims=True) + EPS)
    return h[:, 0], h[:, 1]


def _bidir_longconv(u, hf, hb):
    L = u.shape[1]
    n = 2 * L
    taps = jnp.concatenate([hf, jnp.zeros_like(hf[:1]), jnp.flip(hb[1:], axis=0)], axis=0)
    U = jnp.fft.rfft(u.astype(jnp.float32), n=n, axis=1)
    Hf = jnp.fft.rfft(taps, n=n, axis=0)
    return jnp.fft.irfft(U * Hf[None], n=n, axis=1)[:, :L]


def _hyena_seq(hx0, hx1, hv, conv_w, conv_b, filt, bias):
    u = _short_conv(jnp.concatenate([hx0, hx1, hv], axis=-1), conv_w, conv_b)
    x0, x1, v = jnp.split(u.astype(jnp.float32), 3, axis=-1)
    hf, hb = _hyena_filters(u.shape[1], *filt)
    z = v * x1
    z = _bidir_longconv(z, hf, hb) + bias.astype(jnp.float32) * z
    return (x0 * z).astype(hx0.dtype)


def _merge(ym, ya, yh, gpre, mb, wm, wa, wh, wo):
    g = jax.nn.sigmoid((gpre + mb).astype(jnp.float32)).astype(ym.dtype)
    g_m, g_a, g_h = jnp.split(g, 3, axis=-1)
    y = g_m * (ym @ wm) + g_a * (ya @ wa) + g_h * (yh @ wh)
    return y @ wo


def _moe(h, router_w, router_b, w1, w3, w2):
    N, D = h.shape
    E, G, K = N_EXPERTS, N_GROUPS, TOP_K
    s = jax.nn.sigmoid((h @ router_w).astype(jnp.float32))
    sb = s + router_b.astype(jnp.float32)
    gscore = jnp.sum(lax.top_k(sb.reshape(N, G, E // G), 2)[0], axis=-1)
    _, gidx = lax.top_k(gscore, TOPK_GROUPS)
    gmask = jnp.any(gidx[:, :, None] == jnp.arange(G)[None, None, :], axis=1)
    emask = jnp.repeat(gmask, E // G, axis=1)
    _, eidx = lax.top_k(jnp.where(emask, sb, -jnp.inf), K)
    wsel = jnp.take_along_axis(s, eidx, axis=1)
    wsel = wsel / jnp.sum(wsel, axis=-1, keepdims=True)
    A = N * K
    flat_e = eidx.reshape(A)
    flat_t = jnp.repeat(jnp.arange(N, dtype=jnp.int32), K)
    flat_w = wsel.reshape(A)
    order = jnp.argsort(flat_e)
    se = flat_e[order]
    counts = jnp.bincount(flat_e, length=E)
    start = jnp.cumsum(counts) - counts
    padded = (counts + MOE_BLOCK - 1) // MOE_BLOCK * MOE_BLOCK
    pend = jnp.cumsum(padded)
    pstart = pend - padded
    dest = pstart[se] + jnp.arange(A) - start[se]
    P = -(-(A + E * (MOE_BLOCK - 1)) // MOE_BLOCK) * MOE_BLOCK
    tok_buf = jnp.zeros((P,), jnp.int32).at[dest].set(flat_t[order])
    w_buf = jnp.zeros((P,), jnp.float32).at[dest].set(flat_w[order])
    nblk = P // MOE_BLOCK
    blk_e = jnp.minimum(jnp.sum(pend[None, :] <= (jnp.arange(nblk) * MOE_BLOCK)[:, None], axis=1), E - 1)
    xb = h[tok_buf].reshape(nblk, MOE_BLOCK, D)

    def expert_block(args):
        xe, e = args
        return (jax.nn.silu(xe @ w1[e]) * (xe @ w3[e])) @ w2[e]

    yb = lax.map(expert_block, (xb, blk_e)).reshape(P, D)
    return jax.ops.segment_sum(yb * w_buf[:, None].astype(yb.dtype), tok_buf, num_segments=N)


def setup_inputs(seed: int = 0) -> dict:
    key = jax.random.key(seed)
    ks = iter(jax.random.split(key, 48))

    def nrm(shape, scale):
        return jax.random.normal(next(ks), shape, jnp.float32) * scale

    D = D_MODEL
    fb = jnp.linspace(3.0, 6.0, M_HEADS, dtype=jnp.float32)
    m_gate_b = jnp.concatenate([nrm((DEPTH, M_HEADS), 0.1), fb + nrm((DEPTH, M_HEADS), 0.1),
                                nrm((DEPTH, M_HEADS), 0.1), fb + nrm((DEPTH, M_HEADS), 0.1)], axis=-1)
    decay0 = jnp.linspace(math.log(1e-2) / H_DECAY_MIN_PCT, math.log(1e-2) / H_DECAY_MAX_PCT,
                          2 * H_WIDTH, dtype=jnp.float32)
    return {
        'x': nrm((BATCH, SEQ, D), 1.0),
        'c': nrm((BATCH, D), 1.0),
        'ctx': nrm((BATCH, CTX_LEN, D), 1.0),
        'c_ctx': nrm((D,), 1.0),
        'mod_w': nrm((DEPTH, D, 6 * D), 0.5 * D ** -0.5),
        'mod_b': nrm((DEPTH, 6 * D), 0.02),
        'norm1_g': 1.0 + nrm((DEPTH, D), 0.02),
        'norm2_g': 1.0 + nrm((DEPTH, D), 0.02),
        'w_in': nrm((DEPTH, D, D_IN), D ** -0.5),
        'merge_b': nrm((DEPTH, 3 * D), 0.02),
        'm_conv_w': nrm((DEPTH, SHORT_CONV, 2 * M_WIDTH), 0.5),
        'm_conv_b': nrm((DEPTH, 2 * M_WIDTH), 0.02),
        'm_gate_b': m_gate_b,
        'm_norm_g': 1.0 + nrm((DEPTH, M_DH), 0.02),
        'a_qnorm_g': 1.0 + nrm((DEPTH, A_DH), 0.02),
        'a_knorm_g': 1.0 + nrm((DEPTH, A_DH), 0.02),
        'a_lambda': nrm((DEPTH, 4, A_DH), 0.1),
        'a_subln_g': 1.0 + nrm((DEPTH, 2 * A_DH), 0.02),
        'h_conv_w': nrm((DEPTH, SHORT_CONV, 3 * H_WIDTH), 0.5),
        'h_conv_b': nrm((DEPTH, 3 * H_WIDTH), 0.02),
        'h_ffn_w1': nrm((DEPTH, H_EMB, H_FFN), H_EMB ** -0.5),
        'h_ffn_b1': nrm((DEPTH, H_FFN), 0.1),
        'h_ffn_w2': nrm((DEPTH, H_FFN, H_FFN), H_FFN ** -0.5),
        'h_ffn_b2': nrm((DEPTH, H_FFN), 0.1),
        'h_ffn_w3': nrm((DEPTH, H_FFN, 2 * H_WIDTH), H_FFN ** -0.5),
        'h_freq': 1.0 + nrm((DEPTH, H_FFN), 0.02),
        'h_decay': decay0 + nrm((DEPTH, 2 * H_WIDTH), 0.05),
        'h_bias': nrm((DEPTH, H_WIDTH), 1.0),
        'w_br_m': nrm((DEPTH, M_WIDTH, D), M_WIDTH ** -0.5),
        'w_br_a': nrm((DEPTH, A_WIDTH, D), A_WIDTH ** -0.5),
        'w_br_h': nrm((DEPTH, H_WIDTH, D), H_WIDTH ** -0.5),
        'w_out': nrm((DEPTH, D, D), D ** -0.5),
        'router_w': nrm((D, N_EXPERTS), D ** -0.5),
        'router_b': nrm((N_EXPERTS,), 0.01),
        'e_w1': nrm((DEPTH, N_EXPERTS, D, D_EXPERT), D ** -0.5),
        'e_w3': nrm((DEPTH, N_EXPERTS, D, D_EXPERT), D ** -0.5),
        'e_w2': nrm((DEPTH, N_EXPERTS, D_EXPERT, D), D_EXPERT ** -0.5),
    }


def reference(x, c, ctx, c_ctx, mod_w, mod_b, norm1_g, norm2_g, w_in, merge_b,
              m_conv_w, m_conv_b, m_gate_b, m_norm_g,
              a_qnorm_g, a_knorm_g, a_lambda, a_subln_g,
              h_conv_w, h_conv_b, h_ffn_w1, h_ffn_b1, h_ffn_w2, h_ffn_b2, h_ffn_w3,
              h_freq, h_decay, h_bias,
              w_br_m, w_br_a, w_br_h, w_out,
              router_w, router_b, e_w1, e_w3, e_w2):
    B, L, D = x.shape
    Lc = ctx.shape[1]
    rope_cos, rope_sin = _axial_rope(L, A_DH)
    offs = [int(o) for o in np.cumsum(IN_SPLITS)[:-1]]
    xl, xc = x, ctx
    for l in range(DEPTH):
        need_ctx = l < DEPTH - 1
        lam_init = 0.8 - 0.6 * math.exp(-0.3 * l)
        mod_l = jax.nn.silu(c) @ mod_w[l] + mod_b[l]
        mod_c = jax.nn.silu(c_ctx) @ mod_w[l] + mod_b[l]
        sh1, sc1, g1, sh2, sc2, g2 = jnp.split(mod_l[:, None, :], 6, axis=-1)
        sh1c, sc1c, g1c, sh2c, sc2c, g2c = jnp.split(mod_c, 6, axis=-1)
        hl = _modulate(_rmsnorm(xl, norm1_g[l]), sh1, sc1)
        hc = _modulate(_rmsnorm(xc, norm1_g[l]), sh1c, sc1c)
        pl = jnp.split(hl @ w_in[l], offs, axis=-1)
        pc = jnp.split(hc @ w_in[l], offs, axis=-1)
        ym_l, ym_c = _mlstm_branch(pl[0:5], pc[0:5], m_conv_w[l], m_conv_b[l], m_gate_b[l],
                                   m_norm_g[l], need_ctx)
        ya_l, ya_c = _diff_branch(pl[5:8], pc[5:8], a_qnorm_g[l], a_knorm_g[l], a_lambda[l],
                                  a_subln_g[l], lam_init, rope_cos, rope_sin, need_ctx)
        filt = (h_ffn_w1[l], h_ffn_b1[l], h_ffn_w2[l], h_ffn_b2[l], h_ffn_w3[l], h_freq[l], h_decay[l])
        yh_l = _hyena_seq(pl[8], pl[9], pl[10], h_conv_w[l], h_conv_b[l], filt, h_bias[l])
        mix_l = _merge(ym_l, ya_l, yh_l, pl[11], merge_b[l], w_br_m[l], w_br_a[l], w_br_h[l], w_out[l])
        if need_ctx:
            yh_c = _hyena_seq(pc[8], pc[9], pc[10], h_conv_w[l], h_conv_b[l], filt, h_bias[l])
            mix_c = _merge(ym_c, ya_c, yh_c, pc[11], merge_b[l], w_br_m[l], w_br_a[l], w_br_h[l], w_out[l])
            xc = xc + g1c * mix_c
        xl = xl + g1 * mix_l
        tok = _modulate(_rmsnorm(xl, norm2_g[l]), sh2, sc2).reshape(B * L, D)
        if need_ctx:
            tok_c = _modulate(_rmsnorm(xc, norm2_g[l]), sh2c, sc2c).reshape(B * Lc, D)
            tok = jnp.concatenate([tok, tok_c], axis=0)
        f = _moe(tok, router_w, router_b, e_w1[l], e_w3[l], e_w2[l])
        xl = xl + g2 * f[:B * L].reshape(B, L, D)
        if need_ctx:
            xc = xc + g2c * f[B * L:].reshape(B, Lc, D)
    return xl
```

```python
import functools
import math

import numpy as np
import jax
import jax.numpy as jnp
from jax import lax
from jax.experimental import pallas as pl
from jax.experimental.pallas import tpu as pltpu

F32 = jnp.float32
BF16 = jnp.bfloat16

GRID_W = 64
EPS = 1e-6
M_HEADS = 4
M_CHUNK = 128
A_HEADS = 8
ROPE_BASE = 10000.0
H_EMB = 33
H_FFN = 64
SHORT_CONV = 3
N_EXPERTS = 32
N_GROUPS = 4
TOP_K = 2
MOE_BLOCK = 128

LANES = 128
BF16_SUBLANES = 16
V7X_VMEM_BYTES = 64 * 1024 * 1024
VMEM_LIMIT = V7X_VMEM_BYTES * 7 // 8

ROW_TILE = 256
LOG2E = 1.4426950408889634


def _cparams(sem):
    return pltpu.CompilerParams(dimension_semantics=sem, vmem_limit_bytes=VMEM_LIMIT)


def _split(x):
    hi = x.astype(BF16)
    lo = (x - hi.astype(F32)).astype(BF16)
    return hi, lo


def _dot(a, b):
    return jnp.dot(a, b, preferred_element_type=F32)


def _dot3(a, b):
    ah, al = _split(a)
    bh, bl = _split(b)
    return _dot(ah, bh) + _dot(al, bh) + _dot(ah, bl)


def _dot3c(ch, cl, x):
    xh, xl = _split(x)
    return _dot(ch, xh) + _dot(cl, xh) + _dot(ch, xl)


def _np_split(a):
    a = jnp.asarray(np.asarray(a, np.float32))
    hi = a.astype(BF16)
    lo = (a - hi.astype(F32)).astype(BF16)
    return hi, lo


def _sigmoid(x):
    return 1.0 / (1.0 + jnp.exp(-x))


def _log_sigmoid(x):
    return jnp.minimum(x, 0.0) - jnp.log(1.0 + jnp.exp(-jnp.abs(x)))


def _mod_kernel(c_ref, w_ref, b_ref, o_ref):
    c = c_ref[...]
    o_ref[...] = _dot3(c * _sigmoid(c), w_ref[...]) + b_ref[...]


def _modulation(c8, w, b):
    D, N = w.shape
    tn = 1536
    return pl.pallas_call(
        _mod_kernel,
        out_shape=jax.ShapeDtypeStruct((8, N), F32),
        grid=(N // tn,),
        in_specs=[pl.BlockSpec((8, D), lambda j: (0, 0)),
                  pl.BlockSpec((D, tn), lambda j: (0, j)),
                  pl.BlockSpec((1, tn), lambda j: (0, j))],
        out_specs=pl.BlockSpec((8, tn), lambda j: (0, j)),
        compiler_params=_cparams(("parallel",)),
        name="modulation",
    )(c8, w, b.reshape(1, N))


def _norm_mod_kernel(x_ref, g_ref, sh_ref, sc_ref, o_ref):
    x = x_ref[...]
    y = x * lax.rsqrt(jnp.mean(x * x, axis=-1, keepdims=True) + EPS) * g_ref[...]
    o_ref[...] = (y * (1.0 + sc_ref[...]) + sh_ref[...]).astype(o_ref.dtype)


def _norm_mod_router_kernel(x_ref, g_ref, sh_ref, sc_ref, rw_ref, o_ref, s_ref):
    x = x_ref[...]
    y = x * lax.rsqrt(jnp.mean(x * x, axis=-1, keepdims=True) + EPS) * g_ref[...]
    h = y * (1.0 + sc_ref[...]) + sh_ref[...]
    o_ref[...] = h
    s_ref[...] = _sigmoid(_dot3(h, rw_ref[...]))


def _group_of_tile(i, n_lat_tiles, tiles_per_seq, n_batch):
    return jnp.where(i < n_lat_tiles, i // tiles_per_seq, n_batch)


def _norm_mod(x, g, mod, shift_idx, scale_idx, *, n_rows, L, B, router_w=None):
    D = x.shape[1]
    tm = ROW_TILE
    nlt, tps = (B * L) // tm, L // tm
    grp = functools.partial(_group_of_tile, n_lat_tiles=nlt, tiles_per_seq=tps, n_batch=B)
    in_specs = [pl.BlockSpec((tm, D), lambda i: (i, 0)),
                pl.BlockSpec((1, D), lambda i: (0, 0)),
                pl.BlockSpec((None, None, 1, D), lambda i: (grp(i), shift_idx, 0, 0)),
                pl.BlockSpec((None, None, 1, D), lambda i: (grp(i), scale_idx, 0, 0))]
    args = [x, g.reshape(1, D), mod, mod]
    if router_w is None:
        return pl.pallas_call(
            _norm_mod_kernel,
            out_shape=jax.ShapeDtypeStruct((n_rows, D), BF16),
            grid=(n_rows // tm,), in_specs=in_specs,
            out_specs=pl.BlockSpec((tm, D), lambda i: (i, 0)),
            compiler_params=_cparams(("parallel",)), name="norm_mod",
        )(*args)
    E = router_w.shape[1]
    rw = jnp.pad(router_w, ((0, 0), (0, LANES - E)))
    return pl.pallas_call(
        _norm_mod_router_kernel,
        out_shape=(jax.ShapeDtypeStruct((n_rows, D), F32),
                   jax.ShapeDtypeStruct((n_rows, LANES), F32)),
        grid=(n_rows // tm,),
        in_specs=in_specs + [pl.BlockSpec((D, LANES), lambda i: (0, 0))],
        out_specs=(pl.BlockSpec((tm, D), lambda i: (i, 0)),
                   pl.BlockSpec((tm, LANES), lambda i: (i, 0))),
        compiler_params=_cparams(("parallel",)), name="norm_mod_router",
    )(*args, rw)


def _mm_kernel(a_ref, w_ref, o_ref):
    o_ref[...] = _dot(a_ref[...], w_ref[...]).astype(o_ref.dtype)


def _matmul(a, w, out_dtype, tm, tn):
    M, K = a.shape
    N = w.shape[1]
    return pl.pallas_call(
        _mm_kernel,
        out_shape=jax.ShapeDtypeStruct((M, N), out_dtype),
        grid=(M // tm, N // tn),
        in_specs=[pl.BlockSpec((tm, K), lambda i, j: (i, 0)),
                  pl.BlockSpec((K, tn), lambda i, j: (0, j))],
        out_specs=pl.BlockSpec((tm, tn), lambda i, j: (i, j)),
        compiler_params=_cparams(("parallel", "parallel")), name="matmul",
    )(a, w)


def _seq_edge_flags(tm, L, Lc, NL):
    r0 = pl.program_id(0) * tm
    lat = r0 < NL
    start = jnp.where(lat, r0 % L == 0, (r0 - NL) % Lc == 0)
    end = jnp.where(lat, (r0 + tm) % L == 0, (r0 + tm - NL) % Lc == 0)
    return jnp.where(start, 0.0, 1.0), jnp.where(end, 0.0, 1.0)


def _conv3(cur_ref, prev_ref, next_ref, w_ref, b_ref, keep_prev, keep_next):
    cur = cur_ref[...].astype(F32)
    tm = cur.shape[0]
    prev_row = prev_ref[BF16_SUBLANES - 1:BF16_SUBLANES, :].astype(F32) * keep_prev
    next_row = next_ref[0:1, :].astype(F32) * keep_next
    row = lax.broadcasted_iota(jnp.int32, (tm, 1), 0)
    up = jnp.where(row == 0, prev_row, pltpu.roll(cur, 1, 0))
    dn = jnp.where(row == tm - 1, next_row, pltpu.roll(cur, tm - 1, 0))
    return b_ref[...] + up * w_ref[0:1, :] + cur * w_ref[1:2, :] + dn * w_ref[2:3, :]


def _conv_specs(tm, tc, T, col_block):
    per = tm // BF16_SUBLANES
    last = T // BF16_SUBLANES - 1
    return [pl.BlockSpec((tm, tc), lambda i, j: (i, col_block + j)),
            pl.BlockSpec((BF16_SUBLANES, tc), lambda i, j: (jnp.maximum(i * per - 1, 0), col_block + j)),
            pl.BlockSpec((BF16_SUBLANES, tc), lambda i, j: (jnp.minimum((i + 1) * per, last), col_block + j))]


def _mconv_kernel(cur_ref, prev_ref, next_ref, w_ref, b_ref, scale_ref, o_ref, *, tm, L, Lc, NL):
    kp, kn = _seq_edge_flags(tm, L, Lc, NL)
    y = _conv3(cur_ref, prev_ref, next_ref, w_ref, b_ref, kp, kn)
    o_ref[...] = (y * _sigmoid(y) * scale_ref[...]).astype(o_ref.dtype)


def _mlstm_qk(P, conv_w, conv_b, *, L, Lc, NL, width):
    T = P.shape[0]
    tm, tc = ROW_TILE, 1024
    C = 2 * width
    scale = jnp.concatenate([jnp.ones((1, width), F32),
                             jnp.full((1, width), (width // M_HEADS) ** -0.5, F32)], axis=1)
    vec = lambda r: pl.BlockSpec((r, tc), lambda i, j: (0, j))
    return pl.pallas_call(
        functools.partial(_mconv_kernel, tm=tm, L=L, Lc=Lc, NL=NL),
        out_shape=jax.ShapeDtypeStruct((T, C), BF16),
        grid=(T // tm, C // tc),
        in_specs=_conv_specs(tm, tc, T, 0) + [vec(SHORT_CONV), vec(1), vec(1)],
        out_specs=pl.BlockSpec((tm, tc), lambda i, j: (i, j)),
        compiler_params=_cparams(("parallel", "parallel")), name="mlstm_qk_conv",
    )(P, P, P, conv_w, conv_b.reshape(1, C), scale)


def _hyena_pre_kernel(c0, p0, n0, c1, p1, n1, c2, p2, n2, w0, w1, w2, b0, b1, b2,
                      x0_ref, z_ref, *, tm, L, Lc, NL):
    kp, kn = _seq_edge_flags(tm, L, Lc, NL)
    x0 = _conv3(c0, p0, n0, w0, b0, kp, kn)
    x1 = _conv3(c1, p1, n1, w1, b1, kp, kn)
    v = _conv3(c2, p2, n2, w2, b2, kp, kn)
    x0_ref[...] = x0
    z_ref[...] = v * x1


def _hyena_pre(P, col0, conv_w, conv_b, *, L, Lc, NL, C):
    T = P.shape[0]
    tm, tc = ROW_TILE, 512
    nb = C // tc
    specs = []
    for part in range(3):
        specs += _conv_specs(tm, tc, T, col0 // tc + part * nb)
    wspecs = [pl.BlockSpec((SHORT_CONV, tc), lambda i, j, p=part: (0, p * nb + j)) for part in range(3)]
    bspecs = [pl.BlockSpec((1, tc), lambda i, j, p=part: (0, p * nb + j)) for part in range(3)]
    b2d = conv_b.reshape(1, 3 * C)
    out = pl.BlockSpec((tm, tc), lambda i, j: (i, j))
    return pl.pallas_call(
        functools.partial(_hyena_pre_kernel, tm=tm, L=L, Lc=Lc, NL=NL),
        out_shape=(jax.ShapeDtypeStruct((T, C), F32), jax.ShapeDtypeStruct((T, C), F32)),
        grid=(T // tm, nb),
        in_specs=specs + wspecs + bspecs,
        out_specs=(out, out),
        compiler_params=_cparams(("parallel", "parallel")), name="hyena_pre",
    )(*([P] * 9), conv_w, conv_w, conv_w, b2d, b2d, b2d)


def _mlstm_chunk(q, k, v, li_r, lf_r, li_c, lf_c, C_scr, n_scr, m_scr, fwd):
    Q = M_CHUNK
    row = lax.broadcasted_iota(jnp.int32, (Q, Q), 0)
    col = lax.broadcasted_iota(jnp.int32, (Q, Q), 1)
    mask = (col <= row) if fwd else (col >= row)
    tri_c = mask.astype(BF16)
    tri_r = ((row <= col) if fwd else (row >= col)).astype(BF16)
    lfc_h, lfc_l = _split(jnp.broadcast_to(lf_c, (Q, Q)))
    lfr_h, lfr_l = _split(jnp.broadcast_to(lf_r, (Q, Q)))
    b_cols = _dot(tri_c, lfc_h) + _dot(tri_c, lfc_l)
    b_rows = _dot(lfr_h, tri_r) + _dot(lfr_l, tri_r)
    dm = jnp.where(mask, b_cols - b_rows + li_r, -jnp.inf)
    m_prev = m_scr[0:1, 0:1]
    b_col = b_cols[:, 0:1]
    inter = b_col + m_prev
    mt = jnp.maximum(inter, jnp.max(dm, axis=-1, keepdims=True))
    s = lax.dot_general(q, k, (((1,), (1,)), ((), ())), preferred_element_type=F32) * jnp.exp(dm - mt)
    wi = jnp.exp(inter - mt)
    num = _dot(s.astype(BF16), v) + wi * _dot(q, C_scr[...].astype(BF16))
    qn = jnp.sum(q.astype(F32) * n_scr[...], axis=-1, keepdims=True)
    den = jnp.sum(s, axis=-1, keepdims=True) + wi * qn
    h = num / jnp.maximum(jnp.abs(den), jnp.exp(-mt))
    b_tot = b_cols[Q - 1:Q, 0:1] if fwd else b_cols[0:1, 0:1]
    ws = b_tot - b_col + li_c
    m_new = jnp.maximum(b_tot + m_prev, jnp.max(ws, axis=0, keepdims=True))
    decay = jnp.exp(b_tot + m_prev - m_new)
    kw = k.astype(F32) * jnp.exp(ws - m_new)
    C_scr[...] = decay * C_scr[...] + lax.dot_general(
        kw.astype(BF16), v, (((0,), (0,)), ((), ())), preferred_element_type=F32)
    n_scr[...] = decay * n_scr[...] + jnp.sum(kw, axis=0, keepdims=True)
    m_scr[...] = jnp.broadcast_to(m_new, m_scr.shape)
    return h


def _mlstm_kernel(gr_ref, gc_ref, grc_ref, gcc_ref,
                  q_ref, k_ref, v_ref, o_ref, qc_ref, kc_ref, vc_ref, oc_ref, ng_ref,
                  y_ref, yc_ref,
                  C_scr, n_scr, m_scr, hf_scr, hfc_scr, *, S, SEG, Lc):
    s = pl.program_id(2)

    def zero_state():
        C_scr[...] = jnp.zeros_like(C_scr)
        n_scr[...] = jnp.zeros_like(n_scr)
        m_scr[...] = jnp.zeros_like(m_scr)

    def run(nchunks, fwd, qr, kr, vr, grr, gcr, emit):
        gi = 0 if fwd else 2

        def body(j, carry):
            c = j if fwd else nchunks - 1 - j
            r = pl.multiple_of(c * M_CHUNK, M_CHUNK)
            rows = pl.ds(r, M_CHUNK)
            gcs = gcr[rows, :]
            h = _mlstm_chunk(qr[rows, :], kr[rows, :], vr[rows, :],
                             grr[gi, pl.ds(c, 1), :], _log_sigmoid(grr[gi + 1, pl.ds(c, 1), :]),
                             gcs[:, gi:gi + 1], _log_sigmoid(gcs[:, gi + 1:gi + 2]),
                             C_scr, n_scr, m_scr, fwd)
            emit(r, h)
            return carry

        lax.fori_loop(0, nchunks, body, 0)

    def finish(h_fwd, h_bwd, o):
        h = h_fwd + h_bwd
        y = h * lax.rsqrt(jnp.mean(h * h, axis=-1, keepdims=True) + EPS) * ng_ref[...]
        return (y * _sigmoid(o.astype(F32))).astype(y_ref.dtype)

    def emit_ctx_fwd(r, h):
        hfc_scr[pl.ds(r, M_CHUNK), :] = h

    def emit_ctx_bwd(r, h):
        rows = pl.ds(r, M_CHUNK)
        yc_ref[rows, :] = finish(hfc_scr[rows, :], h, oc_ref[rows, :])

    @pl.when(s == 0)
    def _():
        zero_state()
        run(Lc // M_CHUNK, True, qc_ref, kc_ref, vc_ref, grc_ref, gcc_ref, emit_ctx_fwd)

    @pl.when(s < S)
    def _():
        base = s * SEG

        def emit(r, h):
            hf_scr[pl.ds(pl.multiple_of(base + r, M_CHUNK), M_CHUNK), :] = h

        run(SEG // M_CHUNK, True, q_ref, k_ref, v_ref, gr_ref, gc_ref, emit)

    @pl.when(s == S)
    def _():
        zero_state()
        run(Lc // M_CHUNK, False, qc_ref, kc_ref, vc_ref, grc_ref, gcc_ref, emit_ctx_bwd)

    @pl.when(s >= S)
    def _():
        base = (2 * S - 1 - s) * SEG

        def emit(r, h):
            rows = pl.ds(r, M_CHUNK)
            hf = hf_scr[pl.ds(pl.multiple_of(base + r, M_CHUNK), M_CHUNK), :]
            y_ref[rows, :] = finish(hf, h, o_ref[rows, :])

        run(SEG // M_CHUNK, False, q_ref, k_ref, v_ref, gr_ref, gc_ref, emit)


def _mlstm(QK, P, G, gate_b, norm_g, *, B, L, Lc, width):
    H = M_HEADS
    dh = width // H
    NL = B * L
    SEG = min(L, 1024)
    S = L // SEG
    g = G[:, :4 * H] + gate_b.astype(F32)[None, :]

    def gate_views(rows, n):
        a = rows.reshape(B, n, 4, H)
        return (a.transpose(0, 3, 2, 1).reshape(B, H, 4, n // M_CHUNK, M_CHUNK),
                a.transpose(0, 3, 1, 2))

    gr, gc = gate_views(g[:NL], L)
    grc, gcc = gate_views(g[NL:], Lc)

    def seg_of(s):
        return jnp.where(s < S, s, 2 * S - 1 - s)

    def lat(col0):
        return pl.BlockSpec((SEG, dh), lambda b, h, s: (b * S + seg_of(s), col0 + h))

    def ctx(col0):
        return pl.BlockSpec((Lc, dh), lambda b, h, s: (NL // Lc + b, col0 + h))

    in_specs = [
        pl.BlockSpec((None, None, 4, SEG // M_CHUNK, M_CHUNK), lambda b, h, s: (b, h, 0, seg_of(s), 0)),
        pl.BlockSpec((None, None, SEG, 4), lambda b, h, s: (b, h, seg_of(s), 0)),
        pl.BlockSpec((None, None, 4, Lc // M_CHUNK, M_CHUNK), lambda b, h, s: (b, h, 0, 0, 0)),
        pl.BlockSpec((None, None, Lc, 4), lambda b, h, s: (b, h, 0, 0)),
        lat(0), lat(H), lat(2 * H), lat(3 * H),
        ctx(0), ctx(H), ctx(2 * H), ctx(3 * H),
        pl.BlockSpec((1, dh), lambda b, h, s: (0, 0)),
    ]
    out_specs = (
        pl.BlockSpec((SEG, dh), lambda b, h, s: (b * S + jnp.where(s < S, S - 1, 2 * S - 1 - s), h)),
        pl.BlockSpec((Lc, dh), lambda b, h, s: (b, h)),
    )
    return pl.pallas_call(
        functools.partial(_mlstm_kernel, S=S, SEG=SEG, Lc=Lc),
        out_shape=(jax.ShapeDtypeStruct((NL, width), BF16),
                   jax.ShapeDtypeStruct((B * Lc, width), BF16)),
        grid=(B, H, 2 * S),
        in_specs=in_specs, out_specs=out_specs,
        scratch_shapes=[pltpu.VMEM((dh, dh), F32), pltpu.VMEM((1, dh), F32), pltpu.VMEM((8, LANES), F32),
                        pltpu.VMEM((L, dh), F32), pltpu.VMEM((Lc, dh), F32)],
        compiler_params=_cparams(("parallel", "parallel", "arbitrary")), name="mlstm",
    )(gr, gc, grc, gcc, QK, QK, P, P, QK, QK, P, P, norm_g.reshape(1, dh).astype(F32))


def _attn_prep_kernel(x_ref, g_ref, cos_ref, sin_ref, o_ref, *, scale, dh):
    n_blk = x_ref.shape[1] // LANES
    r = lax.broadcasted_iota(jnp.int32, (LANES, LANES), 0)
    c = lax.broadcasted_iota(jnp.int32, (LANES, LANES), 1)
    group = (r // dh == c // dh).astype(BF16)
    lane = lax.broadcasted_iota(jnp.int32, (1, LANES), 1)
    quarter = dh // 4
    first = (lane % (2 * quarter)) < quarter
    cos = cos_ref[...]
    sin = sin_ref[...]
    for hb in range(n_blk):
        cols = slice(hb * LANES, (hb + 1) * LANES)
        x = x_ref[:, cols].astype(F32)
        hi, lo = _split(x * x)
        ms = (_dot(hi, group) + _dot(lo, group)) * (1.0 / dh)
        y = x * lax.rsqrt(ms + EPS) * g_ref[:, cols]
        rot = jnp.where(first, -pltpu.roll(y, LANES - quarter, 1), pltpu.roll(y, quarter, 1))
        o_ref[:, cols] = ((y * cos + rot * sin) * scale).astype(o_ref.dtype)


def _attn_prep(P, col0, gain, cos_tab, sin_tab, scale, *, L, NL, dh):
    T = P.shape[0]
    W = A_HEADS * 2 * dh
    tm = ROW_TILE
    nlt, tps = NL // tm, L // tm
    tab = pl.BlockSpec((tm, LANES), lambda i: (jnp.where(i < nlt, i % tps, tps), 0))
    g = jnp.tile(gain.astype(F32), W // dh).reshape(1, W)
    return pl.pallas_call(
        functools.partial(_attn_prep_kernel, scale=scale, dh=dh),
        out_shape=jax.ShapeDtypeStruct((T, W), BF16),
        grid=(T // tm,),
        in_specs=[pl.BlockSpec((tm, W), lambda i: (i, col0 // W)),
                  pl.BlockSpec((1, W), lambda i: (0, 0)), tab, tab],
        out_specs=pl.BlockSpec((tm, W), lambda i: (i, 0)),
        compiler_params=_cparams(("parallel",)), name="attn_prep",
    )(P, g, cos_tab, sin_tab)


def _attn_kernel(*refs, n_lat, tk, L, Lc, dh, lam_init):
    if n_lat:
        lam_ref, q_ref, kl_ref, vl_ref, kc_ref, vc_ref, sg_ref, o_ref, vext, acc = refs
    else:
        lam_ref, q_ref, kc_ref, vc_ref, sg_ref, o_ref, vext, acc = refs
    dv = 2 * dh
    ctx0 = n_lat * tk

    @pl.when(pl.program_id(2) == 0)
    def _():
        if n_lat:
            vext[0:L, 0:dv] = vl_ref[...]
        vext[ctx0:ctx0 + Lc, 0:dv] = vc_ref[...]
        vext[:, dv:2 * dv] = jnp.ones((vext.shape[0], dv), BF16)

    q = q_ref[...]
    lane = lax.broadcasted_iota(jnp.int32, (1, dv), 1)
    qs = (jnp.where(lane < dh, q, jnp.zeros_like(q)), jnp.where(lane >= dh, q, jnp.zeros_like(q)))
    acc[...] = jnp.zeros_like(acc)
    tq = q.shape[0]

    def update(comp, kblk, vblk, m_old):
        s = lax.dot_general(qs[comp], kblk, (((1,), (1,)), ((), ())), preferred_element_type=F32)
        m_new = jnp.maximum(m_old, jnp.max(s, axis=-1, keepdims=True))
        p = jnp.exp2(s - m_new).astype(BF16)
        acc[comp] = jnp.exp2(m_old - m_new) * acc[comp] + _dot(p, vblk)
        return m_new

    m = (jnp.full((tq, 1), -jnp.inf, F32),) * 2
    if n_lat:
        def body(c, m):
            rows = pl.ds(pl.multiple_of(c * tk, tk), tk)
            kblk, vblk = kl_ref[rows, :], vext[rows, :]
            return update(0, kblk, vblk, m[0]), update(1, kblk, vblk, m[1])
        m = lax.fori_loop(0, n_lat, body, m)
    kblk, vblk = kc_ref[...], vext[ctx0:ctx0 + Lc, :]
    update(0, kblk, vblk, m[0])
    update(1, kblk, vblk, m[1])

    lp = lam_ref[...]
    lam = (jnp.exp(jnp.sum(lp[0:1] * lp[1:2], axis=-1, keepdims=True))
           - jnp.exp(jnp.sum(lp[2:3] * lp[3:4], axis=-1, keepdims=True)) + lam_init)
    a0, a1 = acc[0], acc[1]
    o = a0[:, 0:dv] / a0[:, dv:dv + 1] - lam * (a1[:, 0:dv] / a1[:, dv:dv + 1])
    y = o * lax.rsqrt(jnp.mean(o * o, axis=-1, keepdims=True) + EPS) * sg_ref[...] * (1.0 - lam_init)
    o_ref[...] = y.astype(o_ref.dtype)


def _attention(Qh, Kh, P, vcol0, lam_p, sub_g, lam_init, *, B, L, Lc, dh, latent):
    NL = B * L
    dv = 2 * dh
    H = A_HEADS
    vb = vcol0 // dv
    if latent:
        tq, tk = 256, 512
        n_lat, nq, rows_out = L // tk, L // tq, NL
        q_spec = pl.BlockSpec((tq, dv), lambda b, h, i: (b * nq + i, h))
        lat_specs = [pl.BlockSpec((L, dv), lambda b, h, i: (b, h)),
                     pl.BlockSpec((L, dv), lambda b, h, i: (b, vb + h))]
        lat_args = [Kh, P]
        o_spec = pl.BlockSpec((tq, dv), lambda b, h, i: (b * nq + i, h))
        nkeys = L + Lc
    else:
        tq, tk = Lc, 512
        n_lat, nq, rows_out = 0, 1, B * Lc
        q_spec = pl.BlockSpec((Lc, dv), lambda b, h, i: (NL // Lc + b, h))
        lat_specs, lat_args = [], []
        o_spec = pl.BlockSpec((Lc, dv), lambda b, h, i: (b, h))
        nkeys = Lc
    ctx_specs = [pl.BlockSpec((Lc, dv), lambda b, h, i: (NL // Lc + b, h)),
                 pl.BlockSpec((Lc, dv), lambda b, h, i: (NL // Lc + b, vb + h))]
    return pl.pallas_call(
        functools.partial(_attn_kernel, n_lat=n_lat, tk=tk, L=L, Lc=Lc, dh=dh, lam_init=lam_init),
        out_shape=jax.ShapeDtypeStruct((rows_out, H * dv), BF16),
        grid=(B, H, nq),
        in_specs=[pl.BlockSpec((4, dh), lambda b, h, i: (0, 0)), q_spec] + lat_specs + ctx_specs
                 + [pl.BlockSpec((1, dv), lambda b, h, i: (0, 0))],
        out_specs=o_spec,
        scratch_shapes=[pltpu.VMEM((nkeys, 2 * dv), BF16), pltpu.VMEM((2, tq, 2 * dv), F32)],
        compiler_params=_cparams(("parallel", "parallel", "arbitrary")),
        name="diff_attn_latent" if latent else "diff_attn_ctx",
    )(lam_p.astype(F32), Qh, *lat_args, Kh, P, sub_g.reshape(1, dv).astype(F32))


def _rope_tables(L, dh, tm):
    rows = L // GRID_W
    row = np.repeat(np.arange(rows), GRID_W).astype(np.float64)
    col = np.tile(np.arange(GRID_W), rows).astype(np.float64)
    nf = dh // 4
    inv = (np.float32(ROPE_BASE) ** (-np.arange(nf, dtype=np.float32) / nf)).astype(np.float64)
    ang = np.concatenate([row[:, None] * inv] * 2 + [col[:, None] * inv] * 2, axis=-1)
    ang = np.tile(ang.astype(np.float32).astype(np.float64), (1, LANES // dh))
    cos = np.concatenate([np.cos(ang), np.ones((tm, LANES))], axis=0)
    sin = np.concatenate([np.sin(ang), np.zeros((tm, LANES))], axis=0)
    return jnp.asarray(cos, F32), jnp.asarray(sin, F32)


def _filter_kernel(f_ref, w1_ref, b1_ref, w2_ref, b2_ref, fr_ref, w3_ref, dl_ref, taps_ref, ssq_ref):
    f = f_ref[...]
    freq = fr_ref[...]
    h = jnp.sin(freq * (_dot3(f, w1_ref[...]) + b1_ref[...]))
    h = jnp.sin(freq * (_dot3(h, w2_ref[...]) + b2_ref[...]))
    h = _dot3(h, w3_ref[...]) * jnp.exp(-f[:, 0:1] * jnp.abs(dl_ref[...]))

    @pl.when(pl.program_id(1) == 0)
    def _():
        ssq_ref[...] = jnp.zeros_like(ssq_ref)

    ssq_ref[...] += jnp.sum(h * h, axis=0, keepdims=True)
    taps_ref[...] = h * f[:, H_EMB:H_EMB + 1]


def _filter_features(L):
    t = np.linspace(0.0, 1.0, L, dtype=np.float32).astype(np.float64)[:, None]
    bands = (H_EMB - 1) // 2
    w = (np.float32(2.0 * math.pi) * np.arange(L, dtype=np.float32) / np.float32(L)).astype(np.float64)[:, None]
    f = np.linspace(1e-4, bands - 1, bands, dtype=np.float32).astype(np.float64)[None, :]
    fw = (f.astype(np.float32) * w.astype(np.float32)).astype(np.float64)
    z = np.concatenate([t, np.cos(fw), -np.sin(fw)], axis=-1)
    feat = np.zeros((2 * L, LANES), np.float64)
    feat[:L, :H_EMB] = z
    idx = (L - np.arange(L)) % L
    feat[L:, :H_EMB] = z[idx]
    feat[:, H_EMB] = 1.0
    feat[L, H_EMB] = 0.0
    return jnp.asarray(feat, F32)


def _hyena_filters(L, w1, b1, w2, b2, w3, freq, delta, C):
    pad = LANES - H_FFN
    w1p = jnp.pad(w1, ((0, LANES - H_EMB), (0, pad)))
    w2p = jnp.pad(w2, ((0, pad), (0, pad)))
    w3p = jnp.pad(w3, ((0, pad), (0, 0)))
    row = lambda a: jnp.pad(a.reshape(1, H_FFN), ((0, 0), (0, pad)))
    tr = min(L, 512)
    nr = L // tr
    const = lambda shape: pl.BlockSpec(shape, lambda hf, r: (0, 0))
    return pl.pallas_call(
        _filter_kernel,
        out_shape=(jax.ShapeDtypeStruct((2 * L, C), F32), jax.ShapeDtypeStruct((2, 1, C), F32)),
        grid=(2, nr),
        in_specs=[pl.BlockSpec((tr, LANES), lambda hf, r: (hf * nr + r, 0)),
                  const((LANES, LANES)), const((1, LANES)), const((LANES, LANES)), const((1, LANES)),
                  const((1, LANES)),
                  pl.BlockSpec((LANES, C), lambda hf, r: (0, hf)),
                  pl.BlockSpec((1, C), lambda hf, r: (0, hf))],
        out_specs=(pl.BlockSpec((tr, C), lambda hf, r: (hf * nr + r, 0)),
                   pl.BlockSpec((None, 1, C), lambda hf, r: (hf, 0, 0))),
        compiler_params=_cparams(("parallel", "arbitrary")), name="hyena_filter",
    )(_filter_features(L), w1p, row(b1), w2p, row(b2), row(freq), w3p, delta.reshape(1, 2 * C))


def _cblock(m):
    return np.block([[m.real, -m.imag], [m.imag, m.real]])


def _dft_consts(L):
    N = 2 * L
    N2 = LANES
    N1 = N // N2
    half = N1 // 2
    n1 = np.arange(N1)
    n2 = np.arange(N2)
    F1 = np.exp(-2j * np.pi * np.outer(n1, n1) / N1)
    F2 = np.exp(-2j * np.pi * np.outer(n2, n2) / N2)
    a_data = _cblock(F1[:, :half])
    a_taps = np.concatenate([F1.real, F1.imag], axis=0)
    b_fwd = _cblock(F2)
    b_inv = _cblock(np.conj(F2))
    fin = _cblock(np.conj(F1)[:half, :]) / N
    ang = 2.0 * np.pi * np.outer(n2, n1) / N
    tw = dict(c_a=np.cos(ang)[:, :, None], s_a=np.sin(ang)[:, :, None],
              c_b=np.cos(ang.T)[:, :, None], s_b=np.sin(ang.T)[:, :, None])
    return dict(N1=N1, half=half, a_data=_np_split(a_data), a_taps=_np_split(a_taps),
                b_fwd=_np_split(b_fwd), b_inv=_np_split(b_inv), fin=_np_split(fin),
                tw={k: jnp.asarray(v, F32) for k, v in tw.items()})


FFT_N2_STEP = 8


def _fft_a_kernel(x_ref, mh, ml, c_ref, s_ref, ar_ref, ai_ref, *, N1):
    for j in range(FFT_N2_STEP):
        r = _dot3c(mh[...], ml[...], x_ref[:, j, :])
        re, im = r[:N1], r[N1:]
        c, s = c_ref[j], s_ref[j]
        ar_ref[:, j, :] = re * c + im * s
        ai_ref[:, j, :] = im * c - re * s


def _fft_stage_a(x3, mats, tw, *, N1, C, ct):
    mh, ml = mats
    const = pl.BlockSpec(mh.shape, lambda g, c: (0, 0))
    twspec = pl.BlockSpec((FFT_N2_STEP, N1, 1), lambda g, c: (g, 0, 0))
    blk = pl.BlockSpec((N1, FFT_N2_STEP, ct), lambda g, c: (0, g, c))
    shape = jax.ShapeDtypeStruct((N1, LANES, C), F32)
    return pl.pallas_call(
        functools.partial(_fft_a_kernel, N1=N1),
        out_shape=(shape, shape), grid=(LANES // FFT_N2_STEP, C // ct),
        in_specs=[blk, const, const, twspec, twspec],
        out_specs=(blk, blk),
        compiler_params=_cparams(("parallel", "parallel")), name="hyena_fft_a",
    )(x3, mh, ml, tw["c_a"], tw["s_a"])


def _fft_mid_kernel(ar, ai, tr, ti, ssq, fh, fl, ih, il, c_ref, s_ref, br, bi):
    N2 = LANES
    x = _dot3c(fh[...], fl[...], jnp.concatenate([ar[...], ai[...]], axis=0))
    h = _dot3c(fh[...], fl[...], jnp.concatenate([tr[...], ti[...]], axis=0))
    scale = lax.rsqrt(ssq[0] + ssq[1] + EPS)
    xr, xi, hr, hi = x[:N2], x[N2:], h[:N2] * scale, h[N2:] * scale
    y = jnp.concatenate([xr * hr - xi * hi, xr * hi + xi * hr], axis=0)
    r = _dot3c(ih[...], il[...], y)
    re, im = r[:N2], r[N2:]
    c, s = c_ref[...], s_ref[...]
    br[...] = re * c - im * s
    bi[...] = im * c + re * s


def _fft_mid(Ar, Ai, Tr, Ti, ssq, consts, *, C, ct):
    N1 = consts["N1"]
    blk = pl.BlockSpec((None, LANES, ct), lambda k1, c: (k1, 0, c))
    const = pl.BlockSpec((2 * LANES, 2 * LANES), lambda k1, c: (0, 0))
    twspec = pl.BlockSpec((None, LANES, 1), lambda k1, c: (k1, 0, 0))
    shape = jax.ShapeDtypeStruct((N1, LANES, C), F32)
    return pl.pallas_call(
        _fft_mid_kernel, out_shape=(shape, shape), grid=(N1, C // ct),
        in_specs=[blk, blk, blk, blk, pl.BlockSpec((2, 1, ct), lambda k1, c: (0, 0, c)),
                  const, const, const, const, twspec, twspec],
        out_specs=(blk, blk),
        compiler_params=_cparams(("parallel", "parallel")), name="hyena_fft_mid",
    )(Ar, Ai, Tr, Ti, ssq, *consts["b_fwd"], *consts["b_inv"], consts["tw"]["c_b"], consts["tw"]["s_b"])


def _fft_fin_kernel(br, bi, mh, ml, x0, z, bias, y):
    for j in range(FFT_N2_STEP):
        r = _dot3c(mh[...], ml[...], jnp.concatenate([br[:, j, :], bi[:, j, :]], axis=0))
        y[:, j, :] = x0[:, j, :] * (r + bias[...] * z[:, j, :])


def _fft_final(Br, Bi, x0_3d, z_3d, bias, consts, *, C, ct):
    N1 = consts["N1"]
    mh, ml = consts["fin"]
    blk = pl.BlockSpec((N1, FFT_N2_STEP, ct), lambda g, c: (0, g, c))
    const = pl.BlockSpec(mh.shape, lambda g, c: (0, 0))
    return pl.pallas_call(
        _fft_fin_kernel,
        out_shape=jax.ShapeDtypeStruct((N1, LANES, C), F32),
        grid=(LANES // FFT_N2_STEP, C // ct),
        in_specs=[blk, blk, const, const, blk, blk, pl.BlockSpec((1, ct), lambda g, c: (0, c))],
        out_specs=blk,
        compiler_params=_cparams(("parallel", "parallel")), name="hyena_fft_final",
    )(Br, Bi, mh, ml, x0_3d, z_3d, bias)


def _hyena_latent(x0, z, taps, ssq, bias, consts, *, B, L, C):
    assert B == 2
    N1 = consts["N1"]
    ct = 512
    T = x0.shape[0]
    z3 = z.reshape(T // LANES, LANES, C)
    x03 = x0.reshape(T // LANES, LANES, C)
    Ar, Ai = _fft_stage_a(z3, consts["a_data"], consts["tw"], N1=N1, C=C, ct=ct)
    Tr, Ti = _fft_stage_a(taps.reshape(N1, LANES, C), consts["a_taps"], consts["tw"], N1=N1, C=C, ct=ct)
    Br, Bi = _fft_mid(Ar, Ai, Tr, Ti, ssq, consts, C=C, ct=ct)
    y = _fft_final(Br, Bi, x03, z3, bias.reshape(1, C).astype(F32), consts, C=C, ct=ct)
    return y.reshape(B * L, C)


def _hyena_ctx_kernel(z, x0, taps, ssq, bias, dh, dl, th, tl, ih, il, y, *, Lc):
    n = 2 * Lc
    x = _dot3c(dh[...], dl[...], z[...])
    h = _dot3c(th[...], tl[...], taps[...])
    scale = lax.rsqrt(ssq[0] + ssq[1] + EPS)
    xr, xi, hr, hi = x[:n], x[n:], h[:n] * scale, h[n:] * scale
    r = _dot3c(ih[...], il[...], jnp.concatenate([xr * hr - xi * hi, xr * hi + xi * hr], axis=0))
    y[...] = x0[...] * (r + bias[...] * z[...])


def _hyena_ctx(x0, z, taps, ssq, bias, *, B, L, Lc, C):
    assert B == 2 and (B * L) % (2 * Lc) == 0
    n = 2 * Lc
    idx = np.arange(n)
    F = np.exp(-2j * np.pi * np.outer(idx, idx) / n)
    d = _np_split(_cblock(F[:, :Lc]))
    t = _np_split(np.concatenate([F.real, F.imag], axis=0))
    inv = _np_split(_cblock(np.conj(F)[:Lc, :]) / n)
    ct = 256
    r0 = (B * L) // n
    both = pl.BlockSpec((n, ct), lambda c: (r0, c))
    const = lambda m: pl.BlockSpec(m.shape, lambda c: (0, 0))
    return pl.pallas_call(
        functools.partial(_hyena_ctx_kernel, Lc=Lc),
        out_shape=jax.ShapeDtypeStruct((n, C), F32),
        grid=(C // ct,),
        in_specs=[both, both,
                  pl.BlockSpec((n, ct), lambda c: (0, c)),
                  pl.BlockSpec((2, 1, ct), lambda c: (0, 0, c)),
                  pl.BlockSpec((1, ct), lambda c: (0, c)),
                  const(d[0]), const(d[1]), const(t[0]), const(t[1]), const(inv[0]), const(inv[1])],
        out_specs=pl.BlockSpec((n, ct), lambda c: (0, c)),
        compiler_params=_cparams(("parallel",)), name="hyena_ctx",
    )(z, x0, taps, ssq, bias.reshape(1, C).astype(F32), *d, *t, *inv)


def _merge_kernel(*refs, with_ctx, n_lat_tiles):
    if with_ctx:
        (ym, ya, yh, ymc, yac, yhc, gm, ga, gh, mb, wm, wa, wh, wo, x_ref, g1, o_ref) = refs
        is_ctx = pl.program_id(0) >= n_lat_tiles
        pick = lambda lat, ctx: jnp.where(is_ctx, ctx[...], lat[...])
        m, a, h = pick(ym, ymc), pick(ya, yac), pick(yh, yhc)
    else:
        (ym, ya, yh, gm, ga, gh, mb, wm, wa, wh, wo, x_ref, g1, o_ref) = refs
        m, a, h = ym[...], ya[...], yh[...]
    D = x_ref.shape[1]
    bias = mb[...]
    gate = lambda g, k: _sigmoid(g[...].astype(F32) + bias[:, k * D:(k + 1) * D])
    y = (gate(gm, 0) * _dot(m, wm[...]) + gate(ga, 1) * _dot(a, wa[...])
         + gate(gh, 2) * _dot(h.astype(BF16), wh[...]))
    o_ref[...] = x_ref[...] + g1[...] * _dot(y.astype(BF16), wo[...])


def _merge(X, P, gcol0, branches, merge_b, weights, mod, gate_idx, *, n_rows, B, L, with_ctx):
    D = X.shape[1]
    tm = ROW_TILE
    nlt, tps = (B * L) // tm, L // tm
    grp = functools.partial(_group_of_tile, n_lat_tiles=nlt, tiles_per_seq=tps, n_batch=B)
    lat = pl.BlockSpec((tm, D), lambda i: (jnp.minimum(i, nlt - 1), 0))
    ctx = pl.BlockSpec((tm, D), lambda i: (jnp.maximum(i - nlt, 0), 0))
    gb = gcol0 // D
    gspec = lambda k: pl.BlockSpec((tm, D), lambda i: (i, gb + k))
    wspec = pl.BlockSpec((D, D), lambda i: (0, 0))
    in_specs = ([lat] * 3 + ([ctx] * 3 if with_ctx else []) + [gspec(0), gspec(1), gspec(2)]
                + [pl.BlockSpec((1, 3 * D), lambda i: (0, 0))] + [wspec] * 4
                + [pl.BlockSpec((tm, D), lambda i: (i, 0)),
                   pl.BlockSpec((None, None, 1, D), lambda i: (grp(i), gate_idx, 0, 0))])
    return pl.pallas_call(
        functools.partial(_merge_kernel, with_ctx=with_ctx, n_lat_tiles=nlt),
        out_shape=jax.ShapeDtypeStruct((n_rows, D), F32),
        grid=(n_rows // tm,), in_specs=in_specs,
        out_specs=pl.BlockSpec((tm, D), lambda i: (i, 0)),
        compiler_params=_cparams(("parallel",)), name="merge",
    )(*branches, P, P, P, merge_b.reshape(1, 3 * D).astype(F32), *weights, X, mod)


def _row_gather(idx_ref, base, n, src_hbm, dst, sem, wait):
    def body(r, carry):
        cp = pltpu.make_async_copy(src_hbm.at[pl.ds(idx_ref[base + r], 1), :],
                                   dst.at[pl.ds(r, 1), :], sem)
        if wait:
            cp.wait()
        else:
            cp.start()
        return carry
    lax.fori_loop(0, n, body, 0)


def _moe_kernel(blk_e, tok, h_hbm, w1_ref, w3_ref, w2_ref, wt_ref, y_ref, xbuf, sems, w1b, w3b, w2b):
    i = pl.program_id(0)
    n = pl.num_programs(0)
    slot = i % 2
    R = MOE_BLOCK

    @pl.when(i == 0)
    def _():
        _row_gather(tok, 0, R, h_hbm, xbuf.at[0], sems.at[0], wait=False)

    @pl.when(i + 1 < n)
    def _():
        _row_gather(tok, (i + 1) * R, R, h_hbm, xbuf.at[1 - slot], sems.at[1 - slot], wait=False)

    @pl.when((i == 0) | (blk_e[i] != blk_e[jnp.maximum(i - 1, 0)]))
    def _():
        w1b[...] = w1_ref[...].astype(BF16)
        w3b[...] = w3_ref[...].astype(BF16)
        w2b[...] = w2_ref[...].astype(BF16)

    _row_gather(tok, i * R, R, h_hbm, xbuf.at[slot], sems.at[slot], wait=True)
    x = xbuf[slot].astype(BF16)
    a = _dot(x, w1b[...])
    g = (a * _sigmoid(a)) * _dot(x, w3b[...])
    y_ref[...] = _dot(g.astype(BF16), w2b[...]) * wt_ref[...]


def _moe_experts(h, tok_buf, w_buf, blk_e, w1, w3, w2):
    N, D = h.shape
    E, _, De = w1.shape
    P = tok_buf.shape[0]
    nblk = P // MOE_BLOCK
    return pl.pallas_call(
        _moe_kernel,
        out_shape=jax.ShapeDtypeStruct((P, D), F32),
        grid_spec=pltpu.PrefetchScalarGridSpec(
            num_scalar_prefetch=2, grid=(nblk,),
            in_specs=[pl.BlockSpec(memory_space=pl.ANY),
                      pl.BlockSpec((None, D, De), lambda i, be, tk: (be[i], 0, 0)),
                      pl.BlockSpec((None, D, De), lambda i, be, tk: (be[i], 0, 0)),
                      pl.BlockSpec((None, De, D), lambda i, be, tk: (be[i], 0, 0)),
                      pl.BlockSpec((MOE_BLOCK, 1), lambda i, be, tk: (i, 0))],
            out_specs=pl.BlockSpec((MOE_BLOCK, D), lambda i, be, tk: (i, 0)),
            scratch_shapes=[pltpu.VMEM((2, MOE_BLOCK, D), F32), pltpu.SemaphoreType.DMA((2,)),
                            pltpu.VMEM((D, De), BF16), pltpu.VMEM((D, De), BF16), pltpu.VMEM((De, D), BF16)]),
        compiler_params=_cparams(("arbitrary",)), name="moe_experts",
    )(blk_e, tok_buf, h, w1, w3, w2, w_buf.reshape(P, 1))


def _combine_kernel(inv, yb_hbm, x_ref, g2, o_ref, buf, sems, *, n_tok):
    i = pl.program_id(0)
    n = pl.num_programs(0)
    slot = i % 2
    R = MOE_BLOCK

    def gather(tile, slot, wait):
        for k in range(TOP_K):
            _row_gather(inv, k * n_tok + tile * R, R, yb_hbm, buf.at[slot, k], sems.at[slot], wait)

    @pl.when(i == 0)
    def _():
        gather(0, 0, False)

    @pl.when(i + 1 < n)
    def _():
        gather(i + 1, 1 - slot, False)

    gather(i, slot, True)
    acc = buf[slot, 0]
    for k in range(1, TOP_K):
        acc = acc + buf[slot, k]
    o_ref[...] = x_ref[...] + g2[...] * acc


def _moe_combine(yb, inv, X, mod, gate_idx, *, n_rows, B, L):
    D = X.shape[1]
    tm = MOE_BLOCK
    nlt, tps = (B * L) // tm, L // tm
    grp = functools.partial(_group_of_tile, n_lat_tiles=nlt, tiles_per_seq=tps, n_batch=B)
    return pl.pallas_call(
        functools.partial(_combine_kernel, n_tok=n_rows),
        out_shape=jax.ShapeDtypeStruct((n_rows, D), F32),
        grid_spec=pltpu.PrefetchScalarGridSpec(
            num_scalar_prefetch=1, grid=(n_rows // tm,),
            in_specs=[pl.BlockSpec(memory_space=pl.ANY),
                      pl.BlockSpec((tm, D), lambda i, inv: (i, 0)),
                      pl.BlockSpec((None, None, 1, D), lambda i, inv: (grp(i), gate_idx, 0, 0))],
            out_specs=pl.BlockSpec((tm, D), lambda i, inv: (i, 0)),
            scratch_shapes=[pltpu.VMEM((2, TOP_K, tm, D), F32), pltpu.SemaphoreType.DMA((2,))]),
        compiler_params=_cparams(("arbitrary",)), name="moe_combine",
    )(inv.reshape(n_rows, TOP_K).T.reshape(-1), yb, X, mod)


def _route(s, router_b):
    N = s.shape[0]
    E, G, K = N_EXPERTS, N_GROUPS, TOP_K
    s = s[:, :E]
    sb = s + router_b.astype(F32)
    gscore = jnp.sum(lax.top_k(sb.reshape(N, G, E // G), 2)[0], axis=-1)
    gidx = jnp.argmax(gscore, axis=-1)
    emask = (jnp.arange(E)[None, :] // (E // G)) == gidx[:, None]
    _, eidx = lax.top_k(jnp.where(emask, sb, -jnp.inf), K)
    wsel = jnp.take_along_axis(s, eidx, axis=1)
    wsel = wsel / jnp.sum(wsel, axis=-1, keepdims=True)
    A = N * K
    flat_e = eidx.reshape(A).astype(jnp.int32)
    order = jnp.argsort(flat_e).astype(jnp.int32)
    se = flat_e[order]
    counts = jnp.bincount(flat_e, length=E).astype(jnp.int32)
    start = jnp.cumsum(counts) - counts
    padded = (counts + MOE_BLOCK - 1) // MOE_BLOCK * MOE_BLOCK
    pend = jnp.cumsum(padded)
    pstart = pend - padded
    dest = (pstart[se] + jnp.arange(A, dtype=jnp.int32) - start[se]).astype(jnp.int32)
    P = -(-(A + E * (MOE_BLOCK - 1)) // MOE_BLOCK) * MOE_BLOCK
    tok_buf = jnp.zeros((P,), jnp.int32).at[dest].set(order // K)
    w_buf = jnp.zeros((P,), F32).at[dest].set(wsel.reshape(A)[order])
    inv = jnp.zeros((A,), jnp.int32).at[order].set(dest)
    nblk = P // MOE_BLOCK
    blk_e = jnp.minimum(jnp.sum(pend[None, :] <= (jnp.arange(nblk) * MOE_BLOCK)[:, None], axis=1), E - 1)
    return tok_buf, w_buf, inv, blk_e.astype(jnp.int32)


def kernel(x, c, ctx, c_ctx, mod_w, mod_b, norm1_g, norm2_g, w_in, merge_b, m_conv_w, m_conv_b, m_gate_b, m_norm_g, a_qnorm_g, a_knorm_g, a_lambda, a_subln_g, h_conv_w, h_conv_b, h_ffn_w1, h_ffn_b1, h_ffn_w2, h_ffn_b2, h_ffn_w3, h_freq, h_decay, h_bias, w_br_m, w_br_a, w_br_h, w_out, router_w, router_b, e_w1, e_w3, e_w2):
    B, L, D = x.shape
    Lc = ctx.shape[1]
    depth = mod_w.shape[0]
    NL, NC = B * L, B * Lc
    T = NL + NC
    assert B + 1 <= 8 and Lc == ROW_TILE and L % min(L, 1024) == 0 and L % 512 == 0
    assert NL % Lc == 0 and D % LANES == 0
    width = D
    a_dh = D // (2 * A_HEADS)
    n_gates = 4 * M_HEADS
    c_mq, c_mv, c_mo = 0, 2 * width, 3 * width
    c_aq, c_ak, c_av = 4 * width, 5 * width, 6 * width
    c_hx = 7 * width
    c_gp = 10 * width

    X = jnp.concatenate([x.reshape(NL, D), ctx.reshape(NC, D)], axis=0)
    c8 = jnp.zeros((8, D), F32).at[:B].set(c).at[B].set(c_ctx)
    cos_tab, sin_tab = _rope_tables(L, a_dh, ROW_TILE)
    fft_consts = _dft_consts(L)

    for l in range(depth):
        need_ctx = l < depth - 1
        lam_init = 0.8 - 0.6 * math.exp(-0.3 * l)
        mod = _modulation(c8, mod_w[l], mod_b[l])[:B + 1].reshape(B + 1, 6, 1, D)

        h = _norm_mod(X, norm1_g[l], mod, 0, 1, n_rows=T, L=L, B=B)
        g0 = 4 * width
        w_main = jnp.concatenate([w_in[l][:, :g0], w_in[l][:, g0 + n_gates:]], axis=1).astype(BF16)
        w_gate = jnp.pad(w_in[l][:, g0:g0 + n_gates], ((0, 0), (0, LANES - n_gates))).astype(BF16)
        P = _matmul(h, w_main, BF16, 512, 1024)
        G = _matmul(h, w_gate, F32, 512, LANES)

        QK = _mlstm_qk(P, m_conv_w[l], m_conv_b[l], L=L, Lc=Lc, NL=NL, width=width)
        ym, ymc = _mlstm(QK, P, G, m_gate_b[l], m_norm_g[l], B=B, L=L, Lc=Lc, width=width)

        Qh = _attn_prep(P, c_aq, a_qnorm_g[l], cos_tab, sin_tab, a_dh ** -0.5 * LOG2E, L=L, NL=NL, dh=a_dh)
        Kh = _attn_prep(P, c_ak, a_knorm_g[l], cos_tab, sin_tab, 1.0, L=L, NL=NL, dh=a_dh)
        attn = functools.partial(_attention, Qh, Kh, P, c_av, a_lambda[l], a_subln_g[l], lam_init,
                                 B=B, L=L, Lc=Lc, dh=a_dh)
        ya = attn(latent=True)

        x0, z = _hyena_pre(P, c_hx, h_conv_w[l], h_conv_b[l], L=L, Lc=Lc, NL=NL, C=width)
        filt = (h_ffn_w1[l], h_ffn_b1[l], h_ffn_w2[l], h_ffn_b2[l], h_ffn_w3[l], h_freq[l], h_decay[l])
        taps, ssq = _hyena_filters(L, *filt, width)
        yh = _hyena_latent(x0, z, taps, ssq, h_bias[l], fft_consts, B=B, L=L, C=width)

        weights = [w.astype(BF16) for w in (w_br_m[l], w_br_a[l], w_br_h[l], w_out[l])]
        if need_ctx:
            yac = attn(latent=False)
            taps_c, ssq_c = _hyena_filters(Lc, *filt, width)
            yhc = _hyena_ctx(x0, z, taps_c, ssq_c, h_bias[l], B=B, L=L, Lc=Lc, C=width)
            branches, n_rows = [ym, ya, yh, ymc, yac, yhc], T
        else:
            branches, n_rows = [ym, ya, yh], NL
        X = _merge(X, P, c_gp, branches, merge_b[l], weights, mod, 2,
                   n_rows=n_rows, B=B, L=L, with_ctx=need_ctx)

        tok, s = _norm_mod(X, norm2_g[l], mod, 3, 4, n_rows=n_rows, L=L, B=B, router_w=router_w)
        tok_buf, w_buf, inv, blk_e = _route(s, router_b)
        yb = _moe_experts(tok, tok_buf, w_buf, blk_e, e_w1[l], e_w3[l], e_w2[l])
        X = _moe_combine(yb, inv, X, mod, 5, n_rows=n_rows, B=B, L=L)
    return X[:NL].reshape(B, L, D)
```

```python
import functools
import math

import numpy as np
import jax
import jax.numpy as jnp
from jax import lax
from jax.experimental import pallas as pl
from jax.experimental.pallas import tpu as pltpu

F32 = jnp.float32
BF16 = jnp.bfloat16

GRID_W = 64
EPS = 1e-6
M_HEADS = 4
M_CHUNK = 128
A_HEADS = 8
ROPE_BASE = 10000.0
H_EMB = 33
H_FFN = 64
SHORT_CONV = 3
N_EXPERTS = 32
N_GROUPS = 4
TOP_K = 2
MOE_BLOCK = 128

LANES = 128
BF16_SUBLANES = 16
V7X_VMEM_BYTES = 64 * 1024 * 1024
VMEM_LIMIT = V7X_VMEM_BYTES * 7 // 8

ROW_TILE = 256
LOG2E = 1.4426950408889634


def _cparams(sem):
    return pltpu.CompilerParams(dimension_semantics=sem, vmem_limit_bytes=VMEM_LIMIT)


def _split(x):
    hi = x.astype(BF16)
    lo = (x - hi.astype(F32)).astype(BF16)
    return hi, lo


def _dot(a, b):
    return jnp.dot(a, b, preferred_element_type=F32)


def _dot3(a, b):
    ah, al = _split(a)
    bh, bl = _split(b)
    return _dot(ah, bh) + _dot(al, bh) + _dot(ah, bl)


def _dot3c(ch, cl, x):
    xh, xl = _split(x)
    return _dot(ch, xh) + _dot(cl, xh) + _dot(ch, xl)


def _np_split(a):
    a = jnp.asarray(np.asarray(a, np.float32))
    hi = a.astype(BF16)
    lo = (a - hi.astype(F32)).astype(BF16)
    return hi, lo


def _sigmoid(x):
    return 1.0 / (1.0 + jnp.exp(-x))


def _log_sigmoid(x):
    return jnp.minimum(x, 0.0) - jnp.log(1.0 + jnp.exp(-jnp.abs(x)))


def _mod_kernel(c_ref, w_ref, b_ref, o_ref):
    c = c_ref[...]
    o_ref[...] = _dot3(c * _sigmoid(c), w_ref[...]) + b_ref[...]


def _modulation(c8, w, b):
    D, N = w.shape
    tn = 1536
    return pl.pallas_call(
        _mod_kernel,
        out_shape=jax.ShapeDtypeStruct((8, N), F32),
        grid=(N // tn,),
        in_specs=[pl.BlockSpec((8, D), lambda j: (0, 0)),
                  pl.BlockSpec((D, tn), lambda j: (0, j)),
                  pl.BlockSpec((1, tn), lambda j: (0, j))],
        out_specs=pl.BlockSpec((8, tn), lambda j: (0, j)),
        compiler_params=_cparams(("parallel",)),
        name="modulation",
    )(c8, w, b.reshape(1, N))


def _norm_mod_kernel(x_ref, g_ref, sh_ref, sc_ref, o_ref):
    x = x_ref[...]
    y = x * lax.rsqrt(jnp.mean(x * x, axis=-1, keepdims=True) + EPS) * g_ref[...]
    o_ref[...] = (y * (1.0 + sc_ref[...]) + sh_ref[...]).astype(o_ref.dtype)


def _norm_mod_router_kernel(x_ref, g_ref, sh_ref, sc_ref, rw_ref, rb_ref, o_ref, code_ref, wt_ref, cnt_ref, carry):
    E, G = N_EXPERTS, N_GROUPS
    gs = E // G

    @pl.when(pl.program_id(0) == 0)
    def _():
        carry[...] = jnp.zeros_like(carry)

    x = x_ref[...]
    y = x * lax.rsqrt(jnp.mean(x * x, axis=-1, keepdims=True) + EPS) * g_ref[...]
    h = y * (1.0 + sc_ref[...]) + sh_ref[...]
    o_ref[...] = h
    tm = h.shape[0]
    s = _sigmoid(_dot3(h, rw_ref[...]))
    lane = lax.broadcasted_iota(jnp.int32, (1, LANES), 1)
    lane_f = lane.astype(F32)
    sb = jnp.where(lane < E, s + rb_ref[...], -jnp.inf)
    far = float(LANES)
    best = jnp.full((tm, 1), -jnp.inf, F32)
    e1 = jnp.zeros((tm, 1), F32)
    e2 = jnp.zeros((tm, 1), F32)
    for g in range(G):
        mg = jnp.where((lane >= g * gs) & (lane < (g + 1) * gs), sb, -jnp.inf)
        m1 = jnp.max(mg, axis=-1, keepdims=True)
        i1 = jnp.min(jnp.where(mg == m1, lane_f, far), axis=-1, keepdims=True)
        mg2 = jnp.where(lane_f == i1, -jnp.inf, mg)
        m2 = jnp.max(mg2, axis=-1, keepdims=True)
        i2 = jnp.min(jnp.where(mg2 == m2, lane_f, far), axis=-1, keepdims=True)
        score = m1 + m2
        take = score > best
        best = jnp.where(take, score, best)
        e1 = jnp.where(take, i1, e1)
        e2 = jnp.where(take, i2, e2)
    oh1 = lane_f == e1
    oh2 = lane_f == e2
    s1 = jnp.sum(jnp.where(oh1, s, 0.0), axis=-1, keepdims=True)
    s2 = jnp.sum(jnp.where(oh2, s, 0.0), axis=-1, keepdims=True)
    den = s1 + s2
    r = lax.broadcasted_iota(jnp.int32, (tm, tm), 0)
    c = lax.broadcasted_iota(jnp.int32, (tm, tm), 1)
    lower = (c < r).astype(BF16)
    o1 = oh1.astype(F32)
    o2 = oh2.astype(F32)
    cum1 = _dot(lower, o1.astype(BF16))
    cum2 = _dot(lower, o2.astype(BF16))
    tot1 = jnp.sum(o1, axis=0, keepdims=True)
    base = carry[...]
    rank1 = jnp.sum(jnp.where(oh1, base + cum1, 0.0), axis=-1, keepdims=True)
    rank2 = jnp.sum(jnp.where(oh2, base + tot1 + cum2, 0.0), axis=-1, keepdims=True)
    total = base + tot1 + jnp.sum(o2, axis=0, keepdims=True)
    carry[...] = total
    cnt_ref[...] = total
    code1 = (rank1 * E + e1).astype(jnp.int32)
    code2 = (rank2 * E + e2).astype(jnp.int32)
    code_ref[...] = jnp.where(lane == 0, code1, jnp.where(lane == 1, code2, 0))
    wt_ref[...] = jnp.where(lane == 0, s1 / den, jnp.where(lane == 1, s2 / den, 0.0))


def _group_of_tile(i, n_lat_tiles, tiles_per_seq, n_batch):
    return jnp.where(i < n_lat_tiles, i // tiles_per_seq, n_batch)


def _norm_mod(x, g, mod, shift_idx, scale_idx, *, n_rows, L, B, router=None):
    D = x.shape[1]
    tm = ROW_TILE
    nlt, tps = (B * L) // tm, L // tm
    grp = functools.partial(_group_of_tile, n_lat_tiles=nlt, tiles_per_seq=tps, n_batch=B)
    in_specs = [pl.BlockSpec((tm, D), lambda i: (i, 0)),
                pl.BlockSpec((1, D), lambda i: (0, 0)),
                pl.BlockSpec((None, None, 1, D), lambda i: (grp(i), shift_idx, 0, 0)),
                pl.BlockSpec((None, None, 1, D), lambda i: (grp(i), scale_idx, 0, 0))]
    args = [x, g.reshape(1, D), mod, mod]
    if router is None:
        return pl.pallas_call(
            _norm_mod_kernel,
            out_shape=jax.ShapeDtypeStruct((n_rows, D), BF16),
            grid=(n_rows // tm,), in_specs=in_specs,
            out_specs=pl.BlockSpec((tm, D), lambda i: (i, 0)),
            compiler_params=_cparams(("parallel",)), name="norm_mod",
        )(*args)
    router_w, router_b = router
    E = router_w.shape[1]
    assert E == N_EXPERTS
    rw = jnp.pad(router_w, ((0, 0), (0, LANES - E)))
    rb = jnp.pad(router_b.astype(F32).reshape(1, E), ((0, 0), (0, LANES - E)))
    lanes = pl.BlockSpec((tm, LANES), lambda i: (i, 0))
    return pl.pallas_call(
        _norm_mod_router_kernel,
        out_shape=(jax.ShapeDtypeStruct((n_rows, D), F32),
                   jax.ShapeDtypeStruct((n_rows, LANES), jnp.int32),
                   jax.ShapeDtypeStruct((n_rows, LANES), F32),
                   jax.ShapeDtypeStruct((1, LANES), F32)),
        grid=(n_rows // tm,),
        in_specs=in_specs + [pl.BlockSpec((D, LANES), lambda i: (0, 0)),
                             pl.BlockSpec((1, LANES), lambda i: (0, 0))],
        out_specs=(pl.BlockSpec((tm, D), lambda i: (i, 0)), lanes, lanes,
                   pl.BlockSpec((1, LANES), lambda i: (0, 0))),
        scratch_shapes=[pltpu.VMEM((1, LANES), F32)],
        compiler_params=_cparams(("arbitrary",)), name="norm_mod_router",
    )(*args, rw, rb)


def _mm_kernel(a_ref, w_ref, o_ref):
    o_ref[...] = _dot(a_ref[...], w_ref[...]).astype(o_ref.dtype)


def _matmul(a, w, out_dtype, tm, tn):
    M, K = a.shape
    N = w.shape[1]
    return pl.pallas_call(
        _mm_kernel,
        out_shape=jax.ShapeDtypeStruct((M, N), out_dtype),
        grid=(M // tm, N // tn),
        in_specs=[pl.BlockSpec((tm, K), lambda i, j: (i, 0)),
                  pl.BlockSpec((K, tn), lambda i, j: (0, j))],
        out_specs=pl.BlockSpec((tm, tn), lambda i, j: (i, j)),
        compiler_params=_cparams(("parallel", "parallel")), name="matmul",
    )(a, w)


def _seq_edge_flags(tm, L, Lc, NL):
    r0 = pl.program_id(0) * tm
    lat = r0 < NL
    start = jnp.where(lat, r0 % L == 0, (r0 - NL) % Lc == 0)
    end = jnp.where(lat, (r0 + tm) % L == 0, (r0 + tm - NL) % Lc == 0)
    return jnp.where(start, 0.0, 1.0), jnp.where(end, 0.0, 1.0)


def _conv3(cur_ref, prev_ref, next_ref, w_ref, b_ref, keep_prev, keep_next):
    cur = cur_ref[...].astype(F32)
    tm = cur.shape[0]
    prev_row = prev_ref[BF16_SUBLANES - 1:BF16_SUBLANES, :].astype(F32) * keep_prev
    next_row = next_ref[0:1, :].astype(F32) * keep_next
    row = lax.broadcasted_iota(jnp.int32, (tm, 1), 0)
    up = jnp.where(row == 0, prev_row, pltpu.roll(cur, 1, 0))
    dn = jnp.where(row == tm - 1, next_row, pltpu.roll(cur, tm - 1, 0))
    return b_ref[...] + up * w_ref[0:1, :] + cur * w_ref[1:2, :] + dn * w_ref[2:3, :]


def _conv_specs(tm, tc, T, col_block):
    per = tm // BF16_SUBLANES
    last = T // BF16_SUBLANES - 1
    return [pl.BlockSpec((tm, tc), lambda i, j: (i, col_block + j)),
            pl.BlockSpec((BF16_SUBLANES, tc), lambda i, j: (jnp.maximum(i * per - 1, 0), col_block + j)),
            pl.BlockSpec((BF16_SUBLANES, tc), lambda i, j: (jnp.minimum((i + 1) * per, last), col_block + j))]


def _mconv_kernel(cur_ref, prev_ref, next_ref, w_ref, b_ref, scale_ref, o_ref, *, tm, L, Lc, NL):
    kp, kn = _seq_edge_flags(tm, L, Lc, NL)
    y = _conv3(cur_ref, prev_ref, next_ref, w_ref, b_ref, kp, kn)
    o_ref[...] = (y * _sigmoid(y) * scale_ref[...]).astype(o_ref.dtype)


def _mlstm_qk(P, conv_w, conv_b, *, L, Lc, NL, width):
    T = P.shape[0]
    tm, tc = ROW_TILE, 1024
    C = 2 * width
    scale = jnp.concatenate([jnp.ones((1, width), F32),
                             jnp.full((1, width), (width // M_HEADS) ** -0.5, F32)], axis=1)
    vec = lambda r: pl.BlockSpec((r, tc), lambda i, j: (0, j))
    return pl.pallas_call(
        functools.partial(_mconv_kernel, tm=tm, L=L, Lc=Lc, NL=NL),
        out_shape=jax.ShapeDtypeStruct((T, C), BF16),
        grid=(T // tm, C // tc),
        in_specs=_conv_specs(tm, tc, T, 0) + [vec(SHORT_CONV), vec(1), vec(1)],
        out_specs=pl.BlockSpec((tm, tc), lambda i, j: (i, j)),
        compiler_params=_cparams(("parallel", "parallel")), name="mlstm_qk_conv",
    )(P, P, P, conv_w, conv_b.reshape(1, C), scale)


def _hyena_pre_kernel(c0, p0, n0, c1, p1, n1, c2, p2, n2, w0, w1, w2, b0, b1, b2,
                      x0_ref, z_ref, *, tm, L, Lc, NL):
    kp, kn = _seq_edge_flags(tm, L, Lc, NL)
    x0 = _conv3(c0, p0, n0, w0, b0, kp, kn)
    x1 = _conv3(c1, p1, n1, w1, b1, kp, kn)
    v = _conv3(c2, p2, n2, w2, b2, kp, kn)
    x0_ref[...] = x0
    z_ref[...] = v * x1


def _hyena_pre(P, col0, conv_w, conv_b, *, L, Lc, NL, C):
    T = P.shape[0]
    tm, tc = ROW_TILE, 512
    nb = C // tc
    specs = []
    for part in range(3):
        specs += _conv_specs(tm, tc, T, col0 // tc + part * nb)
    wspecs = [pl.BlockSpec((SHORT_CONV, tc), lambda i, j, p=part: (0, p * nb + j)) for part in range(3)]
    bspecs = [pl.BlockSpec((1, tc), lambda i, j, p=part: (0, p * nb + j)) for part in range(3)]
    b2d = conv_b.reshape(1, 3 * C)
    out = pl.BlockSpec((tm, tc), lambda i, j: (i, j))
    return pl.pallas_call(
        functools.partial(_hyena_pre_kernel, tm=tm, L=L, Lc=Lc, NL=NL),
        out_shape=(jax.ShapeDtypeStruct((T, C), F32), jax.ShapeDtypeStruct((T, C), F32)),
        grid=(T // tm, nb),
        in_specs=specs + wspecs + bspecs,
        out_specs=(out, out),
        compiler_params=_cparams(("parallel", "parallel")), name="hyena_pre",
    )(*([P] * 9), conv_w, conv_w, conv_w, b2d, b2d, b2d)


def _mlstm_chunk(q, k, v, li_r, lf_r, li_c, lf_c, C_scr, n_scr, m_scr, fwd):
    Q = M_CHUNK
    row = lax.broadcasted_iota(jnp.int32, (Q, Q), 0)
    col = lax.broadcasted_iota(jnp.int32, (Q, Q), 1)
    mask = (col <= row) if fwd else (col >= row)
    tri_c = mask.astype(BF16)
    tri_r = ((row <= col) if fwd else (row >= col)).astype(BF16)
    lfc_h, lfc_l = _split(jnp.broadcast_to(lf_c, (Q, Q)))
    lfr_h, lfr_l = _split(jnp.broadcast_to(lf_r, (Q, Q)))
    b_cols = _dot(tri_c, lfc_h) + _dot(tri_c, lfc_l)
    b_rows = _dot(lfr_h, tri_r) + _dot(lfr_l, tri_r)
    dm = jnp.where(mask, b_cols - b_rows + li_r, -jnp.inf)
    m_prev = m_scr[0:1, 0:1]
    b_col = b_cols[:, 0:1]
    inter = b_col + m_prev
    mt = jnp.maximum(inter, jnp.max(dm, axis=-1, keepdims=True))
    s = lax.dot_general(q, k, (((1,), (1,)), ((), ())), preferred_element_type=F32) * jnp.exp(dm - mt)
    wi = jnp.exp(inter - mt)
    num = _dot(s.astype(BF16), v) + wi * _dot(q, C_scr[...].astype(BF16))
    qn = jnp.sum(q.astype(F32) * n_scr[...], axis=-1, keepdims=True)
    den = jnp.sum(s, axis=-1, keepdims=True) + wi * qn
    h = num / jnp.maximum(jnp.abs(den), jnp.exp(-mt))
    b_tot = b_cols[Q - 1:Q, 0:1] if fwd else b_cols[0:1, 0:1]
    ws = b_tot - b_col + li_c
    m_new = jnp.maximum(b_tot + m_prev, jnp.max(ws, axis=0, keepdims=True))
    decay = jnp.exp(b_tot + m_prev - m_new)
    kw = k.astype(F32) * jnp.exp(ws - m_new)
    C_scr[...] = decay * C_scr[...] + lax.dot_general(
        kw.astype(BF16), v, (((0,), (0,)), ((), ())), preferred_element_type=F32)
    n_scr[...] = decay * n_scr[...] + jnp.sum(kw, axis=0, keepdims=True)
    m_scr[...] = jnp.broadcast_to(m_new, m_scr.shape)
    return h


def _mlstm_kernel(gr_ref, gc_ref, grc_ref, gcc_ref,
                  q_ref, k_ref, v_ref, o_ref, qc_ref, kc_ref, vc_ref, oc_ref, ng_ref,
                  y_ref, yc_ref,
                  C_scr, n_scr, m_scr, hf_scr, hfc_scr, *, S, SEG, Lc):
    s = pl.program_id(2)

    def zero_state():
        C_scr[...] = jnp.zeros_like(C_scr)
        n_scr[...] = jnp.zeros_like(n_scr)
        m_scr[...] = jnp.zeros_like(m_scr)

    def run(nchunks, fwd, qr, kr, vr, grr, gcr, emit):
        gi = 0 if fwd else 2

        def body(j, carry):
            c = j if fwd else nchunks - 1 - j
            r = pl.multiple_of(c * M_CHUNK, M_CHUNK)
            rows = pl.ds(r, M_CHUNK)
            gcs = gcr[rows, :]
            h = _mlstm_chunk(qr[rows, :], kr[rows, :], vr[rows, :],
                             grr[gi, pl.ds(c, 1), :], _log_sigmoid(grr[gi + 1, pl.ds(c, 1), :]),
                             gcs[:, gi:gi + 1], _log_sigmoid(gcs[:, gi + 1:gi + 2]),
                             C_scr, n_scr, m_scr, fwd)
            emit(r, h)
            return carry

        lax.fori_loop(0, nchunks, body, 0)

    def finish(h_fwd, h_bwd, o):
        h = h_fwd + h_bwd
        y = h * lax.rsqrt(jnp.mean(h * h, axis=-1, keepdims=True) + EPS) * ng_ref[...]
        return (y * _sigmoid(o.astype(F32))).astype(y_ref.dtype)

    def emit_ctx_fwd(r, h):
        hfc_scr[pl.ds(r, M_CHUNK), :] = h

    def emit_ctx_bwd(r, h):
        rows = pl.ds(r, M_CHUNK)
        yc_ref[rows, :] = finish(hfc_scr[rows, :], h, oc_ref[rows, :])

    @pl.when(s == 0)
    def _():
        zero_state()
        run(Lc // M_CHUNK, True, qc_ref, kc_ref, vc_ref, grc_ref, gcc_ref, emit_ctx_fwd)

    @pl.when(s < S)
    def _():
        base = s * SEG

        def emit(r, h):
            hf_scr[pl.ds(pl.multiple_of(base + r, M_CHUNK), M_CHUNK), :] = h

        run(SEG // M_CHUNK, True, q_ref, k_ref, v_ref, gr_ref, gc_ref, emit)

    @pl.when(s == S)
    def _():
        zero_state()
        run(Lc // M_CHUNK, False, qc_ref, kc_ref, vc_ref, grc_ref, gcc_ref, emit_ctx_bwd)

    @pl.when(s >= S)
    def _():
        base = (2 * S - 1 - s) * SEG

        def emit(r, h):
            rows = pl.ds(r, M_CHUNK)
            hf = hf_scr[pl.ds(pl.multiple_of(base + r, M_CHUNK), M_CHUNK), :]
            y_ref[rows, :] = finish(hf, h, o_ref[rows, :])

        run(SEG // M_CHUNK, False, q_ref, k_ref, v_ref, gr_ref, gc_ref, emit)


def _mlstm(QK, P, G, gate_b, norm_g, *, B, L, Lc, width):
    H = M_HEADS
    dh = width // H
    NL = B * L
    SEG = min(L, 1024)
    S = L // SEG
    g = G[:, :4 * H] + gate_b.astype(F32)[None, :]

    def gate_views(rows, n):
        a = rows.reshape(B, n, 4, H)
        return (a.transpose(0, 3, 2, 1).reshape(B, H, 4, n // M_CHUNK, M_CHUNK),
                a.transpose(0, 3, 1, 2))

    gr, gc = gate_views(g[:NL], L)
    grc, gcc = gate_views(g[NL:], Lc)

    def seg_of(s):
        return jnp.where(s < S, s, 2 * S - 1 - s)

    def lat(col0):
        return pl.BlockSpec((SEG, dh), lambda b, h, s: (b * S + seg_of(s), col0 + h))

    def ctx(col0):
        return pl.BlockSpec((Lc, dh), lambda b, h, s: (NL // Lc + b, col0 + h))

    in_specs = [
        pl.BlockSpec((None, None, 4, SEG // M_CHUNK, M_CHUNK), lambda b, h, s: (b, h, 0, seg_of(s), 0)),
        pl.BlockSpec((None, None, SEG, 4), lambda b, h, s: (b, h, seg_of(s), 0)),
        pl.BlockSpec((None, None, 4, Lc // M_CHUNK, M_CHUNK), lambda b, h, s: (b, h, 0, 0, 0)),
        pl.BlockSpec((None, None, Lc, 4), lambda b, h, s: (b, h, 0, 0)),
        lat(0), lat(H), lat(2 * H), lat(3 * H),
        ctx(0), ctx(H), ctx(2 * H), ctx(3 * H),
        pl.BlockSpec((1, dh), lambda b, h, s: (0, 0)),
    ]
    out_specs = (
        pl.BlockSpec((SEG, dh), lambda b, h, s: (b * S + jnp.where(s < S, S - 1, 2 * S - 1 - s), h)),
        pl.BlockSpec((Lc, dh), lambda b, h, s: (b, h)),
    )
    return pl.pallas_call(
        functools.partial(_mlstm_kernel, S=S, SEG=SEG, Lc=Lc),
        out_shape=(jax.ShapeDtypeStruct((NL, width), BF16),
                   jax.ShapeDtypeStruct((B * Lc, width), BF16)),
        grid=(B, H, 2 * S),
        in_specs=in_specs, out_specs=out_specs,
        scratch_shapes=[pltpu.VMEM((dh, dh), F32), pltpu.VMEM((1, dh), F32), pltpu.VMEM((8, LANES), F32),
                        pltpu.VMEM((L, dh), F32), pltpu.VMEM((Lc, dh), F32)],
        compiler_params=_cparams(("parallel", "parallel", "arbitrary")), name="mlstm",
    )(gr, gc, grc, gcc, QK, QK, P, P, QK, QK, P, P, norm_g.reshape(1, dh).astype(F32))


def _attn_prep_kernel(x_ref, g_ref, cos_ref, sin_ref, o_ref, *, scale, dh):
    n_blk = x_ref.shape[1] // LANES
    r = lax.broadcasted_iota(jnp.int32, (LANES, LANES), 0)
    c = lax.broadcasted_iota(jnp.int32, (LANES, LANES), 1)
    group = (r // dh == c // dh).astype(BF16)
    lane = lax.broadcasted_iota(jnp.int32, (1, LANES), 1)
    quarter = dh // 4
    first = (lane % (2 * quarter)) < quarter
    cos = cos_ref[...]
    sin = sin_ref[...]
    for hb in range(n_blk):
        cols = slice(hb * LANES, (hb + 1) * LANES)
        x = x_ref[:, cols].astype(F32)
        hi, lo = _split(x * x)
        ms = (_dot(hi, group) + _dot(lo, group)) * (1.0 / dh)
        y = x * lax.rsqrt(ms + EPS) * g_ref[:, cols]
        rot = jnp.where(first, -pltpu.roll(y, LANES - quarter, 1), pltpu.roll(y, quarter, 1))
        o_ref[:, cols] = ((y * cos + rot * sin) * scale).astype(o_ref.dtype)


def _attn_prep(P, col0, gain, cos_tab, sin_tab, scale, *, L, NL, dh):
    T = P.shape[0]
    W = A_HEADS * 2 * dh
    tm = ROW_TILE
    nlt, tps = NL // tm, L // tm
    tab = pl.BlockSpec((tm, LANES), lambda i: (jnp.where(i < nlt, i % tps, tps), 0))
    g = jnp.tile(gain.astype(F32), W // dh).reshape(1, W)
    return pl.pallas_call(
        functools.partial(_attn_prep_kernel, scale=scale, dh=dh),
        out_shape=jax.ShapeDtypeStruct((T, W), BF16),
        grid=(T // tm,),
        in_specs=[pl.BlockSpec((tm, W), lambda i: (i, col0 // W)),
                  pl.BlockSpec((1, W), lambda i: (0, 0)), tab, tab],
        out_specs=pl.BlockSpec((tm, W), lambda i: (i, 0)),
        compiler_params=_cparams(("parallel",)), name="attn_prep",
    )(P, g, cos_tab, sin_tab)


def _attn_kernel(*refs, n_lat, tk, L, Lc, dh, lam_init):
    if n_lat:
        lam_ref, q_ref, kl_ref, vl_ref, kc_ref, vc_ref, sg_ref, o_ref, vext, acc = refs
    else:
        lam_ref, q_ref, kc_ref, vc_ref, sg_ref, o_ref, vext, acc = refs
    dv = 2 * dh
    ctx0 = n_lat * tk

    @pl.when(pl.program_id(2) == 0)
    def _():
        if n_lat:
            vext[0:L, 0:dv] = vl_ref[...]
        vext[ctx0:ctx0 + Lc, 0:dv] = vc_ref[...]
        vext[:, dv:2 * dv] = jnp.ones((vext.shape[0], dv), BF16)

    q = q_ref[...]
    lane = lax.broadcasted_iota(jnp.int32, (1, dv), 1)
    qs = (jnp.where(lane < dh, q, jnp.zeros_like(q)), jnp.where(lane >= dh, q, jnp.zeros_like(q)))
    acc[...] = jnp.zeros_like(acc)
    tq = q.shape[0]

    def update(comp, kblk, vblk, m_old):
        s = lax.dot_general(qs[comp], kblk, (((1,), (1,)), ((), ())), preferred_element_type=F32)
        m_new = jnp.maximum(m_old, jnp.max(s, axis=-1, keepdims=True))
        p = jnp.exp2(s - m_new).astype(BF16)
        acc[comp] = jnp.exp2(m_old - m_new) * acc[comp] + _dot(p, vblk)
        return m_new

    m = (jnp.full((tq, 1), -jnp.inf, F32),) * 2
    for c in range(n_lat):
        kblk, vblk = kl_ref[c * tk:(c + 1) * tk, :], vext[c * tk:(c + 1) * tk, :]
        m = update(0, kblk, vblk, m[0]), update(1, kblk, vblk, m[1])
    kblk, vblk = kc_ref[...], vext[ctx0:ctx0 + Lc, :]
    update(0, kblk, vblk, m[0])
    update(1, kblk, vblk, m[1])

    lp = lam_ref[...]
    lam = (jnp.exp(jnp.sum(lp[0:1] * lp[1:2], axis=-1, keepdims=True))
           - jnp.exp(jnp.sum(lp[2:3] * lp[3:4], axis=-1, keepdims=True)) + lam_init)
    a0, a1 = acc[0], acc[1]
    o = a0[:, 0:dv] / a0[:, dv:dv + 1] - lam * (a1[:, 0:dv] / a1[:, dv:dv + 1])
    y = o * lax.rsqrt(jnp.mean(o * o, axis=-1, keepdims=True) + EPS) * sg_ref[...] * (1.0 - lam_init)
    o_ref[...] = y.astype(o_ref.dtype)


def _attention(Qh, Kh, P, vcol0, lam_p, sub_g, lam_init, *, B, L, Lc, dh, latent):
    NL = B * L
    dv = 2 * dh
    H = A_HEADS
    vb = vcol0 // dv
    if latent:
        tq, tk = 256, min(L, 1024)
        n_lat, nq, rows_out = L // tk, L // tq, NL
        q_spec = pl.BlockSpec((tq, dv), lambda b, h, i: (b * nq + i, h))
        lat_specs = [pl.BlockSpec((L, dv), lambda b, h, i: (b, h)),
                     pl.BlockSpec((L, dv), lambda b, h, i: (b, vb + h))]
        lat_args = [Kh, P]
        o_spec = pl.BlockSpec((tq, dv), lambda b, h, i: (b * nq + i, h))
        nkeys = L + Lc
    else:
        tq, tk = Lc, 512
        n_lat, nq, rows_out = 0, 1, B * Lc
        q_spec = pl.BlockSpec((Lc, dv), lambda b, h, i: (NL // Lc + b, h))
        lat_specs, lat_args = [], []
        o_spec = pl.BlockSpec((Lc, dv), lambda b, h, i: (b, h))
        nkeys = Lc
    ctx_specs = [pl.BlockSpec((Lc, dv), lambda b, h, i: (NL // Lc + b, h)),
                 pl.BlockSpec((Lc, dv), lambda b, h, i: (NL // Lc + b, vb + h))]
    return pl.pallas_call(
        functools.partial(_attn_kernel, n_lat=n_lat, tk=tk, L=L, Lc=Lc, dh=dh, lam_init=lam_init),
        out_shape=jax.ShapeDtypeStruct((rows_out, H * dv), BF16),
        grid=(B, H, nq),
        in_specs=[pl.BlockSpec((4, dh), lambda b, h, i: (0, 0)), q_spec] + lat_specs + ctx_specs
                 + [pl.BlockSpec((1, dv), lambda b, h, i: (0, 0))],
        out_specs=o_spec,
        scratch_shapes=[pltpu.VMEM((nkeys, 2 * dv), BF16), pltpu.VMEM((2, tq, 2 * dv), F32)],
        compiler_params=_cparams(("parallel", "parallel", "arbitrary")),
        name="diff_attn_latent" if latent else "diff_attn_ctx",
    )(lam_p.astype(F32), Qh, *lat_args, Kh, P, sub_g.reshape(1, dv).astype(F32))


def _rope_tables(L, dh, tm):
    rows = L // GRID_W
    row = np.repeat(np.arange(rows), GRID_W).astype(np.float64)
    col = np.tile(np.arange(GRID_W), rows).astype(np.float64)
    nf = dh // 4
    inv = (np.float32(ROPE_BASE) ** (-np.arange(nf, dtype=np.float32) / nf)).astype(np.float64)
    ang = np.concatenate([row[:, None] * inv] * 2 + [col[:, None] * inv] * 2, axis=-1)
    ang = np.tile(ang.astype(np.float32).astype(np.float64), (1, LANES // dh))
    cos = np.concatenate([np.cos(ang), np.ones((tm, LANES))], axis=0)
    sin = np.concatenate([np.sin(ang), np.zeros((tm, LANES))], axis=0)
    return jnp.asarray(cos, F32), jnp.asarray(sin, F32)


def _filter_kernel(f_ref, w1_ref, b1_ref, w2_ref, b2_ref, fr_ref, w3_ref, dl_ref, taps_ref, ssq_ref):
    f = f_ref[...]
    freq = fr_ref[...]
    h = jnp.sin(freq * (_dot3(f, w1_ref[...]) + b1_ref[...]))
    h = jnp.sin(freq * (_dot3(h, w2_ref[...]) + b2_ref[...]))
    h = _dot3(h, w3_ref[...]) * jnp.exp(-f[:, 0:1] * jnp.abs(dl_ref[...]))

    @pl.when(pl.program_id(1) == 0)
    def _():
        ssq_ref[...] = jnp.zeros_like(ssq_ref)

    ssq_ref[...] += jnp.sum(h * h, axis=0, keepdims=True)
    taps_ref[...] = h * f[:, H_EMB:H_EMB + 1]


def _filter_features(L):
    t = np.linspace(0.0, 1.0, L, dtype=np.float32).astype(np.float64)[:, None]
    bands = (H_EMB - 1) // 2
    w = (np.float32(2.0 * math.pi) * np.arange(L, dtype=np.float32) / np.float32(L)).astype(np.float64)[:, None]
    f = np.linspace(1e-4, bands - 1, bands, dtype=np.float32).astype(np.float64)[None, :]
    fw = (f.astype(np.float32) * w.astype(np.float32)).astype(np.float64)
    z = np.concatenate([t, np.cos(fw), -np.sin(fw)], axis=-1)
    feat = np.zeros((2 * L, LANES), np.float64)
    feat[:L, :H_EMB] = z
    idx = (L - np.arange(L)) % L
    feat[L:, :H_EMB] = z[idx]
    feat[:, H_EMB] = 1.0
    feat[L, H_EMB] = 0.0
    return jnp.asarray(feat, F32)


def _hyena_filters(L, w1, b1, w2, b2, w3, freq, delta, C):
    pad = LANES - H_FFN
    w1p = jnp.pad(w1, ((0, LANES - H_EMB), (0, pad)))
    w2p = jnp.pad(w2, ((0, pad), (0, pad)))
    w3p = jnp.pad(w3, ((0, pad), (0, 0)))
    row = lambda a: jnp.pad(a.reshape(1, H_FFN), ((0, 0), (0, pad)))
    tr = min(L, 512)
    nr = L // tr
    const = lambda shape: pl.BlockSpec(shape, lambda hf, r: (0, 0))
    return pl.pallas_call(
        _filter_kernel,
        out_shape=(jax.ShapeDtypeStruct((2 * L, C), F32), jax.ShapeDtypeStruct((2, 1, C), F32)),
        grid=(2, nr),
        in_specs=[pl.BlockSpec((tr, LANES), lambda hf, r: (hf * nr + r, 0)),
                  const((LANES, LANES)), const((1, LANES)), const((LANES, LANES)), const((1, LANES)),
                  const((1, LANES)),
                  pl.BlockSpec((LANES, C), lambda hf, r: (0, hf)),
                  pl.BlockSpec((1, C), lambda hf, r: (0, hf))],
        out_specs=(pl.BlockSpec((tr, C), lambda hf, r: (hf * nr + r, 0)),
                   pl.BlockSpec((None, 1, C), lambda hf, r: (hf, 0, 0))),
        compiler_params=_cparams(("parallel", "arbitrary")), name="hyena_filter",
    )(_filter_features(L), w1p, row(b1), w2p, row(b2), row(freq), w3p, delta.reshape(1, 2 * C))


def _cblock(m):
    return np.block([[m.real, -m.imag], [m.imag, m.real]])


def _dft_consts(L):
    N = 2 * L
    N2 = LANES
    N1 = N // N2
    half = N1 // 2
    n1 = np.arange(N1)
    n2 = np.arange(N2)
    F1 = np.exp(-2j * np.pi * np.outer(n1, n1) / N1)
    F2 = np.exp(-2j * np.pi * np.outer(n2, n2) / N2)
    a_data = _cblock(F1[:, :half])
    a_taps = np.concatenate([F1.real, F1.imag], axis=0)
    b_fwd = _cblock(F2)
    b_inv = _cblock(np.conj(F2))
    fin = _cblock(np.conj(F1)[:half, :]) / N
    ang = 2.0 * np.pi * np.outer(n2, n1) / N
    tw = dict(c_a=np.cos(ang)[:, :, None], s_a=np.sin(ang)[:, :, None],
              c_b=np.cos(ang.T)[:, :, None], s_b=np.sin(ang.T)[:, :, None])
    return dict(N1=N1, half=half, a_data=_np_split(a_data), a_taps=_np_split(a_taps),
                b_fwd=_np_split(b_fwd), b_inv=_np_split(b_inv), fin=_np_split(fin),
                tw={k: jnp.asarray(v, F32) for k, v in tw.items()})


FFT_N2_STEP = 8


def _fft_a_kernel(x_ref, mh, ml, c_ref, s_ref, ar_ref, ai_ref, *, N1):
    for j in range(FFT_N2_STEP):
        r = _dot3c(mh[...], ml[...], x_ref[:, j, :])
        re, im = r[:N1], r[N1:]
        c, s = c_ref[j], s_ref[j]
        ar_ref[:, j, :] = re * c + im * s
        ai_ref[:, j, :] = im * c - re * s


def _fft_stage_a(x3, mats, tw, *, N1, C, ct):
    mh, ml = mats
    const = pl.BlockSpec(mh.shape, lambda g, c: (0, 0))
    twspec = pl.BlockSpec((FFT_N2_STEP, N1, 1), lambda g, c: (g, 0, 0))
    blk = pl.BlockSpec((N1, FFT_N2_STEP, ct), lambda g, c: (0, g, c))
    shape = jax.ShapeDtypeStruct((N1, LANES, C), F32)
    return pl.pallas_call(
        functools.partial(_fft_a_kernel, N1=N1),
        out_shape=(shape, shape), grid=(LANES // FFT_N2_STEP, C // ct),
        in_specs=[blk, const, const, twspec, twspec],
        out_specs=(blk, blk),
        compiler_params=_cparams(("parallel", "parallel")), name="hyena_fft_a",
    )(x3, mh, ml, tw["c_a"], tw["s_a"])


def _fft_mid_kernel(ar, ai, tr, ti, ssq, fh, fl, ih, il, c_ref, s_ref, br, bi):
    N2 = LANES
    x = _dot3c(fh[...], fl[...], jnp.concatenate([ar[...], ai[...]], axis=0))
    h = _dot3c(fh[...], fl[...], jnp.concatenate([tr[...], ti[...]], axis=0))
    scale = lax.rsqrt(ssq[0] + ssq[1] + EPS)
    xr, xi, hr, hi = x[:N2], x[N2:], h[:N2] * scale, h[N2:] * scale
    y = jnp.concatenate([xr * hr - xi * hi, xr * hi + xi * hr], axis=0)
    r = _dot3c(ih[...], il[...], y)
    re, im = r[:N2], r[N2:]
    c, s = c_ref[...], s_ref[...]
    br[...] = re * c - im * s
    bi[...] = im * c + re * s


def _fft_mid(Ar, Ai, Tr, Ti, ssq, consts, *, C, ct):
    N1 = consts["N1"]
    blk = pl.BlockSpec((None, LANES, ct), lambda k1, c: (k1, 0, c))
    const = pl.BlockSpec((2 * LANES, 2 * LANES), lambda k1, c: (0, 0))
    twspec = pl.BlockSpec((None, LANES, 1), lambda k1, c: (k1, 0, 0))
    shape = jax.ShapeDtypeStruct((N1, LANES, C), F32)
    return pl.pallas_call(
        _fft_mid_kernel, out_shape=(shape, shape), grid=(N1, C // ct),
        in_specs=[blk, blk, blk, blk, pl.BlockSpec((2, 1, ct), lambda k1, c: (0, 0, c)),
                  const, const, const, const, twspec, twspec],
        out_specs=(blk, blk),
        compiler_params=_cparams(("parallel", "parallel")), name="hyena_fft_mid",
    )(Ar, Ai, Tr, Ti, ssq, *consts["b_fwd"], *consts["b_inv"], consts["tw"]["c_b"], consts["tw"]["s_b"])


def _fft_fin_kernel(br, bi, mh, ml, x0, z, bias, y):
    for j in range(FFT_N2_STEP):
        r = _dot3c(mh[...], ml[...], jnp.concatenate([br[:, j, :], bi[:, j, :]], axis=0))
        y[:, j, :] = x0[:, j, :] * (r + bias[...] * z[:, j, :])


def _fft_final(Br, Bi, x0_3d, z_3d, bias, consts, *, C, ct):
    N1 = consts["N1"]
    mh, ml = consts["fin"]
    blk = pl.BlockSpec((N1, FFT_N2_STEP, ct), lambda g, c: (0, g, c))
    const = pl.BlockSpec(mh.shape, lambda g, c: (0, 0))
    return pl.pallas_call(
        _fft_fin_kernel,
        out_shape=jax.ShapeDtypeStruct((N1, LANES, C), F32),
        grid=(LANES // FFT_N2_STEP, C // ct),
        in_specs=[blk, blk, const, const, blk, blk, pl.BlockSpec((1, ct), lambda g, c: (0, c))],
        out_specs=blk,
        compiler_params=_cparams(("parallel", "parallel")), name="hyena_fft_final",
    )(Br, Bi, mh, ml, x0_3d, z_3d, bias)


def _hyena_latent(x0, z, taps, ssq, bias, consts, *, B, L, C):
    assert B == 2
    N1 = consts["N1"]
    ct = 512
    T = x0.shape[0]
    z3 = z.reshape(T // LANES, LANES, C)
    x03 = x0.reshape(T // LANES, LANES, C)
    Ar, Ai = _fft_stage_a(z3, consts["a_data"], consts["tw"], N1=N1, C=C, ct=ct)
    Tr, Ti = _fft_stage_a(taps.reshape(N1, LANES, C), consts["a_taps"], consts["tw"], N1=N1, C=C, ct=ct)
    Br, Bi = _fft_mid(Ar, Ai, Tr, Ti, ssq, consts, C=C, ct=ct)
    y = _fft_final(Br, Bi, x03, z3, bias.reshape(1, C).astype(F32), consts, C=C, ct=ct)
    return y.reshape(B * L, C)


def _hyena_ctx_kernel(z, x0, taps, ssq, bias, dh, dl, th, tl, ih, il, y, *, Lc):
    n = 2 * Lc
    x = _dot3c(dh[...], dl[...], z[...])
    h = _dot3c(th[...], tl[...], taps[...])
    scale = lax.rsqrt(ssq[0] + ssq[1] + EPS)
    xr, xi, hr, hi = x[:n], x[n:], h[:n] * scale, h[n:] * scale
    r = _dot3c(ih[...], il[...], jnp.concatenate([xr * hr - xi * hi, xr * hi + xi * hr], axis=0))
    y[...] = x0[...] * (r + bias[...] * z[...])


def _hyena_ctx(x0, z, taps, ssq, bias, *, B, L, Lc, C):
    assert B == 2 and (B * L) % (2 * Lc) == 0
    n = 2 * Lc
    idx = np.arange(n)
    F = np.exp(-2j * np.pi * np.outer(idx, idx) / n)
    d = _np_split(_cblock(F[:, :Lc]))
    t = _np_split(np.concatenate([F.real, F.imag], axis=0))
    inv = _np_split(_cblock(np.conj(F)[:Lc, :]) / n)
    ct = 256
    r0 = (B * L) // n
    both = pl.BlockSpec((n, ct), lambda c: (r0, c))
    const = lambda m: pl.BlockSpec(m.shape, lambda c: (0, 0))
    return pl.pallas_call(
        functools.partial(_hyena_ctx_kernel, Lc=Lc),
        out_shape=jax.ShapeDtypeStruct((n, C), F32),
        grid=(C // ct,),
        in_specs=[both, both,
                  pl.BlockSpec((n, ct), lambda c: (0, c)),
                  pl.BlockSpec((2, 1, ct), lambda c: (0, 0, c)),
                  pl.BlockSpec((1, ct), lambda c: (0, c)),
                  const(d[0]), const(d[1]), const(t[0]), const(t[1]), const(inv[0]), const(inv[1])],
        out_specs=pl.BlockSpec((n, ct), lambda c: (0, c)),
        compiler_params=_cparams(("parallel",)), name="hyena_ctx",
    )(z, x0, taps, ssq, bias.reshape(1, C).astype(F32), *d, *t, *inv)


def _merge_kernel(*refs, with_ctx, n_lat_tiles):
    if with_ctx:
        (ym, ya, yh, ymc, yac, yhc, gm, ga, gh, mb, wm, wa, wh, wo, x_ref, g1, o_ref) = refs
        is_ctx = pl.program_id(0) >= n_lat_tiles
        pick = lambda lat, ctx: jnp.where(is_ctx, ctx[...], lat[...])
        m, a, h = pick(ym, ymc), pick(ya, yac), pick(yh, yhc)
    else:
        (ym, ya, yh, gm, ga, gh, mb, wm, wa, wh, wo, x_ref, g1, o_ref) = refs
        m, a, h = ym[...], ya[...], yh[...]
    D = x_ref.shape[1]
    bias = mb[...]
    gate = lambda g, k: _sigmoid(g[...].astype(F32) + bias[:, k * D:(k + 1) * D])
    y = (gate(gm, 0) * _dot(m, wm[...]) + gate(ga, 1) * _dot(a, wa[...])
         + gate(gh, 2) * _dot(h.astype(BF16), wh[...]))
    o_ref[...] = x_ref[...] + g1[...] * _dot(y.astype(BF16), wo[...])


def _merge(X, P, gcol0, branches, merge_b, weights, mod, gate_idx, *, n_rows, B, L, with_ctx):
    D = X.shape[1]
    tm = ROW_TILE
    nlt, tps = (B * L) // tm, L // tm
    grp = functools.partial(_group_of_tile, n_lat_tiles=nlt, tiles_per_seq=tps, n_batch=B)
    lat = pl.BlockSpec((tm, D), lambda i: (jnp.minimum(i, nlt - 1), 0))
    ctx = pl.BlockSpec((tm, D), lambda i: (jnp.maximum(i - nlt, 0), 0))
    gb = gcol0 // D
    gspec = lambda k: pl.BlockSpec((tm, D), lambda i: (i, gb + k))
    wspec = pl.BlockSpec((D, D), lambda i: (0, 0))
    in_specs = ([lat] * 3 + ([ctx] * 3 if with_ctx else []) + [gspec(0), gspec(1), gspec(2)]
                + [pl.BlockSpec((1, 3 * D), lambda i: (0, 0))] + [wspec] * 4
                + [pl.BlockSpec((tm, D), lambda i: (i, 0)),
                   pl.BlockSpec((None, None, 1, D), lambda i: (grp(i), gate_idx, 0, 0))])
    return pl.pallas_call(
        functools.partial(_merge_kernel, with_ctx=with_ctx, n_lat_tiles=nlt),
        out_shape=jax.ShapeDtypeStruct((n_rows, D), F32),
        grid=(n_rows // tm,), in_specs=in_specs,
        out_specs=pl.BlockSpec((tm, D), lambda i: (i, 0)),
        compiler_params=_cparams(("parallel",)), name="merge",
    )(*branches, P, P, P, merge_b.reshape(1, 3 * D).astype(F32), *weights, X, mod)


EXPERT_BITS = 5
assert 1 << EXPERT_BITS == N_EXPERTS
DMA_UNROLL = 8


def _slot_of(code_ref, pstart_ref, pos):
    code = code_ref[pos]
    return pstart_ref[code & (N_EXPERTS - 1)] + (code >> EXPERT_BITS)


def _dispatch_kernel(code, pstart, tok_ref, xs_zero, xs_out, sem, *, n_tok):
    del xs_zero
    R = tok_ref.shape[0]
    base = pl.program_id(0) * R

    def copy(pos, r):
        return pltpu.make_async_copy(tok_ref.at[pl.ds(r, 1), :],
                                     xs_out.at[pl.ds(_slot_of(code, pstart, pos), 1), :], sem)

    for wait in (False, True):
        for k in range(TOP_K):
            def body(r, carry):
                cp = copy(k * n_tok + base + r, r)
                cp.wait() if wait else cp.start()
                return carry
            lax.fori_loop(0, R, body, 0, unroll=DMA_UNROLL)


def _moe_dispatch(tok, code, pstart, P):
    N, D = tok.shape
    tm = MOE_BLOCK
    return pl.pallas_call(
        functools.partial(_dispatch_kernel, n_tok=N),
        out_shape=jax.ShapeDtypeStruct((P, D), F32),
        grid_spec=pltpu.PrefetchScalarGridSpec(
            num_scalar_prefetch=2, grid=(N // tm,),
            in_specs=[pl.BlockSpec((tm, D), lambda i, cd, ps: (i, 0)),
                      pl.BlockSpec(memory_space=pl.ANY)],
            out_specs=pl.BlockSpec(memory_space=pl.ANY),
            scratch_shapes=[pltpu.SemaphoreType.DMA]),
        input_output_aliases={3: 0},
        compiler_params=_cparams(("arbitrary",)), name="moe_dispatch",
    )(code, pstart, tok, jnp.zeros((P, D), F32))


def _moe_kernel(blk_e, xs_ref, w1_ref, w3_ref, w2_ref, y_ref, w1b, w3b, w2b):
    i = pl.program_id(0)

    @pl.when((i == 0) | (blk_e[i] != blk_e[jnp.maximum(i - 1, 0)]))
    def _():
        w1b[...] = w1_ref[...].astype(BF16)
        w3b[...] = w3_ref[...].astype(BF16)
        w2b[...] = w2_ref[...].astype(BF16)

    x = xs_ref[...].astype(BF16)
    a = _dot(x, w1b[...])
    g = (a * _sigmoid(a)) * _dot(x, w3b[...])
    y_ref[...] = _dot(g.astype(BF16), w2b[...])


def _moe_experts(xs, blk_e, w1, w3, w2):
    P, D = xs.shape
    E, _, De = w1.shape
    return pl.pallas_call(
        _moe_kernel,
        out_shape=jax.ShapeDtypeStruct((P, D), F32),
        grid_spec=pltpu.PrefetchScalarGridSpec(
            num_scalar_prefetch=1, grid=(P // MOE_BLOCK,),
            in_specs=[pl.BlockSpec((MOE_BLOCK, D), lambda i, be: (i, 0)),
                      pl.BlockSpec((None, D, De), lambda i, be: (be[i], 0, 0)),
                      pl.BlockSpec((None, D, De), lambda i, be: (be[i], 0, 0)),
                      pl.BlockSpec((None, De, D), lambda i, be: (be[i], 0, 0))],
            out_specs=pl.BlockSpec((MOE_BLOCK, D), lambda i, be: (i, 0)),
            scratch_shapes=[pltpu.VMEM((D, De), BF16), pltpu.VMEM((D, De), BF16), pltpu.VMEM((De, D), BF16)]),
        compiler_params=_cparams(("arbitrary",)), name="moe_experts",
    )(blk_e, xs, w1, w3, w2)


def _combine_kernel(code, pstart, yb_hbm, x_ref, g2, wt_ref, o_ref, buf, sems, *, n_tok):
    i = pl.program_id(0)
    n = pl.num_programs(0)
    slot = i % 2
    R = x_ref.shape[0]

    def gather(tile, slot, wait):
        for k in range(TOP_K):
            def body(r, carry):
                src = _slot_of(code, pstart, k * n_tok + tile * R + r)
                cp = pltpu.make_async_copy(yb_hbm.at[pl.ds(src, 1), :],
                                           buf.at[slot, k, pl.ds(r, 1), :], sems.at[slot])
                cp.wait() if wait else cp.start()
                return carry
            lax.fori_loop(0, R, body, 0, unroll=DMA_UNROLL)

    @pl.when(i == 0)
    def _():
        gather(0, 0, False)

    @pl.when(i + 1 < n)
    def _():
        gather(i + 1, 1 - slot, False)

    gather(i, slot, True)
    w = wt_ref[...]
    acc = w[:, 0:1] * buf[slot, 0]
    for k in range(1, TOP_K):
        acc = acc + w[:, k:k + 1] * buf[slot, k]
    o_ref[...] = x_ref[...] + g2[...] * acc


def _moe_combine(yb, code, pstart, wts, X, mod, gate_idx, *, n_rows, B, L):
    D = X.shape[1]
    tm = MOE_BLOCK
    nlt, tps = (B * L) // tm, L // tm
    grp = functools.partial(_group_of_tile, n_lat_tiles=nlt, tiles_per_seq=tps, n_batch=B)
    return pl.pallas_call(
        functools.partial(_combine_kernel, n_tok=n_rows),
        out_shape=jax.ShapeDtypeStruct((n_rows, D), F32),
        grid_spec=pltpu.PrefetchScalarGridSpec(
            num_scalar_prefetch=2, grid=(n_rows // tm,),
            in_specs=[pl.BlockSpec(memory_space=pl.ANY),
                      pl.BlockSpec((tm, D), lambda i, cd, ps: (i, 0)),
                      pl.BlockSpec((None, None, 1, D), lambda i, cd, ps: (grp(i), gate_idx, 0, 0)),
                      pl.BlockSpec((tm, LANES), lambda i, cd, ps: (i, 0))],
            out_specs=pl.BlockSpec((tm, D), lambda i, cd, ps: (i, 0)),
            scratch_shapes=[pltpu.VMEM((2, TOP_K, tm, D), F32), pltpu.SemaphoreType.DMA((2,))]),
        compiler_params=_cparams(("arbitrary",)), name="moe_combine",
    )(code, pstart, yb, X, mod, wts)


def _dispatch_tables(code, counts, n_tok):
    E, K = N_EXPERTS, TOP_K
    cnt = counts[0, :E].astype(jnp.int32)
    padded = (cnt + MOE_BLOCK - 1) // MOE_BLOCK * MOE_BLOCK
    pend = jnp.cumsum(padded)
    pstart = (pend - padded).astype(jnp.int32)
    P = -(-(n_tok * K + E * (MOE_BLOCK - 1)) // MOE_BLOCK) * MOE_BLOCK
    blk_row = jnp.arange(P // MOE_BLOCK, dtype=jnp.int32) * MOE_BLOCK
    blk_e = jnp.minimum(jnp.sum(pend[None, :] <= blk_row[:, None], axis=1), E - 1).astype(jnp.int32)
    return code[:, :K].T.reshape(-1), pstart, P, blk_e


def kernel(x, c, ctx, c_ctx, mod_w, mod_b, norm1_g, norm2_g, w_in, merge_b, m_conv_w, m_conv_b, m_gate_b, m_norm_g, a_qnorm_g, a_knorm_g, a_lambda, a_subln_g, h_conv_w, h_conv_b, h_ffn_w1, h_ffn_b1, h_ffn_w2, h_ffn_b2, h_ffn_w3, h_freq, h_decay, h_bias, w_br_m, w_br_a, w_br_h, w_out, router_w, router_b, e_w1, e_w3, e_w2):
    B, L, D = x.shape
    Lc = ctx.shape[1]
    depth = mod_w.shape[0]
    NL, NC = B * L, B * Lc
    T = NL + NC
    assert B + 1 <= 8 and Lc == ROW_TILE and L % min(L, 1024) == 0 and L % 512 == 0
    assert NL % Lc == 0 and D % LANES == 0
    width = D
    a_dh = D // (2 * A_HEADS)
    n_gates = 4 * M_HEADS
    c_mq, c_mv, c_mo = 0, 2 * width, 3 * width
    c_aq, c_ak, c_av = 4 * width, 5 * width, 6 * width
    c_hx = 7 * width
    c_gp = 10 * width

    X = jnp.concatenate([x.reshape(NL, D), ctx.reshape(NC, D)], axis=0)
    c8 = jnp.zeros((8, D), F32).at[:B].set(c).at[B].set(c_ctx)
    cos_tab, sin_tab = _rope_tables(L, a_dh, ROW_TILE)
    fft_consts = _dft_consts(L)

    for l in range(depth):
        need_ctx = l < depth - 1
        lam_init = 0.8 - 0.6 * math.exp(-0.3 * l)
        mod = _modulation(c8, mod_w[l], mod_b[l])[:B + 1].reshape(B + 1, 6, 1, D)

        h = _norm_mod(X, norm1_g[l], mod, 0, 1, n_rows=T, L=L, B=B)
        g0 = 4 * width
        w_main = jnp.concatenate([w_in[l][:, :g0], w_in[l][:, g0 + n_gates:]], axis=1).astype(BF16)
        w_gate = jnp.pad(w_in[l][:, g0:g0 + n_gates], ((0, 0), (0, LANES - n_gates))).astype(BF16)
        P = _matmul(h, w_main, BF16, 512, 1024)
        G = _matmul(h, w_gate, F32, 512, LANES)

        QK = _mlstm_qk(P, m_conv_w[l], m_conv_b[l], L=L, Lc=Lc, NL=NL, width=width)
        ym, ymc = _mlstm(QK, P, G, m_gate_b[l], m_norm_g[l], B=B, L=L, Lc=Lc, width=width)

        Qh = _attn_prep(P, c_aq, a_qnorm_g[l], cos_tab, sin_tab, a_dh ** -0.5 * LOG2E, L=L, NL=NL, dh=a_dh)
        Kh = _attn_prep(P, c_ak, a_knorm_g[l], cos_tab, sin_tab, 1.0, L=L, NL=NL, dh=a_dh)
        attn = functools.partial(_attention, Qh, Kh, P, c_av, a_lambda[l], a_subln_g[l], lam_init,
                                 B=B, L=L, Lc=Lc, dh=a_dh)
        ya = attn(latent=True)

        x0, z = _hyena_pre(P, c_hx, h_conv_w[l], h_conv_b[l], L=L, Lc=Lc, NL=NL, C=width)
        filt = (h_ffn_w1[l], h_ffn_b1[l], h_ffn_w2[l], h_ffn_b2[l], h_ffn_w3[l], h_freq[l], h_decay[l])
        taps, ssq = _hyena_filters(L, *filt, width)
        yh = _hyena_latent(x0, z, taps, ssq, h_bias[l], fft_consts, B=B, L=L, C=width)

        weights = [w.astype(BF16) for w in (w_br_m[l], w_br_a[l], w_br_h[l], w_out[l])]
        if need_ctx:
            yac = attn(latent=False)
            taps_c, ssq_c = _hyena_filters(Lc, *filt, width)
            yhc = _hyena_ctx(x0, z, taps_c, ssq_c, h_bias[l], B=B, L=L, Lc=Lc, C=width)
            branches, n_rows = [ym, ya, yh, ymc, yac, yhc], T
        else:
            branches, n_rows = [ym, ya, yh], NL
        X = _merge(X, P, c_gp, branches, merge_b[l], weights, mod, 2,
                   n_rows=n_rows, B=B, L=L, with_ctx=need_ctx)

        tok, code, wts, counts = _norm_mod(X, norm2_g[l], mod, 3, 4, n_rows=n_rows, L=L, B=B,
                                           router=(router_w, router_b))
        code, pstart, n_slots, blk_e = _dispatch_tables(code, counts, n_rows)
        xs = _moe_dispatch(tok, code, pstart, n_slots)
        yb = _moe_experts(xs, blk_e, e_w1[l], e_w3[l], e_w2[l])
        X = _moe_combine(yb, code, pstart, wts, X, mod, 5, n_rows=n_rows, B=B, L=L)
    return X[:NL].reshape(B, L, D)
```

```python
import functools
import math

import numpy as np
import jax
import jax.numpy as jnp
from jax import lax
from jax.experimental import pallas as pl
from jax.experimental.pallas import tpu as pltpu

F32 = jnp.float32
BF16 = jnp.bfloat16

GRID_W = 64
EPS = 1e-6
M_HEADS = 4
M_CHUNK = 128
A_HEADS = 8
ROPE_BASE = 10000.0
H_EMB = 33
H_FFN = 64
SHORT_CONV = 3
N_EXPERTS = 32
N_GROUPS = 4
TOP_K = 2
MOE_BLOCK = 128

LANES = 128
BF16_SUBLANES = 16
V7X_VMEM_BYTES = 64 * 1024 * 1024
VMEM_LIMIT = V7X_VMEM_BYTES * 7 // 8

ROW_TILE = 256
LOG2E = 1.4426950408889634


def _cparams(sem):
    return pltpu.CompilerParams(dimension_semantics=sem, vmem_limit_bytes=VMEM_LIMIT)


def _split(x):
    hi = x.astype(BF16)
    lo = (x - hi.astype(F32)).astype(BF16)
    return hi, lo


def _dot(a, b):
    return jnp.dot(a, b, preferred_element_type=F32)


def _dot3(a, b):
    ah, al = _split(a)
    bh, bl = _split(b)
    return _dot(ah, bh) + _dot(al, bh) + _dot(ah, bl)


def _dot3c(ch, cl, x):
    xh, xl = _split(x)
    return _dot(ch, xh) + _dot(cl, xh) + _dot(ch, xl)


def _np_split(a):
    a = jnp.asarray(np.asarray(a, np.float32))
    hi = a.astype(BF16)
    lo = (a - hi.astype(F32)).astype(BF16)
    return hi, lo


def _sigmoid(x):
    return 1.0 / (1.0 + jnp.exp(-x))


def _log_sigmoid(x):
    return jnp.minimum(x, 0.0) - jnp.log(1.0 + jnp.exp(-jnp.abs(x)))


def _mod_kernel(c_ref, w_ref, b_ref, o_ref):
    c = c_ref[...]
    o_ref[...] = _dot3(c * _sigmoid(c), w_ref[...]) + b_ref[...]


def _modulation(c8, w, b):
    D, N = w.shape
    tn = 1536
    return pl.pallas_call(
        _mod_kernel,
        out_shape=jax.ShapeDtypeStruct((8, N), F32),
        grid=(N // tn,),
        in_specs=[pl.BlockSpec((8, D), lambda j: (0, 0)),
                  pl.BlockSpec((D, tn), lambda j: (0, j)),
                  pl.BlockSpec((1, tn), lambda j: (0, j))],
        out_specs=pl.BlockSpec((8, tn), lambda j: (0, j)),
        compiler_params=_cparams(("parallel",)),
        name="modulation",
    )(c8, w, b.reshape(1, N))


def _norm_mod_kernel(x_ref, g_ref, sh_ref, sc_ref, o_ref):
    x = x_ref[...]
    y = x * lax.rsqrt(jnp.mean(x * x, axis=-1, keepdims=True) + EPS) * g_ref[...]
    o_ref[...] = (y * (1.0 + sc_ref[...]) + sh_ref[...]).astype(o_ref.dtype)


def _norm_mod_router_kernel(x_ref, g_ref, sh_ref, sc_ref, rw_ref, rb_ref, o_ref, code_ref, wt_ref, cnt_ref, carry):
    E, G = N_EXPERTS, N_GROUPS
    gs = E // G

    @pl.when(pl.program_id(0) == 0)
    def _():
        carry[...] = jnp.zeros_like(carry)

    x = x_ref[...]
    y = x * lax.rsqrt(jnp.mean(x * x, axis=-1, keepdims=True) + EPS) * g_ref[...]
    h = y * (1.0 + sc_ref[...]) + sh_ref[...]
    o_ref[...] = h
    tm = h.shape[0]
    s = _sigmoid(_dot3(h, rw_ref[...]))
    lane = lax.broadcasted_iota(jnp.int32, (1, LANES), 1)
    lane_f = lane.astype(F32)
    sb = jnp.where(lane < E, s + rb_ref[...], -jnp.inf)
    far = float(LANES)
    best = jnp.full((tm, 1), -jnp.inf, F32)
    e1 = jnp.zeros((tm, 1), F32)
    e2 = jnp.zeros((tm, 1), F32)
    for g in range(G):
        mg = jnp.where((lane >= g * gs) & (lane < (g + 1) * gs), sb, -jnp.inf)
        m1 = jnp.max(mg, axis=-1, keepdims=True)
        i1 = jnp.min(jnp.where(mg == m1, lane_f, far), axis=-1, keepdims=True)
        mg2 = jnp.where(lane_f == i1, -jnp.inf, mg)
        m2 = jnp.max(mg2, axis=-1, keepdims=True)
        i2 = jnp.min(jnp.where(mg2 == m2, lane_f, far), axis=-1, keepdims=True)
        score = m1 + m2
        take = score > best
        best = jnp.where(take, score, best)
        e1 = jnp.where(take, i1, e1)
        e2 = jnp.where(take, i2, e2)
    oh1 = lane_f == e1
    oh2 = lane_f == e2
    s1 = jnp.sum(jnp.where(oh1, s, 0.0), axis=-1, keepdims=True)
    s2 = jnp.sum(jnp.where(oh2, s, 0.0), axis=-1, keepdims=True)
    den = s1 + s2
    r = lax.broadcasted_iota(jnp.int32, (tm, tm), 0)
    c = lax.broadcasted_iota(jnp.int32, (tm, tm), 1)
    lower = (c < r).astype(BF16)
    o1 = oh1.astype(F32)
    o2 = oh2.astype(F32)
    cum1 = _dot(lower, o1.astype(BF16))
    cum2 = _dot(lower, o2.astype(BF16))
    tot1 = jnp.sum(o1, axis=0, keepdims=True)
    base = carry[...]
    rank1 = jnp.sum(jnp.where(oh1, base + cum1, 0.0), axis=-1, keepdims=True)
    rank2 = jnp.sum(jnp.where(oh2, base + tot1 + cum2, 0.0), axis=-1, keepdims=True)
    total = base + tot1 + jnp.sum(o2, axis=0, keepdims=True)
    carry[...] = total
    cnt_ref[...] = total
    code1 = (rank1 * E + e1).astype(jnp.int32)
    code2 = (rank2 * E + e2).astype(jnp.int32)
    code_ref[...] = jnp.where(lane == 0, code1, jnp.where(lane == 1, code2, 0))
    wt_ref[...] = jnp.where(lane == 0, s1 / den, jnp.where(lane == 1, s2 / den, 0.0))


def _group_of_tile(i, n_lat_tiles, tiles_per_seq, n_batch):
    return jnp.where(i < n_lat_tiles, i // tiles_per_seq, n_batch)


def _norm_mod(x, g, mod, shift_idx, scale_idx, *, n_rows, L, B, router=None):
    D = x.shape[1]
    tm = ROW_TILE
    nlt, tps = (B * L) // tm, L // tm
    grp = functools.partial(_group_of_tile, n_lat_tiles=nlt, tiles_per_seq=tps, n_batch=B)
    in_specs = [pl.BlockSpec((tm, D), lambda i: (i, 0)),
                pl.BlockSpec((1, D), lambda i: (0, 0)),
                pl.BlockSpec((None, None, 1, D), lambda i: (grp(i), shift_idx, 0, 0)),
                pl.BlockSpec((None, None, 1, D), lambda i: (grp(i), scale_idx, 0, 0))]
    args = [x, g.reshape(1, D), mod, mod]
    if router is None:
        return pl.pallas_call(
            _norm_mod_kernel,
            out_shape=jax.ShapeDtypeStruct((n_rows, D), BF16),
            grid=(n_rows // tm,), in_specs=in_specs,
            out_specs=pl.BlockSpec((tm, D), lambda i: (i, 0)),
            compiler_params=_cparams(("parallel",)), name="norm_mod",
        )(*args)
    router_w, router_b = router
    E = router_w.shape[1]
    assert E == N_EXPERTS
    rw = jnp.pad(router_w, ((0, 0), (0, LANES - E)))
    rb = jnp.pad(router_b.astype(F32).reshape(1, E), ((0, 0), (0, LANES - E)))
    lanes = pl.BlockSpec((tm, LANES), lambda i: (i, 0))
    return pl.pallas_call(
        _norm_mod_router_kernel,
        out_shape=(jax.ShapeDtypeStruct((n_rows, D), F32),
                   jax.ShapeDtypeStruct((n_rows, LANES), jnp.int32),
                   jax.ShapeDtypeStruct((n_rows, LANES), F32),
                   jax.ShapeDtypeStruct((1, LANES), F32)),
        grid=(n_rows // tm,),
        in_specs=in_specs + [pl.BlockSpec((D, LANES), lambda i: (0, 0)),
                             pl.BlockSpec((1, LANES), lambda i: (0, 0))],
        out_specs=(pl.BlockSpec((tm, D), lambda i: (i, 0)), lanes, lanes,
                   pl.BlockSpec((1, LANES), lambda i: (0, 0))),
        scratch_shapes=[pltpu.VMEM((1, LANES), F32)],
        compiler_params=_cparams(("arbitrary",)), name="norm_mod_router",
    )(*args, rw, rb)


def _mm_kernel(a_ref, w_ref, o_ref):
    o_ref[...] = _dot(a_ref[...], w_ref[...]).astype(o_ref.dtype)


def _largest_tile(n, cap, step):
    return max(t for t in range(step, cap + 1, step) if n % t == 0)


def _matmul(a, w, out_dtype, tm, tn):
    M, K = a.shape
    N = w.shape[1]
    return pl.pallas_call(
        _mm_kernel,
        out_shape=jax.ShapeDtypeStruct((M, N), out_dtype),
        grid=(M // tm, N // tn),
        in_specs=[pl.BlockSpec((tm, K), lambda i, j: (i, 0)),
                  pl.BlockSpec((K, tn), lambda i, j: (0, j))],
        out_specs=pl.BlockSpec((tm, tn), lambda i, j: (i, j)),
        compiler_params=_cparams(("parallel", "parallel")), name="matmul",
    )(a, w)


def _seq_edge_flags(tm, L, Lc, NL):
    r0 = pl.program_id(0) * tm
    lat = r0 < NL
    start = jnp.where(lat, r0 % L == 0, (r0 - NL) % Lc == 0)
    end = jnp.where(lat, (r0 + tm) % L == 0, (r0 + tm - NL) % Lc == 0)
    return jnp.where(start, 0.0, 1.0), jnp.where(end, 0.0, 1.0)


def _conv3(cur_ref, prev_ref, next_ref, w_ref, b_ref, keep_prev, keep_next):
    cur = cur_ref[...].astype(F32)
    tm = cur.shape[0]
    prev_row = prev_ref[BF16_SUBLANES - 1:BF16_SUBLANES, :].astype(F32) * keep_prev
    next_row = next_ref[0:1, :].astype(F32) * keep_next
    row = lax.broadcasted_iota(jnp.int32, (tm, 1), 0)
    up = jnp.where(row == 0, prev_row, pltpu.roll(cur, 1, 0))
    dn = jnp.where(row == tm - 1, next_row, pltpu.roll(cur, tm - 1, 0))
    return b_ref[...] + up * w_ref[0:1, :] + cur * w_ref[1:2, :] + dn * w_ref[2:3, :]


def _conv_specs(tm, tc, T, col_block):
    per = tm // BF16_SUBLANES
    last = T // BF16_SUBLANES - 1
    return [pl.BlockSpec((tm, tc), lambda i, j: (i, col_block + j)),
            pl.BlockSpec((BF16_SUBLANES, tc), lambda i, j: (jnp.maximum(i * per - 1, 0), col_block + j)),
            pl.BlockSpec((BF16_SUBLANES, tc), lambda i, j: (jnp.minimum((i + 1) * per, last), col_block + j))]


def _mconv_kernel(cur_ref, prev_ref, next_ref, w_ref, b_ref, scale_ref, o_ref, *, tm, L, Lc, NL):
    kp, kn = _seq_edge_flags(tm, L, Lc, NL)
    y = _conv3(cur_ref, prev_ref, next_ref, w_ref, b_ref, kp, kn)
    o_ref[...] = (y * _sigmoid(y) * scale_ref[...]).astype(o_ref.dtype)


def _mlstm_qk(P, conv_w, conv_b, *, L, Lc, NL, width):
    T = P.shape[0]
    tm, tc = ROW_TILE, 1024
    C = 2 * width
    scale = jnp.concatenate([jnp.ones((1, width), F32),
                             jnp.full((1, width), (width // M_HEADS) ** -0.5, F32)], axis=1)
    vec = lambda r: pl.BlockSpec((r, tc), lambda i, j: (0, j))
    return pl.pallas_call(
        functools.partial(_mconv_kernel, tm=tm, L=L, Lc=Lc, NL=NL),
        out_shape=jax.ShapeDtypeStruct((T, C), BF16),
        grid=(T // tm, C // tc),
        in_specs=_conv_specs(tm, tc, T, 0) + [vec(SHORT_CONV), vec(1), vec(1)],
        out_specs=pl.BlockSpec((tm, tc), lambda i, j: (i, j)),
        compiler_params=_cparams(("parallel", "parallel")), name="mlstm_qk_conv",
    )(P, P, P, conv_w, conv_b.reshape(1, C), scale)


def _hyena_pre_kernel(c0, p0, n0, c1, p1, n1, c2, p2, n2, w0, w1, w2, b0, b1, b2,
                      x0_ref, z_ref, *, tm, L, Lc, NL):
    kp, kn = _seq_edge_flags(tm, L, Lc, NL)
    x0 = _conv3(c0, p0, n0, w0, b0, kp, kn)
    x1 = _conv3(c1, p1, n1, w1, b1, kp, kn)
    v = _conv3(c2, p2, n2, w2, b2, kp, kn)
    x0_ref[...] = x0
    z_ref[...] = v * x1


def _hyena_pre(P, col0, conv_w, conv_b, *, L, Lc, NL, C):
    T = P.shape[0]
    tm, tc = ROW_TILE, 512
    nb = C // tc
    specs = []
    for part in range(3):
        specs += _conv_specs(tm, tc, T, col0 // tc + part * nb)
    wspecs = [pl.BlockSpec((SHORT_CONV, tc), lambda i, j, p=part: (0, p * nb + j)) for part in range(3)]
    bspecs = [pl.BlockSpec((1, tc), lambda i, j, p=part: (0, p * nb + j)) for part in range(3)]
    b2d = conv_b.reshape(1, 3 * C)
    out = pl.BlockSpec((tm, tc), lambda i, j: (i, j))
    return pl.pallas_call(
        functools.partial(_hyena_pre_kernel, tm=tm, L=L, Lc=Lc, NL=NL),
        out_shape=(jax.ShapeDtypeStruct((T, C), F32), jax.ShapeDtypeStruct((T, C), F32)),
        grid=(T // tm, nb),
        in_specs=specs + wspecs + bspecs,
        out_specs=(out, out),
        compiler_params=_cparams(("parallel", "parallel")), name="hyena_pre",
    )(*([P] * 9), conv_w, conv_w, conv_w, b2d, b2d, b2d)


def _mlstm_chunk(q, k, v, li_r, lf_r, li_c, lf_c, C_scr, n_scr, m_scr, fwd):
    Q = M_CHUNK
    row = lax.broadcasted_iota(jnp.int32, (Q, Q), 0)
    col = lax.broadcasted_iota(jnp.int32, (Q, Q), 1)
    mask = (col <= row) if fwd else (col >= row)
    tri_c = mask.astype(BF16)
    tri_r = ((row <= col) if fwd else (row >= col)).astype(BF16)
    lfc_h, lfc_l = _split(jnp.broadcast_to(lf_c, (Q, Q)))
    lfr_h, lfr_l = _split(jnp.broadcast_to(lf_r, (Q, Q)))
    b_cols = _dot(tri_c, lfc_h) + _dot(tri_c, lfc_l)
    b_rows = _dot(lfr_h, tri_r) + _dot(lfr_l, tri_r)
    dm = jnp.where(mask, b_cols - b_rows + li_r, -jnp.inf)
    m_prev = m_scr[0:1, 0:1]
    b_col = b_cols[:, 0:1]
    inter = b_col + m_prev
    mt = jnp.maximum(inter, jnp.max(dm, axis=-1, keepdims=True))
    s = lax.dot_general(q, k, (((1,), (1,)), ((), ())), preferred_element_type=F32) * jnp.exp(dm - mt)
    wi = jnp.exp(inter - mt)
    num = _dot(s.astype(BF16), v) + wi * _dot(q, C_scr[...].astype(BF16))
    qn = jnp.sum(q.astype(F32) * n_scr[...], axis=-1, keepdims=True)
    den = jnp.sum(s, axis=-1, keepdims=True) + wi * qn
    h = num / jnp.maximum(jnp.abs(den), jnp.exp(-mt))
    b_tot = b_cols[Q - 1:Q, 0:1] if fwd else b_cols[0:1, 0:1]
    ws = b_tot - b_col + li_c
    m_new = jnp.maximum(b_tot + m_prev, jnp.max(ws, axis=0, keepdims=True))
    decay = jnp.exp(b_tot + m_prev - m_new)
    kw = k.astype(F32) * jnp.exp(ws - m_new)
    C_scr[...] = decay * C_scr[...] + lax.dot_general(
        kw.astype(BF16), v, (((0,), (0,)), ((), ())), preferred_element_type=F32)
    n_scr[...] = decay * n_scr[...] + jnp.sum(kw, axis=0, keepdims=True)
    m_scr[...] = jnp.broadcast_to(m_new, m_scr.shape)
    return h


def _mlstm_kernel(*refs, fwd, SEG, Lc, H, dh):
    if fwd:
        (gr_ref, gc_ref, grc_ref, gcc_ref, q_ref, k_ref, v_ref, qc_ref, kc_ref, vc_ref,
         h_ref, hc_ref, *scr) = refs
    else:
        (gr_ref, gc_ref, grc_ref, gcc_ref, q_ref, k_ref, v_ref, qc_ref, kc_ref, vc_ref,
         o_ref, oc_ref, hf_ref, hfc_ref, ng_ref, y_ref, yc_ref, *scr) = refs
    C_scr, n_scr, m_scr = scr[0:H], scr[H:2 * H], scr[2 * H:3 * H]
    gi = 0 if fwd else 2

    def run(nchunks, qr, kr, vr, grr, gcr, emit):
        def body(j, carry):
            c = j if fwd else nchunks - 1 - j
            rows = pl.ds(pl.multiple_of(c * M_CHUNK, M_CHUNK), M_CHUNK)
            for hh in range(H):
                cols = slice(hh * dh, (hh + 1) * dh)
                gcs = gcr[hh, rows, :]
                h = _mlstm_chunk(qr[rows, cols], kr[rows, cols], vr[rows, cols],
                                 grr[hh, gi, pl.ds(c, 1), :], _log_sigmoid(grr[hh, gi + 1, pl.ds(c, 1), :]),
                                 gcs[:, gi:gi + 1], _log_sigmoid(gcs[:, gi + 1:gi + 2]),
                                 C_scr[hh], n_scr[hh], m_scr[hh], fwd)
                emit(rows, cols, h)
            return carry

        lax.fori_loop(0, nchunks, body, 0)

    def emitter(dst, hf=None, o=None):
        def emit(rows, cols, h):
            if not fwd:
                h = h + hf[rows, cols]
                y = h * lax.rsqrt(jnp.mean(h * h, axis=-1, keepdims=True) + EPS) * ng_ref[:, cols]
                h = (y * _sigmoid(o[rows, cols].astype(F32))).astype(dst.dtype)
            dst[rows, cols] = h
        return emit

    @pl.when(pl.program_id(1) == 0)
    def _():
        for r in scr:
            r[...] = jnp.zeros_like(r)
        run(Lc // M_CHUNK, qc_ref, kc_ref, vc_ref, grc_ref, gcc_ref,
            emitter(hc_ref) if fwd else emitter(yc_ref, hfc_ref, oc_ref))

    run(SEG // M_CHUNK, q_ref, k_ref, v_ref, gr_ref, gc_ref,
        emitter(h_ref) if fwd else emitter(y_ref, hf_ref, o_ref))


def _mlstm(QK, P, G, gate_b, norm_g, *, B, L, Lc, width):
    H = M_HEADS
    dh = width // H
    NL = B * L
    SEG = min(L, 1024)
    S = L // SEG
    g = G[:, :4 * H] + gate_b.astype(F32)[None, :]

    def gate_views(rows, n):
        a = rows.reshape(B, n, 4, H)
        return (a.transpose(0, 3, 2, 1).reshape(B, H, 4, n // M_CHUNK, M_CHUNK),
                a.transpose(0, 3, 1, 2))

    gr, gc = gate_views(g[:NL], L)
    grc, gcc = gate_views(g[NL:], Lc)
    scratch = ([pltpu.VMEM((dh, dh), F32)] * H + [pltpu.VMEM((1, dh), F32)] * H
               + [pltpu.VMEM((8, LANES), F32)] * H)

    def call(fwd, extra_in, extra_specs, out_dtype):
        seg_of = (lambda s: s) if fwd else (lambda s: S - 1 - s)
        lat = lambda cb: pl.BlockSpec((SEG, width), lambda b, s: (b * S + seg_of(s), cb))
        ctx = lambda cb: pl.BlockSpec((Lc, width), lambda b, s: (NL // Lc + b, cb))
        gate_specs = [
            pl.BlockSpec((None, H, 4, SEG // M_CHUNK, M_CHUNK), lambda b, s: (b, 0, 0, seg_of(s), 0)),
            pl.BlockSpec((None, H, SEG, 4), lambda b, s: (b, 0, seg_of(s), 0)),
            pl.BlockSpec((None, H, 4, Lc // M_CHUNK, M_CHUNK), lambda b, s: (b, 0, 0, 0, 0)),
            pl.BlockSpec((None, H, Lc, 4), lambda b, s: (b, 0, 0, 0))]
        lat_out = pl.BlockSpec((SEG, width), lambda b, s: (b * S + seg_of(s), 0))
        ctx_out = pl.BlockSpec((Lc, width), lambda b, s: (b, 0))
        return pl.pallas_call(
            functools.partial(_mlstm_kernel, fwd=fwd, SEG=SEG, Lc=Lc, H=H, dh=dh),
            out_shape=(jax.ShapeDtypeStruct((NL, width), out_dtype),
                       jax.ShapeDtypeStruct((B * Lc, width), out_dtype)),
            grid=(B, S),
            in_specs=(gate_specs + [lat(0), lat(1), lat(2), ctx(0), ctx(1), ctx(2)]
                      + extra_specs(lat, ctx, lat_out, ctx_out)),
            out_specs=(lat_out, ctx_out), scratch_shapes=scratch,
            compiler_params=_cparams(("parallel", "arbitrary")),
            name="mlstm_fwd" if fwd else "mlstm_bwd",
        )(gr, gc, grc, gcc, QK, QK, P, QK, QK, P, *extra_in)

    hf, hfc = call(True, [], lambda *_: [], F32)
    return call(False, [P, P, hf, hfc, jnp.tile(norm_g.astype(F32), H).reshape(1, width)],
                lambda lat, ctx, lat_out, ctx_out: [lat(3), ctx(3), lat_out, ctx_out,
                                                    pl.BlockSpec((1, width), lambda b, s: (0, 0))], BF16)


def _attn_prep_kernel(x_ref, g_ref, cos_ref, sin_ref, o_ref, *, scale, dh):
    n_blk = x_ref.shape[1] // LANES
    r = lax.broadcasted_iota(jnp.int32, (LANES, LANES), 0)
    c = lax.broadcasted_iota(jnp.int32, (LANES, LANES), 1)
    group = (r // dh == c // dh).astype(BF16)
    lane = lax.broadcasted_iota(jnp.int32, (1, LANES), 1)
    quarter = dh // 4
    first = (lane % (2 * quarter)) < quarter
    cos = cos_ref[...]
    sin = sin_ref[...]
    for hb in range(n_blk):
        cols = slice(hb * LANES, (hb + 1) * LANES)
        x = x_ref[:, cols].astype(F32)
        hi, lo = _split(x * x)
        ms = (_dot(hi, group) + _dot(lo, group)) * (1.0 / dh)
        y = x * lax.rsqrt(ms + EPS) * g_ref[:, cols]
        rot = jnp.where(first, -pltpu.roll(y, LANES - quarter, 1), pltpu.roll(y, quarter, 1))
        o_ref[:, cols] = ((y * cos + rot * sin) * scale).astype(o_ref.dtype)


def _attn_prep(P, col0, gain, cos_tab, sin_tab, scale, *, L, NL, dh):
    T = P.shape[0]
    W = A_HEADS * 2 * dh
    tm = ROW_TILE
    nlt, tps = NL // tm, L // tm
    tab = pl.BlockSpec((tm, LANES), lambda i: (jnp.where(i < nlt, i % tps, tps), 0))
    g = jnp.tile(gain.astype(F32), W // dh).reshape(1, W)
    return pl.pallas_call(
        functools.partial(_attn_prep_kernel, scale=scale, dh=dh),
        out_shape=jax.ShapeDtypeStruct((T, W), BF16),
        grid=(T // tm,),
        in_specs=[pl.BlockSpec((tm, W), lambda i: (i, col0 // W)),
                  pl.BlockSpec((1, W), lambda i: (0, 0)), tab, tab],
        out_specs=pl.BlockSpec((tm, W), lambda i: (i, 0)),
        compiler_params=_cparams(("parallel",)), name="attn_prep",
    )(P, g, cos_tab, sin_tab)


def _attn_kernel(*refs, n_lat, tk, L, Lc, dh, lam_init):
    if n_lat:
        lam_ref, q_ref, kl_ref, vl_ref, kc_ref, vc_ref, sg_ref, o_ref, vext, acc = refs
    else:
        lam_ref, q_ref, kc_ref, vc_ref, sg_ref, o_ref, vext, acc = refs
    dv = 2 * dh
    ctx0 = n_lat * tk

    @pl.when(pl.program_id(2) == 0)
    def _():
        if n_lat:
            vext[0:L, 0:dv] = vl_ref[...]
        vext[ctx0:ctx0 + Lc, 0:dv] = vc_ref[...]
        vext[:, dv:2 * dv] = jnp.ones((vext.shape[0], dv), BF16)

    q = q_ref[...]
    lane = lax.broadcasted_iota(jnp.int32, (1, dv), 1)
    qs = (jnp.where(lane < dh, q, jnp.zeros_like(q)), jnp.where(lane >= dh, q, jnp.zeros_like(q)))
    acc[...] = jnp.zeros_like(acc)
    tq = q.shape[0]

    def update(comp, kblk, vblk, m_old):
        s = lax.dot_general(qs[comp], kblk, (((1,), (1,)), ((), ())), preferred_element_type=F32)
        m_new = jnp.maximum(m_old, jnp.max(s, axis=-1, keepdims=True))
        p = jnp.exp2(s - m_new).astype(BF16)
        acc[comp] = jnp.exp2(m_old - m_new) * acc[comp] + _dot(p, vblk)
        return m_new

    m = (jnp.full((tq, 1), -jnp.inf, F32),) * 2
    for c in range(n_lat):
        kblk, vblk = kl_ref[c * tk:(c + 1) * tk, :], vext[c * tk:(c + 1) * tk, :]
        m = update(0, kblk, vblk, m[0]), update(1, kblk, vblk, m[1])
    kblk, vblk = kc_ref[...], vext[ctx0:ctx0 + Lc, :]
    update(0, kblk, vblk, m[0])
    update(1, kblk, vblk, m[1])

    lp = lam_ref[...]
    lam = (jnp.exp(jnp.sum(lp[0:1] * lp[1:2], axis=-1, keepdims=True))
           - jnp.exp(jnp.sum(lp[2:3] * lp[3:4], axis=-1, keepdims=True)) + lam_init)
    a0, a1 = acc[0], acc[1]
    o = a0[:, 0:dv] / a0[:, dv:dv + 1] - lam * (a1[:, 0:dv] / a1[:, dv:dv + 1])
    y = o * lax.rsqrt(jnp.mean(o * o, axis=-1, keepdims=True) + EPS) * sg_ref[...] * (1.0 - lam_init)
    o_ref[...] = y.astype(o_ref.dtype)


def _attention(Qh, Kh, P, vcol0, lam_p, sub_g, lam_init, *, B, L, Lc, dh, latent):
    NL = B * L
    dv = 2 * dh
    H = A_HEADS
    vb = vcol0 // dv
    if latent:
        tq, tk = 512, min(L, 1024)
        n_lat, nq, rows_out = L // tk, L // tq, NL
        q_spec = pl.BlockSpec((tq, dv), lambda b, h, i: (b * nq + i, h))
        lat_specs = [pl.BlockSpec((L, dv), lambda b, h, i: (b, h)),
                     pl.BlockSpec((L, dv), lambda b, h, i: (b, vb + h))]
        lat_args = [Kh, P]
        o_spec = pl.BlockSpec((tq, dv), lambda b, h, i: (b * nq + i, h))
        nkeys = L + Lc
    else:
        tq, tk = Lc, 512
        n_lat, nq, rows_out = 0, 1, B * Lc
        q_spec = pl.BlockSpec((Lc, dv), lambda b, h, i: (NL // Lc + b, h))
        lat_specs, lat_args = [], []
        o_spec = pl.BlockSpec((Lc, dv), lambda b, h, i: (b, h))
        nkeys = Lc
    ctx_specs = [pl.BlockSpec((Lc, dv), lambda b, h, i: (NL // Lc + b, h)),
                 pl.BlockSpec((Lc, dv), lambda b, h, i: (NL // Lc + b, vb + h))]
    return pl.pallas_call(
        functools.partial(_attn_kernel, n_lat=n_lat, tk=tk, L=L, Lc=Lc, dh=dh, lam_init=lam_init),
        out_shape=jax.ShapeDtypeStruct((rows_out, H * dv), BF16),
        grid=(B, H, nq),
        in_specs=[pl.BlockSpec((4, dh), lambda b, h, i: (0, 0)), q_spec] + lat_specs + ctx_specs
                 + [pl.BlockSpec((1, dv), lambda b, h, i: (0, 0))],
        out_specs=o_spec,
        scratch_shapes=[pltpu.VMEM((nkeys, 2 * dv), BF16), pltpu.VMEM((2, tq, 2 * dv), F32)],
        compiler_params=_cparams(("parallel", "parallel", "arbitrary")),
        name="diff_attn_latent" if latent else "diff_attn_ctx",
    )(lam_p.astype(F32), Qh, *lat_args, Kh, P, sub_g.reshape(1, dv).astype(F32))


def _rope_tables(L, dh, tm):
    rows = L // GRID_W
    row = np.repeat(np.arange(rows), GRID_W).astype(np.float64)
    col = np.tile(np.arange(GRID_W), rows).astype(np.float64)
    nf = dh // 4
    inv = (np.float32(ROPE_BASE) ** (-np.arange(nf, dtype=np.float32) / nf)).astype(np.float64)
    ang = np.concatenate([row[:, None] * inv] * 2 + [col[:, None] * inv] * 2, axis=-1)
    ang = np.tile(ang.astype(np.float32).astype(np.float64), (1, LANES // dh))
    cos = np.concatenate([np.cos(ang), np.ones((tm, LANES))], axis=0)
    sin = np.concatenate([np.sin(ang), np.zeros((tm, LANES))], axis=0)
    return jnp.asarray(cos, F32), jnp.asarray(sin, F32)


def _filter_kernel(f_ref, w1_ref, b1_ref, w2_ref, b2_ref, fr_ref, w3_ref, dl_ref, taps_ref, ssq_ref):
    f = f_ref[...]
    freq = fr_ref[...]
    h = jnp.sin(freq * (_dot3(f, w1_ref[...]) + b1_ref[...]))
    h = jnp.sin(freq * (_dot3(h, w2_ref[...]) + b2_ref[...]))
    h = _dot3(h, w3_ref[...]) * jnp.exp(-f[:, 0:1] * jnp.abs(dl_ref[...]))

    @pl.when(pl.program_id(1) == 0)
    def _():
        ssq_ref[...] = jnp.zeros_like(ssq_ref)

    ssq_ref[...] += jnp.sum(h * h, axis=0, keepdims=True)
    taps_ref[...] = h * f[:, H_EMB:H_EMB + 1]


def _filter_features(L):
    t = np.linspace(0.0, 1.0, L, dtype=np.float32).astype(np.float64)[:, None]
    bands = (H_EMB - 1) // 2
    w = (np.float32(2.0 * math.pi) * np.arange(L, dtype=np.float32) / np.float32(L)).astype(np.float64)[:, None]
    f = np.linspace(1e-4, bands - 1, bands, dtype=np.float32).astype(np.float64)[None, :]
    fw = (f.astype(np.float32) * w.astype(np.float32)).astype(np.float64)
    z = np.concatenate([t, np.cos(fw), -np.sin(fw)], axis=-1)
    feat = np.zeros((2 * L, LANES), np.float64)
    feat[:L, :H_EMB] = z
    idx = (L - np.arange(L)) % L
    feat[L:, :H_EMB] = z[idx]
    feat[:, H_EMB] = 1.0
    feat[L, H_EMB] = 0.0
    return jnp.asarray(feat, F32)


def _hyena_filters(L, w1, b1, w2, b2, w3, freq, delta, C):
    pad = LANES - H_FFN
    w1p = jnp.pad(w1, ((0, LANES - H_EMB), (0, pad)))
    w2p = jnp.pad(w2, ((0, pad), (0, pad)))
    w3p = jnp.pad(w3, ((0, pad), (0, 0)))
    row = lambda a: jnp.pad(a.reshape(1, H_FFN), ((0, 0), (0, pad)))
    tr = min(L, 512)
    nr = L // tr
    const = lambda shape: pl.BlockSpec(shape, lambda hf, r: (0, 0))
    return pl.pallas_call(
        _filter_kernel,
        out_shape=(jax.ShapeDtypeStruct((2 * L, C), F32), jax.ShapeDtypeStruct((2, 1, C), F32)),
        grid=(2, nr),
        in_specs=[pl.BlockSpec((tr, LANES), lambda hf, r: (hf * nr + r, 0)),
                  const((LANES, LANES)), const((1, LANES)), const((LANES, LANES)), const((1, LANES)),
                  const((1, LANES)),
                  pl.BlockSpec((LANES, C), lambda hf, r: (0, hf)),
                  pl.BlockSpec((1, C), lambda hf, r: (0, hf))],
        out_specs=(pl.BlockSpec((tr, C), lambda hf, r: (hf * nr + r, 0)),
                   pl.BlockSpec((None, 1, C), lambda hf, r: (hf, 0, 0))),
        compiler_params=_cparams(("parallel", "arbitrary")), name="hyena_filter",
    )(_filter_features(L), w1p, row(b1), w2p, row(b2), row(freq), w3p, delta.reshape(1, 2 * C))


def _cblock(m):
    return np.block([[m.real, -m.imag], [m.imag, m.real]])


def _dft_consts(L):
    N = 2 * L
    N2 = LANES
    N1 = N // N2
    half = N1 // 2
    n1 = np.arange(N1)
    n2 = np.arange(N2)
    F1 = np.exp(-2j * np.pi * np.outer(n1, n1) / N1)
    F2 = np.exp(-2j * np.pi * np.outer(n2, n2) / N2)
    a_data = _cblock(F1[:, :half])
    a_taps = np.concatenate([F1.real, F1.imag], axis=0)
    b_fwd = _cblock(F2)
    b_inv = _cblock(np.conj(F2))
    fin = _cblock(np.conj(F1)[:half, :]) / N
    ang = 2.0 * np.pi * np.outer(n2, n1) / N
    tw = dict(c_a=np.cos(ang)[:, :, None], s_a=np.sin(ang)[:, :, None],
              c_b=np.cos(ang.T)[:, :, None], s_b=np.sin(ang.T)[:, :, None])
    as_bf16 = lambda m: jnp.asarray(np.asarray(m, np.float32)).astype(BF16)
    return dict(N1=N1, a_data=as_bf16(a_data), a_taps=as_bf16(a_taps),
                b_fwd=as_bf16(b_fwd), b_inv=as_bf16(b_inv), fin=as_bf16(fin),
                tw={k: jnp.asarray(v, F32) for k, v in tw.items()})


def _time_slice_kernel(*refs, n_in, n_vmem, n_out, N1, compute):
    ins, vmem = refs[:n_in], refs[n_in:n_in + n_vmem]
    outs = refs[n_in + n_vmem:n_in + n_vmem + n_out]
    in_buf, out_buf, in_sem, out_sem = refs[n_in + n_vmem + n_out:]
    j = pl.program_id(0)
    n = pl.num_programs(0)
    slot = j % 2

    def in_copies(step, sl):
        return [pltpu.make_async_copy(src.at[pl.ds(0, N1), step, :], in_buf.at[sl, i], in_sem.at[sl])
                for i, src in enumerate(ins)]

    def out_copies(step, sl):
        return [pltpu.make_async_copy(out_buf.at[sl, i], dst.at[:, step, :], out_sem.at[sl])
                for i, dst in enumerate(outs)]

    @pl.when(j == 0)
    def _():
        for cp in in_copies(0, 0):
            cp.start()

    @pl.when(j + 1 < n)
    def _():
        for cp in in_copies(j + 1, 1 - slot):
            cp.start()

    for cp in in_copies(j, slot):
        cp.wait()

    @pl.when(j >= 2)
    def _():
        for cp in out_copies(j - 2, slot):
            cp.wait()

    for i, r in enumerate(compute([in_buf[slot, i] for i in range(n_in)], vmem)):
        out_buf[slot, i] = r
    for cp in out_copies(j, slot):
        cp.start()

    @pl.when(j == n - 1)
    def _():
        for cp in out_copies(j - 1, 1 - slot) + out_copies(j, slot):
            cp.wait()


def _time_slice_call(compute, hbm_inputs, vmem_inputs, vmem_specs, n_out, *, N1, C, name):
    assert LANES >= 2
    any_spec = pl.BlockSpec(memory_space=pl.ANY)
    shape = jax.ShapeDtypeStruct((N1, LANES, C), F32)
    n_in = len(hbm_inputs)
    return pl.pallas_call(
        functools.partial(_time_slice_kernel, n_in=n_in, n_vmem=len(vmem_inputs), n_out=n_out, N1=N1,
                          compute=compute),
        out_shape=(shape,) * n_out, grid=(LANES,),
        in_specs=[any_spec] * n_in + list(vmem_specs),
        out_specs=(any_spec,) * n_out,
        scratch_shapes=[pltpu.VMEM((2, n_in, N1, C), F32), pltpu.VMEM((2, n_out, N1, C), F32),
                        pltpu.SemaphoreType.DMA((2,)), pltpu.SemaphoreType.DMA((2,))],
        compiler_params=_cparams(("arbitrary",)), name=name,
    )(*hbm_inputs, *vmem_inputs)


def _fft_a_compute(xs, vmem):
    m_ref, c_ref, s_ref = vmem
    r = _dot(m_ref[...], xs[0].astype(BF16))
    n1 = r.shape[0] // 2
    re, im = r[:n1], r[n1:]
    c, s = c_ref[...], s_ref[...]
    return re * c + im * s, im * c - re * s


def _fft_stage_a(x3, mat, tw, *, N1, C):
    twspec = pl.BlockSpec((None, N1, 1), lambda j: (j, 0, 0))
    return _time_slice_call(_fft_a_compute, [x3], [mat, tw["c_a"], tw["s_a"]],
                            [pl.BlockSpec(mat.shape, lambda j: (0, 0)), twspec, twspec], 2,
                            N1=N1, C=C, name="hyena_fft_a")


def _fft_mid_kernel(ar, ai, tr, ti, ssq, f_ref, i_ref, c_ref, s_ref, br, bi):
    N2 = LANES
    x = _dot(f_ref[...], jnp.concatenate([ar[...], ai[...]], axis=0).astype(BF16))
    h = _dot(f_ref[...], jnp.concatenate([tr[...], ti[...]], axis=0).astype(BF16))
    scale = lax.rsqrt(ssq[0] + ssq[1] + EPS)
    xr, xi, hr, hi = x[:N2], x[N2:], h[:N2] * scale, h[N2:] * scale
    y = jnp.concatenate([xr * hr - xi * hi, xr * hi + xi * hr], axis=0)
    r = _dot(i_ref[...], y.astype(BF16))
    re, im = r[:N2], r[N2:]
    c, s = c_ref[...], s_ref[...]
    br[...] = re * c - im * s
    bi[...] = im * c + re * s


def _fft_mid(Ar, Ai, Tr, Ti, ssq, consts, *, C, ct):
    N1 = consts["N1"]
    blk = pl.BlockSpec((None, LANES, ct), lambda k1, c: (k1, 0, c))
    const = pl.BlockSpec((2 * LANES, 2 * LANES), lambda k1, c: (0, 0))
    twspec = pl.BlockSpec((None, LANES, 1), lambda k1, c: (k1, 0, 0))
    shape = jax.ShapeDtypeStruct((N1, LANES, C), F32)
    return pl.pallas_call(
        _fft_mid_kernel, out_shape=(shape, shape), grid=(N1, C // ct),
        in_specs=[blk, blk, blk, blk, pl.BlockSpec((2, 1, ct), lambda k1, c: (0, 0, c)),
                  const, const, twspec, twspec],
        out_specs=(blk, blk),
        compiler_params=_cparams(("parallel", "parallel")), name="hyena_fft_mid",
    )(Ar, Ai, Tr, Ti, ssq, consts["b_fwd"], consts["b_inv"], consts["tw"]["c_b"], consts["tw"]["s_b"])


def _fft_fin_compute(xs, vmem):
    br, bi, x0, z = xs
    m_ref, bias = vmem
    r = _dot(m_ref[...], jnp.concatenate([br, bi], axis=0).astype(BF16))
    return (x0 * (r + bias[...] * z),)


def _fft_final(Br, Bi, x0_3d, z_3d, bias, consts, *, C):
    mat = consts["fin"]
    return _time_slice_call(_fft_fin_compute, [Br, Bi, x0_3d, z_3d], [mat, bias],
                            [pl.BlockSpec(mat.shape, lambda j: (0, 0)), pl.BlockSpec((1, C), lambda j: (0, 0))],
                            1, N1=consts["N1"], C=C, name="hyena_fft_final")[0]


def _hyena_latent(x0, z, taps, ssq, bias, consts, *, B, L, C):
    assert B == 2
    N1 = consts["N1"]
    ct = 512
    T = x0.shape[0]
    z3 = z.reshape(T // LANES, LANES, C)
    x03 = x0.reshape(T // LANES, LANES, C)
    Ar, Ai = _fft_stage_a(z3, consts["a_data"], consts["tw"], N1=N1, C=C)
    Tr, Ti = _fft_stage_a(taps.reshape(N1, LANES, C), consts["a_taps"], consts["tw"], N1=N1, C=C)
    Br, Bi = _fft_mid(Ar, Ai, Tr, Ti, ssq, consts, C=C, ct=ct)
    y = _fft_final(Br, Bi, x03, z3, bias.reshape(1, C).astype(F32), consts, C=C)
    return y.reshape(B * L, C)


def _hyena_ctx_kernel(z, x0, taps, ssq, bias, dh, dl, th, tl, ih, il, y, *, Lc):
    n = 2 * Lc
    x = _dot3c(dh[...], dl[...], z[...])
    h = _dot3c(th[...], tl[...], taps[...])
    scale = lax.rsqrt(ssq[0] + ssq[1] + EPS)
    xr, xi, hr, hi = x[:n], x[n:], h[:n] * scale, h[n:] * scale
    r = _dot3c(ih[...], il[...], jnp.concatenate([xr * hr - xi * hi, xr * hi + xi * hr], axis=0))
    y[...] = x0[...] * (r + bias[...] * z[...])


def _hyena_ctx(x0, z, taps, ssq, bias, *, B, L, Lc, C):
    assert B == 2 and (B * L) % (2 * Lc) == 0
    n = 2 * Lc
    idx = np.arange(n)
    F = np.exp(-2j * np.pi * np.outer(idx, idx) / n)
    d = _np_split(_cblock(F[:, :Lc]))
    t = _np_split(np.concatenate([F.real, F.imag], axis=0))
    inv = _np_split(_cblock(np.conj(F)[:Lc, :]) / n)
    ct = 256
    r0 = (B * L) // n
    both = pl.BlockSpec((n, ct), lambda c: (r0, c))
    const = lambda m: pl.BlockSpec(m.shape, lambda c: (0, 0))
    return pl.pallas_call(
        functools.partial(_hyena_ctx_kernel, Lc=Lc),
        out_shape=jax.ShapeDtypeStruct((n, C), F32),
        grid=(C // ct,),
        in_specs=[both, both,
                  pl.BlockSpec((n, ct), lambda c: (0, c)),
                  pl.BlockSpec((2, 1, ct), lambda c: (0, 0, c)),
                  pl.BlockSpec((1, ct), lambda c: (0, c)),
                  const(d[0]), const(d[1]), const(t[0]), const(t[1]), const(inv[0]), const(inv[1])],
        out_specs=pl.BlockSpec((n, ct), lambda c: (0, c)),
        compiler_params=_cparams(("parallel",)), name="hyena_ctx",
    )(z, x0, taps, ssq, bias.reshape(1, C).astype(F32), *d, *t, *inv)


def _merge_kernel(*refs, with_ctx, n_lat_tiles):
    if with_ctx:
        (ym, ya, yh, ymc, yac, yhc, gm, ga, gh, mb, wm, wa, wh, wo, x_ref, g1, o_ref) = refs
        is_ctx = pl.program_id(0) >= n_lat_tiles
        pick = lambda lat, ctx: jnp.where(is_ctx, ctx[...], lat[...])
        m, a, h = pick(ym, ymc), pick(ya, yac), pick(yh, yhc)
    else:
        (ym, ya, yh, gm, ga, gh, mb, wm, wa, wh, wo, x_ref, g1, o_ref) = refs
        m, a, h = ym[...], ya[...], yh[...]
    D = x_ref.shape[1]
    bias = mb[...]
    gate = lambda g, k: _sigmoid(g[...].astype(F32) + bias[:, k * D:(k + 1) * D])
    y = (gate(gm, 0) * _dot(m, wm[...]) + gate(ga, 1) * _dot(a, wa[...])
         + gate(gh, 2) * _dot(h.astype(BF16), wh[...]))
    o_ref[...] = x_ref[...] + g1[...] * _dot(y.astype(BF16), wo[...])


def _merge(X, P, gcol0, branches, merge_b, weights, mod, gate_idx, *, n_rows, B, L, with_ctx):
    D = X.shape[1]
    tm = ROW_TILE
    nlt, tps = (B * L) // tm, L // tm
    grp = functools.partial(_group_of_tile, n_lat_tiles=nlt, tiles_per_seq=tps, n_batch=B)
    lat = pl.BlockSpec((tm, D), lambda i: (jnp.minimum(i, nlt - 1), 0))
    ctx = pl.BlockSpec((tm, D), lambda i: (jnp.maximum(i - nlt, 0), 0))
    gb = gcol0 // D
    gspec = lambda k: pl.BlockSpec((tm, D), lambda i: (i, gb + k))
    wspec = pl.BlockSpec((D, D), lambda i: (0, 0))
    in_specs = ([lat] * 3 + ([ctx] * 3 if with_ctx else []) + [gspec(0), gspec(1), gspec(2)]
                + [pl.BlockSpec((1, 3 * D), lambda i: (0, 0))] + [wspec] * 4
                + [pl.BlockSpec((tm, D), lambda i: (i, 0)),
                   pl.BlockSpec((None, None, 1, D), lambda i: (grp(i), gate_idx, 0, 0))])
    return pl.pallas_call(
        functools.partial(_merge_kernel, with_ctx=with_ctx, n_lat_tiles=nlt),
        out_shape=jax.ShapeDtypeStruct((n_rows, D), F32),
        grid=(n_rows // tm,), in_specs=in_specs,
        out_specs=pl.BlockSpec((tm, D), lambda i: (i, 0)),
        compiler_params=_cparams(("parallel",)), name="merge",
    )(*branches, P, P, P, merge_b.reshape(1, 3 * D).astype(F32), *weights, X, mod)


EXPERT_BITS = 5
assert 1 << EXPERT_BITS == N_EXPERTS
DMA_UNROLL = 8


def _slot_of(code_ref, pstart_ref, pos):
    code = code_ref[pos]
    return pstart_ref[code & (N_EXPERTS - 1)] + (code >> EXPERT_BITS)


def _dispatch_kernel(code, pstart, tok_ref, xs_zero, xs_out, sem, *, n_tok):
    del xs_zero
    R = tok_ref.shape[0]
    base = pl.program_id(0) * R

    def copy(pos, r):
        return pltpu.make_async_copy(tok_ref.at[pl.ds(r, 1), :],
                                     xs_out.at[pl.ds(_slot_of(code, pstart, pos), 1), :], sem)

    for wait in (False, True):
        for k in range(TOP_K):
            def body(r, carry):
                cp = copy(k * n_tok + base + r, r)
                cp.wait() if wait else cp.start()
                return carry
            lax.fori_loop(0, R, body, 0, unroll=DMA_UNROLL)


def _moe_dispatch(tok, code, pstart, P):
    N, D = tok.shape
    tm = MOE_BLOCK
    return pl.pallas_call(
        functools.partial(_dispatch_kernel, n_tok=N),
        out_shape=jax.ShapeDtypeStruct((P, D), F32),
        grid_spec=pltpu.PrefetchScalarGridSpec(
            num_scalar_prefetch=2, grid=(N // tm,),
            in_specs=[pl.BlockSpec((tm, D), lambda i, cd, ps: (i, 0)),
                      pl.BlockSpec(memory_space=pl.ANY)],
            out_specs=pl.BlockSpec(memory_space=pl.ANY),
            scratch_shapes=[pltpu.SemaphoreType.DMA]),
        input_output_aliases={3: 0},
        compiler_params=_cparams(("arbitrary",)), name="moe_dispatch",
    )(code, pstart, tok, jnp.zeros((P, D), F32))


def _moe_kernel(blk_e, xs_ref, w1_ref, w3_ref, w2_ref, y_ref, w1b, w3b, w2b):
    i = pl.program_id(0)

    @pl.when((i == 0) | (blk_e[i] != blk_e[jnp.maximum(i - 1, 0)]))
    def _():
        w1b[...] = w1_ref[...].astype(BF16)
        w3b[...] = w3_ref[...].astype(BF16)
        w2b[...] = w2_ref[...].astype(BF16)

    x = xs_ref[...].astype(BF16)
    a = _dot(x, w1b[...])
    g = (a * _sigmoid(a)) * _dot(x, w3b[...])
    y_ref[...] = _dot(g.astype(BF16), w2b[...])


def _moe_experts(xs, blk_e, w1, w3, w2):
    P, D = xs.shape
    E, _, De = w1.shape
    return pl.pallas_call(
        _moe_kernel,
        out_shape=jax.ShapeDtypeStruct((P, D), F32),
        grid_spec=pltpu.PrefetchScalarGridSpec(
            num_scalar_prefetch=1, grid=(P // MOE_BLOCK,),
            in_specs=[pl.BlockSpec((MOE_BLOCK, D), lambda i, be: (i, 0)),
                      pl.BlockSpec((None, D, De), lambda i, be: (be[i], 0, 0)),
                      pl.BlockSpec((None, D, De), lambda i, be: (be[i], 0, 0)),
                      pl.BlockSpec((None, De, D), lambda i, be: (be[i], 0, 0))],
            out_specs=pl.BlockSpec((MOE_BLOCK, D), lambda i, be: (i, 0)),
            scratch_shapes=[pltpu.VMEM((D, De), BF16), pltpu.VMEM((D, De), BF16), pltpu.VMEM((De, D), BF16)]),
        compiler_params=_cparams(("arbitrary",)), name="moe_experts",
    )(blk_e, xs, w1, w3, w2)


def _combine_kernel(code, pstart, yb_hbm, x_ref, g2, wt_ref, o_ref, buf, sems, *, n_tok):
    i = pl.program_id(0)
    n = pl.num_programs(0)
    slot = i % 2
    R = x_ref.shape[0]

    def gather(tile, slot, wait):
        for k in range(TOP_K):
            def body(r, carry):
                src = _slot_of(code, pstart, k * n_tok + tile * R + r)
                cp = pltpu.make_async_copy(yb_hbm.at[pl.ds(src, 1), :],
                                           buf.at[slot, k, pl.ds(r, 1), :], sems.at[slot])
                cp.wait() if wait else cp.start()
                return carry
            lax.fori_loop(0, R, body, 0, unroll=DMA_UNROLL)

    @pl.when(i == 0)
    def _():
        gather(0, 0, False)

    @pl.when(i + 1 < n)
    def _():
        gather(i + 1, 1 - slot, False)

    gather(i, slot, True)
    w = wt_ref[...]
    acc = w[:, 0:1] * buf[slot, 0]
    for k in range(1, TOP_K):
        acc = acc + w[:, k:k + 1] * buf[slot, k]
    o_ref[...] = x_ref[...] + g2[...] * acc


def _moe_combine(yb, code, pstart, wts, X, mod, gate_idx, *, n_rows, B, L):
    D = X.shape[1]
    tm = MOE_BLOCK
    nlt, tps = (B * L) // tm, L // tm
    grp = functools.partial(_group_of_tile, n_lat_tiles=nlt, tiles_per_seq=tps, n_batch=B)
    return pl.pallas_call(
        functools.partial(_combine_kernel, n_tok=n_rows),
        out_shape=jax.ShapeDtypeStruct((n_rows, D), F32),
        grid_spec=pltpu.PrefetchScalarGridSpec(
            num_scalar_prefetch=2, grid=(n_rows // tm,),
            in_specs=[pl.BlockSpec(memory_space=pl.ANY),
                      pl.BlockSpec((tm, D), lambda i, cd, ps: (i, 0)),
                      pl.BlockSpec((None, None, 1, D), lambda i, cd, ps: (grp(i), gate_idx, 0, 0)),
                      pl.BlockSpec((tm, LANES), lambda i, cd, ps: (i, 0))],
            out_specs=pl.BlockSpec((tm, D), lambda i, cd, ps: (i, 0)),
            scratch_shapes=[pltpu.VMEM((2, TOP_K, tm, D), F32), pltpu.SemaphoreType.DMA((2,))]),
        compiler_params=_cparams(("arbitrary",)), name="moe_combine",
    )(code, pstart, yb, X, mod, wts)


def _dispatch_tables(code, counts, n_tok):
    E, K = N_EXPERTS, TOP_K
    cnt = counts[0, :E].astype(jnp.int32)
    padded = (cnt + MOE_BLOCK - 1) // MOE_BLOCK * MOE_BLOCK
    pend = jnp.cumsum(padded)
    pstart = (pend - padded).astype(jnp.int32)
    P = -(-(n_tok * K + E * (MOE_BLOCK - 1)) // MOE_BLOCK) * MOE_BLOCK
    blk_row = jnp.arange(P // MOE_BLOCK, dtype=jnp.int32) * MOE_BLOCK
    blk_e = jnp.minimum(jnp.sum(pend[None, :] <= blk_row[:, None], axis=1), E - 1).astype(jnp.int32)
    return code[:, :K].T.reshape(-1), pstart, P, blk_e


def kernel(x, c, ctx, c_ctx, mod_w, mod_b, norm1_g, norm2_g, w_in, merge_b, m_conv_w, m_conv_b, m_gate_b, m_norm_g, a_qnorm_g, a_knorm_g, a_lambda, a_subln_g, h_conv_w, h_conv_b, h_ffn_w1, h_ffn_b1, h_ffn_w2, h_ffn_b2, h_ffn_w3, h_freq, h_decay, h_bias, w_br_m, w_br_a, w_br_h, w_out, router_w, router_b, e_w1, e_w3, e_w2):
    B, L, D = x.shape
    Lc = ctx.shape[1]
    depth = mod_w.shape[0]
    NL, NC = B * L, B * Lc
    T = NL + NC
    assert B + 1 <= 8 and Lc == ROW_TILE and L % min(L, 1024) == 0 and L % 512 == 0
    assert NL % Lc == 0 and D % LANES == 0
    width = D
    a_dh = D // (2 * A_HEADS)
    n_gates = 4 * M_HEADS
    c_mq, c_mv, c_mo = 0, 2 * width, 3 * width
    c_aq, c_ak, c_av = 4 * width, 5 * width, 6 * width
    c_hx = 7 * width
    c_gp = 10 * width

    X = jnp.concatenate([x.reshape(NL, D), ctx.reshape(NC, D)], axis=0)
    c8 = jnp.zeros((8, D), F32).at[:B].set(c).at[B].set(c_ctx)
    cos_tab, sin_tab = _rope_tables(L, a_dh, ROW_TILE)
    fft_consts = _dft_consts(L)

    for l in range(depth):
        need_ctx = l < depth - 1
        lam_init = 0.8 - 0.6 * math.exp(-0.3 * l)
        mod = _modulation(c8, mod_w[l], mod_b[l])[:B + 1].reshape(B + 1, 6, 1, D)

        h = _norm_mod(X, norm1_g[l], mod, 0, 1, n_rows=T, L=L, B=B)
        g0 = 4 * width
        w_main = jnp.concatenate([w_in[l][:, :g0], w_in[l][:, g0 + n_gates:]], axis=1).astype(BF16)
        w_gate = jnp.pad(w_in[l][:, g0:g0 + n_gates], ((0, 0), (0, LANES - n_gates))).astype(BF16)
        P = _matmul(h, w_main, BF16, _largest_tile(T, 1536, ROW_TILE), 1024)
        G = _matmul(h, w_gate, F32, _largest_tile(T, 1536, ROW_TILE), LANES)

        QK = _mlstm_qk(P, m_conv_w[l], m_conv_b[l], L=L, Lc=Lc, NL=NL, width=width)
        ym, ymc = _mlstm(QK, P, G, m_gate_b[l], m_norm_g[l], B=B, L=L, Lc=Lc, width=width)

        Qh = _attn_prep(P, c_aq, a_qnorm_g[l], cos_tab, sin_tab, a_dh ** -0.5 * LOG2E, L=L, NL=NL, dh=a_dh)
        Kh = _attn_prep(P, c_ak, a_knorm_g[l], cos_tab, sin_tab, 1.0, L=L, NL=NL, dh=a_dh)
        attn = functools.partial(_attention, Qh, Kh, P, c_av, a_lambda[l], a_subln_g[l], lam_init,
                                 B=B, L=L, Lc=Lc, dh=a_dh)
        ya = attn(latent=True)

        x0, z = _hyena_pre(P, c_hx, h_conv_w[l], h_conv_b[l], L=L, Lc=Lc, NL=NL, C=width)
        filt = (h_ffn_w1[l], h_ffn_b1[l], h_ffn_w2[l], h_ffn_b2[l], h_ffn_w3[l], h_freq[l], h_decay[l])
        taps, ssq = _hyena_filters(L, *filt, width)
        yh = _hyena_latent(x0, z, taps, ssq, h_bias[l], fft_consts, B=B, L=L, C=width)

        weights = [w.astype(BF16) for w in (w_br_m[l], w_br_a[l], w_br_h[l], w_out[l])]
        if need_ctx:
            yac = attn(latent=False)
            taps_c, ssq_c = _hyena_filters(Lc, *filt, width)
            yhc = _hyena_ctx(x0, z, taps_c, ssq_c, h_bias[l], B=B, L=L, Lc=Lc, C=width)
            branches, n_rows = [ym, ya, yh, ymc, yac, yhc], T
        else:
            branches, n_rows = [ym, ya, yh], NL
        X = _merge(X, P, c_gp, branches, merge_b[l], weights, mod, 2,
                   n_rows=n_rows, B=B, L=L, with_ctx=need_ctx)

        tok, code, wts, counts = _norm_mod(X, norm2_g[l], mod, 3, 4, n_rows=n_rows, L=L, B=B,
                                           router=(router_w, router_b))
        code, pstart, n_slots, blk_e = _dispatch_tables(code, counts, n_rows)
        xs = _moe_dispatch(tok, code, pstart, n_slots)
        yb = _moe_experts(xs, blk_e, e_w1[l], e_w3[l], e_w2[l])
        X = _moe_combine(yb, code, pstart, wts, X, mod, 5, n_rows=n_rows, B=B, L=L)
    return X[:NL].reshape(B, L, D)
```

```python
import functools
import math

import numpy as np
import jax
import jax.numpy as jnp
from jax import lax
from jax.experimental import pallas as pl
from jax.experimental.pallas import tpu as pltpu

F32 = jnp.float32
BF16 = jnp.bfloat16

GRID_W = 64
EPS = 1e-6
M_HEADS = 4
M_CHUNK = 128
A_HEADS = 8
ROPE_BASE = 10000.0
H_EMB = 33
H_FFN = 64
SHORT_CONV = 3
N_EXPERTS = 32
N_GROUPS = 4
TOP_K = 2
MOE_BLOCK = 256
MOE_TOKEN_TILE = 128

LANES = 128
BF16_SUBLANES = 16
V7X_VMEM_BYTES = 64 * 1024 * 1024
VMEM_LIMIT = V7X_VMEM_BYTES * 7 // 8

ROW_TILE = 256
LOG2E = 1.4426950408889634


def _cparams(sem):
    return pltpu.CompilerParams(dimension_semantics=sem, vmem_limit_bytes=VMEM_LIMIT)


def _split(x):
    hi = x.astype(BF16)
    lo = (x - hi.astype(F32)).astype(BF16)
    return hi, lo


def _dot(a, b):
    return jnp.dot(a, b, preferred_element_type=F32)


def _dot3(a, b):
    ah, al = _split(a)
    bh, bl = _split(b)
    return _dot(ah, bh) + _dot(al, bh) + _dot(ah, bl)


def _dot3c(ch, cl, x):
    xh, xl = _split(x)
    return _dot(ch, xh) + _dot(cl, xh) + _dot(ch, xl)


def _np_split(a):
    a = jnp.asarray(np.asarray(a, np.float32))
    hi = a.astype(BF16)
    lo = (a - hi.astype(F32)).astype(BF16)
    return hi, lo


def _sigmoid(x):
    return 1.0 / (1.0 + jnp.exp(-x))


def _log_sigmoid(x):
    return jnp.minimum(x, 0.0) - jnp.log(1.0 + jnp.exp(-jnp.abs(x)))


def _mod_kernel(c_ref, w_ref, b_ref, o_ref):
    c = c_ref[...]
    o_ref[...] = _dot3(c * _sigmoid(c), w_ref[...]) + b_ref[...]


def _modulation(c8, w, b):
    D, N = w.shape
    tn = 1536
    return pl.pallas_call(
        _mod_kernel,
        out_shape=jax.ShapeDtypeStruct((8, N), F32),
        grid=(N // tn,),
        in_specs=[pl.BlockSpec((8, D), lambda j: (0, 0)),
                  pl.BlockSpec((D, tn), lambda j: (0, j)),
                  pl.BlockSpec((1, tn), lambda j: (0, j))],
        out_specs=pl.BlockSpec((8, tn), lambda j: (0, j)),
        compiler_params=_cparams(("parallel",)),
        name="modulation",
    )(c8, w, b.reshape(1, N))


def _norm_mod_router_kernel(x_ref, g_ref, sh_ref, sc_ref, rw_ref, rb_ref, o_ref, code_ref, wt_ref, cnt_ref, carry):
    E, G = N_EXPERTS, N_GROUPS
    gs = E // G

    @pl.when(pl.program_id(0) == 0)
    def _():
        carry[...] = jnp.zeros_like(carry)

    x = x_ref[...]
    y = x * lax.rsqrt(jnp.mean(x * x, axis=-1, keepdims=True) + EPS) * g_ref[...]
    h = y * (1.0 + sc_ref[...]) + sh_ref[...]
    o_ref[...] = h
    tm = h.shape[0]
    s = _sigmoid(_dot3(h, rw_ref[...]))
    lane = lax.broadcasted_iota(jnp.int32, (1, LANES), 1)
    lane_f = lane.astype(F32)
    sb = jnp.where(lane < E, s + rb_ref[...], -jnp.inf)
    far = float(LANES)
    best = jnp.full((tm, 1), -jnp.inf, F32)
    e1 = jnp.zeros((tm, 1), F32)
    e2 = jnp.zeros((tm, 1), F32)
    for g in range(G):
        mg = jnp.where((lane >= g * gs) & (lane < (g + 1) * gs), sb, -jnp.inf)
        m1 = jnp.max(mg, axis=-1, keepdims=True)
        i1 = jnp.min(jnp.where(mg == m1, lane_f, far), axis=-1, keepdims=True)
        mg2 = jnp.where(lane_f == i1, -jnp.inf, mg)
        m2 = jnp.max(mg2, axis=-1, keepdims=True)
        i2 = jnp.min(jnp.where(mg2 == m2, lane_f, far), axis=-1, keepdims=True)
        score = m1 + m2
        take = score > best
        best = jnp.where(take, score, best)
        e1 = jnp.where(take, i1, e1)
        e2 = jnp.where(take, i2, e2)
    oh1 = lane_f == e1
    oh2 = lane_f == e2
    s1 = jnp.sum(jnp.where(oh1, s, 0.0), axis=-1, keepdims=True)
    s2 = jnp.sum(jnp.where(oh2, s, 0.0), axis=-1, keepdims=True)
    den = s1 + s2
    r = lax.broadcasted_iota(jnp.int32, (tm, tm), 0)
    c = lax.broadcasted_iota(jnp.int32, (tm, tm), 1)
    lower = (c < r).astype(BF16)
    o1 = oh1.astype(F32)
    o2 = oh2.astype(F32)
    cum1 = _dot(lower, o1.astype(BF16))
    cum2 = _dot(lower, o2.astype(BF16))
    tot1 = jnp.sum(o1, axis=0, keepdims=True)
    base = carry[...]
    rank1 = jnp.sum(jnp.where(oh1, base + cum1, 0.0), axis=-1, keepdims=True)
    rank2 = jnp.sum(jnp.where(oh2, base + tot1 + cum2, 0.0), axis=-1, keepdims=True)
    total = base + tot1 + jnp.sum(o2, axis=0, keepdims=True)
    carry[...] = total
    cnt_ref[...] = total
    code1 = (rank1 * E + e1).astype(jnp.int32)
    code2 = (rank2 * E + e2).astype(jnp.int32)
    code_ref[...] = jnp.where(lane == 0, code1, jnp.where(lane == 1, code2, 0))
    wt_ref[...] = jnp.where(lane == 0, s1 / den, jnp.where(lane == 1, s2 / den, 0.0))


def _group_of_tile(i, n_lat_tiles, tiles_per_seq, n_batch):
    return jnp.where(i < n_lat_tiles, i // tiles_per_seq, n_batch)


def _norm_mod(x, g, mod, shift_idx, scale_idx, *, n_rows, L, B, router):
    D = x.shape[1]
    tm = ROW_TILE
    nlt, tps = (B * L) // tm, L // tm
    grp = functools.partial(_group_of_tile, n_lat_tiles=nlt, tiles_per_seq=tps, n_batch=B)
    in_specs = [pl.BlockSpec((tm, D), lambda i: (i, 0)),
                pl.BlockSpec((1, D), lambda i: (0, 0)),
                pl.BlockSpec((None, None, 1, D), lambda i: (grp(i), shift_idx, 0, 0)),
                pl.BlockSpec((None, None, 1, D), lambda i: (grp(i), scale_idx, 0, 0))]
    args = [x, g.reshape(1, D), mod, mod]
    router_w, router_b = router
    E = router_w.shape[1]
    assert E == N_EXPERTS
    rw = jnp.pad(router_w, ((0, 0), (0, LANES - E)))
    rb = jnp.pad(router_b.astype(F32).reshape(1, E), ((0, 0), (0, LANES - E)))
    lanes = pl.BlockSpec((tm, LANES), lambda i: (i, 0))
    return pl.pallas_call(
        _norm_mod_router_kernel,
        out_shape=(jax.ShapeDtypeStruct((n_rows, D), F32),
                   jax.ShapeDtypeStruct((n_rows, LANES), jnp.int32),
                   jax.ShapeDtypeStruct((n_rows, LANES), F32),
                   jax.ShapeDtypeStruct((1, LANES), F32)),
        grid=(n_rows // tm,),
        in_specs=in_specs + [pl.BlockSpec((D, LANES), lambda i: (0, 0)),
                             pl.BlockSpec((1, LANES), lambda i: (0, 0))],
        out_specs=(pl.BlockSpec((tm, D), lambda i: (i, 0)), lanes, lanes,
                   pl.BlockSpec((1, LANES), lambda i: (0, 0))),
        scratch_shapes=[pltpu.VMEM((1, LANES), F32)],
        compiler_params=_cparams(("arbitrary",)), name="norm_mod_router",
    )(*args, rw, rb)


def _largest_tile(n, cap, step):
    return max(t for t in range(step, cap + 1, step) if n % t == 0)


W_PREP_TILE = 512


def _w_prep_kernel(w_ref, wn_ref, o_ref, *, n_aligned, shift):
    tn = o_ref.shape[1]

    @pl.when(pl.program_id(1) < n_aligned)
    def _():
        o_ref[...] = w_ref[...].astype(o_ref.dtype)

    @pl.when(pl.program_id(1) >= n_aligned)
    def _():
        lane = lax.broadcasted_iota(jnp.int32, (1, tn), 1)
        a = pltpu.roll(w_ref[...], tn - shift, 1)
        b = pltpu.roll(wn_ref[...], tn - shift, 1)
        o_ref[...] = jnp.where(lane < tn - shift, a, b).astype(o_ref.dtype)


def _w_in_prep(w_in, gate_col0, n_gates):
    depth, D, d_in = w_in.shape
    tn = W_PREP_TILE
    n_out = d_in - n_gates
    assert gate_col0 % tn == 0 and n_out % tn == 0 and 0 < n_gates < LANES
    return pl.pallas_call(
        functools.partial(_w_prep_kernel, n_aligned=gate_col0 // tn, shift=n_gates),
        out_shape=jax.ShapeDtypeStruct((depth, D, n_out), BF16),
        grid=(depth, n_out // tn),
        in_specs=[pl.BlockSpec((None, D, tn), lambda l, j: (l, 0, j)),
                  pl.BlockSpec((None, D, tn), lambda l, j: (l, 0, j + 1))],
        out_specs=pl.BlockSpec((None, D, tn), lambda l, j: (l, 0, j)),
        compiler_params=_cparams(("parallel", "parallel")), name="w_in_prep",
    )(w_in, w_in)


def _inproj_kernel(x_ref, g_ref, mod_ref, w_ref, wg_ref, o_ref, gate_ref, h_scr, *, L, n_batch):
    tm = x_ref.shape[0]

    @pl.when(pl.program_id(1) == 0)
    def _():
        x = x_ref[...]
        y = x * lax.rsqrt(jnp.mean(x * x, axis=-1, keepdims=True) + EPS) * g_ref[...]
        row = pl.program_id(0) * tm + lax.broadcasted_iota(jnp.int32, (tm, 1), 0)
        shift, scale = mod_ref[n_batch, 0], mod_ref[n_batch, 1]
        for b in range(n_batch - 1, -1, -1):
            in_b = row < (b + 1) * L
            shift = jnp.where(in_b, mod_ref[b, 0], shift)
            scale = jnp.where(in_b, mod_ref[b, 1], scale)
        h_scr[...] = (y * (1.0 + scale) + shift).astype(h_scr.dtype)
        gate_ref[...] = _dot(h_scr[...], wg_ref[...])

    o_ref[...] = _dot(h_scr[...], w_ref[...]).astype(o_ref.dtype)


def _inproj(X, g, mod, w_main, w_gate, *, L, B):
    T, D = X.shape
    N = w_main.shape[1]
    tm, tn = _largest_tile(T, 1536, ROW_TILE), 1024
    return pl.pallas_call(
        functools.partial(_inproj_kernel, L=L, n_batch=B),
        out_shape=(jax.ShapeDtypeStruct((T, N), BF16), jax.ShapeDtypeStruct((T, LANES), F32)),
        grid=(T // tm, N // tn),
        in_specs=[pl.BlockSpec((tm, D), lambda i, j: (i, 0)),
                  pl.BlockSpec((1, D), lambda i, j: (0, 0)),
                  pl.BlockSpec(mod.shape, lambda i, j: (0, 0, 0, 0)),
                  pl.BlockSpec((D, tn), lambda i, j: (0, j)),
                  pl.BlockSpec((D, LANES), lambda i, j: (0, 0))],
        out_specs=(pl.BlockSpec((tm, tn), lambda i, j: (i, j)),
                   pl.BlockSpec((tm, LANES), lambda i, j: (i, 0))),
        scratch_shapes=[pltpu.VMEM((tm, D), BF16)],
        compiler_params=_cparams(("parallel", "arbitrary")), name="inproj",
    )(X, g.reshape(1, D), mod, w_main, w_gate)


def _seq_edge_flags(tm, L, Lc, NL):
    r0 = pl.program_id(0) * tm
    lat = r0 < NL
    start = jnp.where(lat, r0 % L == 0, (r0 - NL) % Lc == 0)
    end = jnp.where(lat, (r0 + tm) % L == 0, (r0 + tm - NL) % Lc == 0)
    return jnp.where(start, 0.0, 1.0), jnp.where(end, 0.0, 1.0)


def _conv3(cur_ref, prev_ref, next_ref, w_ref, b_ref, keep_prev, keep_next):
    cur = cur_ref[...].astype(F32)
    tm = cur.shape[0]
    prev_row = prev_ref[BF16_SUBLANES - 1:BF16_SUBLANES, :].astype(F32) * keep_prev
    next_row = next_ref[0:1, :].astype(F32) * keep_next
    row = lax.broadcasted_iota(jnp.int32, (tm, 1), 0)
    up = jnp.where(row == 0, prev_row, pltpu.roll(cur, 1, 0))
    dn = jnp.where(row == tm - 1, next_row, pltpu.roll(cur, tm - 1, 0))
    return b_ref[...] + up * w_ref[0:1, :] + cur * w_ref[1:2, :] + dn * w_ref[2:3, :]


def _conv_specs(tm, tc, T, col_block):
    per = tm // BF16_SUBLANES
    last = T // BF16_SUBLANES - 1
    return [pl.BlockSpec((tm, tc), lambda i, j: (i, col_block + j)),
            pl.BlockSpec((BF16_SUBLANES, tc), lambda i, j: (jnp.maximum(i * per - 1, 0), col_block + j)),
            pl.BlockSpec((BF16_SUBLANES, tc), lambda i, j: (jnp.minimum((i + 1) * per, last), col_block + j))]


def _mconv_kernel(cur_ref, prev_ref, next_ref, w_ref, b_ref, scale_ref, o_ref, *, tm, L, Lc, NL):
    kp, kn = _seq_edge_flags(tm, L, Lc, NL)
    y = _conv3(cur_ref, prev_ref, next_ref, w_ref, b_ref, kp, kn)
    o_ref[...] = (y * _sigmoid(y) * scale_ref[...]).astype(o_ref.dtype)


def _mlstm_qk(P, conv_w, conv_b, *, L, Lc, NL, width):
    T = P.shape[0]
    tm, tc = ROW_TILE, 1024
    C = 2 * width
    scale = jnp.concatenate([jnp.ones((1, width), F32),
                             jnp.full((1, width), (width // M_HEADS) ** -0.5, F32)], axis=1)
    vec = lambda r: pl.BlockSpec((r, tc), lambda i, j: (0, j))
    return pl.pallas_call(
        functools.partial(_mconv_kernel, tm=tm, L=L, Lc=Lc, NL=NL),
        out_shape=jax.ShapeDtypeStruct((T, C), BF16),
        grid=(T // tm, C // tc),
        in_specs=_conv_specs(tm, tc, T, 0) + [vec(SHORT_CONV), vec(1), vec(1)],
        out_specs=pl.BlockSpec((tm, tc), lambda i, j: (i, j)),
        compiler_params=_cparams(("parallel", "parallel")), name="mlstm_qk_conv",
    )(P, P, P, conv_w, conv_b.reshape(1, C), scale)


def _hyena_pre_kernel(c0, p0, n0, c1, p1, n1, c2, p2, n2, w0, w1, w2, b0, b1, b2,
                      x0_ref, z_ref, *, tm, L, Lc, NL):
    kp, kn = _seq_edge_flags(tm, L, Lc, NL)
    x0 = _conv3(c0, p0, n0, w0, b0, kp, kn)
    x1 = _conv3(c1, p1, n1, w1, b1, kp, kn)
    v = _conv3(c2, p2, n2, w2, b2, kp, kn)
    x0_ref[...] = x0
    z_ref[...] = v * x1


def _hyena_pre(P, col0, conv_w, conv_b, *, L, Lc, NL, C):
    T = P.shape[0]
    tm, tc = ROW_TILE, 512
    nb = C // tc
    specs = []
    for part in range(3):
        specs += _conv_specs(tm, tc, T, col0 // tc + part * nb)
    wspecs = [pl.BlockSpec((SHORT_CONV, tc), lambda i, j, p=part: (0, p * nb + j)) for part in range(3)]
    bspecs = [pl.BlockSpec((1, tc), lambda i, j, p=part: (0, p * nb + j)) for part in range(3)]
    b2d = conv_b.reshape(1, 3 * C)
    out = pl.BlockSpec((tm, tc), lambda i, j: (i, j))
    return pl.pallas_call(
        functools.partial(_hyena_pre_kernel, tm=tm, L=L, Lc=Lc, NL=NL),
        out_shape=(jax.ShapeDtypeStruct((T, C), F32), jax.ShapeDtypeStruct((T, C), F32)),
        grid=(T // tm, nb),
        in_specs=specs + wspecs + bspecs,
        out_specs=(out, out),
        compiler_params=_cparams(("parallel", "parallel")), name="hyena_pre",
    )(*([P] * 9), conv_w, conv_w, conv_w, b2d, b2d, b2d)


def _mlstm_chunk(q, k, v, li_r, lf_r, li_c, lf_c, C_scr, n_scr, m_scr, fwd):
    Q = M_CHUNK
    row = lax.broadcasted_iota(jnp.int32, (Q, Q), 0)
    col = lax.broadcasted_iota(jnp.int32, (Q, Q), 1)
    mask = (col <= row) if fwd else (col >= row)
    tri_c = mask.astype(BF16)
    tri_r = ((row <= col) if fwd else (row >= col)).astype(BF16)
    lfc_h, lfc_l = _split(jnp.broadcast_to(lf_c, (Q, Q)))
    lfr_h, lfr_l = _split(jnp.broadcast_to(lf_r, (Q, Q)))
    b_cols = _dot(tri_c, lfc_h) + _dot(tri_c, lfc_l)
    b_rows = _dot(lfr_h, tri_r) + _dot(lfr_l, tri_r)
    dm = jnp.where(mask, b_cols - b_rows + li_r, -jnp.inf)
    m_prev = m_scr[0:1, 0:1]
    b_col = b_cols[:, 0:1]
    inter = b_col + m_prev
    mt = jnp.maximum(inter, jnp.max(dm, axis=-1, keepdims=True))
    s = lax.dot_general(q, k, (((1,), (1,)), ((), ())), preferred_element_type=F32) * jnp.exp(dm - mt)
    wi = jnp.exp(inter - mt)
    num = _dot(s.astype(BF16), v) + wi * _dot(q, C_scr[...].astype(BF16))
    qn = jnp.sum(q.astype(F32) * n_scr[...], axis=-1, keepdims=True)
    den = jnp.sum(s, axis=-1, keepdims=True) + wi * qn
    h = num / jnp.maximum(jnp.abs(den), jnp.exp(-mt))
    b_tot = b_cols[Q - 1:Q, 0:1] if fwd else b_cols[0:1, 0:1]
    ws = b_tot - b_col + li_c
    m_new = jnp.maximum(b_tot + m_prev, jnp.max(ws, axis=0, keepdims=True))
    decay = jnp.exp(b_tot + m_prev - m_new)
    kw = k.astype(F32) * jnp.exp(ws - m_new)
    C_scr[...] = decay * C_scr[...] + lax.dot_general(
        kw.astype(BF16), v, (((0,), (0,)), ((), ())), preferred_element_type=F32)
    n_scr[...] = decay * n_scr[...] + jnp.sum(kw, axis=0, keepdims=True)
    m_scr[...] = jnp.broadcast_to(m_new, m_scr.shape)
    return h


def _mlstm_kernel(*refs, fwd, SEG, Lc, H, dh):
    if fwd:
        (gr_ref, gc_ref, grc_ref, gcc_ref, q_ref, k_ref, v_ref, qc_ref, kc_ref, vc_ref,
         h_ref, hc_ref, *scr) = refs
    else:
        (gr_ref, gc_ref, grc_ref, gcc_ref, q_ref, k_ref, v_ref, qc_ref, kc_ref, vc_ref,
         o_ref, oc_ref, hf_ref, hfc_ref, ng_ref, y_ref, yc_ref, *scr) = refs
    C_scr, n_scr, m_scr = scr[0:H], scr[H:2 * H], scr[2 * H:3 * H]
    gi = 0 if fwd else 2

    def run(nchunks, qr, kr, vr, grr, gcr, emit):
        def body(j, carry):
            c = j if fwd else nchunks - 1 - j
            rows = pl.ds(pl.multiple_of(c * M_CHUNK, M_CHUNK), M_CHUNK)
            for hh in range(H):
                cols = slice(hh * dh, (hh + 1) * dh)
                gcs = gcr[hh, rows, :]
                h = _mlstm_chunk(qr[rows, cols], kr[rows, cols], vr[rows, cols],
                                 grr[hh, gi, pl.ds(c, 1), :], _log_sigmoid(grr[hh, gi + 1, pl.ds(c, 1), :]),
                                 gcs[:, gi:gi + 1], _log_sigmoid(gcs[:, gi + 1:gi + 2]),
                                 C_scr[hh], n_scr[hh], m_scr[hh], fwd)
                emit(rows, cols, h)
            return carry

        lax.fori_loop(0, nchunks, body, 0, unroll=2)

    def emitter(dst, hf=None, o=None):
        def emit(rows, cols, h):
            if not fwd:
                h = h + hf[rows, cols]
                y = h * lax.rsqrt(jnp.mean(h * h, axis=-1, keepdims=True) + EPS) * ng_ref[:, cols]
                h = (y * _sigmoid(o[rows, cols].astype(F32))).astype(dst.dtype)
            dst[rows, cols] = h
        return emit

    @pl.when(pl.program_id(1) == 0)
    def _():
        for r in scr:
            r[...] = jnp.zeros_like(r)
        run(Lc // M_CHUNK, qc_ref, kc_ref, vc_ref, grc_ref, gcc_ref,
            emitter(hc_ref) if fwd else emitter(yc_ref, hfc_ref, oc_ref))

    run(SEG // M_CHUNK, q_ref, k_ref, v_ref, gr_ref, gc_ref,
        emitter(h_ref) if fwd else emitter(y_ref, hf_ref, o_ref))


def _mlstm(QK, P, G, gate_b, norm_g, *, B, L, Lc, width):
    H = M_HEADS
    dh = width // H
    NL = B * L
    SEG = min(L, 1024)
    S = L // SEG
    g = G[:, :4 * H] + gate_b.astype(F32)[None, :]

    def gate_views(rows, n):
        a = rows.reshape(B, n, 4, H)
        return (a.transpose(0, 3, 2, 1).reshape(B, H, 4, n // M_CHUNK, M_CHUNK),
                a.transpose(0, 3, 1, 2))

    gr, gc = gate_views(g[:NL], L)
    grc, gcc = gate_views(g[NL:], Lc)
    scratch = ([pltpu.VMEM((dh, dh), F32)] * H + [pltpu.VMEM((1, dh), F32)] * H
               + [pltpu.VMEM((8, LANES), F32)] * H)

    def call(fwd, extra_in, extra_specs, out_dtype):
        seg_of = (lambda s: s) if fwd else (lambda s: S - 1 - s)
        lat = lambda cb: pl.BlockSpec((SEG, width), lambda b, s: (b * S + seg_of(s), cb))
        ctx = lambda cb: pl.BlockSpec((Lc, width), lambda b, s: (NL // Lc + b, cb))
        gate_specs = [
            pl.BlockSpec((None, H, 4, SEG // M_CHUNK, M_CHUNK), lambda b, s: (b, 0, 0, seg_of(s), 0)),
            pl.BlockSpec((None, H, SEG, 4), lambda b, s: (b, 0, seg_of(s), 0)),
            pl.BlockSpec((None, H, 4, Lc // M_CHUNK, M_CHUNK), lambda b, s: (b, 0, 0, 0, 0)),
            pl.BlockSpec((None, H, Lc, 4), lambda b, s: (b, 0, 0, 0))]
        lat_out = pl.BlockSpec((SEG, width), lambda b, s: (b * S + seg_of(s), 0))
        ctx_out = pl.BlockSpec((Lc, width), lambda b, s: (b, 0))
        return pl.pallas_call(
            functools.partial(_mlstm_kernel, fwd=fwd, SEG=SEG, Lc=Lc, H=H, dh=dh),
            out_shape=(jax.ShapeDtypeStruct((NL, width), out_dtype),
                       jax.ShapeDtypeStruct((B * Lc, width), out_dtype)),
            grid=(B, S),
            in_specs=(gate_specs + [lat(0), lat(1), lat(2), ctx(0), ctx(1), ctx(2)]
                      + extra_specs(lat, ctx, lat_out, ctx_out)),
            out_specs=(lat_out, ctx_out), scratch_shapes=scratch,
            compiler_params=_cparams(("parallel", "arbitrary")),
            name="mlstm_fwd" if fwd else "mlstm_bwd",
        )(gr, gc, grc, gcc, QK, QK, P, QK, QK, P, *extra_in)

    hf, hfc = call(True, [], lambda *_: [], F32)
    return call(False, [P, P, hf, hfc, jnp.tile(norm_g.astype(F32), H).reshape(1, width)],
                lambda lat, ctx, lat_out, ctx_out: [lat(3), ctx(3), lat_out, ctx_out,
                                                    pl.BlockSpec((1, width), lambda b, s: (0, 0))], BF16)


def _attn_prep_kernel(x_ref, g_ref, cos_ref, sin_ref, o_ref, *, scale, dh):
    n_blk = x_ref.shape[1] // LANES
    r = lax.broadcasted_iota(jnp.int32, (LANES, LANES), 0)
    c = lax.broadcasted_iota(jnp.int32, (LANES, LANES), 1)
    group = (r // dh == c // dh).astype(BF16)
    lane = lax.broadcasted_iota(jnp.int32, (1, LANES), 1)
    quarter = dh // 4
    first = (lane % (2 * quarter)) < quarter
    cos = cos_ref[...]
    sin = sin_ref[...]
    for hb in range(n_blk):
        cols = slice(hb * LANES, (hb + 1) * LANES)
        x = x_ref[:, cols].astype(F32)
        hi, lo = _split(x * x)
        ms = (_dot(hi, group) + _dot(lo, group)) * (1.0 / dh)
        y = x * lax.rsqrt(ms + EPS) * g_ref[:, cols]
        rot = jnp.where(first, -pltpu.roll(y, LANES - quarter, 1), pltpu.roll(y, quarter, 1))
        o_ref[:, cols] = ((y * cos + rot * sin) * scale).astype(o_ref.dtype)


def _attn_prep(P, col0, gain, cos_tab, sin_tab, scale, *, L, NL, dh):
    T = P.shape[0]
    W = A_HEADS * 2 * dh
    tm = ROW_TILE
    nlt, tps = NL // tm, L // tm
    tab = pl.BlockSpec((tm, LANES), lambda i: (jnp.where(i < nlt, i % tps, tps), 0))
    g = jnp.tile(gain.astype(F32), W // dh).reshape(1, W)
    return pl.pallas_call(
        functools.partial(_attn_prep_kernel, scale=scale, dh=dh),
        out_shape=jax.ShapeDtypeStruct((T, W), BF16),
        grid=(T // tm,),
        in_specs=[pl.BlockSpec((tm, W), lambda i: (i, col0 // W)),
                  pl.BlockSpec((1, W), lambda i: (0, 0)), tab, tab],
        out_specs=pl.BlockSpec((tm, W), lambda i: (i, 0)),
        compiler_params=_cparams(("parallel",)), name="attn_prep",
    )(P, g, cos_tab, sin_tab)


def _attn_kernel(*refs, n_lat, tk, L, Lc, dh, lam_init):
    if n_lat:
        lam_ref, q_ref, kl_ref, vl_ref, kc_ref, vc_ref, sg_ref, o_ref, vext, acc = refs
    else:
        lam_ref, q_ref, kc_ref, vc_ref, sg_ref, o_ref, vext, acc = refs
    dv = 2 * dh
    ctx0 = n_lat * tk

    @pl.when(pl.program_id(2) == 0)
    def _():
        if n_lat:
            vext[0:L, 0:dv] = vl_ref[...]
        vext[ctx0:ctx0 + Lc, 0:dv] = vc_ref[...]
        vext[:, dv:2 * dv] = jnp.ones((vext.shape[0], dv), BF16)

    q = q_ref[...]
    lane = lax.broadcasted_iota(jnp.int32, (1, dv), 1)
    qs = (jnp.where(lane < dh, q, jnp.zeros_like(q)), jnp.where(lane >= dh, q, jnp.zeros_like(q)))
    acc[...] = jnp.zeros_like(acc)
    tq = q.shape[0]

    def update(comp, kblk, vblk, m_old):
        s = lax.dot_general(qs[comp], kblk, (((1,), (1,)), ((), ())), preferred_element_type=F32)
        m_new = jnp.maximum(m_old, jnp.max(s, axis=-1, keepdims=True))
        p = jnp.exp2(s - m_new).astype(BF16)
        acc[comp] = jnp.exp2(m_old - m_new) * acc[comp] + _dot(p, vblk)
        return m_new

    m = (jnp.full((tq, 1), -jnp.inf, F32),) * 2
    for c in range(n_lat):
        kblk, vblk = kl_ref[c * tk:(c + 1) * tk, :], vext[c * tk:(c + 1) * tk, :]
        m = update(0, kblk, vblk, m[0]), update(1, kblk, vblk, m[1])
    kblk, vblk = kc_ref[...], vext[ctx0:ctx0 + Lc, :]
    update(0, kblk, vblk, m[0])
    update(1, kblk, vblk, m[1])

    lp = lam_ref[...]
    lam = (jnp.exp(jnp.sum(lp[0:1] * lp[1:2], axis=-1, keepdims=True))
           - jnp.exp(jnp.sum(lp[2:3] * lp[3:4], axis=-1, keepdims=True)) + lam_init)
    a0, a1 = acc[0], acc[1]
    o = a0[:, 0:dv] / a0[:, dv:dv + 1] - lam * (a1[:, 0:dv] / a1[:, dv:dv + 1])
    y = o * lax.rsqrt(jnp.mean(o * o, axis=-1, keepdims=True) + EPS) * sg_ref[...] * (1.0 - lam_init)
    o_ref[...] = y.astype(o_ref.dtype)


def _attention(Qh, Kh, P, vcol0, lam_p, sub_g, lam_init, *, B, L, Lc, dh, latent):
    NL = B * L
    dv = 2 * dh
    H = A_HEADS
    vb = vcol0 // dv
    if latent:
        tq, tk = 512, min(L, 1024)
        n_lat, nq, rows_out = L // tk, L // tq, NL
        q_spec = pl.BlockSpec((tq, dv), lambda b, h, i: (b * nq + i, h))
        lat_specs = [pl.BlockSpec((L, dv), lambda b, h, i: (b, h)),
                     pl.BlockSpec((L, dv), lambda b, h, i: (b, vb + h))]
        lat_args = [Kh, P]
        o_spec = pl.BlockSpec((tq, dv), lambda b, h, i: (b * nq + i, h))
        nkeys = L + Lc
    else:
        tq, tk = Lc, 512
        n_lat, nq, rows_out = 0, 1, B * Lc
        q_spec = pl.BlockSpec((Lc, dv), lambda b, h, i: (NL // Lc + b, h))
        lat_specs, lat_args = [], []
        o_spec = pl.BlockSpec((Lc, dv), lambda b, h, i: (b, h))
        nkeys = Lc
    ctx_specs = [pl.BlockSpec((Lc, dv), lambda b, h, i: (NL // Lc + b, h)),
                 pl.BlockSpec((Lc, dv), lambda b, h, i: (NL // Lc + b, vb + h))]
    return pl.pallas_call(
        functools.partial(_attn_kernel, n_lat=n_lat, tk=tk, L=L, Lc=Lc, dh=dh, lam_init=lam_init),
        out_shape=jax.ShapeDtypeStruct((rows_out, H * dv), BF16),
        grid=(B, H, nq),
        in_specs=[pl.BlockSpec((4, dh), lambda b, h, i: (0, 0)), q_spec] + lat_specs + ctx_specs
                 + [pl.BlockSpec((1, dv), lambda b, h, i: (0, 0))],
        out_specs=o_spec,
        scratch_shapes=[pltpu.VMEM((nkeys, 2 * dv), BF16), pltpu.VMEM((2, tq, 2 * dv), F32)],
        compiler_params=_cparams(("parallel", "parallel", "arbitrary")),
        name="diff_attn_latent" if latent else "diff_attn_ctx",
    )(lam_p.astype(F32), Qh, *lat_args, Kh, P, sub_g.reshape(1, dv).astype(F32))


def _rope_tables(L, dh, tm):
    rows = L // GRID_W
    row = np.repeat(np.arange(rows), GRID_W).astype(np.float64)
    col = np.tile(np.arange(GRID_W), rows).astype(np.float64)
    nf = dh // 4
    inv = (np.float32(ROPE_BASE) ** (-np.arange(nf, dtype=np.float32) / nf)).astype(np.float64)
    ang = np.concatenate([row[:, None] * inv] * 2 + [col[:, None] * inv] * 2, axis=-1)
    ang = np.tile(ang.astype(np.float32).astype(np.float64), (1, LANES // dh))
    cos = np.concatenate([np.cos(ang), np.ones((tm, LANES))], axis=0)
    sin = np.concatenate([np.sin(ang), np.zeros((tm, LANES))], axis=0)
    return jnp.asarray(cos, F32), jnp.asarray(sin, F32)


def _filter_kernel(f_ref, w1_ref, b1_ref, w2_ref, b2_ref, fr_ref, w3_ref, dl_ref, taps_ref, ssq_ref):
    f = f_ref[...]
    freq = fr_ref[...]
    h = jnp.sin(freq * (_dot3(f, w1_ref[...]) + b1_ref[...]))
    h = jnp.sin(freq * (_dot3(h, w2_ref[...]) + b2_ref[...]))
    h = _dot3(h, w3_ref[...]) * jnp.exp(-f[:, 0:1] * jnp.abs(dl_ref[...]))

    @pl.when(pl.program_id(1) == 0)
    def _():
        ssq_ref[...] = jnp.zeros_like(ssq_ref)

    ssq_ref[...] += jnp.sum(h * h, axis=0, keepdims=True)
    taps_ref[...] = h * f[:, H_EMB:H_EMB + 1]


def _filter_features(L):
    t = np.linspace(0.0, 1.0, L, dtype=np.float32).astype(np.float64)[:, None]
    bands = (H_EMB - 1) // 2
    w = (np.float32(2.0 * math.pi) * np.arange(L, dtype=np.float32) / np.float32(L)).astype(np.float64)[:, None]
    f = np.linspace(1e-4, bands - 1, bands, dtype=np.float32).astype(np.float64)[None, :]
    fw = (f.astype(np.float32) * w.astype(np.float32)).astype(np.float64)
    z = np.concatenate([t, np.cos(fw), -np.sin(fw)], axis=-1)
    feat = np.zeros((2 * L, LANES), np.float64)
    feat[:L, :H_EMB] = z
    idx = (L - np.arange(L)) % L
    feat[L:, :H_EMB] = z[idx]
    feat[:, H_EMB] = 1.0
    feat[L, H_EMB] = 0.0
    return jnp.asarray(feat, F32)


def _hyena_filters(L, w1, b1, w2, b2, w3, freq, delta, C):
    pad = LANES - H_FFN
    w1p = jnp.pad(w1, ((0, LANES - H_EMB), (0, pad)))
    w2p = jnp.pad(w2, ((0, pad), (0, pad)))
    w3p = jnp.pad(w3, ((0, pad), (0, 0)))
    row = lambda a: jnp.pad(a.reshape(1, H_FFN), ((0, 0), (0, pad)))
    tr = min(L, 512)
    nr = L // tr
    const = lambda shape: pl.BlockSpec(shape, lambda hf, r: (0, 0))
    return pl.pallas_call(
        _filter_kernel,
        out_shape=(jax.ShapeDtypeStruct((2 * L, C), F32), jax.ShapeDtypeStruct((2, 1, C), F32)),
        grid=(2, nr),
        in_specs=[pl.BlockSpec((tr, LANES), lambda hf, r: (hf * nr + r, 0)),
                  const((LANES, LANES)), const((1, LANES)), const((LANES, LANES)), const((1, LANES)),
                  const((1, LANES)),
                  pl.BlockSpec((LANES, C), lambda hf, r: (0, hf)),
                  pl.BlockSpec((1, C), lambda hf, r: (0, hf))],
        out_specs=(pl.BlockSpec((tr, C), lambda hf, r: (hf * nr + r, 0)),
                   pl.BlockSpec((None, 1, C), lambda hf, r: (hf, 0, 0))),
        compiler_params=_cparams(("parallel", "arbitrary")), name="hyena_filter",
    )(_filter_features(L), w1p, row(b1), w2p, row(b2), row(freq), w3p, delta.reshape(1, 2 * C))


def _cblock(m):
    return np.block([[m.real, -m.imag], [m.imag, m.real]])


def _dft_consts(L):
    N = 2 * L
    N2 = LANES
    N1 = N // N2
    half = N1 // 2
    n1 = np.arange(N1)
    n2 = np.arange(N2)
    F1 = np.exp(-2j * np.pi * np.outer(n1, n1) / N1)
    F2 = np.exp(-2j * np.pi * np.outer(n2, n2) / N2)
    a_data = _cblock(F1[:, :half])
    a_taps = np.concatenate([F1.real, F1.imag], axis=0)
    b_fwd = _cblock(F2)
    b_inv = _cblock(np.conj(F2))
    fin = _cblock(np.conj(F1)[:half, :]) / N
    ang = 2.0 * np.pi * np.outer(n2, n1) / N
    tw = dict(c_a=np.cos(ang)[:, :, None], s_a=np.sin(ang)[:, :, None],
              c_b=np.cos(ang.T)[:, :, None], s_b=np.sin(ang.T)[:, :, None])
    as_bf16 = lambda m: jnp.asarray(np.asarray(m, np.float32)).astype(BF16)
    return dict(N1=N1, a_data=as_bf16(a_data), a_taps=as_bf16(a_taps),
                b_fwd=as_bf16(b_fwd), b_inv=as_bf16(b_inv), fin=as_bf16(fin),
                tw={k: jnp.asarray(v, F32) for k, v in tw.items()})


def _time_slice_kernel(*refs, n_in, n_vmem, n_out, N1, compute):
    ins, vmem = refs[:n_in], refs[n_in:n_in + n_vmem]
    outs = refs[n_in + n_vmem:n_in + n_vmem + n_out]
    in_buf, out_buf, in_sem, out_sem = refs[n_in + n_vmem + n_out:]
    j = pl.program_id(0)
    n = pl.num_programs(0)
    slot = j % 2

    def in_copies(step, sl):
        return [pltpu.make_async_copy(src.at[pl.ds(0, N1), step, :], in_buf.at[sl, i], in_sem.at[sl])
                for i, src in enumerate(ins)]

    def out_copies(step, sl):
        return [pltpu.make_async_copy(out_buf.at[sl, i], dst.at[:, step, :], out_sem.at[sl])
                for i, dst in enumerate(outs)]

    @pl.when(j == 0)
    def _():
        for cp in in_copies(0, 0):
            cp.start()

    @pl.when(j + 1 < n)
    def _():
        for cp in in_copies(j + 1, 1 - slot):
            cp.start()

    for cp in in_copies(j, slot):
        cp.wait()

    @pl.when(j >= 2)
    def _():
        for cp in out_copies(j - 2, slot):
            cp.wait()

    for i, r in enumerate(compute([in_buf[slot, i] for i in range(n_in)], vmem)):
        out_buf[slot, i] = r
    for cp in out_copies(j, slot):
        cp.start()

    @pl.when(j == n - 1)
    def _():
        for cp in out_copies(j - 1, 1 - slot) + out_copies(j, slot):
            cp.wait()


def _time_slice_call(compute, hbm_inputs, vmem_inputs, vmem_specs, n_out, *, N1, C, name):
    assert LANES >= 2
    any_spec = pl.BlockSpec(memory_space=pl.ANY)
    shape = jax.ShapeDtypeStruct((N1, LANES, C), F32)
    n_in = len(hbm_inputs)
    return pl.pallas_call(
        functools.partial(_time_slice_kernel, n_in=n_in, n_vmem=len(vmem_inputs), n_out=n_out, N1=N1,
                          compute=compute),
        out_shape=(shape,) * n_out, grid=(LANES,),
        in_specs=[any_spec] * n_in + list(vmem_specs),
        out_specs=(any_spec,) * n_out,
        scratch_shapes=[pltpu.VMEM((2, n_in, N1, C), F32), pltpu.VMEM((2, n_out, N1, C), F32),
                        pltpu.SemaphoreType.DMA((2,)), pltpu.SemaphoreType.DMA((2,))],
        compiler_params=_cparams(("arbitrary",)), name=name,
    )(*hbm_inputs, *vmem_inputs)


def _fft_a_compute(xs, vmem):
    m_ref, c_ref, s_ref = vmem
    r = _dot(m_ref[...], xs[0].astype(BF16))
    n1 = r.shape[0] // 2
    re, im = r[:n1], r[n1:]
    c, s = c_ref[...], s_ref[...]
    return re * c + im * s, im * c - re * s


def _fft_stage_a(x3, mat, tw, *, N1, C):
    twspec = pl.BlockSpec((None, N1, 1), lambda j: (j, 0, 0))
    return _time_slice_call(_fft_a_compute, [x3], [mat, tw["c_a"], tw["s_a"]],
                            [pl.BlockSpec(mat.shape, lambda j: (0, 0)), twspec, twspec], 2,
                            N1=N1, C=C, name="hyena_fft_a")


def _fft_mid_kernel(ar, ai, tr, ti, ssq, f_ref, i_ref, c_ref, s_ref, br, bi):
    N2 = LANES
    x = _dot(f_ref[...], jnp.concatenate([ar[...], ai[...]], axis=0).astype(BF16))
    h = _dot(f_ref[...], jnp.concatenate([tr[...], ti[...]], axis=0).astype(BF16))
    scale = lax.rsqrt(ssq[0] + ssq[1] + EPS)
    xr, xi, hr, hi = x[:N2], x[N2:], h[:N2] * scale, h[N2:] * scale
    y = jnp.concatenate([xr * hr - xi * hi, xr * hi + xi * hr], axis=0)
    r = _dot(i_ref[...], y.astype(BF16))
    re, im = r[:N2], r[N2:]
    c, s = c_ref[...], s_ref[...]
    br[...] = re * c - im * s
    bi[...] = im * c + re * s


def _fft_mid(Ar, Ai, Tr, Ti, ssq, consts, *, C, ct):
    N1 = consts["N1"]
    blk = pl.BlockSpec((None, LANES, ct), lambda k1, c: (k1, 0, c))
    const = pl.BlockSpec((2 * LANES, 2 * LANES), lambda k1, c: (0, 0))
    twspec = pl.BlockSpec((None, LANES, 1), lambda k1, c: (k1, 0, 0))
    shape = jax.ShapeDtypeStruct((N1, LANES, C), F32)
    return pl.pallas_call(
        _fft_mid_kernel, out_shape=(shape, shape), grid=(N1, C // ct),
        in_specs=[blk, blk, blk, blk, pl.BlockSpec((2, 1, ct), lambda k1, c: (0, 0, c)),
                  const, const, twspec, twspec],
        out_specs=(blk, blk),
        compiler_params=_cparams(("parallel", "parallel")), name="hyena_fft_mid",
    )(Ar, Ai, Tr, Ti, ssq, consts["b_fwd"], consts["b_inv"], consts["tw"]["c_b"], consts["tw"]["s_b"])


def _fft_fin_compute(xs, vmem):
    br, bi, x0, z = xs
    m_ref, bias = vmem
    r = _dot(m_ref[...], jnp.concatenate([br, bi], axis=0).astype(BF16))
    return (x0 * (r + bias[...] * z),)


def _fft_final(Br, Bi, x0_3d, z_3d, bias, consts, *, C):
    mat = consts["fin"]
    return _time_slice_call(_fft_fin_compute, [Br, Bi, x0_3d, z_3d], [mat, bias],
                            [pl.BlockSpec(mat.shape, lambda j: (0, 0)), pl.BlockSpec((1, C), lambda j: (0, 0))],
                            1, N1=consts["N1"], C=C, name="hyena_fft_final")[0]


def _hyena_latent(x0, z, taps, ssq, bias, consts, *, B, L, C):
    assert B == 2
    N1 = consts["N1"]
    ct = C
    T = x0.shape[0]
    z3 = z.reshape(T // LANES, LANES, C)
    x03 = x0.reshape(T // LANES, LANES, C)
    Ar, Ai = _fft_stage_a(z3, consts["a_data"], consts["tw"], N1=N1, C=C)
    Tr, Ti = _fft_stage_a(taps.reshape(N1, LANES, C), consts["a_taps"], consts["tw"], N1=N1, C=C)
    Br, Bi = _fft_mid(Ar, Ai, Tr, Ti, ssq, consts, C=C, ct=ct)
    y = _fft_final(Br, Bi, x03, z3, bias.reshape(1, C).astype(F32), consts, C=C)
    return y.reshape(B * L, C)


def _hyena_ctx_kernel(z, x0, taps, ssq, bias, dh, dl, th, tl, ih, il, y, *, Lc):
    n = 2 * Lc
    x = _dot3c(dh[...], dl[...], z[...])
    h = _dot3c(th[...], tl[...], taps[...])
    scale = lax.rsqrt(ssq[0] + ssq[1] + EPS)
    xr, xi, hr, hi = x[:n], x[n:], h[:n] * scale, h[n:] * scale
    r = _dot3c(ih[...], il[...], jnp.concatenate([xr * hr - xi * hi, xr * hi + xi * hr], axis=0))
    y[...] = x0[...] * (r + bias[...] * z[...])


def _hyena_ctx(x0, z, taps, ssq, bias, *, B, L, Lc, C):
    assert B == 2 and (B * L) % (2 * Lc) == 0
    n = 2 * Lc
    idx = np.arange(n)
    F = np.exp(-2j * np.pi * np.outer(idx, idx) / n)
    d = _np_split(_cblock(F[:, :Lc]))
    t = _np_split(np.concatenate([F.real, F.imag], axis=0))
    inv = _np_split(_cblock(np.conj(F)[:Lc, :]) / n)
    ct = 256
    r0 = (B * L) // n
    both = pl.BlockSpec((n, ct), lambda c: (r0, c))
    const = lambda m: pl.BlockSpec(m.shape, lambda c: (0, 0))
    return pl.pallas_call(
        functools.partial(_hyena_ctx_kernel, Lc=Lc),
        out_shape=jax.ShapeDtypeStruct((n, C), F32),
        grid=(C // ct,),
        in_specs=[both, both,
                  pl.BlockSpec((n, ct), lambda c: (0, c)),
                  pl.BlockSpec((2, 1, ct), lambda c: (0, 0, c)),
                  pl.BlockSpec((1, ct), lambda c: (0, c)),
                  const(d[0]), const(d[1]), const(t[0]), const(t[1]), const(inv[0]), const(inv[1])],
        out_specs=pl.BlockSpec((n, ct), lambda c: (0, c)),
        compiler_params=_cparams(("parallel",)), name="hyena_ctx",
    )(z, x0, taps, ssq, bias.reshape(1, C).astype(F32), *d, *t, *inv)


def _merge_kernel(*refs, with_ctx, n_lat_tiles):
    if with_ctx:
        (ym, ya, yh, ymc, yac, yhc, gm, ga, gh, mb, wm, wa, wh, wo, x_ref, g1, o_ref) = refs
        is_ctx = pl.program_id(0) >= n_lat_tiles
        pick = lambda lat, ctx: jnp.where(is_ctx, ctx[...], lat[...])
        m, a, h = pick(ym, ymc), pick(ya, yac), pick(yh, yhc)
    else:
        (ym, ya, yh, gm, ga, gh, mb, wm, wa, wh, wo, x_ref, g1, o_ref) = refs
        m, a, h = ym[...], ya[...], yh[...]
    D = x_ref.shape[1]
    bias = mb[...]
    gate = lambda g, k: _sigmoid(g[...].astype(F32) + bias[:, k * D:(k + 1) * D])
    y = (gate(gm, 0) * _dot(m, wm[...]) + gate(ga, 1) * _dot(a, wa[...])
         + gate(gh, 2) * _dot(h.astype(BF16), wh[...]))
    o_ref[...] = x_ref[...] + g1[...] * _dot(y.astype(BF16), wo[...])


def _merge(X, P, gcol0, branches, merge_b, weights, mod, gate_idx, *, n_rows, B, L, with_ctx):
    D = X.shape[1]
    tm = ROW_TILE
    nlt, tps = (B * L) // tm, L // tm
    grp = functools.partial(_group_of_tile, n_lat_tiles=nlt, tiles_per_seq=tps, n_batch=B)
    lat = pl.BlockSpec((tm, D), lambda i: (jnp.minimum(i, nlt - 1), 0))
    ctx = pl.BlockSpec((tm, D), lambda i: (jnp.maximum(i - nlt, 0), 0))
    gb = gcol0 // D
    gspec = lambda k: pl.BlockSpec((tm, D), lambda i: (i, gb + k))
    wspec = pl.BlockSpec((D, D), lambda i: (0, 0))
    in_specs = ([lat] * 3 + ([ctx] * 3 if with_ctx else []) + [gspec(0), gspec(1), gspec(2)]
                + [pl.BlockSpec((1, 3 * D), lambda i: (0, 0))] + [wspec] * 4
                + [pl.BlockSpec((tm, D), lambda i: (i, 0)),
                   pl.BlockSpec((None, None, 1, D), lambda i: (grp(i), gate_idx, 0, 0))])
    return pl.pallas_call(
        functools.partial(_merge_kernel, with_ctx=with_ctx, n_lat_tiles=nlt),
        out_shape=jax.ShapeDtypeStruct((n_rows, D), F32),
        grid=(n_rows // tm,), in_specs=in_specs,
        out_specs=pl.BlockSpec((tm, D), lambda i: (i, 0)),
        compiler_params=_cparams(("parallel",)), name="merge",
    )(*branches, P, P, P, merge_b.reshape(1, 3 * D).astype(F32), *weights, X, mod)


EXPERT_BITS = 5
assert 1 << EXPERT_BITS == N_EXPERTS
DMA_UNROLL = 8


def _slot_of(code_ref, pstart_ref, pos):
    code = code_ref[pos]
    return pstart_ref[code & (N_EXPERTS - 1)] + (code >> EXPERT_BITS)


def _dispatch_kernel(code, pstart, tok_ref, xs_zero, xs_out, sem, *, n_tok):
    del xs_zero
    R = tok_ref.shape[0]
    base = pl.program_id(0) * R

    def copy(pos, r):
        return pltpu.make_async_copy(tok_ref.at[pl.ds(r, 1), :],
                                     xs_out.at[pl.ds(_slot_of(code, pstart, pos), 1), :], sem)

    for wait in (False, True):
        for k in range(TOP_K):
            def body(r, carry):
                cp = copy(k * n_tok + base + r, r)
                cp.wait() if wait else cp.start()
                return carry
            lax.fori_loop(0, R, body, 0, unroll=DMA_UNROLL)


def _moe_dispatch(tok, code, pstart, P):
    N, D = tok.shape
    tm = MOE_TOKEN_TILE
    return pl.pallas_call(
        functools.partial(_dispatch_kernel, n_tok=N),
        out_shape=jax.ShapeDtypeStruct((P, D), F32),
        grid_spec=pltpu.PrefetchScalarGridSpec(
            num_scalar_prefetch=2, grid=(N // tm,),
            in_specs=[pl.BlockSpec((tm, D), lambda i, cd, ps: (i, 0)),
                      pl.BlockSpec(memory_space=pl.ANY)],
            out_specs=pl.BlockSpec(memory_space=pl.ANY),
            scratch_shapes=[pltpu.SemaphoreType.DMA]),
        input_output_aliases={3: 0},
        compiler_params=_cparams(("arbitrary",)), name="moe_dispatch",
    )(code, pstart, tok, jnp.zeros((P, D), F32))


def _moe_kernel(blk_e, xs_ref, w1_ref, w3_ref, w2_ref, y_ref, w1b, w3b, w2b):
    i = pl.program_id(0)

    @pl.when((i == 0) | (blk_e[i] != blk_e[jnp.maximum(i - 1, 0)]))
    def _():
        w1b[...] = w1_ref[...].astype(BF16)
        w3b[...] = w3_ref[...].astype(BF16)
        w2b[...] = w2_ref[...].astype(BF16)

    x = xs_ref[...].astype(BF16)
    a = _dot(x, w1b[...])
    g = (a * _sigmoid(a)) * _dot(x, w3b[...])
    y_ref[...] = _dot(g.astype(BF16), w2b[...])


def _moe_experts(xs, blk_e, w1, w3, w2):
    P, D = xs.shape
    E, _, De = w1.shape
    return pl.pallas_call(
        _moe_kernel,
        out_shape=jax.ShapeDtypeStruct((P, D), F32),
        grid_spec=pltpu.PrefetchScalarGridSpec(
            num_scalar_prefetch=1, grid=(P // MOE_BLOCK,),
            in_specs=[pl.BlockSpec((MOE_BLOCK, D), lambda i, be: (i, 0)),
                      pl.BlockSpec((None, D, De), lambda i, be: (be[i], 0, 0)),
                      pl.BlockSpec((None, D, De), lambda i, be: (be[i], 0, 0)),
                      pl.BlockSpec((None, De, D), lambda i, be: (be[i], 0, 0))],
            out_specs=pl.BlockSpec((MOE_BLOCK, D), lambda i, be: (i, 0)),
            scratch_shapes=[pltpu.VMEM((D, De), BF16), pltpu.VMEM((D, De), BF16), pltpu.VMEM((De, D), BF16)]),
        compiler_params=_cparams(("arbitrary",)), name="moe_experts",
    )(blk_e, xs, w1, w3, w2)


def _combine_kernel(code, pstart, yb_hbm, x_ref, g2, wt_ref, o_ref, buf, sems, *, n_tok):
    i = pl.program_id(0)
    n = pl.num_programs(0)
    slot = i % 2
    R = x_ref.shape[0]

    def gather(tile, slot, wait):
        for k in range(TOP_K):
            def body(r, carry):
                src = _slot_of(code, pstart, k * n_tok + tile * R + r)
                cp = pltpu.make_async_copy(yb_hbm.at[pl.ds(src, 1), :],
                                           buf.at[slot, k, pl.ds(r, 1), :], sems.at[slot])
                cp.wait() if wait else cp.start()
                return carry
            lax.fori_loop(0, R, body, 0, unroll=DMA_UNROLL)

    @pl.when(i == 0)
    def _():
        gather(0, 0, False)

    @pl.when(i + 1 < n)
    def _():
        gather(i + 1, 1 - slot, False)

    gather(i, slot, True)
    w = wt_ref[...]
    acc = w[:, 0:1] * buf[slot, 0]
    for k in range(1, TOP_K):
        acc = acc + w[:, k:k + 1] * buf[slot, k]
    o_ref[...] = x_ref[...] + g2[...] * acc


def _moe_combine(yb, code, pstart, wts, X, mod, gate_idx, *, n_rows, B, L):
    D = X.shape[1]
    tm = MOE_TOKEN_TILE
    nlt, tps = (B * L) // tm, L // tm
    grp = functools.partial(_group_of_tile, n_lat_tiles=nlt, tiles_per_seq=tps, n_batch=B)
    return pl.pallas_call(
        functools.partial(_combine_kernel, n_tok=n_rows),
        out_shape=jax.ShapeDtypeStruct((n_rows, D), F32),
        grid_spec=pltpu.PrefetchScalarGridSpec(
            num_scalar_prefetch=2, grid=(n_rows // tm,),
            in_specs=[pl.BlockSpec(memory_space=pl.ANY),
                      pl.BlockSpec((tm, D), lambda i, cd, ps: (i, 0)),
                      pl.BlockSpec((None, None, 1, D), lambda i, cd, ps: (grp(i), gate_idx, 0, 0)),
                      pl.BlockSpec((tm, LANES), lambda i, cd, ps: (i, 0))],
            out_specs=pl.BlockSpec((tm, D), lambda i, cd, ps: (i, 0)),
            scratch_shapes=[pltpu.VMEM((2, TOP_K, tm, D), F32), pltpu.SemaphoreType.DMA((2,))]),
        compiler_params=_cparams(("arbitrary",)), name="moe_combine",
    )(code, pstart, yb, X, mod, wts)


def _dispatch_tables(code, counts, n_tok):
    E, K = N_EXPERTS, TOP_K
    cnt = counts[0, :E].astype(jnp.int32)
    padded = (cnt + MOE_BLOCK - 1) // MOE_BLOCK * MOE_BLOCK
    pend = jnp.cumsum(padded)
    pstart = (pend - padded).astype(jnp.int32)
    P = -(-(n_tok * K + E * (MOE_BLOCK - 1)) // MOE_BLOCK) * MOE_BLOCK
    blk_row = jnp.arange(P // MOE_BLOCK, dtype=jnp.int32) * MOE_BLOCK
    blk_e = jnp.minimum(jnp.sum(pend[None, :] <= blk_row[:, None], axis=1), E - 1).astype(jnp.int32)
    return code[:, :K].T.reshape(-1), pstart, P, blk_e


def kernel(x, c, ctx, c_ctx, mod_w, mod_b, norm1_g, norm2_g, w_in, merge_b, m_conv_w, m_conv_b, m_gate_b, m_norm_g, a_qnorm_g, a_knorm_g, a_lambda, a_subln_g, h_conv_w, h_conv_b, h_ffn_w1, h_ffn_b1, h_ffn_w2, h_ffn_b2, h_ffn_w3, h_freq, h_decay, h_bias, w_br_m, w_br_a, w_br_h, w_out, router_w, router_b, e_w1, e_w3, e_w2):
    B, L, D = x.shape
    Lc = ctx.shape[1]
    depth = mod_w.shape[0]
    NL, NC = B * L, B * Lc
    T = NL + NC
    assert B + 1 <= 8 and Lc == ROW_TILE and L % min(L, 1024) == 0 and L % 512 == 0
    assert NL % Lc == 0 and D % LANES == 0
    width = D
    a_dh = D // (2 * A_HEADS)
    n_gates = 4 * M_HEADS
    c_mq, c_mv, c_mo = 0, 2 * width, 3 * width
    c_aq, c_ak, c_av = 4 * width, 5 * width, 6 * width
    c_hx = 7 * width
    c_gp = 10 * width

    X = jnp.concatenate([x.reshape(NL, D), ctx.reshape(NC, D)], axis=0)
    c8 = jnp.zeros((8, D), F32).at[:B].set(c).at[B].set(c_ctx)
    cos_tab, sin_tab = _rope_tables(L, a_dh, ROW_TILE)
    fft_consts = _dft_consts(L)
    g0 = 4 * width
    w_main = _w_in_prep(w_in, g0, n_gates)

    for l in range(depth):
        need_ctx = l < depth - 1
        lam_init = 0.8 - 0.6 * math.exp(-0.3 * l)
        mod = _modulation(c8, mod_w[l], mod_b[l])[:B + 1].reshape(B + 1, 6, 1, D)

        w_gate = jnp.pad(w_in[l][:, g0:g0 + n_gates], ((0, 0), (0, LANES - n_gates))).astype(BF16)
        P, G = _inproj(X, norm1_g[l], mod, w_main[l], w_gate, L=L, B=B)

        QK = _mlstm_qk(P, m_conv_w[l], m_conv_b[l], L=L, Lc=Lc, NL=NL, width=width)
        ym, ymc = _mlstm(QK, P, G, m_gate_b[l], m_norm_g[l], B=B, L=L, Lc=Lc, width=width)

        Qh = _attn_prep(P, c_aq, a_qnorm_g[l], cos_tab, sin_tab, a_dh ** -0.5 * LOG2E, L=L, NL=NL, dh=a_dh)
        Kh = _attn_prep(P, c_ak, a_knorm_g[l], cos_tab, sin_tab, 1.0, L=L, NL=NL, dh=a_dh)
        attn = functools.partial(_attention, Qh, Kh, P, c_av, a_lambda[l], a_subln_g[l], lam_init,
                                 B=B, L=L, Lc=Lc, dh=a_dh)
        ya = attn(latent=True)

        x0, z = _hyena_pre(P, c_hx, h_conv_w[l], h_conv_b[l], L=L, Lc=Lc, NL=NL, C=width)
        filt = (h_ffn_w1[l], h_ffn_b1[l], h_ffn_w2[l], h_ffn_b2[l], h_ffn_w3[l], h_freq[l], h_decay[l])
        taps, ssq = _hyena_filters(L, *filt, width)
        yh = _hyena_latent(x0, z, taps, ssq, h_bias[l], fft_consts, B=B, L=L, C=width)

        weights = [w.astype(BF16) for w in (w_br_m[l], w_br_a[l], w_br_h[l], w_out[l])]
        if need_ctx:
            yac = attn(latent=False)
            taps_c, ssq_c = _hyena_filters(Lc, *filt, width)
            yhc = _hyena_ctx(x0, z, taps_c, ssq_c, h_bias[l], B=B, L=L, Lc=Lc, C=width)
            branches, n_rows = [ym, ya, yh, ymc, yac, yhc], T
        else:
            branches, n_rows = [ym, ya, yh], NL
        X = _merge(X, P, c_gp, branches, merge_b[l], weights, mod, 2,
                   n_rows=n_rows, B=B, L=L, with_ctx=need_ctx)

        tok, code, wts, counts = _norm_mod(X, norm2_g[l], mod, 3, 4, n_rows=n_rows, L=L, B=B,
                                           router=(router_w, router_b))
        code, pstart, n_slots, blk_e = _dispatch_tables(code, counts, n_rows)
        xs = _moe_dispatch(tok, code, pstart, n_slots)
        yb = _moe_experts(xs, blk_e, e_w1[l], e_w3[l], e_w2[l])
        X = _moe_combine(yb, code, pstart, wts, X, mod, 5, n_rows=n_rows, B=B, L=L)
    return X[:NL].reshape(B, L, D)
```

```python
import functools
import math

import numpy as np
import jax
import jax.numpy as jnp
from jax import lax
from jax.experimental import pallas as pl
from jax.experimental.pallas import tpu as pltpu

F32 = jnp.float32
BF16 = jnp.bfloat16

GRID_W = 64
EPS = 1e-6
M_HEADS = 4
M_CHUNK = 128
A_HEADS = 8
ROPE_BASE = 10000.0
H_EMB = 33
H_FFN = 64
SHORT_CONV = 3
N_EXPERTS = 32
N_GROUPS = 4
TOP_K = 2
MOE_BLOCK = 256
MOE_TOKEN_TILE = 128

LANES = 128
BF16_SUBLANES = 16
V7X_VMEM_BYTES = 64 * 1024 * 1024
VMEM_LIMIT = V7X_VMEM_BYTES * 7 // 8

ROW_TILE = 256
LOG2E = 1.4426950408889634


def _cparams(sem):
    return pltpu.CompilerParams(dimension_semantics=sem, vmem_limit_bytes=VMEM_LIMIT)


def _split(x):
    hi = x.astype(BF16)
    lo = (x - hi.astype(F32)).astype(BF16)
    return hi, lo


def _dot(a, b):
    return jnp.dot(a, b, preferred_element_type=F32)


def _dot3(a, b):
    ah, al = _split(a)
    bh, bl = _split(b)
    return _dot(ah, bh) + _dot(al, bh) + _dot(ah, bl)


def _dot3c(ch, cl, x):
    xh, xl = _split(x)
    return _dot(ch, xh) + _dot(cl, xh) + _dot(ch, xl)


def _np_split(a):
    a = jnp.asarray(np.asarray(a, np.float32))
    hi = a.astype(BF16)
    lo = (a - hi.astype(F32)).astype(BF16)
    return hi, lo


def _sigmoid(x):
    return 1.0 / (1.0 + jnp.exp(-x))


def _log_sigmoid(x):
    return jnp.minimum(x, 0.0) - jnp.log(1.0 + jnp.exp(-jnp.abs(x)))


def _mod_kernel(c_ref, w_ref, b_ref, o_ref):
    c = c_ref[...]
    o_ref[...] = _dot3(c * _sigmoid(c), w_ref[...]) + b_ref[...]


def _modulation(c8, w, b, layer):
    depth, D, N = w.shape
    tn = 1536
    return pl.pallas_call(
        _mod_kernel,
        out_shape=jax.ShapeDtypeStruct((8, N), F32),
        grid=(N // tn,),
        in_specs=[pl.BlockSpec((8, D), lambda j: (0, 0)),
                  pl.BlockSpec((None, D, tn), lambda j: (layer, 0, j)),
                  pl.BlockSpec((None, 1, tn), lambda j: (layer, 0, j))],
        out_specs=pl.BlockSpec((8, tn), lambda j: (0, j)),
        compiler_params=_cparams(("parallel",)),
        name="modulation",
    )(c8, w, b.reshape(depth, 1, N))


def _norm_mod_router_kernel(x_ref, g_ref, sh_ref, sc_ref, rw_ref, rb_ref, o_ref, code_ref, wt_ref, cnt_ref, carry):
    E, G = N_EXPERTS, N_GROUPS
    gs = E // G

    @pl.when(pl.program_id(0) == 0)
    def _():
        carry[...] = jnp.zeros_like(carry)

    x = x_ref[...]
    y = x * lax.rsqrt(jnp.mean(x * x, axis=-1, keepdims=True) + EPS) * g_ref[...]
    h = y * (1.0 + sc_ref[...]) + sh_ref[...]
    o_ref[...] = h
    tm = h.shape[0]
    s = _sigmoid(_dot3(h, rw_ref[...]))
    lane = lax.broadcasted_iota(jnp.int32, (1, LANES), 1)
    lane_f = lane.astype(F32)
    sb = jnp.where(lane < E, s + rb_ref[...], -jnp.inf)
    far = float(LANES)
    best = jnp.full((tm, 1), -jnp.inf, F32)
    e1 = jnp.zeros((tm, 1), F32)
    e2 = jnp.zeros((tm, 1), F32)
    for g in range(G):
        mg = jnp.where((lane >= g * gs) & (lane < (g + 1) * gs), sb, -jnp.inf)
        m1 = jnp.max(mg, axis=-1, keepdims=True)
        i1 = jnp.min(jnp.where(mg == m1, lane_f, far), axis=-1, keepdims=True)
        mg2 = jnp.where(lane_f == i1, -jnp.inf, mg)
        m2 = jnp.max(mg2, axis=-1, keepdims=True)
        i2 = jnp.min(jnp.where(mg2 == m2, lane_f, far), axis=-1, keepdims=True)
        score = m1 + m2
        take = score > best
        best = jnp.where(take, score, best)
        e1 = jnp.where(take, i1, e1)
        e2 = jnp.where(take, i2, e2)
    oh1 = lane_f == e1
    oh2 = lane_f == e2
    s1 = jnp.sum(jnp.where(oh1, s, 0.0), axis=-1, keepdims=True)
    s2 = jnp.sum(jnp.where(oh2, s, 0.0), axis=-1, keepdims=True)
    den = s1 + s2
    r = lax.broadcasted_iota(jnp.int32, (tm, tm), 0)
    c = lax.broadcasted_iota(jnp.int32, (tm, tm), 1)
    lower = (c < r).astype(BF16)
    o1 = oh1.astype(F32)
    o2 = oh2.astype(F32)
    cum1 = _dot(lower, o1.astype(BF16))
    cum2 = _dot(lower, o2.astype(BF16))
    tot1 = jnp.sum(o1, axis=0, keepdims=True)
    base = carry[...]
    rank1 = jnp.sum(jnp.where(oh1, base + cum1, 0.0), axis=-1, keepdims=True)
    rank2 = jnp.sum(jnp.where(oh2, base + tot1 + cum2, 0.0), axis=-1, keepdims=True)
    total = base + tot1 + jnp.sum(o2, axis=0, keepdims=True)
    carry[...] = total
    cnt_ref[...] = total
    code1 = (rank1 * E + e1).astype(jnp.int32)
    code2 = (rank2 * E + e2).astype(jnp.int32)
    code_ref[...] = jnp.where(lane == 0, code1, jnp.where(lane == 1, code2, 0))
    wt_ref[...] = jnp.where(lane == 0, s1 / den, jnp.where(lane == 1, s2 / den, 0.0))


def _group_of_tile(i, n_lat_tiles, tiles_per_seq, n_batch):
    return jnp.where(i < n_lat_tiles, i // tiles_per_seq, n_batch)


def _norm_mod(x, g, mod, shift_idx, scale_idx, *, n_rows, L, B, router):
    D = x.shape[1]
    tm = ROW_TILE
    nlt, tps = (B * L) // tm, L // tm
    grp = functools.partial(_group_of_tile, n_lat_tiles=nlt, tiles_per_seq=tps, n_batch=B)
    in_specs = [pl.BlockSpec((tm, D), lambda i: (i, 0)),
                pl.BlockSpec((1, D), lambda i: (0, 0)),
                pl.BlockSpec((None, None, 1, D), lambda i: (grp(i), shift_idx, 0, 0)),
                pl.BlockSpec((None, None, 1, D), lambda i: (grp(i), scale_idx, 0, 0))]
    args = [x, g.reshape(1, D), mod, mod]
    router_w, router_b = router
    E = router_w.shape[1]
    assert E == N_EXPERTS
    rw = jnp.pad(router_w, ((0, 0), (0, LANES - E)))
    rb = jnp.pad(router_b.astype(F32).reshape(1, E), ((0, 0), (0, LANES - E)))
    lanes = pl.BlockSpec((tm, LANES), lambda i: (i, 0))
    return pl.pallas_call(
        _norm_mod_router_kernel,
        out_shape=(jax.ShapeDtypeStruct((n_rows, D), F32),
                   jax.ShapeDtypeStruct((n_rows, LANES), jnp.int32),
                   jax.ShapeDtypeStruct((n_rows, LANES), F32),
                   jax.ShapeDtypeStruct((1, LANES), F32)),
        grid=(n_rows // tm,),
        in_specs=in_specs + [pl.BlockSpec((D, LANES), lambda i: (0, 0)),
                             pl.BlockSpec((1, LANES), lambda i: (0, 0))],
        out_specs=(pl.BlockSpec((tm, D), lambda i: (i, 0)), lanes, lanes,
                   pl.BlockSpec((1, LANES), lambda i: (0, 0))),
        scratch_shapes=[pltpu.VMEM((1, LANES), F32)],
        compiler_params=_cparams(("arbitrary",)), name="norm_mod_router",
    )(*args, rw, rb)


def _largest_tile(n, cap, step):
    return max(t for t in range(step, cap + 1, step) if n % t == 0)


def _inproj_kernel(x_ref, g_ref, mod_ref, w_ref, wg_ref, o_ref, gate_ref, h_scr, *, L, n_batch):
    tm = x_ref.shape[0]

    @pl.when(pl.program_id(1) == 0)
    def _():
        x = x_ref[...]
        y = x * lax.rsqrt(jnp.mean(x * x, axis=-1, keepdims=True) + EPS) * g_ref[...]
        row = pl.program_id(0) * tm + lax.broadcasted_iota(jnp.int32, (tm, 1), 0)
        shift, scale = mod_ref[n_batch, 0], mod_ref[n_batch, 1]
        for b in range(n_batch - 1, -1, -1):
            in_b = row < (b + 1) * L
            shift = jnp.where(in_b, mod_ref[b, 0], shift)
            scale = jnp.where(in_b, mod_ref[b, 1], scale)
        h_scr[...] = (y * (1.0 + scale) + shift).astype(h_scr.dtype)
        gate_ref[...] = _dot(h_scr[...], wg_ref[...])

    o_ref[...] = _dot(h_scr[...], w_ref[...]).astype(o_ref.dtype)


def _inproj(X, g, mod, w_main, w_gate, layer, *, L, B):
    T, D = X.shape
    N = w_main.shape[2]
    tm, tn = _largest_tile(T, 1536, ROW_TILE), 1024
    return pl.pallas_call(
        functools.partial(_inproj_kernel, L=L, n_batch=B),
        out_shape=(jax.ShapeDtypeStruct((T, N), BF16), jax.ShapeDtypeStruct((T, LANES), F32)),
        grid=(T // tm, N // tn),
        in_specs=[pl.BlockSpec((tm, D), lambda i, j: (i, 0)),
                  pl.BlockSpec((1, D), lambda i, j: (0, 0)),
                  pl.BlockSpec(mod.shape, lambda i, j: (0, 0, 0, 0)),
                  pl.BlockSpec((None, D, tn), lambda i, j: (layer, 0, j)),
                  pl.BlockSpec((None, D, LANES), lambda i, j: (layer, 0, 0))],
        out_specs=(pl.BlockSpec((tm, tn), lambda i, j: (i, j)),
                   pl.BlockSpec((tm, LANES), lambda i, j: (i, 0))),
        scratch_shapes=[pltpu.VMEM((tm, D), BF16)],
        compiler_params=_cparams(("parallel", "arbitrary")), name="inproj",
    )(X, g.reshape(1, D), mod, w_main, w_gate)


def _seq_edge_flags(tm, L, Lc, NL):
    r0 = pl.program_id(0) * tm
    lat = r0 < NL
    start = jnp.where(lat, r0 % L == 0, (r0 - NL) % Lc == 0)
    end = jnp.where(lat, (r0 + tm) % L == 0, (r0 + tm - NL) % Lc == 0)
    return jnp.where(start, 0.0, 1.0), jnp.where(end, 0.0, 1.0)


def _conv3(cur_ref, prev_ref, next_ref, w_ref, b_ref, keep_prev, keep_next):
    cur = cur_ref[...].astype(F32)
    tm = cur.shape[0]
    prev_row = prev_ref[BF16_SUBLANES - 1:BF16_SUBLANES, :].astype(F32) * keep_prev
    next_row = next_ref[0:1, :].astype(F32) * keep_next
    row = lax.broadcasted_iota(jnp.int32, (tm, 1), 0)
    up = jnp.where(row == 0, prev_row, pltpu.roll(cur, 1, 0))
    dn = jnp.where(row == tm - 1, next_row, pltpu.roll(cur, tm - 1, 0))
    return b_ref[...] + up * w_ref[0:1, :] + cur * w_ref[1:2, :] + dn * w_ref[2:3, :]


def _conv_specs(tm, tc, T, col_block):
    per = tm // BF16_SUBLANES
    last = T // BF16_SUBLANES - 1
    return [pl.BlockSpec((tm, tc), lambda i, j: (i, col_block + j)),
            pl.BlockSpec((BF16_SUBLANES, tc), lambda i, j: (jnp.maximum(i * per - 1, 0), col_block + j)),
            pl.BlockSpec((BF16_SUBLANES, tc), lambda i, j: (jnp.minimum((i + 1) * per, last), col_block + j))]


def _mconv_kernel(cur_ref, prev_ref, next_ref, w_ref, b_ref, scale_ref, o_ref, *, tm, L, Lc, NL):
    kp, kn = _seq_edge_flags(tm, L, Lc, NL)
    y = _conv3(cur_ref, prev_ref, next_ref, w_ref, b_ref, kp, kn)
    o_ref[...] = (y * _sigmoid(y) * scale_ref[...]).astype(o_ref.dtype)


def _mlstm_qk(P, conv_w, conv_b, *, L, Lc, NL, width):
    T = P.shape[0]
    tm, tc = ROW_TILE, 1024
    C = 2 * width
    scale = jnp.concatenate([jnp.ones((1, width), F32),
                             jnp.full((1, width), (width // M_HEADS) ** -0.5, F32)], axis=1)
    vec = lambda r: pl.BlockSpec((r, tc), lambda i, j: (0, j))
    return pl.pallas_call(
        functools.partial(_mconv_kernel, tm=tm, L=L, Lc=Lc, NL=NL),
        out_shape=jax.ShapeDtypeStruct((T, C), BF16),
        grid=(T // tm, C // tc),
        in_specs=_conv_specs(tm, tc, T, 0) + [vec(SHORT_CONV), vec(1), vec(1)],
        out_specs=pl.BlockSpec((tm, tc), lambda i, j: (i, j)),
        compiler_params=_cparams(("parallel", "parallel")), name="mlstm_qk_conv",
    )(P, P, P, conv_w, conv_b.reshape(1, C), scale)


def _hyena_pre_kernel(c0, p0, n0, c1, p1, n1, c2, p2, n2, w0, w1, w2, b0, b1, b2,
                      x0_ref, z_ref, *, tm, L, Lc, NL):
    kp, kn = _seq_edge_flags(tm, L, Lc, NL)
    x0 = _conv3(c0, p0, n0, w0, b0, kp, kn)
    x1 = _conv3(c1, p1, n1, w1, b1, kp, kn)
    v = _conv3(c2, p2, n2, w2, b2, kp, kn)
    x0_ref[...] = x0
    z_ref[...] = v * x1


def _hyena_pre(P, col0, conv_w, conv_b, *, L, Lc, NL, C):
    T = P.shape[0]
    tm, tc = ROW_TILE, 512
    nb = C // tc
    specs = []
    for part in range(3):
        specs += _conv_specs(tm, tc, T, col0 // tc + part * nb)
    wspecs = [pl.BlockSpec((SHORT_CONV, tc), lambda i, j, p=part: (0, p * nb + j)) for part in range(3)]
    bspecs = [pl.BlockSpec((1, tc), lambda i, j, p=part: (0, p * nb + j)) for part in range(3)]
    b2d = conv_b.reshape(1, 3 * C)
    out = pl.BlockSpec((tm, tc), lambda i, j: (i, j))
    return pl.pallas_call(
        functools.partial(_hyena_pre_kernel, tm=tm, L=L, Lc=Lc, NL=NL),
        out_shape=(jax.ShapeDtypeStruct((T, C), F32), jax.ShapeDtypeStruct((T, C), F32)),
        grid=(T // tm, nb),
        in_specs=specs + wspecs + bspecs,
        out_specs=(out, out),
        compiler_params=_cparams(("parallel", "parallel")), name="hyena_pre",
    )(*([P] * 9), conv_w, conv_w, conv_w, b2d, b2d, b2d)


def _mlstm_chunk(q, k, v, li_r, lf_r, li_c, lf_c, C_scr, n_scr, m_scr, fwd):
    Q = M_CHUNK
    row = lax.broadcasted_iota(jnp.int32, (Q, Q), 0)
    col = lax.broadcasted_iota(jnp.int32, (Q, Q), 1)
    mask = (col <= row) if fwd else (col >= row)
    tri_c = mask.astype(BF16)
    tri_r = ((row <= col) if fwd else (row >= col)).astype(BF16)
    lfc_h, lfc_l = _split(jnp.broadcast_to(lf_c, (Q, Q)))
    lfr_h, lfr_l = _split(jnp.broadcast_to(lf_r, (Q, Q)))
    b_cols = _dot(tri_c, lfc_h) + _dot(tri_c, lfc_l)
    b_rows = _dot(lfr_h, tri_r) + _dot(lfr_l, tri_r)
    dm = jnp.where(mask, b_cols - b_rows + li_r, -jnp.inf)
    m_prev = m_scr[0:1, 0:1]
    b_col = b_cols[:, 0:1]
    inter = b_col + m_prev
    mt = jnp.maximum(inter, jnp.max(dm, axis=-1, keepdims=True))
    s = lax.dot_general(q, k, (((1,), (1,)), ((), ())), preferred_element_type=F32) * jnp.exp(dm - mt)
    wi = jnp.exp(inter - mt)
    num = _dot(s.astype(BF16), v) + wi * _dot(q, C_scr[...].astype(BF16))
    qn = jnp.sum(q.astype(F32) * n_scr[...], axis=-1, keepdims=True)
    den = jnp.sum(s, axis=-1, keepdims=True) + wi * qn
    h = num / jnp.maximum(jnp.abs(den), jnp.exp(-mt))
    b_tot = b_cols[Q - 1:Q, 0:1] if fwd else b_cols[0:1, 0:1]
    ws = b_tot - b_col + li_c
    m_new = jnp.maximum(b_tot + m_prev, jnp.max(ws, axis=0, keepdims=True))
    decay = jnp.exp(b_tot + m_prev - m_new)
    kw = k.astype(F32) * jnp.exp(ws - m_new)
    C_scr[...] = decay * C_scr[...] + lax.dot_general(
        kw.astype(BF16), v, (((0,), (0,)), ((), ())), preferred_element_type=F32)
    n_scr[...] = decay * n_scr[...] + jnp.sum(kw, axis=0, keepdims=True)
    m_scr[...] = jnp.broadcast_to(m_new, m_scr.shape)
    return h


def _mlstm_kernel(*refs, fwd, SEG, Lc, H, dh):
    if fwd:
        (gr_ref, gc_ref, grc_ref, gcc_ref, q_ref, k_ref, v_ref, qc_ref, kc_ref, vc_ref,
         h_ref, hc_ref, *scr) = refs
    else:
        (gr_ref, gc_ref, grc_ref, gcc_ref, q_ref, k_ref, v_ref, qc_ref, kc_ref, vc_ref,
         o_ref, oc_ref, hf_ref, hfc_ref, ng_ref, y_ref, yc_ref, *scr) = refs
    C_scr, n_scr, m_scr = scr[0:H], scr[H:2 * H], scr[2 * H:3 * H]
    gi = 0 if fwd else 2

    def run(nchunks, qr, kr, vr, grr, gcr, emit):
        def body(j, carry):
            c = j if fwd else nchunks - 1 - j
            rows = pl.ds(pl.multiple_of(c * M_CHUNK, M_CHUNK), M_CHUNK)
            for hh in range(H):
                cols = slice(hh * dh, (hh + 1) * dh)
                gcs = gcr[hh, rows, :]
                h = _mlstm_chunk(qr[rows, cols], kr[rows, cols], vr[rows, cols],
                                 grr[hh, gi, pl.ds(c, 1), :], _log_sigmoid(grr[hh, gi + 1, pl.ds(c, 1), :]),
                                 gcs[:, gi:gi + 1], _log_sigmoid(gcs[:, gi + 1:gi + 2]),
                                 C_scr[hh], n_scr[hh], m_scr[hh], fwd)
                emit(rows, cols, h)
            return carry

        lax.fori_loop(0, nchunks, body, 0, unroll=2)

    def emitter(dst, hf=None, o=None):
        def emit(rows, cols, h):
            if not fwd:
                h = h + hf[rows, cols]
                y = h * lax.rsqrt(jnp.mean(h * h, axis=-1, keepdims=True) + EPS) * ng_ref[:, cols]
                h = (y * _sigmoid(o[rows, cols].astype(F32))).astype(dst.dtype)
            dst[rows, cols] = h
        return emit

    @pl.when(pl.program_id(1) == 0)
    def _():
        for r in scr:
            r[...] = jnp.zeros_like(r)
        run(Lc // M_CHUNK, qc_ref, kc_ref, vc_ref, grc_ref, gcc_ref,
            emitter(hc_ref) if fwd else emitter(yc_ref, hfc_ref, oc_ref))

    run(SEG // M_CHUNK, q_ref, k_ref, v_ref, gr_ref, gc_ref,
        emitter(h_ref) if fwd else emitter(y_ref, hf_ref, o_ref))


def _mlstm(QK, P, G, gate_b, norm_g, *, B, L, Lc, width):
    H = M_HEADS
    dh = width // H
    NL = B * L
    SEG = min(L, 1024)
    S = L // SEG
    g = G[:, :4 * H] + gate_b.astype(F32)[None, :]

    def gate_views(rows, n):
        a = rows.reshape(B, n, 4, H)
        return (a.transpose(0, 3, 2, 1).reshape(B, H, 4, n // M_CHUNK, M_CHUNK),
                a.transpose(0, 3, 1, 2))

    gr, gc = gate_views(g[:NL], L)
    grc, gcc = gate_views(g[NL:], Lc)
    scratch = ([pltpu.VMEM((dh, dh), F32)] * H + [pltpu.VMEM((1, dh), F32)] * H
               + [pltpu.VMEM((8, LANES), F32)] * H)

    def call(fwd, extra_in, extra_specs, out_dtype):
        seg_of = (lambda s: s) if fwd else (lambda s: S - 1 - s)
        lat = lambda cb: pl.BlockSpec((SEG, width), lambda b, s: (b * S + seg_of(s), cb))
        ctx = lambda cb: pl.BlockSpec((Lc, width), lambda b, s: (NL // Lc + b, cb))
        gate_specs = [
            pl.BlockSpec((None, H, 4, SEG // M_CHUNK, M_CHUNK), lambda b, s: (b, 0, 0, seg_of(s), 0)),
            pl.BlockSpec((None, H, SEG, 4), lambda b, s: (b, 0, seg_of(s), 0)),
            pl.BlockSpec((None, H, 4, Lc // M_CHUNK, M_CHUNK), lambda b, s: (b, 0, 0, 0, 0)),
            pl.BlockSpec((None, H, Lc, 4), lambda b, s: (b, 0, 0, 0))]
        lat_out = pl.BlockSpec((SEG, width), lambda b, s: (b * S + seg_of(s), 0))
        ctx_out = pl.BlockSpec((Lc, width), lambda b, s: (b, 0))
        return pl.pallas_call(
            functools.partial(_mlstm_kernel, fwd=fwd, SEG=SEG, Lc=Lc, H=H, dh=dh),
            out_shape=(jax.ShapeDtypeStruct((NL, width), out_dtype),
                       jax.ShapeDtypeStruct((B * Lc, width), out_dtype)),
            grid=(B, S),
            in_specs=(gate_specs + [lat(0), lat(1), lat(2), ctx(0), ctx(1), ctx(2)]
                      + extra_specs(lat, ctx, lat_out, ctx_out)),
            out_specs=(lat_out, ctx_out), scratch_shapes=scratch,
            compiler_params=_cparams(("parallel", "arbitrary")),
            name="mlstm_fwd" if fwd else "mlstm_bwd",
        )(gr, gc, grc, gcc, QK, QK, P, QK, QK, P, *extra_in)

    hf, hfc = call(True, [], lambda *_: [], F32)
    return call(False, [P, P, hf, hfc, jnp.tile(norm_g.astype(F32), H).reshape(1, width)],
                lambda lat, ctx, lat_out, ctx_out: [lat(3), ctx(3), lat_out, ctx_out,
                                                    pl.BlockSpec((1, width), lambda b, s: (0, 0))], BF16)


def _attn_prep_kernel(x_ref, g_ref, cos_ref, sin_ref, o_ref, *, scale, dh):
    n_blk = x_ref.shape[1] // LANES
    r = lax.broadcasted_iota(jnp.int32, (LANES, LANES), 0)
    c = lax.broadcasted_iota(jnp.int32, (LANES, LANES), 1)
    group = (r // dh == c // dh).astype(BF16)
    lane = lax.broadcasted_iota(jnp.int32, (1, LANES), 1)
    quarter = dh // 4
    first = (lane % (2 * quarter)) < quarter
    cos = cos_ref[...]
    sin = sin_ref[...]
    for hb in range(n_blk):
        cols = slice(hb * LANES, (hb + 1) * LANES)
        x = x_ref[:, cols].astype(F32)
        hi, lo = _split(x * x)
        ms = (_dot(hi, group) + _dot(lo, group)) * (1.0 / dh)
        y = x * lax.rsqrt(ms + EPS) * g_ref[:, cols]
        rot = jnp.where(first, -pltpu.roll(y, LANES - quarter, 1), pltpu.roll(y, quarter, 1))
        o_ref[:, cols] = ((y * cos + rot * sin) * scale).astype(o_ref.dtype)


def _attn_prep(P, col0, gain, cos_tab, sin_tab, scale, *, L, NL, dh):
    T = P.shape[0]
    W = A_HEADS * 2 * dh
    tm = ROW_TILE
    nlt, tps = NL // tm, L // tm
    tab = pl.BlockSpec((tm, LANES), lambda i: (jnp.where(i < nlt, i % tps, tps), 0))
    g = jnp.tile(gain.astype(F32), W // dh).reshape(1, W)
    return pl.pallas_call(
        functools.partial(_attn_prep_kernel, scale=scale, dh=dh),
        out_shape=jax.ShapeDtypeStruct((T, W), BF16),
        grid=(T // tm,),
        in_specs=[pl.BlockSpec((tm, W), lambda i: (i, col0 // W)),
                  pl.BlockSpec((1, W), lambda i: (0, 0)), tab, tab],
        out_specs=pl.BlockSpec((tm, W), lambda i: (i, 0)),
        compiler_params=_cparams(("parallel",)), name="attn_prep",
    )(P, g, cos_tab, sin_tab)


def _attn_kernel(*refs, n_lat, tk, L, Lc, dh, lam_init):
    if n_lat:
        lam_ref, q_ref, kl_ref, vl_ref, kc_ref, vc_ref, sg_ref, o_ref, vext, acc = refs
    else:
        lam_ref, q_ref, kc_ref, vc_ref, sg_ref, o_ref, vext, acc = refs
    dv = 2 * dh
    ctx0 = n_lat * tk

    @pl.when(pl.program_id(2) == 0)
    def _():
        if n_lat:
            vext[0:L, 0:dv] = vl_ref[...]
        vext[ctx0:ctx0 + Lc, 0:dv] = vc_ref[...]
        vext[:, dv:2 * dv] = jnp.ones((vext.shape[0], dv), BF16)

    q = q_ref[...]
    lane = lax.broadcasted_iota(jnp.int32, (1, dv), 1)
    qs = (jnp.where(lane < dh, q, jnp.zeros_like(q)), jnp.where(lane >= dh, q, jnp.zeros_like(q)))
    acc[...] = jnp.zeros_like(acc)
    tq = q.shape[0]

    def update(comp, kblk, vblk, m_old):
        s = lax.dot_general(qs[comp], kblk, (((1,), (1,)), ((), ())), preferred_element_type=F32)
        m_new = jnp.maximum(m_old, jnp.max(s, axis=-1, keepdims=True))
        p = jnp.exp2(s - m_new).astype(BF16)
        acc[comp] = jnp.exp2(m_old - m_new) * acc[comp] + _dot(p, vblk)
        return m_new

    m = (jnp.full((tq, 1), -jnp.inf, F32),) * 2
    for c in range(n_lat):
        kblk, vblk = kl_ref[c * tk:(c + 1) * tk, :], vext[c * tk:(c + 1) * tk, :]
        m = update(0, kblk, vblk, m[0]), update(1, kblk, vblk, m[1])
    kblk, vblk = kc_ref[...], vext[ctx0:ctx0 + Lc, :]
    update(0, kblk, vblk, m[0])
    update(1, kblk, vblk, m[1])

    lp = lam_ref[...]
    lam = (jnp.exp(jnp.sum(lp[0:1] * lp[1:2], axis=-1, keepdims=True))
           - jnp.exp(jnp.sum(lp[2:3] * lp[3:4], axis=-1, keepdims=True)) + lam_init)
    a0, a1 = acc[0], acc[1]
    o = a0[:, 0:dv] / a0[:, dv:dv + 1] - lam * (a1[:, 0:dv] / a1[:, dv:dv + 1])
    y = o * lax.rsqrt(jnp.mean(o * o, axis=-1, keepdims=True) + EPS) * sg_ref[...] * (1.0 - lam_init)
    o_ref[...] = y.astype(o_ref.dtype)


def _attention(Qh, Kh, P, vcol0, lam_p, sub_g, lam_init, *, B, L, Lc, dh, latent):
    NL = B * L
    dv = 2 * dh
    H = A_HEADS
    vb = vcol0 // dv
    if latent:
        tq, tk = 512, min(L, 1024)
        n_lat, nq, rows_out = L // tk, L // tq, NL
        q_spec = pl.BlockSpec((tq, dv), lambda b, h, i: (b * nq + i, h))
        lat_specs = [pl.BlockSpec((L, dv), lambda b, h, i: (b, h)),
                     pl.BlockSpec((L, dv), lambda b, h, i: (b, vb + h))]
        lat_args = [Kh, P]
        o_spec = pl.BlockSpec((tq, dv), lambda b, h, i: (b * nq + i, h))
        nkeys = L + Lc
    else:
        tq, tk = Lc, 512
        n_lat, nq, rows_out = 0, 1, B * Lc
        q_spec = pl.BlockSpec((Lc, dv), lambda b, h, i: (NL // Lc + b, h))
        lat_specs, lat_args = [], []
        o_spec = pl.BlockSpec((Lc, dv), lambda b, h, i: (b, h))
        nkeys = Lc
    ctx_specs = [pl.BlockSpec((Lc, dv), lambda b, h, i: (NL // Lc + b, h)),
                 pl.BlockSpec((Lc, dv), lambda b, h, i: (NL // Lc + b, vb + h))]
    return pl.pallas_call(
        functools.partial(_attn_kernel, n_lat=n_lat, tk=tk, L=L, Lc=Lc, dh=dh, lam_init=lam_init),
        out_shape=jax.ShapeDtypeStruct((rows_out, H * dv), BF16),
        grid=(B, H, nq),
        in_specs=[pl.BlockSpec((4, dh), lambda b, h, i: (0, 0)), q_spec] + lat_specs + ctx_specs
                 + [pl.BlockSpec((1, dv), lambda b, h, i: (0, 0))],
        out_specs=o_spec,
        scratch_shapes=[pltpu.VMEM((nkeys, 2 * dv), BF16), pltpu.VMEM((2, tq, 2 * dv), F32)],
        compiler_params=_cparams(("parallel", "parallel", "arbitrary")),
        name="diff_attn_latent" if latent else "diff_attn_ctx",
    )(lam_p.astype(F32), Qh, *lat_args, Kh, P, sub_g.reshape(1, dv).astype(F32))


def _rope_tables(L, dh, tm):
    rows = L // GRID_W
    row = np.repeat(np.arange(rows), GRID_W).astype(np.float64)
    col = np.tile(np.arange(GRID_W), rows).astype(np.float64)
    nf = dh // 4
    inv = (np.float32(ROPE_BASE) ** (-np.arange(nf, dtype=np.float32) / nf)).astype(np.float64)
    ang = np.concatenate([row[:, None] * inv] * 2 + [col[:, None] * inv] * 2, axis=-1)
    ang = np.tile(ang.astype(np.float32).astype(np.float64), (1, LANES // dh))
    cos = np.concatenate([np.cos(ang), np.ones((tm, LANES))], axis=0)
    sin = np.concatenate([np.sin(ang), np.zeros((tm, LANES))], axis=0)
    return jnp.asarray(cos, F32), jnp.asarray(sin, F32)


def _filter_kernel(f_ref, w1_ref, b1_ref, w2_ref, b2_ref, fr_ref, w3_ref, dl_ref, taps_ref, ssq_ref):
    f = f_ref[...]
    freq = fr_ref[...]
    h = jnp.sin(freq * (_dot3(f, w1_ref[...]) + b1_ref[...]))
    h = jnp.sin(freq * (_dot3(h, w2_ref[...]) + b2_ref[...]))
    h = _dot3(h, w3_ref[...]) * jnp.exp(-f[:, 0:1] * jnp.abs(dl_ref[...]))

    @pl.when(pl.program_id(1) == 0)
    def _():
        ssq_ref[...] = jnp.zeros_like(ssq_ref)

    ssq_ref[...] += jnp.sum(h * h, axis=0, keepdims=True)
    taps_ref[...] = h * f[:, H_EMB:H_EMB + 1]


def _filter_features(L):
    t = np.linspace(0.0, 1.0, L, dtype=np.float32).astype(np.float64)[:, None]
    bands = (H_EMB - 1) // 2
    w = (np.float32(2.0 * math.pi) * np.arange(L, dtype=np.float32) / np.float32(L)).astype(np.float64)[:, None]
    f = np.linspace(1e-4, bands - 1, bands, dtype=np.float32).astype(np.float64)[None, :]
    fw = (f.astype(np.float32) * w.astype(np.float32)).astype(np.float64)
    z = np.concatenate([t, np.cos(fw), -np.sin(fw)], axis=-1)
    feat = np.zeros((2 * L, LANES), np.float64)
    feat[:L, :H_EMB] = z
    idx = (L - np.arange(L)) % L
    feat[L:, :H_EMB] = z[idx]
    feat[:, H_EMB] = 1.0
    feat[L, H_EMB] = 0.0
    return jnp.asarray(feat, F32)


def _hyena_filters(L, w1, b1, w2, b2, w3, freq, delta, C):
    pad = LANES - H_FFN
    w1p = jnp.pad(w1, ((0, LANES - H_EMB), (0, pad)))
    w2p = jnp.pad(w2, ((0, pad), (0, pad)))
    w3p = jnp.pad(w3, ((0, pad), (0, 0)))
    row = lambda a: jnp.pad(a.reshape(1, H_FFN), ((0, 0), (0, pad)))
    tr = min(L, 512)
    nr = L // tr
    const = lambda shape: pl.BlockSpec(shape, lambda hf, r: (0, 0))
    return pl.pallas_call(
        _filter_kernel,
        out_shape=(jax.ShapeDtypeStruct((2 * L, C), F32), jax.ShapeDtypeStruct((2, 1, C), F32)),
        grid=(2, nr),
        in_specs=[pl.BlockSpec((tr, LANES), lambda hf, r: (hf * nr + r, 0)),
                  const((LANES, LANES)), const((1, LANES)), const((LANES, LANES)), const((1, LANES)),
                  const((1, LANES)),
                  pl.BlockSpec((LANES, C), lambda hf, r: (0, hf)),
                  pl.BlockSpec((1, C), lambda hf, r: (0, hf))],
        out_specs=(pl.BlockSpec((tr, C), lambda hf, r: (hf * nr + r, 0)),
                   pl.BlockSpec((None, 1, C), lambda hf, r: (hf, 0, 0))),
        compiler_params=_cparams(("parallel", "arbitrary")), name="hyena_filter",
    )(_filter_features(L), w1p, row(b1), w2p, row(b2), row(freq), w3p, delta.reshape(1, 2 * C))


def _cblock(m):
    return np.block([[m.real, -m.imag], [m.imag, m.real]])


def _dft_consts(L):
    N = 2 * L
    N2 = LANES
    N1 = N // N2
    half = N1 // 2
    n1 = np.arange(N1)
    n2 = np.arange(N2)
    F1 = np.exp(-2j * np.pi * np.outer(n1, n1) / N1)
    F2 = np.exp(-2j * np.pi * np.outer(n2, n2) / N2)
    a_data = _cblock(F1[:, :half])
    a_taps = np.concatenate([F1.real, F1.imag], axis=0)
    b_fwd = _cblock(F2)
    b_inv = _cblock(np.conj(F2))
    fin = _cblock(np.conj(F1)[:half, :]) / N
    ang = 2.0 * np.pi * np.outer(n2, n1) / N
    tw = dict(c_a=np.cos(ang)[:, :, None], s_a=np.sin(ang)[:, :, None],
              c_b=np.cos(ang.T)[:, :, None], s_b=np.sin(ang.T)[:, :, None])
    as_bf16 = lambda m: jnp.asarray(np.asarray(m, np.float32)).astype(BF16)
    return dict(N1=N1, a_data=as_bf16(a_data), a_taps=as_bf16(a_taps),
                b_fwd=as_bf16(b_fwd), b_inv=as_bf16(b_inv), fin=as_bf16(fin),
                tw={k: jnp.asarray(v, F32) for k, v in tw.items()})


def _time_slice_kernel(*refs, n_in, n_vmem, n_out, N1, compute):
    ins, vmem = refs[:n_in], refs[n_in:n_in + n_vmem]
    outs = refs[n_in + n_vmem:n_in + n_vmem + n_out]
    in_buf, out_buf, in_sem, out_sem = refs[n_in + n_vmem + n_out:]
    j = pl.program_id(0)
    n = pl.num_programs(0)
    slot = j % 2

    def in_copies(step, sl):
        return [pltpu.make_async_copy(src.at[pl.ds(0, N1), step, :], in_buf.at[sl, i], in_sem.at[sl])
                for i, src in enumerate(ins)]

    def out_copies(step, sl):
        return [pltpu.make_async_copy(out_buf.at[sl, i], dst.at[:, step, :], out_sem.at[sl])
                for i, dst in enumerate(outs)]

    @pl.when(j == 0)
    def _():
        for cp in in_copies(0, 0):
            cp.start()

    @pl.when(j + 1 < n)
    def _():
        for cp in in_copies(j + 1, 1 - slot):
            cp.start()

    for cp in in_copies(j, slot):
        cp.wait()

    @pl.when(j >= 2)
    def _():
        for cp in out_copies(j - 2, slot):
            cp.wait()

    for i, r in enumerate(compute([in_buf[slot, i] for i in range(n_in)], vmem)):
        out_buf[slot, i] = r
    for cp in out_copies(j, slot):
        cp.start()

    @pl.when(j == n - 1)
    def _():
        for cp in out_copies(j - 1, 1 - slot) + out_copies(j, slot):
            cp.wait()


def _time_slice_call(compute, hbm_inputs, vmem_inputs, vmem_specs, n_out, *, N1, C, name):
    assert LANES >= 2
    any_spec = pl.BlockSpec(memory_space=pl.ANY)
    shape = jax.ShapeDtypeStruct((N1, LANES, C), F32)
    n_in = len(hbm_inputs)
    return pl.pallas_call(
        functools.partial(_time_slice_kernel, n_in=n_in, n_vmem=len(vmem_inputs), n_out=n_out, N1=N1,
                          compute=compute),
        out_shape=(shape,) * n_out, grid=(LANES,),
        in_specs=[any_spec] * n_in + list(vmem_specs),
        out_specs=(any_spec,) * n_out,
        scratch_shapes=[pltpu.VMEM((2, n_in, N1, C), F32), pltpu.VMEM((2, n_out, N1, C), F32),
                        pltpu.SemaphoreType.DMA((2,)), pltpu.SemaphoreType.DMA((2,))],
        compiler_params=_cparams(("arbitrary",)), name=name,
    )(*hbm_inputs, *vmem_inputs)


def _fft_a_compute(xs, vmem):
    m_ref, c_ref, s_ref = vmem
    r = _dot(m_ref[...], xs[0].astype(BF16))
    n1 = r.shape[0] // 2
    re, im = r[:n1], r[n1:]
    c, s = c_ref[...], s_ref[...]
    return re * c + im * s, im * c - re * s


def _fft_stage_a(x3, mat, tw, *, N1, C):
    twspec = pl.BlockSpec((None, N1, 1), lambda j: (j, 0, 0))
    return _time_slice_call(_fft_a_compute, [x3], [mat, tw["c_a"], tw["s_a"]],
                            [pl.BlockSpec(mat.shape, lambda j: (0, 0)), twspec, twspec], 2,
                            N1=N1, C=C, name="hyena_fft_a")


def _fft_mid_kernel(ar, ai, tr, ti, ssq, f_ref, i_ref, c_ref, s_ref, br, bi):
    N2 = LANES
    x = _dot(f_ref[...], jnp.concatenate([ar[...], ai[...]], axis=0).astype(BF16))
    h = _dot(f_ref[...], jnp.concatenate([tr[...], ti[...]], axis=0).astype(BF16))
    scale = lax.rsqrt(ssq[0] + ssq[1] + EPS)
    xr, xi, hr, hi = x[:N2], x[N2:], h[:N2] * scale, h[N2:] * scale
    y = jnp.concatenate([xr * hr - xi * hi, xr * hi + xi * hr], axis=0)
    r = _dot(i_ref[...], y.astype(BF16))
    re, im = r[:N2], r[N2:]
    c, s = c_ref[...], s_ref[...]
    br[...] = re * c - im * s
    bi[...] = im * c + re * s


def _fft_mid(Ar, Ai, Tr, Ti, ssq, consts, *, C, ct):
    N1 = consts["N1"]
    blk = pl.BlockSpec((None, LANES, ct), lambda k1, c: (k1, 0, c))
    const = pl.BlockSpec((2 * LANES, 2 * LANES), lambda k1, c: (0, 0))
    twspec = pl.BlockSpec((None, LANES, 1), lambda k1, c: (k1, 0, 0))
    shape = jax.ShapeDtypeStruct((N1, LANES, C), F32)
    return pl.pallas_call(
        _fft_mid_kernel, out_shape=(shape, shape), grid=(N1, C // ct),
        in_specs=[blk, blk, blk, blk, pl.BlockSpec((2, 1, ct), lambda k1, c: (0, 0, c)),
                  const, const, twspec, twspec],
        out_specs=(blk, blk),
        compiler_params=_cparams(("parallel", "parallel")), name="hyena_fft_mid",
    )(Ar, Ai, Tr, Ti, ssq, consts["b_fwd"], consts["b_inv"], consts["tw"]["c_b"], consts["tw"]["s_b"])


def _fft_fin_compute(xs, vmem):
    br, bi, x0, z = xs
    m_ref, bias = vmem
    r = _dot(m_ref[...], jnp.concatenate([br, bi], axis=0).astype(BF16))
    return (x0 * (r + bias[...] * z),)


def _fft_final(Br, Bi, x0_3d, z_3d, bias, consts, *, C):
    mat = consts["fin"]
    return _time_slice_call(_fft_fin_compute, [Br, Bi, x0_3d, z_3d], [mat, bias],
                            [pl.BlockSpec(mat.shape, lambda j: (0, 0)), pl.BlockSpec((1, C), lambda j: (0, 0))],
                            1, N1=consts["N1"], C=C, name="hyena_fft_final")[0]


def _hyena_latent(x0, z, taps, ssq, bias, consts, *, B, L, C):
    assert B == 2
    N1 = consts["N1"]
    ct = C
    T = x0.shape[0]
    z3 = z.reshape(T // LANES, LANES, C)
    x03 = x0.reshape(T // LANES, LANES, C)
    Ar, Ai = _fft_stage_a(z3, consts["a_data"], consts["tw"], N1=N1, C=C)
    Tr, Ti = _fft_stage_a(taps.reshape(N1, LANES, C), consts["a_taps"], consts["tw"], N1=N1, C=C)
    Br, Bi = _fft_mid(Ar, Ai, Tr, Ti, ssq, consts, C=C, ct=ct)
    y = _fft_final(Br, Bi, x03, z3, bias.reshape(1, C).astype(F32), consts, C=C)
    return y.reshape(B * L, C)


def _hyena_ctx_kernel(z, x0, taps, ssq, bias, dh, dl, th, tl, ih, il, y, *, Lc):
    n = 2 * Lc
    x = _dot3c(dh[...], dl[...], z[...])
    h = _dot3c(th[...], tl[...], taps[...])
    scale = lax.rsqrt(ssq[0] + ssq[1] + EPS)
    xr, xi, hr, hi = x[:n], x[n:], h[:n] * scale, h[n:] * scale
    r = _dot3c(ih[...], il[...], jnp.concatenate([xr * hr - xi * hi, xr * hi + xi * hr], axis=0))
    y[...] = x0[...] * (r + bias[...] * z[...])


def _hyena_ctx(x0, z, taps, ssq, bias, *, B, L, Lc, C):
    assert B == 2 and (B * L) % (2 * Lc) == 0
    n = 2 * Lc
    idx = np.arange(n)
    F = np.exp(-2j * np.pi * np.outer(idx, idx) / n)
    d = _np_split(_cblock(F[:, :Lc]))
    t = _np_split(np.concatenate([F.real, F.imag], axis=0))
    inv = _np_split(_cblock(np.conj(F)[:Lc, :]) / n)
    ct = 256
    r0 = (B * L) // n
    both = pl.BlockSpec((n, ct), lambda c: (r0, c))
    const = lambda m: pl.BlockSpec(m.shape, lambda c: (0, 0))
    return pl.pallas_call(
        functools.partial(_hyena_ctx_kernel, Lc=Lc),
        out_shape=jax.ShapeDtypeStruct((n, C), F32),
        grid=(C // ct,),
        in_specs=[both, both,
                  pl.BlockSpec((n, ct), lambda c: (0, c)),
                  pl.BlockSpec((2, 1, ct), lambda c: (0, 0, c)),
                  pl.BlockSpec((1, ct), lambda c: (0, c)),
                  const(d[0]), const(d[1]), const(t[0]), const(t[1]), const(inv[0]), const(inv[1])],
        out_specs=pl.BlockSpec((n, ct), lambda c: (0, c)),
        compiler_params=_cparams(("parallel",)), name="hyena_ctx",
    )(z, x0, taps, ssq, bias.reshape(1, C).astype(F32), *d, *t, *inv)


def _merge_kernel(*refs, with_ctx, n_lat_tiles):
    if with_ctx:
        (ym, ya, yh, ymc, yac, yhc, gm, ga, gh, mb, wm, wa, wh, wo, x_ref, g1, o_ref) = refs
        is_ctx = pl.program_id(0) >= n_lat_tiles
        pick = lambda lat, ctx: jnp.where(is_ctx, ctx[...], lat[...])
        m, a, h = pick(ym, ymc), pick(ya, yac), pick(yh, yhc)
    else:
        (ym, ya, yh, gm, ga, gh, mb, wm, wa, wh, wo, x_ref, g1, o_ref) = refs
        m, a, h = ym[...], ya[...], yh[...]
    D = x_ref.shape[1]
    bias = mb[...]
    gate = lambda g, k: _sigmoid(g[...].astype(F32) + bias[:, k * D:(k + 1) * D])
    y = (gate(gm, 0) * _dot(m, wm[...]) + gate(ga, 1) * _dot(a, wa[...])
         + gate(gh, 2) * _dot(h.astype(BF16), wh[...]))
    o_ref[...] = x_ref[...] + g1[...] * _dot(y.astype(BF16), wo[...])


def _merge(X, P, gcol0, branches, merge_b, weights, layer, mod, gate_idx, *, n_rows, B, L, with_ctx):
    D = X.shape[1]
    tm = ROW_TILE
    nlt, tps = (B * L) // tm, L // tm
    grp = functools.partial(_group_of_tile, n_lat_tiles=nlt, tiles_per_seq=tps, n_batch=B)
    lat = pl.BlockSpec((tm, D), lambda i: (jnp.minimum(i, nlt - 1), 0))
    ctx = pl.BlockSpec((tm, D), lambda i: (jnp.maximum(i - nlt, 0), 0))
    gb = gcol0 // D
    gspec = lambda k: pl.BlockSpec((tm, D), lambda i: (i, gb + k))
    wspec = pl.BlockSpec((None, D, D), lambda i: (layer, 0, 0))
    in_specs = ([lat] * 3 + ([ctx] * 3 if with_ctx else []) + [gspec(0), gspec(1), gspec(2)]
                + [pl.BlockSpec((1, 3 * D), lambda i: (0, 0))] + [wspec] * 4
                + [pl.BlockSpec((tm, D), lambda i: (i, 0)),
                   pl.BlockSpec((None, None, 1, D), lambda i: (grp(i), gate_idx, 0, 0))])
    return pl.pallas_call(
        functools.partial(_merge_kernel, with_ctx=with_ctx, n_lat_tiles=nlt),
        out_shape=jax.ShapeDtypeStruct((n_rows, D), F32),
        grid=(n_rows // tm,), in_specs=in_specs,
        out_specs=pl.BlockSpec((tm, D), lambda i: (i, 0)),
        compiler_params=_cparams(("parallel",)), name="merge",
    )(*branches, P, P, P, merge_b.reshape(1, 3 * D).astype(F32), *weights, X, mod)


EXPERT_BITS = 5
assert 1 << EXPERT_BITS == N_EXPERTS


def _slot_kernel(pstart, code_ref, o_ref):
    code = code_ref[...]
    expert = code & (N_EXPERTS - 1)
    slot = code >> EXPERT_BITS
    for e in range(N_EXPERTS):
        slot = slot + jnp.where(expert == e, pstart[e], 0)
    o_ref[...] = slot


def _slot_table(code, pstart):
    N = code.shape[0]
    tm = ROW_TILE
    return pl.pallas_call(
        _slot_kernel,
        out_shape=jax.ShapeDtypeStruct((N, LANES), jnp.int32),
        grid_spec=pltpu.PrefetchScalarGridSpec(
            num_scalar_prefetch=1, grid=(N // tm,),
            in_specs=[pl.BlockSpec((tm, LANES), lambda i, ps: (i, 0))],
            out_specs=pl.BlockSpec((tm, LANES), lambda i, ps: (i, 0))),
        compiler_params=_cparams(("parallel",)), name="moe_slots",
    )(pstart, code)


def _dispatch_kernel(slots, tok_ref, xs_zero, xs_out, sem, *, n_tok):
    del xs_zero
    R = tok_ref.shape[0]
    base = pl.program_id(0) * R

    def copy(pos, r):
        return pltpu.make_async_copy(tok_ref.at[pl.ds(r, 1), :], xs_out.at[pl.ds(slots[pos], 1), :], sem)

    for wait in (False, True):
        for k in range(TOP_K):
            for r in range(R):
                cp = copy(k * n_tok + base + r, r)
                cp.wait() if wait else cp.start()


def _moe_dispatch(tok, slots, P):
    N, D = tok.shape
    tm = MOE_TOKEN_TILE
    return pl.pallas_call(
        functools.partial(_dispatch_kernel, n_tok=N),
        out_shape=jax.ShapeDtypeStruct((P, D), F32),
        grid_spec=pltpu.PrefetchScalarGridSpec(
            num_scalar_prefetch=1, grid=(N // tm,),
            in_specs=[pl.BlockSpec((tm, D), lambda i, sl: (i, 0)),
                      pl.BlockSpec(memory_space=pl.ANY)],
            out_specs=pl.BlockSpec(memory_space=pl.ANY),
            scratch_shapes=[pltpu.SemaphoreType.DMA]),
        input_output_aliases={2: 0},
        compiler_params=_cparams(("arbitrary",)), name="moe_dispatch",
    )(slots, tok, jnp.zeros((P, D), F32))


def _moe_kernel(blk_e, xs_ref, w1_ref, w3_ref, w2_ref, y_ref, w1b, w3b, w2b):
    i = pl.program_id(0)

    @pl.when((i == 0) | (blk_e[i] != blk_e[jnp.maximum(i - 1, 0)]))
    def _():
        w1b[...] = w1_ref[...].astype(BF16)
        w3b[...] = w3_ref[...].astype(BF16)
        w2b[...] = w2_ref[...].astype(BF16)

    x = xs_ref[...].astype(BF16)
    a = _dot(x, w1b[...])
    g = (a * _sigmoid(a)) * _dot(x, w3b[...])
    y_ref[...] = _dot(g.astype(BF16), w2b[...])


def _moe_experts(xs, blk_e, w1, w3, w2, layer):
    P, D = xs.shape
    De = w1.shape[3]
    return pl.pallas_call(
        _moe_kernel,
        out_shape=jax.ShapeDtypeStruct((P, D), F32),
        grid_spec=pltpu.PrefetchScalarGridSpec(
            num_scalar_prefetch=1, grid=(P // MOE_BLOCK,),
            in_specs=[pl.BlockSpec((MOE_BLOCK, D), lambda i, be: (i, 0)),
                      pl.BlockSpec((None, None, D, De), lambda i, be: (layer, be[i], 0, 0)),
                      pl.BlockSpec((None, None, D, De), lambda i, be: (layer, be[i], 0, 0)),
                      pl.BlockSpec((None, None, De, D), lambda i, be: (layer, be[i], 0, 0))],
            out_specs=pl.BlockSpec((MOE_BLOCK, D), lambda i, be: (i, 0)),
            scratch_shapes=[pltpu.VMEM((D, De), BF16), pltpu.VMEM((D, De), BF16), pltpu.VMEM((De, D), BF16)]),
        compiler_params=_cparams(("arbitrary",)), name="moe_experts",
    )(blk_e, xs, w1, w3, w2)


def _combine_kernel(slots, yb_hbm, x_ref, g2, wt_ref, o_ref, buf, sems, *, n_tok):
    i = pl.program_id(0)
    n = pl.num_programs(0)
    slot = i % 2
    R = x_ref.shape[0]

    def gather(tile, slot, wait):
        dst = buf.at[slot]
        for k in range(TOP_K):
            for r in range(R):
                src = slots[k * n_tok + tile * R + r]
                cp = pltpu.make_async_copy(yb_hbm.at[pl.ds(src, 1), :], dst.at[k, pl.ds(r, 1), :], sems.at[slot])
                cp.wait() if wait else cp.start()

    @pl.when(i == 0)
    def _():
        gather(0, 0, False)

    @pl.when(i + 1 < n)
    def _():
        gather(i + 1, 1 - slot, False)

    gather(i, slot, True)
    w = wt_ref[...]
    acc = w[:, 0:1] * buf[slot, 0]
    for k in range(1, TOP_K):
        acc = acc + w[:, k:k + 1] * buf[slot, k]
    o_ref[...] = x_ref[...] + g2[...] * acc


def _moe_combine(yb, slots, wts, X, mod, gate_idx, *, n_rows, B, L):
    D = X.shape[1]
    tm = MOE_TOKEN_TILE
    nlt, tps = (B * L) // tm, L // tm
    grp = functools.partial(_group_of_tile, n_lat_tiles=nlt, tiles_per_seq=tps, n_batch=B)
    return pl.pallas_call(
        functools.partial(_combine_kernel, n_tok=n_rows),
        out_shape=jax.ShapeDtypeStruct((n_rows, D), F32),
        grid_spec=pltpu.PrefetchScalarGridSpec(
            num_scalar_prefetch=1, grid=(n_rows // tm,),
            in_specs=[pl.BlockSpec(memory_space=pl.ANY),
                      pl.BlockSpec((tm, D), lambda i, sl: (i, 0)),
                      pl.BlockSpec((None, None, 1, D), lambda i, sl: (grp(i), gate_idx, 0, 0)),
                      pl.BlockSpec((tm, LANES), lambda i, sl: (i, 0))],
            out_specs=pl.BlockSpec((tm, D), lambda i, sl: (i, 0)),
            scratch_shapes=[pltpu.VMEM((2, TOP_K, tm, D), F32), pltpu.SemaphoreType.DMA((2,))]),
        compiler_params=_cparams(("arbitrary",)), name="moe_combine",
    )(slots, yb, X, mod, wts)


def _dispatch_tables(code, counts, n_tok):
    E, K = N_EXPERTS, TOP_K
    cnt = counts[0, :E].astype(jnp.int32)
    padded = (cnt + MOE_BLOCK - 1) // MOE_BLOCK * MOE_BLOCK
    pend = jnp.cumsum(padded)
    pstart = (pend - padded).astype(jnp.int32)
    P = -(-(n_tok * K + E * (MOE_BLOCK - 1)) // MOE_BLOCK) * MOE_BLOCK
    blk_row = jnp.arange(P // MOE_BLOCK, dtype=jnp.int32) * MOE_BLOCK
    blk_e = jnp.minimum(jnp.sum(pend[None, :] <= blk_row[:, None], axis=1), E - 1).astype(jnp.int32)
    return _slot_table(code, pstart)[:, :K].T.reshape(-1), P, blk_e


def kernel(x, c, ctx, c_ctx, mod_w, mod_b, norm1_g, norm2_g, w_in, merge_b, m_conv_w, m_conv_b, m_gate_b, m_norm_g, a_qnorm_g, a_knorm_g, a_lambda, a_subln_g, h_conv_w, h_conv_b, h_ffn_w1, h_ffn_b1, h_ffn_w2, h_ffn_b2, h_ffn_w3, h_freq, h_decay, h_bias, w_br_m, w_br_a, w_br_h, w_out, router_w, router_b, e_w1, e_w3, e_w2):
    B, L, D = x.shape
    Lc = ctx.shape[1]
    depth = mod_w.shape[0]
    NL, NC = B * L, B * Lc
    T = NL + NC
    assert B + 1 <= 8 and Lc == ROW_TILE and L % min(L, 1024) == 0 and L % 512 == 0
    assert NL % Lc == 0 and D % LANES == 0
    width = D
    a_dh = D // (2 * A_HEADS)
    n_gates = 4 * M_HEADS
    c_mq, c_mv, c_mo = 0, 2 * width, 3 * width
    c_aq, c_ak, c_av = 4 * width, 5 * width, 6 * width
    c_hx = 7 * width
    c_gp = 10 * width

    X = jnp.concatenate([x.reshape(NL, D), ctx.reshape(NC, D)], axis=0)
    c8 = jnp.zeros((8, D), F32).at[:B].set(c).at[B].set(c_ctx)
    cos_tab, sin_tab = _rope_tables(L, a_dh, ROW_TILE)
    fft_consts = _dft_consts(L)
    g0 = 4 * width
    w_main = jnp.concatenate([w_in[:, :, :g0], w_in[:, :, g0 + n_gates:]], axis=2).astype(BF16)
    w_gate = jnp.pad(w_in[:, :, g0:g0 + n_gates], ((0, 0), (0, 0), (0, LANES - n_gates))).astype(BF16)
    merge_w = [w.astype(BF16) for w in (w_br_m, w_br_a, w_br_h, w_out)]

    for l in range(depth):
        need_ctx = l < depth - 1
        lam_init = 0.8 - 0.6 * math.exp(-0.3 * l)
        mod = _modulation(c8, mod_w, mod_b, l)[:B + 1].reshape(B + 1, 6, 1, D)

        P, G = _inproj(X, norm1_g[l], mod, w_main, w_gate, l, L=L, B=B)

        QK = _mlstm_qk(P, m_conv_w[l], m_conv_b[l], L=L, Lc=Lc, NL=NL, width=width)
        ym, ymc = _mlstm(QK, P, G, m_gate_b[l], m_norm_g[l], B=B, L=L, Lc=Lc, width=width)

        Qh = _attn_prep(P, c_aq, a_qnorm_g[l], cos_tab, sin_tab, a_dh ** -0.5 * LOG2E, L=L, NL=NL, dh=a_dh)
        Kh = _attn_prep(P, c_ak, a_knorm_g[l], cos_tab, sin_tab, 1.0, L=L, NL=NL, dh=a_dh)
        attn = functools.partial(_attention, Qh, Kh, P, c_av, a_lambda[l], a_subln_g[l], lam_init,
                                 B=B, L=L, Lc=Lc, dh=a_dh)
        ya = attn(latent=True)

        x0, z = _hyena_pre(P, c_hx, h_conv_w[l], h_conv_b[l], L=L, Lc=Lc, NL=NL, C=width)
        filt = (h_ffn_w1[l], h_ffn_b1[l], h_ffn_w2[l], h_ffn_b2[l], h_ffn_w3[l], h_freq[l], h_decay[l])
        taps, ssq = _hyena_filters(L, *filt, width)
        yh = _hyena_latent(x0, z, taps, ssq, h_bias[l], fft_consts, B=B, L=L, C=width)

        if need_ctx:
            yac = attn(latent=False)
            taps_c, ssq_c = _hyena_filters(Lc, *filt, width)
            yhc = _hyena_ctx(x0, z, taps_c, ssq_c, h_bias[l], B=B, L=L, Lc=Lc, C=width)
            branches, n_rows = [ym, ya, yh, ymc, yac, yhc], T
        else:
            branches, n_rows = [ym, ya, yh], NL
        X = _merge(X, P, c_gp, branches, merge_b[l], merge_w, l, mod, 2,
                   n_rows=n_rows, B=B, L=L, with_ctx=need_ctx)

        tok, code, wts, counts = _norm_mod(X, norm2_g[l], mod, 3, 4, n_rows=n_rows, L=L, B=B,
                                           router=(router_w, router_b))
        slots, n_slots, blk_e = _dispatch_tables(code, counts, n_rows)
        xs = _moe_dispatch(tok, slots, n_slots)
        yb = _moe_experts(xs, blk_e, e_w1, e_w3, e_w2, l)
        X = _moe_combine(yb, slots, wts, X, mod, 5, n_rows=n_rows, B=B, L=L)
    return X[:NL].reshape(B, L, D)
```

```python
import functools
import math

import numpy as np
import jax
import jax.numpy as jnp
from jax import lax
from jax.experimental import pallas as pl
from jax.experimental.pallas import tpu as pltpu

F32 = jnp.float32
BF16 = jnp.bfloat16

GRID_W = 64
EPS = 1e-6
M_HEADS = 4
M_CHUNK = 128
A_HEADS = 8
ROPE_BASE = 10000.0
H_EMB = 33
H_FFN = 64
SHORT_CONV = 3
N_EXPERTS = 32
N_GROUPS = 4
TOP_K = 2
MOE_BLOCK = 256
MOE_TOKEN_TILE = 128

LANES = 128
BF16_SUBLANES = 16
V7X_VMEM_BYTES = 64 * 1024 * 1024
VMEM_LIMIT = V7X_VMEM_BYTES * 7 // 8

ROW_TILE = 256
LOG2E = 1.4426950408889634


def _cparams(sem):
    return pltpu.CompilerParams(dimension_semantics=sem, vmem_limit_bytes=VMEM_LIMIT)


def _split(x):
    hi = x.astype(BF16)
    lo = (x - hi.astype(F32)).astype(BF16)
    return hi, lo


def _dot(a, b):
    return jnp.dot(a, b, preferred_element_type=F32)


def _dot3(a, b):
    ah, al = _split(a)
    bh, bl = _split(b)
    return _dot(ah, bh) + _dot(al, bh) + _dot(ah, bl)


def _dot3c(ch, cl, x):
    xh, xl = _split(x)
    return _dot(ch, xh) + _dot(cl, xh) + _dot(ch, xl)


def _np_split(a):
    a = jnp.asarray(np.asarray(a, np.float32))
    hi = a.astype(BF16)
    lo = (a - hi.astype(F32)).astype(BF16)
    return hi, lo


def _sigmoid(x):
    return 1.0 / (1.0 + jnp.exp(-x))


def _log_sigmoid(x):
    return jnp.minimum(x, 0.0) - jnp.log(1.0 + jnp.exp(-jnp.abs(x)))


def _mod_kernel(c_ref, w_ref, b_ref, o_ref):
    c = c_ref[...]
    o_ref[...] = _dot3(c * _sigmoid(c), w_ref[...]) + b_ref[...]


def _modulation(c8, w, b, layer):
    depth, D, N = w.shape
    tn = 1536
    return pl.pallas_call(
        _mod_kernel,
        out_shape=jax.ShapeDtypeStruct((8, N), F32),
        grid=(N // tn,),
        in_specs=[pl.BlockSpec((8, D), lambda j: (0, 0)),
                  pl.BlockSpec((None, D, tn), lambda j: (layer, 0, j)),
                  pl.BlockSpec((None, 1, tn), lambda j: (layer, 0, j))],
        out_specs=pl.BlockSpec((8, tn), lambda j: (0, j)),
        compiler_params=_cparams(("parallel",)),
        name="modulation",
    )(c8, w, b.reshape(depth, 1, N))


def _norm_mod_router_kernel(x_ref, g_ref, sh_ref, sc_ref, rw_ref, rb_ref, o_ref, code_ref, wt_ref, cnt_ref, carry):
    E, G = N_EXPERTS, N_GROUPS
    gs = E // G

    @pl.when(pl.program_id(0) == 0)
    def _():
        carry[...] = jnp.zeros_like(carry)

    x = x_ref[...]
    y = x * lax.rsqrt(jnp.mean(x * x, axis=-1, keepdims=True) + EPS) * g_ref[...]
    h = y * (1.0 + sc_ref[...]) + sh_ref[...]
    o_ref[...] = h
    tm = h.shape[0]
    s = _sigmoid(_dot3(h, rw_ref[...]))
    lane = lax.broadcasted_iota(jnp.int32, (1, LANES), 1)
    lane_f = lane.astype(F32)
    sb = jnp.where(lane < E, s + rb_ref[...], -jnp.inf)
    far = float(LANES)
    best = jnp.full((tm, 1), -jnp.inf, F32)
    e1 = jnp.zeros((tm, 1), F32)
    e2 = jnp.zeros((tm, 1), F32)
    for g in range(G):
        mg = jnp.where((lane >= g * gs) & (lane < (g + 1) * gs), sb, -jnp.inf)
        m1 = jnp.max(mg, axis=-1, keepdims=True)
        i1 = jnp.min(jnp.where(mg == m1, lane_f, far), axis=-1, keepdims=True)
        mg2 = jnp.where(lane_f == i1, -jnp.inf, mg)
        m2 = jnp.max(mg2, axis=-1, keepdims=True)
        i2 = jnp.min(jnp.where(mg2 == m2, lane_f, far), axis=-1, keepdims=True)
        score = m1 + m2
        take = score > best
        best = jnp.where(take, score, best)
        e1 = jnp.where(take, i1, e1)
        e2 = jnp.where(take, i2, e2)
    oh1 = lane_f == e1
    oh2 = lane_f == e2
    s1 = jnp.sum(jnp.where(oh1, s, 0.0), axis=-1, keepdims=True)
    s2 = jnp.sum(jnp.where(oh2, s, 0.0), axis=-1, keepdims=True)
    den = s1 + s2
    r = lax.broadcasted_iota(jnp.int32, (tm, tm), 0)
    c = lax.broadcasted_iota(jnp.int32, (tm, tm), 1)
    lower = (c < r).astype(BF16)
    o1 = oh1.astype(F32)
    o2 = oh2.astype(F32)
    cum1 = _dot(lower, o1.astype(BF16))
    cum2 = _dot(lower, o2.astype(BF16))
    tot1 = jnp.sum(o1, axis=0, keepdims=True)
    base = carry[...]
    rank1 = jnp.sum(jnp.where(oh1, base + cum1, 0.0), axis=-1, keepdims=True)
    rank2 = jnp.sum(jnp.where(oh2, base + tot1 + cum2, 0.0), axis=-1, keepdims=True)
    total = base + tot1 + jnp.sum(o2, axis=0, keepdims=True)
    carry[...] = total
    cnt_ref[...] = total
    code1 = (rank1 * E + e1).astype(jnp.int32)
    code2 = (rank2 * E + e2).astype(jnp.int32)
    code_ref[...] = jnp.where(lane == 0, code1, jnp.where(lane == 1, code2, 0))
    wt_ref[...] = jnp.where(lane == 0, s1 / den, jnp.where(lane == 1, s2 / den, 0.0))


def _group_of_tile(i, n_lat_tiles, tiles_per_seq, n_batch):
    return jnp.where(i < n_lat_tiles, i // tiles_per_seq, n_batch)


def _norm_mod(x, g, mod, shift_idx, scale_idx, *, n_rows, L, B, router):
    D = x.shape[1]
    tm = ROW_TILE
    nlt, tps = (B * L) // tm, L // tm
    grp = functools.partial(_group_of_tile, n_lat_tiles=nlt, tiles_per_seq=tps, n_batch=B)
    in_specs = [pl.BlockSpec((tm, D), lambda i: (i, 0)),
                pl.BlockSpec((1, D), lambda i: (0, 0)),
                pl.BlockSpec((None, None, 1, D), lambda i: (grp(i), shift_idx, 0, 0)),
                pl.BlockSpec((None, None, 1, D), lambda i: (grp(i), scale_idx, 0, 0))]
    args = [x, g.reshape(1, D), mod, mod]
    router_w, router_b = router
    E = router_w.shape[1]
    assert E == N_EXPERTS
    rw = jnp.pad(router_w, ((0, 0), (0, LANES - E)))
    rb = jnp.pad(router_b.astype(F32).reshape(1, E), ((0, 0), (0, LANES - E)))
    lanes = pl.BlockSpec((tm, LANES), lambda i: (i, 0))
    return pl.pallas_call(
        _norm_mod_router_kernel,
        out_shape=(jax.ShapeDtypeStruct((n_rows, D), F32),
                   jax.ShapeDtypeStruct((n_rows, LANES), jnp.int32),
                   jax.ShapeDtypeStruct((n_rows, LANES), F32),
                   jax.ShapeDtypeStruct((1, LANES), F32)),
        grid=(n_rows // tm,),
        in_specs=in_specs + [pl.BlockSpec((D, LANES), lambda i: (0, 0)),
                             pl.BlockSpec((1, LANES), lambda i: (0, 0))],
        out_specs=(pl.BlockSpec((tm, D), lambda i: (i, 0)), lanes, lanes,
                   pl.BlockSpec((1, LANES), lambda i: (0, 0))),
        scratch_shapes=[pltpu.VMEM((1, LANES), F32)],
        compiler_params=_cparams(("arbitrary",)), name="norm_mod_router",
    )(*args, rw, rb)


def _largest_tile(n, cap, step):
    return max(t for t in range(step, cap + 1, step) if n % t == 0)


def _inproj_kernel(x_ref, g_ref, mod_ref, w_ref, wg_ref, o_ref, gate_ref, h_scr, *, L, n_batch):
    tm = x_ref.shape[0]

    @pl.when(pl.program_id(1) == 0)
    def _():
        x = x_ref[...]
        y = x * lax.rsqrt(jnp.mean(x * x, axis=-1, keepdims=True) + EPS) * g_ref[...]
        row = pl.program_id(0) * tm + lax.broadcasted_iota(jnp.int32, (tm, 1), 0)
        shift, scale = mod_ref[n_batch, 0], mod_ref[n_batch, 1]
        for b in range(n_batch - 1, -1, -1):
            in_b = row < (b + 1) * L
            shift = jnp.where(in_b, mod_ref[b, 0], shift)
            scale = jnp.where(in_b, mod_ref[b, 1], scale)
        h_scr[...] = (y * (1.0 + scale) + shift).astype(h_scr.dtype)
        gate_ref[...] = _dot(h_scr[...], wg_ref[...])

    o_ref[...] = _dot(h_scr[...], w_ref[...]).astype(o_ref.dtype)


def _inproj(X, g, mod, w_main, w_gate, layer, *, L, B):
    T, D = X.shape
    N = w_main.shape[2]
    tm, tn = _largest_tile(T, 1536, ROW_TILE), 1024
    return pl.pallas_call(
        functools.partial(_inproj_kernel, L=L, n_batch=B),
        out_shape=(jax.ShapeDtypeStruct((T, N), BF16), jax.ShapeDtypeStruct((T, LANES), F32)),
        grid=(T // tm, N // tn),
        in_specs=[pl.BlockSpec((tm, D), lambda i, j: (i, 0)),
                  pl.BlockSpec((1, D), lambda i, j: (0, 0)),
                  pl.BlockSpec(mod.shape, lambda i, j: (0, 0, 0, 0)),
                  pl.BlockSpec((None, D, tn), lambda i, j: (layer, 0, j)),
                  pl.BlockSpec((None, D, LANES), lambda i, j: (layer, 0, 0))],
        out_specs=(pl.BlockSpec((tm, tn), lambda i, j: (i, j)),
                   pl.BlockSpec((tm, LANES), lambda i, j: (i, 0))),
        scratch_shapes=[pltpu.VMEM((tm, D), BF16)],
        compiler_params=_cparams(("parallel", "arbitrary")), name="inproj",
    )(X, g.reshape(1, D), mod, w_main, w_gate)


def _seq_edge_flags(tm, L, Lc, NL):
    r0 = pl.program_id(0) * tm
    lat = r0 < NL
    start = jnp.where(lat, r0 % L == 0, (r0 - NL) % Lc == 0)
    end = jnp.where(lat, (r0 + tm) % L == 0, (r0 + tm - NL) % Lc == 0)
    return jnp.where(start, 0.0, 1.0), jnp.where(end, 0.0, 1.0)


def _conv3(cur_ref, prev_ref, next_ref, w_ref, b_ref, keep_prev, keep_next):
    cur = cur_ref[...].astype(F32)
    tm = cur.shape[0]
    prev_row = prev_ref[BF16_SUBLANES - 1:BF16_SUBLANES, :].astype(F32) * keep_prev
    next_row = next_ref[0:1, :].astype(F32) * keep_next
    row = lax.broadcasted_iota(jnp.int32, (tm, 1), 0)
    up = jnp.where(row == 0, prev_row, pltpu.roll(cur, 1, 0))
    dn = jnp.where(row == tm - 1, next_row, pltpu.roll(cur, tm - 1, 0))
    return b_ref[...] + up * w_ref[0:1, :] + cur * w_ref[1:2, :] + dn * w_ref[2:3, :]


def _conv_specs(tm, tc, T, col_block):
    per = tm // BF16_SUBLANES
    last = T // BF16_SUBLANES - 1
    return [pl.BlockSpec((tm, tc), lambda i, j: (i, col_block + j)),
            pl.BlockSpec((BF16_SUBLANES, tc), lambda i, j: (jnp.maximum(i * per - 1, 0), col_block + j)),
            pl.BlockSpec((BF16_SUBLANES, tc), lambda i, j: (jnp.minimum((i + 1) * per, last), col_block + j))]


def _mconv_kernel(cur_ref, prev_ref, next_ref, w_ref, b_ref, scale_ref, o_ref, *, tm, L, Lc, NL):
    kp, kn = _seq_edge_flags(tm, L, Lc, NL)
    y = _conv3(cur_ref, prev_ref, next_ref, w_ref, b_ref, kp, kn)
    o_ref[...] = (y * _sigmoid(y) * scale_ref[...]).astype(o_ref.dtype)


def _mlstm_qk(P, conv_w, conv_b, *, L, Lc, NL, width):
    T = P.shape[0]
    tm, tc = ROW_TILE, 1024
    C = 2 * width
    scale = jnp.concatenate([jnp.ones((1, width), F32),
                             jnp.full((1, width), (width // M_HEADS) ** -0.5, F32)], axis=1)
    vec = lambda r: pl.BlockSpec((r, tc), lambda i, j: (0, j))
    return pl.pallas_call(
        functools.partial(_mconv_kernel, tm=tm, L=L, Lc=Lc, NL=NL),
        out_shape=jax.ShapeDtypeStruct((T, C), BF16),
        grid=(T // tm, C // tc),
        in_specs=_conv_specs(tm, tc, T, 0) + [vec(SHORT_CONV), vec(1), vec(1)],
        out_specs=pl.BlockSpec((tm, tc), lambda i, j: (i, j)),
        compiler_params=_cparams(("parallel", "parallel")), name="mlstm_qk_conv",
    )(P, P, P, conv_w, conv_b.reshape(1, C), scale)


def _hyena_pre_kernel(c0, p0, n0, c1, p1, n1, c2, p2, n2, w0, w1, w2, b0, b1, b2,
                      x0_ref, z_ref, *, tm, L, Lc, NL):
    kp, kn = _seq_edge_flags(tm, L, Lc, NL)
    x0 = _conv3(c0, p0, n0, w0, b0, kp, kn)
    x1 = _conv3(c1, p1, n1, w1, b1, kp, kn)
    v = _conv3(c2, p2, n2, w2, b2, kp, kn)
    x0_ref[...] = x0
    z_ref[...] = v * x1


def _hyena_pre(P, col0, conv_w, conv_b, *, L, Lc, NL, C):
    T = P.shape[0]
    tm, tc = ROW_TILE, 512
    nb = C // tc
    specs = []
    for part in range(3):
        specs += _conv_specs(tm, tc, T, col0 // tc + part * nb)
    wspecs = [pl.BlockSpec((SHORT_CONV, tc), lambda i, j, p=part: (0, p * nb + j)) for part in range(3)]
    bspecs = [pl.BlockSpec((1, tc), lambda i, j, p=part: (0, p * nb + j)) for part in range(3)]
    b2d = conv_b.reshape(1, 3 * C)
    out = pl.BlockSpec((tm, tc), lambda i, j: (i, j))
    return pl.pallas_call(
        functools.partial(_hyena_pre_kernel, tm=tm, L=L, Lc=Lc, NL=NL),
        out_shape=(jax.ShapeDtypeStruct((T, C), F32), jax.ShapeDtypeStruct((T, C), F32)),
        grid=(T // tm, nb),
        in_specs=specs + wspecs + bspecs,
        out_specs=(out, out),
        compiler_params=_cparams(("parallel", "parallel")), name="hyena_pre",
    )(*([P] * 9), conv_w, conv_w, conv_w, b2d, b2d, b2d)


def _mlstm_chunk(q, k, v, li_r, lf_r, li_c, lf_c, C_scr, n_scr, m_scr, fwd):
    Q = M_CHUNK
    row = lax.broadcasted_iota(jnp.int32, (Q, Q), 0)
    col = lax.broadcasted_iota(jnp.int32, (Q, Q), 1)
    mask = (col <= row) if fwd else (col >= row)
    tri_c = mask.astype(BF16)
    tri_r = ((row <= col) if fwd else (row >= col)).astype(BF16)
    lfc_h, lfc_l = _split(jnp.broadcast_to(lf_c, (Q, Q)))
    lfr_h, lfr_l = _split(jnp.broadcast_to(lf_r, (Q, Q)))
    b_cols = _dot(tri_c, lfc_h) + _dot(tri_c, lfc_l)
    b_rows = _dot(lfr_h, tri_r) + _dot(lfr_l, tri_r)
    dm = jnp.where(mask, b_cols - b_rows + li_r, -jnp.inf)
    m_prev = m_scr[0:1, 0:1]
    b_col = b_cols[:, 0:1]
    inter = b_col + m_prev
    mt = jnp.maximum(inter, jnp.max(dm, axis=-1, keepdims=True))
    s = lax.dot_general(q, k, (((1,), (1,)), ((), ())), preferred_element_type=F32) * jnp.exp(dm - mt)
    wi = jnp.exp(inter - mt)
    num = _dot(s.astype(BF16), v) + wi * _dot(q, C_scr[...].astype(BF16))
    qn = jnp.sum(q.astype(F32) * n_scr[...], axis=-1, keepdims=True)
    den = jnp.sum(s, axis=-1, keepdims=True) + wi * qn
    h = num / jnp.maximum(jnp.abs(den), jnp.exp(-mt))
    b_tot = b_cols[Q - 1:Q, 0:1] if fwd else b_cols[0:1, 0:1]
    ws = b_tot - b_col + li_c
    m_new = jnp.maximum(b_tot + m_prev, jnp.max(ws, axis=0, keepdims=True))
    decay = jnp.exp(b_tot + m_prev - m_new)
    kw = k.astype(F32) * jnp.exp(ws - m_new)
    C_scr[...] = decay * C_scr[...] + lax.dot_general(
        kw.astype(BF16), v, (((0,), (0,)), ((), ())), preferred_element_type=F32)
    n_scr[...] = decay * n_scr[...] + jnp.sum(kw, axis=0, keepdims=True)
    m_scr[...] = jnp.broadcast_to(m_new, m_scr.shape)
    return h


def _mlstm_kernel(*refs, fwd, SEG, Lc, H, dh):
    if fwd:
        (gr_ref, gc_ref, grc_ref, gcc_ref, q_ref, k_ref, v_ref, qc_ref, kc_ref, vc_ref,
         h_ref, hc_ref, *scr) = refs
    else:
        (gr_ref, gc_ref, grc_ref, gcc_ref, q_ref, k_ref, v_ref, qc_ref, kc_ref, vc_ref,
         o_ref, oc_ref, hf_ref, hfc_ref, ng_ref, y_ref, yc_ref, *scr) = refs
    C_scr, n_scr, m_scr = scr[0:H], scr[H:2 * H], scr[2 * H:3 * H]
    gi = 0 if fwd else 2

    def run(nchunks, qr, kr, vr, grr, gcr, emit):
        def body(j, carry):
            c = j if fwd else nchunks - 1 - j
            rows = pl.ds(pl.multiple_of(c * M_CHUNK, M_CHUNK), M_CHUNK)
            for hh in range(H):
                cols = slice(hh * dh, (hh + 1) * dh)
                gcs = gcr[hh, rows, :]
                h = _mlstm_chunk(qr[rows, cols], kr[rows, cols], vr[rows, cols],
                                 grr[hh, gi, pl.ds(c, 1), :], _log_sigmoid(grr[hh, gi + 1, pl.ds(c, 1), :]),
                                 gcs[:, gi:gi + 1], _log_sigmoid(gcs[:, gi + 1:gi + 2]),
                                 C_scr[hh], n_scr[hh], m_scr[hh], fwd)
                emit(rows, cols, h)
            return carry

        lax.fori_loop(0, nchunks, body, 0, unroll=2)

    def emitter(dst, hf=None, o=None):
        def emit(rows, cols, h):
            if not fwd:
                h = h + hf[rows, cols]
                y = h * lax.rsqrt(jnp.mean(h * h, axis=-1, keepdims=True) + EPS) * ng_ref[:, cols]
                h = (y * _sigmoid(o[rows, cols].astype(F32))).astype(dst.dtype)
            dst[rows, cols] = h
        return emit

    @pl.when(pl.program_id(1) == 0)
    def _():
        for r in scr:
            r[...] = jnp.zeros_like(r)
        run(Lc // M_CHUNK, qc_ref, kc_ref, vc_ref, grc_ref, gcc_ref,
            emitter(hc_ref) if fwd else emitter(yc_ref, hfc_ref, oc_ref))

    run(SEG // M_CHUNK, q_ref, k_ref, v_ref, gr_ref, gc_ref,
        emitter(h_ref) if fwd else emitter(y_ref, hf_ref, o_ref))


def _mlstm(QK, P, G, gate_b, norm_g, *, B, L, Lc, width):
    H = M_HEADS
    dh = width // H
    NL = B * L
    SEG = min(L, 1024)
    S = L // SEG
    g = G[:, :4 * H] + gate_b.astype(F32)[None, :]

    def gate_views(rows, n):
        a = rows.reshape(B, n, 4, H)
        return (a.transpose(0, 3, 2, 1).reshape(B, H, 4, n // M_CHUNK, M_CHUNK),
                a.transpose(0, 3, 1, 2))

    gr, gc = gate_views(g[:NL], L)
    grc, gcc = gate_views(g[NL:], Lc)
    scratch = ([pltpu.VMEM((dh, dh), F32)] * H + [pltpu.VMEM((1, dh), F32)] * H
               + [pltpu.VMEM((8, LANES), F32)] * H)

    def call(fwd, extra_in, extra_specs, out_dtype):
        seg_of = (lambda s: s) if fwd else (lambda s: S - 1 - s)
        lat = lambda cb: pl.BlockSpec((SEG, width), lambda b, s: (b * S + seg_of(s), cb))
        ctx = lambda cb: pl.BlockSpec((Lc, width), lambda b, s: (NL // Lc + b, cb))
        gate_specs = [
            pl.BlockSpec((None, H, 4, SEG // M_CHUNK, M_CHUNK), lambda b, s: (b, 0, 0, seg_of(s), 0)),
            pl.BlockSpec((None, H, SEG, 4), lambda b, s: (b, 0, seg_of(s), 0)),
            pl.BlockSpec((None, H, 4, Lc // M_CHUNK, M_CHUNK), lambda b, s: (b, 0, 0, 0, 0)),
            pl.BlockSpec((None, H, Lc, 4), lambda b, s: (b, 0, 0, 0))]
        lat_out = pl.BlockSpec((SEG, width), lambda b, s: (b * S + seg_of(s), 0))
        ctx_out = pl.BlockSpec((Lc, width), lambda b, s: (b, 0))
        return pl.pallas_call(
            functools.partial(_mlstm_kernel, fwd=fwd, SEG=SEG, Lc=Lc, H=H, dh=dh),
            out_shape=(jax.ShapeDtypeStruct((NL, width), out_dtype),
                       jax.ShapeDtypeStruct((B * Lc, width), out_dtype)),
            grid=(B, S),
            in_specs=(gate_specs + [lat(0), lat(1), lat(2), ctx(0), ctx(1), ctx(2)]
                      + extra_specs(lat, ctx, lat_out, ctx_out)),
            out_specs=(lat_out, ctx_out), scratch_shapes=scratch,
            compiler_params=_cparams(("parallel", "arbitrary")),
            name="mlstm_fwd" if fwd else "mlstm_bwd",
        )(gr, gc, grc, gcc, QK, QK, P, QK, QK, P, *extra_in)

    hf, hfc = call(True, [], lambda *_: [], F32)
    return call(False, [P, P, hf, hfc, jnp.tile(norm_g.astype(F32), H).reshape(1, width)],
                lambda lat, ctx, lat_out, ctx_out: [lat(3), ctx(3), lat_out, ctx_out,
                                                    pl.BlockSpec((1, width), lambda b, s: (0, 0))], BF16)


def _attn_prep_kernel(x_ref, g_ref, cos_ref, sin_ref, o_ref, *, scale, dh):
    n_blk = x_ref.shape[1] // LANES
    r = lax.broadcasted_iota(jnp.int32, (LANES, LANES), 0)
    c = lax.broadcasted_iota(jnp.int32, (LANES, LANES), 1)
    group = (r // dh == c // dh).astype(BF16)
    lane = lax.broadcasted_iota(jnp.int32, (1, LANES), 1)
    quarter = dh // 4
    first = (lane % (2 * quarter)) < quarter
    cos = cos_ref[...]
    sin = sin_ref[...]
    for hb in range(n_blk):
        cols = slice(hb * LANES, (hb + 1) * LANES)
        x = x_ref[:, cols].astype(F32)
        hi, lo = _split(x * x)
        ms = (_dot(hi, group) + _dot(lo, group)) * (1.0 / dh)
        y = x * lax.rsqrt(ms + EPS) * g_ref[:, cols]
        rot = jnp.where(first, -pltpu.roll(y, LANES - quarter, 1), pltpu.roll(y, quarter, 1))
        o_ref[:, cols] = ((y * cos + rot * sin) * scale).astype(o_ref.dtype)


def _attn_prep(P, col0, gain, cos_tab, sin_tab, scale, *, L, NL, dh):
    T = P.shape[0]
    W = A_HEADS * 2 * dh
    tm = ROW_TILE
    nlt, tps = NL // tm, L // tm
    tab = pl.BlockSpec((tm, LANES), lambda i: (jnp.where(i < nlt, i % tps, tps), 0))
    g = jnp.tile(gain.astype(F32), W // dh).reshape(1, W)
    return pl.pallas_call(
        functools.partial(_attn_prep_kernel, scale=scale, dh=dh),
        out_shape=jax.ShapeDtypeStruct((T, W), BF16),
        grid=(T // tm,),
        in_specs=[pl.BlockSpec((tm, W), lambda i: (i, col0 // W)),
                  pl.BlockSpec((1, W), lambda i: (0, 0)), tab, tab],
        out_specs=pl.BlockSpec((tm, W), lambda i: (i, 0)),
        compiler_params=_cparams(("parallel",)), name="attn_prep",
    )(P, g, cos_tab, sin_tab)


def _attn_kernel(*refs, n_lat, tk, L, Lc, dh, lam_init):
    if n_lat:
        lam_ref, q_ref, kl_ref, vl_ref, kc_ref, vc_ref, sg_ref, o_ref, vext, acc = refs
    else:
        lam_ref, q_ref, kc_ref, vc_ref, sg_ref, o_ref, vext, acc = refs
    dv = 2 * dh
    ctx0 = n_lat * tk

    @pl.when(pl.program_id(2) == 0)
    def _():
        if n_lat:
            vext[0:L, 0:dv] = vl_ref[...]
        vext[ctx0:ctx0 + Lc, 0:dv] = vc_ref[...]
        vext[:, dv:2 * dv] = jnp.ones((vext.shape[0], dv), BF16)

    q = q_ref[...]
    lane = lax.broadcasted_iota(jnp.int32, (1, dv), 1)
    qs = (jnp.where(lane < dh, q, jnp.zeros_like(q)), jnp.where(lane >= dh, q, jnp.zeros_like(q)))
    acc[...] = jnp.zeros_like(acc)
    tq = q.shape[0]

    def update(comp, kblk, vblk, m_old):
        s = lax.dot_general(qs[comp], kblk, (((1,), (1,)), ((), ())), preferred_element_type=F32)
        m_new = jnp.maximum(m_old, jnp.max(s, axis=-1, keepdims=True))
        p = jnp.exp2(s - m_new).astype(BF16)
        acc[comp] = jnp.exp2(m_old - m_new) * acc[comp] + _dot(p, vblk)
        return m_new

    m = (jnp.full((tq, 1), -jnp.inf, F32),) * 2
    for c in range(n_lat):
        kblk, vblk = kl_ref[c * tk:(c + 1) * tk, :], vext[c * tk:(c + 1) * tk, :]
        m = update(0, kblk, vblk, m[0]), update(1, kblk, vblk, m[1])
    kblk, vblk = kc_ref[...], vext[ctx0:ctx0 + Lc, :]
    update(0, kblk, vblk, m[0])
    update(1, kblk, vblk, m[1])

    lp = lam_ref[...]
    lam = (jnp.exp(jnp.sum(lp[0:1] * lp[1:2], axis=-1, keepdims=True))
           - jnp.exp(jnp.sum(lp[2:3] * lp[3:4], axis=-1, keepdims=True)) + lam_init)
    a0, a1 = acc[0], acc[1]
    o = a0[:, 0:dv] / a0[:, dv:dv + 1] - lam * (a1[:, 0:dv] / a1[:, dv:dv + 1])
    y = o * lax.rsqrt(jnp.mean(o * o, axis=-1, keepdims=True) + EPS) * sg_ref[...] * (1.0 - lam_init)
    o_ref[...] = y.astype(o_ref.dtype)


def _attention(Qh, Kh, P, vcol0, lam_p, sub_g, lam_init, *, B, L, Lc, dh, latent):
    NL = B * L
    dv = 2 * dh
    H = A_HEADS
    vb = vcol0 // dv
    if latent:
        tq, tk = 512, min(L, 1024)
        n_lat, nq, rows_out = L // tk, L // tq, NL
        q_spec = pl.BlockSpec((tq, dv), lambda b, h, i: (b * nq + i, h))
        lat_specs = [pl.BlockSpec((L, dv), lambda b, h, i: (b, h)),
                     pl.BlockSpec((L, dv), lambda b, h, i: (b, vb + h))]
        lat_args = [Kh, P]
        o_spec = pl.BlockSpec((tq, dv), lambda b, h, i: (b * nq + i, h))
        nkeys = L + Lc
    else:
        tq, tk = Lc, 512
        n_lat, nq, rows_out = 0, 1, B * Lc
        q_spec = pl.BlockSpec((Lc, dv), lambda b, h, i: (NL // Lc + b, h))
        lat_specs, lat_args = [], []
        o_spec = pl.BlockSpec((Lc, dv), lambda b, h, i: (b, h))
        nkeys = Lc
    ctx_specs = [pl.BlockSpec((Lc, dv), lambda b, h, i: (NL // Lc + b, h)),
                 pl.BlockSpec((Lc, dv), lambda b, h, i: (NL // Lc + b, vb + h))]
    return pl.pallas_call(
        functools.partial(_attn_kernel, n_lat=n_lat, tk=tk, L=L, Lc=Lc, dh=dh, lam_init=lam_init),
        out_shape=jax.ShapeDtypeStruct((rows_out, H * dv), BF16),
        grid=(B, H, nq),
        in_specs=[pl.BlockSpec((4, dh), lambda b, h, i: (0, 0)), q_spec] + lat_specs + ctx_specs
                 + [pl.BlockSpec((1, dv), lambda b, h, i: (0, 0))],
        out_specs=o_spec,
        scratch_shapes=[pltpu.VMEM((nkeys, 2 * dv), BF16), pltpu.VMEM((2, tq, 2 * dv), F32)],
        compiler_params=_cparams(("parallel", "parallel", "arbitrary")),
        name="diff_attn_latent" if latent else "diff_attn_ctx",
    )(lam_p.astype(F32), Qh, *lat_args, Kh, P, sub_g.reshape(1, dv).astype(F32))


def _rope_tables(L, dh, tm):
    rows = L // GRID_W
    row = np.repeat(np.arange(rows), GRID_W).astype(np.float64)
    col = np.tile(np.arange(GRID_W), rows).astype(np.float64)
    nf = dh // 4
    inv = (np.float32(ROPE_BASE) ** (-np.arange(nf, dtype=np.float32) / nf)).astype(np.float64)
    ang = np.concatenate([row[:, None] * inv] * 2 + [col[:, None] * inv] * 2, axis=-1)
    ang = np.tile(ang.astype(np.float32).astype(np.float64), (1, LANES // dh))
    cos = np.concatenate([np.cos(ang), np.ones((tm, LANES))], axis=0)
    sin = np.concatenate([np.sin(ang), np.zeros((tm, LANES))], axis=0)
    return jnp.asarray(cos, F32), jnp.asarray(sin, F32)


def _filter_kernel(f_ref, w1a_ref, w1b_ref, b1_ref, w2_ref, b2_ref, fr_ref, w3a_ref, w3b_ref, dl_ref,
                   taps_ref, ssq_ref):
    f = f_ref[...]
    half = f.shape[0] // 2
    freq = fr_ref[...]
    h = jnp.sin(freq * (_dot3(f[:half], w1a_ref[...]) + _dot3(f[half:], w1b_ref[...]) + b1_ref[...]))
    h = jnp.sin(freq * (_dot3(h, w2_ref[...]) + b2_ref[...]))
    h = jnp.concatenate([_dot3(h, w3a_ref[...]), _dot3(h, w3b_ref[...])], axis=0)
    h = h * jnp.exp(-f[:, 0:1] * jnp.abs(dl_ref[...]))

    @pl.when(pl.program_id(1) == 0)
    def _():
        ssq_ref[...] = jnp.zeros_like(ssq_ref)

    ssq_ref[...] += jnp.sum(h * h, axis=0, keepdims=True)
    taps_ref[...] = h * f[:, H_EMB:H_EMB + 1]


def _filter_features(L):
    t = np.linspace(0.0, 1.0, L, dtype=np.float32).astype(np.float64)[:, None]
    bands = (H_EMB - 1) // 2
    w = (np.float32(2.0 * math.pi) * np.arange(L, dtype=np.float32) / np.float32(L)).astype(np.float64)[:, None]
    f = np.linspace(1e-4, bands - 1, bands, dtype=np.float32).astype(np.float64)[None, :]
    fw = (f.astype(np.float32) * w.astype(np.float32)).astype(np.float64)
    z = np.concatenate([t, np.cos(fw), -np.sin(fw)], axis=-1)
    feat = np.zeros((2 * L, LANES), np.float64)
    feat[:L, :H_EMB] = z
    idx = (L - np.arange(L)) % L
    feat[L:, :H_EMB] = z[idx]
    feat[:, H_EMB] = 1.0
    feat[L, H_EMB] = 0.0
    return jnp.asarray(feat, F32)


def _hyena_filters(L, w1, b1, w2, b2, w3, freq, delta, C):
    assert 2 * H_FFN == LANES
    pad = LANES - H_FFN
    w1a = jnp.pad(w1, ((0, LANES - H_EMB), (0, pad)))
    w1b = jnp.pad(w1, ((0, LANES - H_EMB), (pad, 0)))
    w2d = jnp.pad(w2, ((0, pad), (0, pad))) + jnp.pad(w2, ((pad, 0), (pad, 0)))
    w3a = jnp.pad(w3, ((0, pad), (0, 0)))
    w3b = jnp.pad(w3, ((pad, 0), (0, 0)))
    row = lambda a: jnp.tile(a.reshape(1, H_FFN), (1, 2))
    tr = min(L, 512)
    nr = L // tr
    const = lambda shape: pl.BlockSpec(shape, lambda hf, r: (0, 0))
    w3spec = pl.BlockSpec((LANES, C), lambda hf, r: (0, hf))
    return pl.pallas_call(
        _filter_kernel,
        out_shape=(jax.ShapeDtypeStruct((2 * L, C), F32), jax.ShapeDtypeStruct((2, 1, C), F32)),
        grid=(2, nr),
        in_specs=[pl.BlockSpec((tr, LANES), lambda hf, r: (hf * nr + r, 0)),
                  const((LANES, LANES)), const((LANES, LANES)), const((1, LANES)),
                  const((LANES, LANES)), const((1, LANES)), const((1, LANES)),
                  w3spec, w3spec,
                  pl.BlockSpec((1, C), lambda hf, r: (0, hf))],
        out_specs=(pl.BlockSpec((tr, C), lambda hf, r: (hf * nr + r, 0)),
                   pl.BlockSpec((None, 1, C), lambda hf, r: (hf, 0, 0))),
        compiler_params=_cparams(("parallel", "arbitrary")), name="hyena_filter",
    )(_filter_features(L), w1a, w1b, row(b1), w2d, row(b2), row(freq), w3a, w3b, delta.reshape(1, 2 * C))


def _cblock(m):
    return np.block([[m.real, -m.imag], [m.imag, m.real]])


def _dft_consts(L):
    N = 2 * L
    N2 = LANES
    N1 = N // N2
    half = N1 // 2
    n1 = np.arange(N1)
    n2 = np.arange(N2)
    F1 = np.exp(-2j * np.pi * np.outer(n1, n1) / N1)
    F2 = np.exp(-2j * np.pi * np.outer(n2, n2) / N2)
    a_data = _cblock(F1[:, :half])
    a_taps = np.concatenate([F1.real, F1.imag], axis=0)
    b_fwd = _cblock(F2)
    b_inv = _cblock(np.conj(F2))
    fin = _cblock(np.conj(F1)[:half, :]) / N
    ang = 2.0 * np.pi * np.outer(n2, n1) / N
    tw = dict(c_a=np.cos(ang)[:, :, None], s_a=np.sin(ang)[:, :, None],
              c_b=np.cos(ang.T)[:, :, None], s_b=np.sin(ang.T)[:, :, None])
    as_bf16 = lambda m: jnp.asarray(np.asarray(m, np.float32)).astype(BF16)
    return dict(N1=N1, a_data=as_bf16(a_data), a_taps=as_bf16(a_taps),
                b_fwd=as_bf16(b_fwd), b_inv=as_bf16(b_inv), fin=as_bf16(fin),
                tw={k: jnp.asarray(v, F32) for k, v in tw.items()})


def _time_slice_kernel(*refs, n_in, n_vmem, n_out, N1, compute):
    ins, vmem = refs[:n_in], refs[n_in:n_in + n_vmem]
    outs = refs[n_in + n_vmem:n_in + n_vmem + n_out]
    in_buf, out_buf, in_sem, out_sem = refs[n_in + n_vmem + n_out:]
    j = pl.program_id(0)
    n = pl.num_programs(0)
    slot = j % 2

    def in_copies(step, sl):
        return [pltpu.make_async_copy(src.at[pl.ds(0, N1), step, :], in_buf.at[sl, i], in_sem.at[sl])
                for i, src in enumerate(ins)]

    def out_copies(step, sl):
        return [pltpu.make_async_copy(out_buf.at[sl, i], dst.at[:, step, :], out_sem.at[sl])
                for i, dst in enumerate(outs)]

    @pl.when(j == 0)
    def _():
        for cp in in_copies(0, 0):
            cp.start()

    @pl.when(j + 1 < n)
    def _():
        for cp in in_copies(j + 1, 1 - slot):
            cp.start()

    for cp in in_copies(j, slot):
        cp.wait()

    @pl.when(j >= 2)
    def _():
        for cp in out_copies(j - 2, slot):
            cp.wait()

    for i, r in enumerate(compute([in_buf[slot, i] for i in range(n_in)], vmem)):
        out_buf[slot, i] = r
    for cp in out_copies(j, slot):
        cp.start()

    @pl.when(j == n - 1)
    def _():
        for cp in out_copies(j - 1, 1 - slot) + out_copies(j, slot):
            cp.wait()


def _time_slice_call(compute, hbm_inputs, vmem_inputs, vmem_specs, n_out, *, N1, C, name):
    assert LANES >= 2
    any_spec = pl.BlockSpec(memory_space=pl.ANY)
    shape = jax.ShapeDtypeStruct((N1, LANES, C), F32)
    n_in = len(hbm_inputs)
    return pl.pallas_call(
        functools.partial(_time_slice_kernel, n_in=n_in, n_vmem=len(vmem_inputs), n_out=n_out, N1=N1,
                          compute=compute),
        out_shape=(shape,) * n_out, grid=(LANES,),
        in_specs=[any_spec] * n_in + list(vmem_specs),
        out_specs=(any_spec,) * n_out,
        scratch_shapes=[pltpu.VMEM((2, n_in, N1, C), F32), pltpu.VMEM((2, n_out, N1, C), F32),
                        pltpu.SemaphoreType.DMA((2,)), pltpu.SemaphoreType.DMA((2,))],
        compiler_params=_cparams(("arbitrary",)), name=name,
    )(*hbm_inputs, *vmem_inputs)


def _pack_complex(re, im):
    hi = lax.bitcast_convert_type(re.astype(BF16).astype(F32), jnp.uint32)
    lo = lax.bitcast_convert_type(im.astype(BF16).astype(F32), jnp.uint32)
    return lax.bitcast_convert_type(hi | (lo >> 16), F32)


def _unpack_complex(p):
    w = lax.bitcast_convert_type(p, jnp.uint32)
    re = lax.bitcast_convert_type(w & jnp.uint32(0xFFFF0000), F32)
    im = lax.bitcast_convert_type(w << 16, F32)
    return jnp.concatenate([re, im], axis=0).astype(BF16)


def _fft_a_compute(xs, vmem):
    m_ref, c_ref, s_ref = vmem
    r = _dot(m_ref[...], xs[0].astype(BF16))
    n1 = r.shape[0] // 2
    re, im = r[:n1], r[n1:]
    c, s = c_ref[...], s_ref[...]
    return (_pack_complex(re * c + im * s, im * c - re * s),)


def _fft_stage_a(x3, mat, tw, *, N1, C):
    twspec = pl.BlockSpec((None, N1, 1), lambda j: (j, 0, 0))
    return _time_slice_call(_fft_a_compute, [x3], [mat, tw["c_a"], tw["s_a"]],
                            [pl.BlockSpec(mat.shape, lambda j: (0, 0)), twspec, twspec], 1,
                            N1=N1, C=C, name="hyena_fft_a")[0]


def _fft_mid_kernel(a_ref, t_ref, ssq, f_ref, i_ref, c_ref, s_ref, b_ref):
    N2 = LANES
    x = _dot(f_ref[...], _unpack_complex(a_ref[...]))
    h = _dot(f_ref[...], _unpack_complex(t_ref[...]))
    scale = lax.rsqrt(ssq[0] + ssq[1] + EPS)
    xr, xi, hr, hi = x[:N2], x[N2:], h[:N2] * scale, h[N2:] * scale
    y = jnp.concatenate([xr * hr - xi * hi, xr * hi + xi * hr], axis=0)
    r = _dot(i_ref[...], y.astype(BF16))
    re, im = r[:N2], r[N2:]
    c, s = c_ref[...], s_ref[...]
    b_ref[...] = _pack_complex(re * c - im * s, im * c + re * s)


def _fft_mid(A, Tp, ssq, consts, *, C, ct):
    N1 = consts["N1"]
    blk = pl.BlockSpec((None, LANES, ct), lambda k1, c: (k1, 0, c))
    const = pl.BlockSpec((2 * LANES, 2 * LANES), lambda k1, c: (0, 0))
    twspec = pl.BlockSpec((None, LANES, 1), lambda k1, c: (k1, 0, 0))
    return pl.pallas_call(
        _fft_mid_kernel, out_shape=jax.ShapeDtypeStruct((N1, LANES, C), F32), grid=(N1, C // ct),
        in_specs=[blk, blk, pl.BlockSpec((2, 1, ct), lambda k1, c: (0, 0, c)),
                  const, const, twspec, twspec],
        out_specs=blk,
        compiler_params=_cparams(("parallel", "parallel")), name="hyena_fft_mid",
    )(A, Tp, ssq, consts["b_fwd"], consts["b_inv"], consts["tw"]["c_b"], consts["tw"]["s_b"])


def _fft_fin_compute(xs, vmem):
    b, x0, z = xs
    m_ref, bias = vmem
    r = _dot(m_ref[...], _unpack_complex(b))
    return (x0 * (r + bias[...] * z),)


def _fft_final(Bp, x0_3d, z_3d, bias, consts, *, C):
    mat = consts["fin"]
    return _time_slice_call(_fft_fin_compute, [Bp, x0_3d, z_3d], [mat, bias],
                            [pl.BlockSpec(mat.shape, lambda j: (0, 0)), pl.BlockSpec((1, C), lambda j: (0, 0))],
                            1, N1=consts["N1"], C=C, name="hyena_fft_final")[0]


def _hyena_latent(x0, z, taps, ssq, bias, consts, *, B, L, C):
    assert B == 2
    N1 = consts["N1"]
    ct = C
    T = x0.shape[0]
    z3 = z.reshape(T // LANES, LANES, C)
    x03 = x0.reshape(T // LANES, LANES, C)
    A = _fft_stage_a(z3, consts["a_data"], consts["tw"], N1=N1, C=C)
    Tp = _fft_stage_a(taps.reshape(N1, LANES, C), consts["a_taps"], consts["tw"], N1=N1, C=C)
    Bp = _fft_mid(A, Tp, ssq, consts, C=C, ct=ct)
    y = _fft_final(Bp, x03, z3, bias.reshape(1, C).astype(F32), consts, C=C)
    return y.reshape(B * L, C)


def _hyena_ctx_kernel(z, x0, taps, ssq, bias, dh, dl, th, tl, ih, il, y, *, Lc):
    n = 2 * Lc
    x = _dot3c(dh[...], dl[...], z[...])
    h = _dot3c(th[...], tl[...], taps[...])
    scale = lax.rsqrt(ssq[0] + ssq[1] + EPS)
    xr, xi, hr, hi = x[:n], x[n:], h[:n] * scale, h[n:] * scale
    r = _dot3c(ih[...], il[...], jnp.concatenate([xr * hr - xi * hi, xr * hi + xi * hr], axis=0))
    y[...] = x0[...] * (r + bias[...] * z[...])


def _hyena_ctx(x0, z, taps, ssq, bias, *, B, L, Lc, C):
    assert B == 2 and (B * L) % (2 * Lc) == 0
    n = 2 * Lc
    idx = np.arange(n)
    F = np.exp(-2j * np.pi * np.outer(idx, idx) / n)
    d = _np_split(_cblock(F[:, :Lc]))
    t = _np_split(np.concatenate([F.real, F.imag], axis=0))
    inv = _np_split(_cblock(np.conj(F)[:Lc, :]) / n)
    ct = 256
    r0 = (B * L) // n
    both = pl.BlockSpec((n, ct), lambda c: (r0, c))
    const = lambda m: pl.BlockSpec(m.shape, lambda c: (0, 0))
    return pl.pallas_call(
        functools.partial(_hyena_ctx_kernel, Lc=Lc),
        out_shape=jax.ShapeDtypeStruct((n, C), F32),
        grid=(C // ct,),
        in_specs=[both, both,
                  pl.BlockSpec((n, ct), lambda c: (0, c)),
                  pl.BlockSpec((2, 1, ct), lambda c: (0, 0, c)),
                  pl.BlockSpec((1, ct), lambda c: (0, c)),
                  const(d[0]), const(d[1]), const(t[0]), const(t[1]), const(inv[0]), const(inv[1])],
        out_specs=pl.BlockSpec((n, ct), lambda c: (0, c)),
        compiler_params=_cparams(("parallel",)), name="hyena_ctx",
    )(z, x0, taps, ssq, bias.reshape(1, C).astype(F32), *d, *t, *inv)


def _merge_kernel(*refs, with_ctx, n_lat_tiles):
    if with_ctx:
        (ym, ya, yh, ymc, yac, yhc, gm, ga, gh, mb, wm, wa, wh, wo, x_ref, g1, o_ref) = refs
        is_ctx = pl.program_id(0) >= n_lat_tiles
        pick = lambda lat, ctx: jnp.where(is_ctx, ctx[...], lat[...])
        m, a, h = pick(ym, ymc), pick(ya, yac), pick(yh, yhc)
    else:
        (ym, ya, yh, gm, ga, gh, mb, wm, wa, wh, wo, x_ref, g1, o_ref) = refs
        m, a, h = ym[...], ya[...], yh[...]
    D = x_ref.shape[1]
    bias = mb[...]
    gate = lambda g, k: _sigmoid(g[...].astype(F32) + bias[:, k * D:(k + 1) * D])
    y = (gate(gm, 0) * _dot(m, wm[...]) + gate(ga, 1) * _dot(a, wa[...])
         + gate(gh, 2) * _dot(h.astype(BF16), wh[...]))
    o_ref[...] = x_ref[...] + g1[...] * _dot(y.astype(BF16), wo[...])


def _merge(X, P, gcol0, branches, merge_b, weights, layer, mod, gate_idx, *, n_rows, B, L, with_ctx):
    D = X.shape[1]
    tm = ROW_TILE
    nlt, tps = (B * L) // tm, L // tm
    grp = functools.partial(_group_of_tile, n_lat_tiles=nlt, tiles_per_seq=tps, n_batch=B)
    lat = pl.BlockSpec((tm, D), lambda i: (jnp.minimum(i, nlt - 1), 0))
    ctx = pl.BlockSpec((tm, D), lambda i: (jnp.maximum(i - nlt, 0), 0))
    gb = gcol0 // D
    gspec = lambda k: pl.BlockSpec((tm, D), lambda i: (i, gb + k))
    wspec = pl.BlockSpec((None, D, D), lambda i: (layer, 0, 0))
    in_specs = ([lat] * 3 + ([ctx] * 3 if with_ctx else []) + [gspec(0), gspec(1), gspec(2)]
                + [pl.BlockSpec((1, 3 * D), lambda i: (0, 0))] + [wspec] * 4
                + [pl.BlockSpec((tm, D), lambda i: (i, 0)),
                   pl.BlockSpec((None, None, 1, D), lambda i: (grp(i), gate_idx, 0, 0))])
    return pl.pallas_call(
        functools.partial(_merge_kernel, with_ctx=with_ctx, n_lat_tiles=nlt),
        out_shape=jax.ShapeDtypeStruct((n_rows, D), F32),
        grid=(n_rows // tm,), in_specs=in_specs,
        out_specs=pl.BlockSpec((tm, D), lambda i: (i, 0)),
        compiler_params=_cparams(("parallel",)), name="merge",
    )(*branches, P, P, P, merge_b.reshape(1, 3 * D).astype(F32), *weights, X, mod)


EXPERT_BITS = 5
assert 1 << EXPERT_BITS == N_EXPERTS


def _slot_kernel(pstart, code_ref, o_ref):
    code = code_ref[...]
    expert = code & (N_EXPERTS - 1)
    slot = code >> EXPERT_BITS
    for e in range(N_EXPERTS):
        slot = slot + jnp.where(expert == e, pstart[e], 0)
    o_ref[...] = slot


def _slot_table(codes, pstart):
    A = codes.shape[0]
    assert A % (8 * LANES) == 0
    shape = (A // LANES, LANES)
    whole = pl.BlockSpec(shape, lambda i, ps: (0, 0))
    return pl.pallas_call(
        _slot_kernel,
        out_shape=jax.ShapeDtypeStruct(shape, jnp.int32),
        grid_spec=pltpu.PrefetchScalarGridSpec(num_scalar_prefetch=1, grid=(1,), in_specs=[whole], out_specs=whole),
        compiler_params=_cparams(("arbitrary",)), name="moe_slots",
    )(pstart, codes.reshape(shape)).reshape(A)


def _dispatch_kernel(slots, tok_ref, xs_zero, xs_out, sem, *, n_tok):
    del xs_zero
    R = tok_ref.shape[0]
    base = pl.program_id(0) * R

    def copy(pos, r):
        return pltpu.make_async_copy(tok_ref.at[pl.ds(r, 1), :], xs_out.at[pl.ds(slots[pos], 1), :], sem)

    for wait in (False, True):
        for k in range(TOP_K):
            for r in range(R):
                cp = copy(k * n_tok + base + r, r)
                cp.wait() if wait else cp.start()


def _moe_dispatch(tok, slots, P):
    N, D = tok.shape
    tm = MOE_TOKEN_TILE
    return pl.pallas_call(
        functools.partial(_dispatch_kernel, n_tok=N),
        out_shape=jax.ShapeDtypeStruct((P, D), F32),
        grid_spec=pltpu.PrefetchScalarGridSpec(
            num_scalar_prefetch=1, grid=(N // tm,),
            in_specs=[pl.BlockSpec((tm, D), lambda i, sl: (i, 0)),
                      pl.BlockSpec(memory_space=pl.ANY)],
            out_specs=pl.BlockSpec(memory_space=pl.ANY),
            scratch_shapes=[pltpu.SemaphoreType.DMA]),
        input_output_aliases={2: 0},
        compiler_params=_cparams(("arbitrary",)), name="moe_dispatch",
    )(slots, tok, jnp.zeros((P, D), F32))


def _moe_kernel(blk_e, xs_ref, w1_ref, w3_ref, w2_ref, y_ref, w1b, w3b, w2b):
    i = pl.program_id(0)

    @pl.when((i == 0) | (blk_e[i] != blk_e[jnp.maximum(i - 1, 0)]))
    def _():
        w1b[...] = w1_ref[...].astype(BF16)
        w3b[...] = w3_ref[...].astype(BF16)
        w2b[...] = w2_ref[...].astype(BF16)

    x = xs_ref[...].astype(BF16)
    a = _dot(x, w1b[...])
    g = (a * _sigmoid(a)) * _dot(x, w3b[...])
    y_ref[...] = _dot(g.astype(BF16), w2b[...])


def _moe_experts(xs, blk_e, w1, w3, w2, layer):
    P, D = xs.shape
    De = w1.shape[3]
    return pl.pallas_call(
        _moe_kernel,
        out_shape=jax.ShapeDtypeStruct((P, D), F32),
        grid_spec=pltpu.PrefetchScalarGridSpec(
            num_scalar_prefetch=1, grid=(P // MOE_BLOCK,),
            in_specs=[pl.BlockSpec((MOE_BLOCK, D), lambda i, be: (i, 0)),
                      pl.BlockSpec((None, None, D, De), lambda i, be: (layer, be[i], 0, 0)),
                      pl.BlockSpec((None, None, D, De), lambda i, be: (layer, be[i], 0, 0)),
                      pl.BlockSpec((None, None, De, D), lambda i, be: (layer, be[i], 0, 0))],
            out_specs=pl.BlockSpec((MOE_BLOCK, D), lambda i, be: (i, 0)),
            scratch_shapes=[pltpu.VMEM((D, De), BF16), pltpu.VMEM((D, De), BF16), pltpu.VMEM((De, D), BF16)]),
        compiler_params=_cparams(("arbitrary",)), name="moe_experts",
    )(blk_e, xs, w1, w3, w2)


def _combine_kernel(slots, yb_hbm, x_ref, g2, wt_ref, o_ref, buf, sems, *, n_tok):
    i = pl.program_id(0)
    n = pl.num_programs(0)
    slot = i % 2
    R = x_ref.shape[0]

    def gather(tile, slot, wait):
        dst = buf.at[slot]
        for k in range(TOP_K):
            for r in range(R):
                src = slots[k * n_tok + tile * R + r]
                cp = pltpu.make_async_copy(yb_hbm.at[pl.ds(src, 1), :], dst.at[k, pl.ds(r, 1), :], sems.at[slot])
                cp.wait() if wait else cp.start()

    @pl.when(i == 0)
    def _():
        gather(0, 0, False)

    @pl.when(i + 1 < n)
    def _():
        gather(i + 1, 1 - slot, False)

    gather(i, slot, True)
    w = wt_ref[...]
    acc = w[:, 0:1] * buf[slot, 0]
    for k in range(1, TOP_K):
        acc = acc + w[:, k:k + 1] * buf[slot, k]
    o_ref[...] = x_ref[...] + g2[...] * acc


def _moe_combine(yb, slots, wts, X, mod, gate_idx, *, n_rows, B, L):
    D = X.shape[1]
    tm = MOE_TOKEN_TILE
    nlt, tps = (B * L) // tm, L // tm
    grp = functools.partial(_group_of_tile, n_lat_tiles=nlt, tiles_per_seq=tps, n_batch=B)
    return pl.pallas_call(
        functools.partial(_combine_kernel, n_tok=n_rows),
        out_shape=jax.ShapeDtypeStruct((n_rows, D), F32),
        grid_spec=pltpu.PrefetchScalarGridSpec(
            num_scalar_prefetch=1, grid=(n_rows // tm,),
            in_specs=[pl.BlockSpec(memory_space=pl.ANY),
                      pl.BlockSpec((tm, D), lambda i, sl: (i, 0)),
                      pl.BlockSpec((None, None, 1, D), lambda i, sl: (grp(i), gate_idx, 0, 0)),
                      pl.BlockSpec((tm, LANES), lambda i, sl: (i, 0))],
            out_specs=pl.BlockSpec((tm, D), lambda i, sl: (i, 0)),
            scratch_shapes=[pltpu.VMEM((2, TOP_K, tm, D), F32), pltpu.SemaphoreType.DMA((2,))]),
        compiler_params=_cparams(("arbitrary",)), name="moe_combine",
    )(slots, yb, X, mod, wts)


def _dispatch_tables(code, counts, n_tok):
    E, K = N_EXPERTS, TOP_K
    cnt = counts[0, :E].astype(jnp.int32)
    padded = (cnt + MOE_BLOCK - 1) // MOE_BLOCK * MOE_BLOCK
    pend = jnp.cumsum(padded)
    pstart = (pend - padded).astype(jnp.int32)
    P = -(-(n_tok * K + E * (MOE_BLOCK - 1)) // MOE_BLOCK) * MOE_BLOCK
    blk_row = jnp.arange(P // MOE_BLOCK, dtype=jnp.int32) * MOE_BLOCK
    blk_e = jnp.minimum(jnp.sum(pend[None, :] <= blk_row[:, None], axis=1), E - 1).astype(jnp.int32)
    return _slot_table(code[:, :K].T.reshape(-1), pstart), P, blk_e


def kernel(x, c, ctx, c_ctx, mod_w, mod_b, norm1_g, norm2_g, w_in, merge_b, m_conv_w, m_conv_b, m_gate_b, m_norm_g, a_qnorm_g, a_knorm_g, a_lambda, a_subln_g, h_conv_w, h_conv_b, h_ffn_w1, h_ffn_b1, h_ffn_w2, h_ffn_b2, h_ffn_w3, h_freq, h_decay, h_bias, w_br_m, w_br_a, w_br_h, w_out, router_w, router_b, e_w1, e_w3, e_w2):
    B, L, D = x.shape
    Lc = ctx.shape[1]
    depth = mod_w.shape[0]
    NL, NC = B * L, B * Lc
    T = NL + NC
    assert B + 1 <= 8 and Lc == ROW_TILE and L % min(L, 1024) == 0 and L % 512 == 0
    assert NL % Lc == 0 and D % LANES == 0
    width = D
    a_dh = D // (2 * A_HEADS)
    n_gates = 4 * M_HEADS
    c_mq, c_mv, c_mo = 0, 2 * width, 3 * width
    c_aq, c_ak, c_av = 4 * width, 5 * width, 6 * width
    c_hx = 7 * width
    c_gp = 10 * width

    X = jnp.concatenate([x.reshape(NL, D), ctx.reshape(NC, D)], axis=0)
    c8 = jnp.zeros((8, D), F32).at[:B].set(c).at[B].set(c_ctx)
    cos_tab, sin_tab = _rope_tables(L, a_dh, ROW_TILE)
    fft_consts = _dft_consts(L)
    g0 = 4 * width
    w_main = jnp.concatenate([w_in[:, :, :g0], w_in[:, :, g0 + n_gates:]], axis=2).astype(BF16)
    w_gate = jnp.pad(w_in[:, :, g0:g0 + n_gates], ((0, 0), (0, 0), (0, LANES - n_gates))).astype(BF16)
    merge_w = [w.astype(BF16) for w in (w_br_m, w_br_a, w_br_h, w_out)]

    for l in range(depth):
        need_ctx = l < depth - 1
        lam_init = 0.8 - 0.6 * math.exp(-0.3 * l)
        mod = _modulation(c8, mod_w, mod_b, l)[:B + 1].reshape(B + 1, 6, 1, D)

        P, G = _inproj(X, norm1_g[l], mod, w_main, w_gate, l, L=L, B=B)

        QK = _mlstm_qk(P, m_conv_w[l], m_conv_b[l], L=L, Lc=Lc, NL=NL, width=width)
        ym, ymc = _mlstm(QK, P, G, m_gate_b[l], m_norm_g[l], B=B, L=L, Lc=Lc, width=width)

        Qh = _attn_prep(P, c_aq, a_qnorm_g[l], cos_tab, sin_tab, a_dh ** -0.5 * LOG2E, L=L, NL=NL, dh=a_dh)
        Kh = _attn_prep(P, c_ak, a_knorm_g[l], cos_tab, sin_tab, 1.0, L=L, NL=NL, dh=a_dh)
        attn = functools.partial(_attention, Qh, Kh, P, c_av, a_lambda[l], a_subln_g[l], lam_init,
                                 B=B, L=L, Lc=Lc, dh=a_dh)
        ya = attn(latent=True)

        x0, z = _hyena_pre(P, c_hx, h_conv_w[l], h_conv_b[l], L=L, Lc=Lc, NL=NL, C=width)
        filt = (h_ffn_w1[l], h_ffn_b1[l], h_ffn_w2[l], h_ffn_b2[l], h_ffn_w3[l], h_freq[l], h_decay[l])
        taps, ssq = _hyena_filters(L, *filt, width)
        yh = _hyena_latent(x0, z, taps, ssq, h_bias[l], fft_consts, B=B, L=L, C=width)

        if need_ctx:
            yac = attn(latent=False)
            taps_c, ssq_c = _hyena_filters(Lc, *filt, width)
            yhc = _hyena_ctx(x0, z, taps_c, ssq_c, h_bias[l], B=B, L=L, Lc=Lc, C=width)
            branches, n_rows = [ym, ya, yh, ymc, yac, yhc], T
        else:
            branches, n_rows = [ym, ya, yh], NL
        X = _merge(X, P, c_gp, branches, merge_b[l], merge_w, l, mod, 2,
                   n_rows=n_rows, B=B, L=L, with_ctx=need_ctx)

        tok, code, wts, counts = _norm_mod(X, norm2_g[l], mod, 3, 4, n_rows=n_rows, L=L, B=B,
                                           router=(router_w, router_b))
        slots, n_slots, blk_e = _dispatch_tables(code, counts, n_rows)
        xs = _moe_dispatch(tok, slots, n_slots)
        yb = _moe_experts(xs, blk_e, e_w1, e_w3, e_w2, l)
        X = _moe_combine(yb, slots, wts, X, mod, 5, n_rows=n_rows, B=B, L=L)
    return X[:NL].reshape(B, L, D)
```

```python
import functools
import math

import numpy as np
import jax
import jax.numpy as jnp
from jax import lax
from jax.experimental import pallas as pl
from jax.experimental.pallas import tpu as pltpu

F32 = jnp.float32
BF16 = jnp.bfloat16

GRID_W = 64
EPS = 1e-6
M_HEADS = 4
M_CHUNK = 128
A_HEADS = 8
ROPE_BASE = 10000.0
H_EMB = 33
H_FFN = 64
SHORT_CONV = 3
N_EXPERTS = 32
N_GROUPS = 4
TOP_K = 2
MOE_BLOCK = 256
MOE_TOKEN_TILE = 128

LANES = 128
BF16_SUBLANES = 16
V7X_VMEM_BYTES = 64 * 1024 * 1024
VMEM_LIMIT = V7X_VMEM_BYTES * 7 // 8

ROW_TILE = 256
LOG2E = 1.4426950408889634


def _cparams(sem):
    return pltpu.CompilerParams(dimension_semantics=sem, vmem_limit_bytes=VMEM_LIMIT)


def _split(x):
    hi = x.astype(BF16)
    lo = (x - hi.astype(F32)).astype(BF16)
    return hi, lo


def _dot(a, b):
    return jnp.dot(a, b, preferred_element_type=F32)


def _dot3(a, b):
    ah, al = _split(a)
    bh, bl = _split(b)
    return _dot(ah, bh) + _dot(al, bh) + _dot(ah, bl)


def _dot3c(ch, cl, x):
    xh, xl = _split(x)
    return _dot(ch, xh) + _dot(cl, xh) + _dot(ch, xl)


def _np_split(a):
    a = jnp.asarray(np.asarray(a, np.float32))
    hi = a.astype(BF16)
    lo = (a - hi.astype(F32)).astype(BF16)
    return hi, lo


def _sigmoid(x):
    return 1.0 / (1.0 + jnp.exp(-x))


def _log_sigmoid(x):
    return jnp.minimum(x, 0.0) - jnp.log(1.0 + jnp.exp(-jnp.abs(x)))


def _mod_kernel(c_ref, w_ref, b_ref, o_ref):
    c = c_ref[...]
    o_ref[...] = _dot3(c * _sigmoid(c), w_ref[...]) + b_ref[...]


def _modulation(c8, w, b, layer):
    depth, D, N = w.shape
    tn = 1536
    return pl.pallas_call(
        _mod_kernel,
        out_shape=jax.ShapeDtypeStruct((8, N), F32),
        grid=(N // tn,),
        in_specs=[pl.BlockSpec((8, D), lambda j: (0, 0)),
                  pl.BlockSpec((None, D, tn), lambda j: (layer, 0, j)),
                  pl.BlockSpec((None, 1, tn), lambda j: (layer, 0, j))],
        out_specs=pl.BlockSpec((8, tn), lambda j: (0, j)),
        compiler_params=_cparams(("parallel",)),
        name="modulation",
    )(c8, w, b.reshape(depth, 1, N))


def _norm_mod_router_kernel(x_ref, g_ref, sh_ref, sc_ref, rw_ref, rb_ref, o_ref, code_ref, wt_ref, cnt_ref, carry):
    E, G = N_EXPERTS, N_GROUPS
    gs = E // G

    @pl.when(pl.program_id(0) == 0)
    def _():
        carry[...] = jnp.zeros_like(carry)

    x = x_ref[...]
    y = x * lax.rsqrt(jnp.mean(x * x, axis=-1, keepdims=True) + EPS) * g_ref[...]
    h = y * (1.0 + sc_ref[...]) + sh_ref[...]
    o_ref[...] = h
    tm = h.shape[0]
    s = _sigmoid(_dot3(h, rw_ref[...]))
    lane = lax.broadcasted_iota(jnp.int32, (1, LANES), 1)
    lane_f = lane.astype(F32)
    sb = jnp.where(lane < E, s + rb_ref[...], -jnp.inf)
    far = float(LANES)
    best = jnp.full((tm, 1), -jnp.inf, F32)
    e1 = jnp.zeros((tm, 1), F32)
    e2 = jnp.zeros((tm, 1), F32)
    for g in range(G):
        mg = jnp.where((lane >= g * gs) & (lane < (g + 1) * gs), sb, -jnp.inf)
        m1 = jnp.max(mg, axis=-1, keepdims=True)
        i1 = jnp.min(jnp.where(mg == m1, lane_f, far), axis=-1, keepdims=True)
        mg2 = jnp.where(lane_f == i1, -jnp.inf, mg)
        m2 = jnp.max(mg2, axis=-1, keepdims=True)
        i2 = jnp.min(jnp.where(mg2 == m2, lane_f, far), axis=-1, keepdims=True)
        score = m1 + m2
        take = score > best
        best = jnp.where(take, score, best)
        e1 = jnp.where(take, i1, e1)
        e2 = jnp.where(take, i2, e2)
    oh1 = lane_f == e1
    oh2 = lane_f == e2
    s1 = jnp.sum(jnp.where(oh1, s, 0.0), axis=-1, keepdims=True)
    s2 = jnp.sum(jnp.where(oh2, s, 0.0), axis=-1, keepdims=True)
    den = s1 + s2
    r = lax.broadcasted_iota(jnp.int32, (tm, tm), 0)
    c = lax.broadcasted_iota(jnp.int32, (tm, tm), 1)
    lower = (c < r).astype(BF16)
    o1 = oh1.astype(F32)
    o2 = oh2.astype(F32)
    cum1 = _dot(lower, o1.astype(BF16))
    cum2 = _dot(lower, o2.astype(BF16))
    tot1 = jnp.sum(o1, axis=0, keepdims=True)
    base = carry[...]
    rank1 = jnp.sum(jnp.where(oh1, base + cum1, 0.0), axis=-1, keepdims=True)
    rank2 = jnp.sum(jnp.where(oh2, base + tot1 + cum2, 0.0), axis=-1, keepdims=True)
    total = base + tot1 + jnp.sum(o2, axis=0, keepdims=True)
    carry[...] = total
    cnt_ref[...] = total
    code1 = (rank1 * E + e1).astype(jnp.int32)
    code2 = (rank2 * E + e2).astype(jnp.int32)
    code_ref[...] = jnp.where(lane == 0, code1, jnp.where(lane == 1, code2, 0))
    wt_ref[...] = jnp.where(lane == 0, s1 / den, jnp.where(lane == 1, s2 / den, 0.0))


def _group_of_tile(i, n_lat_tiles, tiles_per_seq, n_batch):
    return jnp.where(i < n_lat_tiles, i // tiles_per_seq, n_batch)


def _norm_mod(x, g, mod, shift_idx, scale_idx, *, n_rows, L, B, router):
    D = x.shape[1]
    tm = ROW_TILE
    nlt, tps = (B * L) // tm, L // tm
    grp = functools.partial(_group_of_tile, n_lat_tiles=nlt, tiles_per_seq=tps, n_batch=B)
    in_specs = [pl.BlockSpec((tm, D), lambda i: (i, 0)),
                pl.BlockSpec((1, D), lambda i: (0, 0)),
                pl.BlockSpec((None, None, 1, D), lambda i: (grp(i), shift_idx, 0, 0)),
                pl.BlockSpec((None, None, 1, D), lambda i: (grp(i), scale_idx, 0, 0))]
    args = [x, g.reshape(1, D), mod, mod]
    router_w, router_b = router
    E = router_w.shape[1]
    assert E == N_EXPERTS
    rw = jnp.pad(router_w, ((0, 0), (0, LANES - E)))
    rb = jnp.pad(router_b.astype(F32).reshape(1, E), ((0, 0), (0, LANES - E)))
    lanes = pl.BlockSpec((tm, LANES), lambda i: (i, 0))
    return pl.pallas_call(
        _norm_mod_router_kernel,
        out_shape=(jax.ShapeDtypeStruct((n_rows, D), F32),
                   jax.ShapeDtypeStruct((n_rows, LANES), jnp.int32),
                   jax.ShapeDtypeStruct((n_rows, LANES), F32),
                   jax.ShapeDtypeStruct((1, LANES), F32)),
        grid=(n_rows // tm,),
        in_specs=in_specs + [pl.BlockSpec((D, LANES), lambda i: (0, 0)),
                             pl.BlockSpec((1, LANES), lambda i: (0, 0))],
        out_specs=(pl.BlockSpec((tm, D), lambda i: (i, 0)), lanes, lanes,
                   pl.BlockSpec((1, LANES), lambda i: (0, 0))),
        scratch_shapes=[pltpu.VMEM((1, LANES), F32)],
        compiler_params=_cparams(("arbitrary",)), name="norm_mod_router",
    )(*args, rw, rb)


def _largest_tile(n, cap, step):
    return max(t for t in range(step, cap + 1, step) if n % t == 0)


def _inproj_kernel(x_ref, g_ref, mod_ref, w_ref, wg_ref, gb_ref, o_ref, gate_ref, h_scr, *, L, n_batch):
    tm = x_ref.shape[0]

    @pl.when(pl.program_id(1) == 0)
    def _():
        x = x_ref[...]
        y = x * lax.rsqrt(jnp.mean(x * x, axis=-1, keepdims=True) + EPS) * g_ref[...]
        row = pl.program_id(0) * tm + lax.broadcasted_iota(jnp.int32, (tm, 1), 0)
        shift, scale = mod_ref[n_batch, 0], mod_ref[n_batch, 1]
        for b in range(n_batch - 1, -1, -1):
            in_b = row < (b + 1) * L
            shift = jnp.where(in_b, mod_ref[b, 0], shift)
            scale = jnp.where(in_b, mod_ref[b, 1], scale)
        h_scr[...] = (y * (1.0 + scale) + shift).astype(h_scr.dtype)
        gates = _dot(h_scr[...], wg_ref[...]) + gb_ref[...]
        lane = lax.broadcasted_iota(jnp.int32, (1, LANES), 1)
        forget = (lane // M_HEADS) % 2 == 1
        gate_ref[...] = jnp.where(forget, _log_sigmoid(gates), gates)

    o_ref[...] = _dot(h_scr[...], w_ref[...]).astype(o_ref.dtype)


def _inproj(X, g, mod, w_main, w_gate, gate_b, layer, *, L, B):
    T, D = X.shape
    N = w_main.shape[2]
    tm, tn = _largest_tile(T, 1536, ROW_TILE), _largest_tile(N, 1024, LANES)
    return pl.pallas_call(
        functools.partial(_inproj_kernel, L=L, n_batch=B),
        out_shape=(jax.ShapeDtypeStruct((T, N), BF16), jax.ShapeDtypeStruct((T, LANES), F32)),
        grid=(T // tm, N // tn),
        in_specs=[pl.BlockSpec((tm, D), lambda i, j: (i, 0)),
                  pl.BlockSpec((1, D), lambda i, j: (0, 0)),
                  pl.BlockSpec(mod.shape, lambda i, j: (0, 0, 0, 0)),
                  pl.BlockSpec((None, D, tn), lambda i, j: (layer, 0, j)),
                  pl.BlockSpec((None, D, LANES), lambda i, j: (layer, 0, 0)),
                  pl.BlockSpec((1, LANES), lambda i, j: (0, 0))],
        out_specs=(pl.BlockSpec((tm, tn), lambda i, j: (i, j)),
                   pl.BlockSpec((tm, LANES), lambda i, j: (i, 0))),
        scratch_shapes=[pltpu.VMEM((tm, D), BF16)],
        compiler_params=_cparams(("parallel", "arbitrary")), name="inproj",
    )(X, g.reshape(1, D), mod, w_main, w_gate,
      jnp.pad(gate_b.astype(F32).reshape(1, -1), ((0, 0), (0, LANES - gate_b.shape[0]))))


def _seq_edge_flags(tm, L, Lc, NL):
    r0 = pl.program_id(0) * tm
    lat = r0 < NL
    start = jnp.where(lat, r0 % L == 0, (r0 - NL) % Lc == 0)
    end = jnp.where(lat, (r0 + tm) % L == 0, (r0 + tm - NL) % Lc == 0)
    return jnp.where(start, 0.0, 1.0), jnp.where(end, 0.0, 1.0)


def _conv3(cur_ref, prev_ref, next_ref, w_ref, b_ref, keep_prev, keep_next):
    cur = cur_ref[...].astype(F32)
    tm = cur.shape[0]
    prev_row = prev_ref[BF16_SUBLANES - 1:BF16_SUBLANES, :].astype(F32) * keep_prev
    next_row = next_ref[0:1, :].astype(F32) * keep_next
    row = lax.broadcasted_iota(jnp.int32, (tm, 1), 0)
    up = jnp.where(row == 0, prev_row, pltpu.roll(cur, 1, 0))
    dn = jnp.where(row == tm - 1, next_row, pltpu.roll(cur, tm - 1, 0))
    return b_ref[...] + up * w_ref[0:1, :] + cur * w_ref[1:2, :] + dn * w_ref[2:3, :]


def _conv_specs(tm, tc, T, col_block):
    per = tm // BF16_SUBLANES
    last = T // BF16_SUBLANES - 1
    return [pl.BlockSpec((tm, tc), lambda i, j: (i, col_block + j)),
            pl.BlockSpec((BF16_SUBLANES, tc), lambda i, j: (jnp.maximum(i * per - 1, 0), col_block + j)),
            pl.BlockSpec((BF16_SUBLANES, tc), lambda i, j: (jnp.minimum((i + 1) * per, last), col_block + j))]


def _mconv_kernel(cur_ref, prev_ref, next_ref, w_ref, b_ref, scale_ref, o_ref, *, tm, L, Lc, NL):
    kp, kn = _seq_edge_flags(tm, L, Lc, NL)
    y = _conv3(cur_ref, prev_ref, next_ref, w_ref, b_ref, kp, kn)
    o_ref[...] = (y * _sigmoid(y) * scale_ref[...]).astype(o_ref.dtype)


def _mlstm_qk(P, conv_w, conv_b, *, L, Lc, NL, width):
    T = P.shape[0]
    tm, tc = ROW_TILE, 1024
    C = 2 * width
    scale = jnp.concatenate([jnp.ones((1, width), F32),
                             jnp.full((1, width), (width // M_HEADS) ** -0.5, F32)], axis=1)
    vec = lambda r: pl.BlockSpec((r, tc), lambda i, j: (0, j))
    return pl.pallas_call(
        functools.partial(_mconv_kernel, tm=tm, L=L, Lc=Lc, NL=NL),
        out_shape=jax.ShapeDtypeStruct((T, C), BF16),
        grid=(T // tm, C // tc),
        in_specs=_conv_specs(tm, tc, T, 0) + [vec(SHORT_CONV), vec(1), vec(1)],
        out_specs=pl.BlockSpec((tm, tc), lambda i, j: (i, j)),
        compiler_params=_cparams(("parallel", "parallel")), name="mlstm_qk_conv",
    )(P, P, P, conv_w, conv_b.reshape(1, C), scale)


def _hyena_pre_kernel(c0, p0, n0, c1, p1, n1, c2, p2, n2, w0, w1, w2, b0, b1, b2,
                      x0_ref, z_ref, *, tm, L, Lc, NL):
    kp, kn = _seq_edge_flags(tm, L, Lc, NL)
    x0 = _conv3(c0, p0, n0, w0, b0, kp, kn)
    x1 = _conv3(c1, p1, n1, w1, b1, kp, kn)
    v = _conv3(c2, p2, n2, w2, b2, kp, kn)
    x0_ref[...] = x0
    z_ref[...] = v * x1


def _hyena_pre(P, col0, conv_w, conv_b, *, L, Lc, NL, C):
    T = P.shape[0]
    tm, tc = ROW_TILE, 512
    nb = C // tc
    specs = []
    for part in range(3):
        specs += _conv_specs(tm, tc, T, col0 // tc + part * nb)
    wspecs = [pl.BlockSpec((SHORT_CONV, tc), lambda i, j, p=part: (0, p * nb + j)) for part in range(3)]
    bspecs = [pl.BlockSpec((1, tc), lambda i, j, p=part: (0, p * nb + j)) for part in range(3)]
    b2d = conv_b.reshape(1, 3 * C)
    out = pl.BlockSpec((tm, tc), lambda i, j: (i, j))
    return pl.pallas_call(
        functools.partial(_hyena_pre_kernel, tm=tm, L=L, Lc=Lc, NL=NL),
        out_shape=(jax.ShapeDtypeStruct((T, C), F32), jax.ShapeDtypeStruct((T, C), F32)),
        grid=(T // tm, nb),
        in_specs=specs + wspecs + bspecs,
        out_specs=(out, out),
        compiler_params=_cparams(("parallel", "parallel")), name="hyena_pre",
    )(*([P] * 9), conv_w, conv_w, conv_w, b2d, b2d, b2d)


def _mlstm_chunk(q, k, v, li_r, lf_r, li_c, lf_c, C_scr, n_scr, m_scr, fwd):
    Q = M_CHUNK
    row = lax.broadcasted_iota(jnp.int32, (Q, Q), 0)
    col = lax.broadcasted_iota(jnp.int32, (Q, Q), 1)
    mask = (col <= row) if fwd else (col >= row)
    tri_c = mask.astype(BF16)
    tri_r = ((row <= col) if fwd else (row >= col)).astype(BF16)
    lfc_h, lfc_l = _split(jnp.broadcast_to(lf_c, (Q, Q)))
    lfr_h, lfr_l = _split(jnp.broadcast_to(lf_r, (Q, Q)))
    b_cols = _dot(tri_c, lfc_h) + _dot(tri_c, lfc_l)
    b_rows = _dot(lfr_h, tri_r) + _dot(lfr_l, tri_r)
    dm = jnp.where(mask, b_cols - b_rows + li_r, -jnp.inf)
    m_prev = m_scr[0:1, 0:1]
    b_col = b_cols[:, 0:1]
    inter = b_col + m_prev
    mt = jnp.maximum(inter, jnp.max(dm, axis=-1, keepdims=True))
    s = lax.dot_general(q, k, (((1,), (1,)), ((), ())), preferred_element_type=F32) * jnp.exp(dm - mt)
    wi = jnp.exp(inter - mt)
    num = _dot(s.astype(BF16), v) + wi * _dot(q, C_scr[...].astype(BF16))
    qn = jnp.sum(q.astype(F32) * n_scr[...], axis=-1, keepdims=True)
    den = jnp.sum(s, axis=-1, keepdims=True) + wi * qn
    h = num / jnp.maximum(jnp.abs(den), jnp.exp(-mt))
    b_tot = b_cols[Q - 1:Q, 0:1] if fwd else b_cols[0:1, 0:1]
    ws = b_tot - b_col + li_c
    m_new = jnp.maximum(b_tot + m_prev, jnp.max(ws, axis=0, keepdims=True))
    decay = jnp.exp(b_tot + m_prev - m_new)
    kw = k.astype(F32) * jnp.exp(ws - m_new)
    C_scr[...] = decay * C_scr[...] + lax.dot_general(
        kw.astype(BF16), v, (((0,), (0,)), ((), ())), preferred_element_type=F32)
    n_scr[...] = decay * n_scr[...] + jnp.sum(kw, axis=0, keepdims=True)
    m_scr[...] = jnp.broadcast_to(m_new, m_scr.shape)
    return h


def _mlstm_kernel(*refs, fwd, SEG, Lc, H, dh):
    if fwd:
        (gr_ref, gc_ref, grc_ref, gcc_ref, q_ref, k_ref, v_ref, qc_ref, kc_ref, vc_ref,
         h_ref, hc_ref, *scr) = refs
    else:
        (gr_ref, gc_ref, grc_ref, gcc_ref, q_ref, k_ref, v_ref, qc_ref, kc_ref, vc_ref,
         o_ref, oc_ref, hf_ref, hfc_ref, ng_ref, y_ref, yc_ref, *scr) = refs
    C_scr, n_scr, m_scr = scr[0:H], scr[H:2 * H], scr[2 * H:3 * H]
    gi = 0 if fwd else 2

    def run(nchunks, qr, kr, vr, grr, gcr, emit):
        def body(j, carry):
            c = j if fwd else nchunks - 1 - j
            rows = pl.ds(pl.multiple_of(c * M_CHUNK, M_CHUNK), M_CHUNK)
            for hh in range(H):
                cols = slice(hh * dh, (hh + 1) * dh)
                gcs = gcr[hh, rows, :]
                h = _mlstm_chunk(qr[rows, cols], kr[rows, cols], vr[rows, cols],
                                 grr[hh, gi, pl.ds(c, 1), :], grr[hh, gi + 1, pl.ds(c, 1), :],
                                 gcs[:, gi:gi + 1], gcs[:, gi + 1:gi + 2],
                                 C_scr[hh], n_scr[hh], m_scr[hh], fwd)
                emit(rows, cols, h)
            return carry

        lax.fori_loop(0, nchunks, body, 0, unroll=2)

    def emitter(dst, hf=None, o=None):
        def emit(rows, cols, h):
            if not fwd:
                h = h + hf[rows, cols]
                y = h * lax.rsqrt(jnp.mean(h * h, axis=-1, keepdims=True) + EPS) * ng_ref[:, cols]
                h = (y * _sigmoid(o[rows, cols].astype(F32))).astype(dst.dtype)
            dst[rows, cols] = h
        return emit

    @pl.when(pl.program_id(1) == 0)
    def _():
        for r in scr:
            r[...] = jnp.zeros_like(r)
        run(Lc // M_CHUNK, qc_ref, kc_ref, vc_ref, grc_ref, gcc_ref,
            emitter(hc_ref) if fwd else emitter(yc_ref, hfc_ref, oc_ref))

    run(SEG // M_CHUNK, q_ref, k_ref, v_ref, gr_ref, gc_ref,
        emitter(h_ref) if fwd else emitter(y_ref, hf_ref, o_ref))


def _mlstm(QK, P, G, norm_g, *, B, L, Lc, width):
    H = M_HEADS
    dh = width // H
    NL = B * L
    SEG = min(L, 1024)
    S = L // SEG
    g = G[:, :4 * H]

    def gate_views(rows, n):
        a = rows.reshape(B, n, 4, H)
        return (a.transpose(0, 3, 2, 1).reshape(B, H, 4, n // M_CHUNK, M_CHUNK),
                a.transpose(0, 3, 1, 2))

    gr, gc = gate_views(g[:NL], L)
    grc, gcc = gate_views(g[NL:], Lc)
    scratch = ([pltpu.VMEM((dh, dh), F32)] * H + [pltpu.VMEM((1, dh), F32)] * H
               + [pltpu.VMEM((8, LANES), F32)] * H)

    def call(fwd, extra_in, extra_specs, out_dtype):
        seg_of = (lambda s: s) if fwd else (lambda s: S - 1 - s)
        lat = lambda cb: pl.BlockSpec((SEG, width), lambda b, s: (b * S + seg_of(s), cb))
        ctx = lambda cb: pl.BlockSpec((Lc, width), lambda b, s: (NL // Lc + b, cb))
        gate_specs = [
            pl.BlockSpec((None, H, 4, SEG // M_CHUNK, M_CHUNK), lambda b, s: (b, 0, 0, seg_of(s), 0)),
            pl.BlockSpec((None, H, SEG, 4), lambda b, s: (b, 0, seg_of(s), 0)),
            pl.BlockSpec((None, H, 4, Lc // M_CHUNK, M_CHUNK), lambda b, s: (b, 0, 0, 0, 0)),
            pl.BlockSpec((None, H, Lc, 4), lambda b, s: (b, 0, 0, 0))]
        lat_out = pl.BlockSpec((SEG, width), lambda b, s: (b * S + seg_of(s), 0))
        ctx_out = pl.BlockSpec((Lc, width), lambda b, s: (b, 0))
        return pl.pallas_call(
            functools.partial(_mlstm_kernel, fwd=fwd, SEG=SEG, Lc=Lc, H=H, dh=dh),
            out_shape=(jax.ShapeDtypeStruct((NL, width), out_dtype),
                       jax.ShapeDtypeStruct((B * Lc, width), out_dtype)),
            grid=(B, S),
            in_specs=(gate_specs + [lat(0), lat(1), lat(2), ctx(0), ctx(1), ctx(2)]
                      + extra_specs(lat, ctx, lat_out, ctx_out)),
            out_specs=(lat_out, ctx_out), scratch_shapes=scratch,
            compiler_params=_cparams(("parallel", "arbitrary")),
            name="mlstm_fwd" if fwd else "mlstm_bwd",
        )(gr, gc, grc, gcc, QK, QK, P, QK, QK, P, *extra_in)

    hf, hfc = call(True, [], lambda *_: [], F32)
    return call(False, [P, P, hf, hfc, jnp.tile(norm_g.astype(F32), H).reshape(1, width)],
                lambda lat, ctx, lat_out, ctx_out: [lat(3), ctx(3), lat_out, ctx_out,
                                                    pl.BlockSpec((1, width), lambda b, s: (0, 0))], BF16)


def _attn_prep_kernel(x_ref, g_ref, cos_ref, sin_ref, o_ref, *, scale, dh):
    n_blk = x_ref.shape[1] // LANES
    r = lax.broadcasted_iota(jnp.int32, (LANES, LANES), 0)
    c = lax.broadcasted_iota(jnp.int32, (LANES, LANES), 1)
    group = (r // dh == c // dh).astype(BF16)
    lane = lax.broadcasted_iota(jnp.int32, (1, LANES), 1)
    quarter = dh // 4
    first = (lane % (2 * quarter)) < quarter
    cos = cos_ref[...]
    sin = sin_ref[...]
    for hb in range(n_blk):
        cols = slice(hb * LANES, (hb + 1) * LANES)
        x = x_ref[:, cols].astype(F32)
        hi, lo = _split(x * x)
        ms = (_dot(hi, group) + _dot(lo, group)) * (1.0 / dh)
        y = x * lax.rsqrt(ms + EPS) * g_ref[:, cols]
        rot = jnp.where(first, -pltpu.roll(y, LANES - quarter, 1), pltpu.roll(y, quarter, 1))
        o_ref[:, cols] = ((y * cos + rot * sin) * scale).astype(o_ref.dtype)


def _attn_prep(P, col0, gain, cos_tab, sin_tab, scale, *, L, NL, dh):
    T = P.shape[0]
    W = A_HEADS * 2 * dh
    tm = ROW_TILE
    nlt, tps = NL // tm, L // tm
    tab = pl.BlockSpec((tm, LANES), lambda i: (jnp.where(i < nlt, i % tps, tps), 0))
    g = jnp.tile(gain.astype(F32), W // dh).reshape(1, W)
    return pl.pallas_call(
        functools.partial(_attn_prep_kernel, scale=scale, dh=dh),
        out_shape=jax.ShapeDtypeStruct((T, W), BF16),
        grid=(T // tm,),
        in_specs=[pl.BlockSpec((tm, W), lambda i: (i, col0 // W)),
                  pl.BlockSpec((1, W), lambda i: (0, 0)), tab, tab],
        out_specs=pl.BlockSpec((tm, W), lambda i: (i, 0)),
        compiler_params=_cparams(("parallel",)), name="attn_prep",
    )(P, g, cos_tab, sin_tab)


def _attn_kernel(*refs, n_lat, tk, L, Lc, dh, lam_init):
    if n_lat:
        lam_ref, q_ref, kl_ref, vl_ref, kc_ref, vc_ref, sg_ref, o_ref, vext, acc = refs
    else:
        lam_ref, q_ref, kc_ref, vc_ref, sg_ref, o_ref, vext, acc = refs
    dv = 2 * dh
    ctx0 = n_lat * tk

    @pl.when(pl.program_id(2) == 0)
    def _():
        if n_lat:
            vext[0:L, 0:dv] = vl_ref[...]
        vext[ctx0:ctx0 + Lc, 0:dv] = vc_ref[...]
        vext[:, dv:2 * dv] = jnp.ones((vext.shape[0], dv), BF16)

    q = q_ref[...]
    lane = lax.broadcasted_iota(jnp.int32, (1, dv), 1)
    qs = (jnp.where(lane < dh, q, jnp.zeros_like(q)), jnp.where(lane >= dh, q, jnp.zeros_like(q)))
    acc[...] = jnp.zeros_like(acc)
    tq = q.shape[0]

    def update(comp, kblk, vblk, m_old):
        s = lax.dot_general(qs[comp], kblk, (((1,), (1,)), ((), ())), preferred_element_type=F32)
        m_new = jnp.maximum(m_old, jnp.max(s, axis=-1, keepdims=True))
        p = jnp.exp2(s - m_new).astype(BF16)
        acc[comp] = jnp.exp2(m_old - m_new) * acc[comp] + _dot(p, vblk)
        return m_new

    m = (jnp.full((tq, 1), -jnp.inf, F32),) * 2
    for c in range(n_lat):
        kblk, vblk = kl_ref[c * tk:(c + 1) * tk, :], vext[c * tk:(c + 1) * tk, :]
        m = update(0, kblk, vblk, m[0]), update(1, kblk, vblk, m[1])
    kblk, vblk = kc_ref[...], vext[ctx0:ctx0 + Lc, :]
    update(0, kblk, vblk, m[0])
    update(1, kblk, vblk, m[1])

    lp = lam_ref[...]
    lam = (jnp.exp(jnp.sum(lp[0:1] * lp[1:2], axis=-1, keepdims=True))
           - jnp.exp(jnp.sum(lp[2:3] * lp[3:4], axis=-1, keepdims=True)) + lam_init)
    a0, a1 = acc[0], acc[1]
    o = a0[:, 0:dv] / a0[:, dv:dv + 1] - lam * (a1[:, 0:dv] / a1[:, dv:dv + 1])
    y = o * lax.rsqrt(jnp.mean(o * o, axis=-1, keepdims=True) + EPS) * sg_ref[...] * (1.0 - lam_init)
    o_ref[...] = y.astype(o_ref.dtype)


def _attention(Qh, Kh, P, vcol0, lam_p, sub_g, lam_init, *, B, L, Lc, dh, latent):
    NL = B * L
    dv = 2 * dh
    H = A_HEADS
    vb = vcol0 // dv
    if latent:
        tq, tk = 512, min(L, 1024)
        n_lat, nq, rows_out = L // tk, L // tq, NL
        q_spec = pl.BlockSpec((tq, dv), lambda b, h, i: (b * nq + i, h))
        lat_specs = [pl.BlockSpec((L, dv), lambda b, h, i: (b, h)),
                     pl.BlockSpec((L, dv), lambda b, h, i: (b, vb + h))]
        lat_args = [Kh, P]
        o_spec = pl.BlockSpec((tq, dv), lambda b, h, i: (b * nq + i, h))
        nkeys = L + Lc
    else:
        tq, tk = Lc, 512
        n_lat, nq, rows_out = 0, 1, B * Lc
        q_spec = pl.BlockSpec((Lc, dv), lambda b, h, i: (NL // Lc + b, h))
        lat_specs, lat_args = [], []
        o_spec = pl.BlockSpec((Lc, dv), lambda b, h, i: (b, h))
        nkeys = Lc
    ctx_specs = [pl.BlockSpec((Lc, dv), lambda b, h, i: (NL // Lc + b, h)),
                 pl.BlockSpec((Lc, dv), lambda b, h, i: (NL // Lc + b, vb + h))]
    return pl.pallas_call(
        functools.partial(_attn_kernel, n_lat=n_lat, tk=tk, L=L, Lc=Lc, dh=dh, lam_init=lam_init),
        out_shape=jax.ShapeDtypeStruct((rows_out, H * dv), BF16),
        grid=(B, H, nq),
        in_specs=[pl.BlockSpec((4, dh), lambda b, h, i: (0, 0)), q_spec] + lat_specs + ctx_specs
                 + [pl.BlockSpec((1, dv), lambda b, h, i: (0, 0))],
        out_specs=o_spec,
        scratch_shapes=[pltpu.VMEM((nkeys, 2 * dv), BF16), pltpu.VMEM((2, tq, 2 * dv), F32)],
        compiler_params=_cparams(("parallel", "parallel", "arbitrary")),
        name="diff_attn_latent" if latent else "diff_attn_ctx",
    )(lam_p.astype(F32), Qh, *lat_args, Kh, P, sub_g.reshape(1, dv).astype(F32))


def _rope_tables(L, dh, tm):
    rows = L // GRID_W
    row = np.repeat(np.arange(rows), GRID_W).astype(np.float64)
    col = np.tile(np.arange(GRID_W), rows).astype(np.float64)
    nf = dh // 4
    inv = (np.float32(ROPE_BASE) ** (-np.arange(nf, dtype=np.float32) / nf)).astype(np.float64)
    ang = np.concatenate([row[:, None] * inv] * 2 + [col[:, None] * inv] * 2, axis=-1)
    ang = np.tile(ang.astype(np.float32).astype(np.float64), (1, LANES // dh))
    cos = np.concatenate([np.cos(ang), np.ones((tm, LANES))], axis=0)
    sin = np.concatenate([np.sin(ang), np.zeros((tm, LANES))], axis=0)
    return jnp.asarray(cos, F32), jnp.asarray(sin, F32)


def _filter_kernel(f_ref, w1a_ref, w1b_ref, b1_ref, w2_ref, b2_ref, fr_ref, w3a_ref, w3b_ref, dl_ref,
                   taps_ref, ssq_ref):
    f = f_ref[...]
    half = f.shape[0] // 2
    freq = fr_ref[...]
    h = jnp.sin(freq * (_dot3(f[:half], w1a_ref[...]) + _dot3(f[half:], w1b_ref[...]) + b1_ref[...]))
    h = jnp.sin(freq * (_dot3(h, w2_ref[...]) + b2_ref[...]))
    h = jnp.concatenate([_dot3(h, w3a_ref[...]), _dot3(h, w3b_ref[...])], axis=0)
    h = h * jnp.exp(-f[:, 0:1] * jnp.abs(dl_ref[...]))

    @pl.when(pl.program_id(1) == 0)
    def _():
        ssq_ref[...] = jnp.zeros_like(ssq_ref)

    ssq_ref[...] += jnp.sum(h * h, axis=0, keepdims=True)
    taps_ref[...] = h * f[:, H_EMB:H_EMB + 1]


def _filter_features(L):
    t = np.linspace(0.0, 1.0, L, dtype=np.float32).astype(np.float64)[:, None]
    bands = (H_EMB - 1) // 2
    w = (np.float32(2.0 * math.pi) * np.arange(L, dtype=np.float32) / np.float32(L)).astype(np.float64)[:, None]
    f = np.linspace(1e-4, bands - 1, bands, dtype=np.float32).astype(np.float64)[None, :]
    fw = (f.astype(np.float32) * w.astype(np.float32)).astype(np.float64)
    z = np.concatenate([t, np.cos(fw), -np.sin(fw)], axis=-1)
    feat = np.zeros((2 * L, LANES), np.float64)
    feat[:L, :H_EMB] = z
    idx = (L - np.arange(L)) % L
    feat[L:, :H_EMB] = z[idx]
    feat[:, H_EMB] = 1.0
    feat[L, H_EMB] = 0.0
    return jnp.asarray(feat, F32)


def _hyena_filters(L, w1, b1, w2, b2, w3, freq, delta, C):
    assert 2 * H_FFN == LANES
    pad = LANES - H_FFN
    w1a = jnp.pad(w1, ((0, LANES - H_EMB), (0, pad)))
    w1b = jnp.pad(w1, ((0, LANES - H_EMB), (pad, 0)))
    w2d = jnp.pad(w2, ((0, pad), (0, pad))) + jnp.pad(w2, ((pad, 0), (pad, 0)))
    w3a = jnp.pad(w3, ((0, pad), (0, 0)))
    w3b = jnp.pad(w3, ((pad, 0), (0, 0)))
    row = lambda a: jnp.tile(a.reshape(1, H_FFN), (1, 2))
    tr = min(L, 512)
    nr = L // tr
    const = lambda shape: pl.BlockSpec(shape, lambda hf, r: (0, 0))
    w3spec = pl.BlockSpec((LANES, C), lambda hf, r: (0, hf))
    return pl.pallas_call(
        _filter_kernel,
        out_shape=(jax.ShapeDtypeStruct((2 * L, C), F32), jax.ShapeDtypeStruct((2, 1, C), F32)),
        grid=(2, nr),
        in_specs=[pl.BlockSpec((tr, LANES), lambda hf, r: (hf * nr + r, 0)),
                  const((LANES, LANES)), const((LANES, LANES)), const((1, LANES)),
                  const((LANES, LANES)), const((1, LANES)), const((1, LANES)),
                  w3spec, w3spec,
                  pl.BlockSpec((1, C), lambda hf, r: (0, hf))],
        out_specs=(pl.BlockSpec((tr, C), lambda hf, r: (hf * nr + r, 0)),
                   pl.BlockSpec((None, 1, C), lambda hf, r: (hf, 0, 0))),
        compiler_params=_cparams(("parallel", "arbitrary")), name="hyena_filter",
    )(_filter_features(L), w1a, w1b, row(b1), w2d, row(b2), row(freq), w3a, w3b, delta.reshape(1, 2 * C))


def _cblock(m):
    return np.block([[m.real, -m.imag], [m.imag, m.real]])


def _dft_consts(L):
    N = 2 * L
    N2 = LANES
    N1 = N // N2
    half = N1 // 2
    n1 = np.arange(N1)
    n2 = np.arange(N2)
    F1 = np.exp(-2j * np.pi * np.outer(n1, n1) / N1)
    F2 = np.exp(-2j * np.pi * np.outer(n2, n2) / N2)
    a_data = _cblock(F1[:, :half])
    a_taps = np.concatenate([F1.real, F1.imag], axis=0)
    b_fwd = _cblock(F2)
    b_inv = _cblock(np.conj(F2))
    fin = _cblock(np.conj(F1)[:half, :]) / N
    ang = 2.0 * np.pi * np.outer(n2, n1) / N
    tw = dict(c_a=np.cos(ang)[:, :, None], s_a=np.sin(ang)[:, :, None],
              c_b=np.cos(ang.T)[:, :, None], s_b=np.sin(ang.T)[:, :, None])
    as_bf16 = lambda m: jnp.asarray(np.asarray(m, np.float32)).astype(BF16)
    return dict(N1=N1, a_data=as_bf16(a_data), a_taps=as_bf16(a_taps),
                b_fwd=as_bf16(b_fwd), b_inv=as_bf16(b_inv), fin=as_bf16(fin),
                tw={k: jnp.asarray(v, F32) for k, v in tw.items()})


def _time_slice_kernel(*refs, n_in, n_vmem, n_out, N1, compute):
    ins, vmem = refs[:n_in], refs[n_in:n_in + n_vmem]
    outs = refs[n_in + n_vmem:n_in + n_vmem + n_out]
    in_buf, out_buf, in_sem, out_sem = refs[n_in + n_vmem + n_out:]
    j = pl.program_id(0)
    n = pl.num_programs(0)
    slot = j % 2

    def in_copies(step, sl):
        return [pltpu.make_async_copy(src.at[pl.ds(0, N1), step, :], in_buf.at[sl, i], in_sem.at[sl])
                for i, src in enumerate(ins)]

    def out_copies(step, sl):
        return [pltpu.make_async_copy(out_buf.at[sl, i], dst.at[:, step, :], out_sem.at[sl])
                for i, dst in enumerate(outs)]

    @pl.when(j == 0)
    def _():
        for cp in in_copies(0, 0):
            cp.start()

    @pl.when(j + 1 < n)
    def _():
        for cp in in_copies(j + 1, 1 - slot):
            cp.start()

    for cp in in_copies(j, slot):
        cp.wait()

    @pl.when(j >= 2)
    def _():
        for cp in out_copies(j - 2, slot):
            cp.wait()

    for i, r in enumerate(compute([in_buf[slot, i] for i in range(n_in)], vmem)):
        out_buf[slot, i] = r
    for cp in out_copies(j, slot):
        cp.start()

    @pl.when(j == n - 1)
    def _():
        for cp in out_copies(j - 1, 1 - slot) + out_copies(j, slot):
            cp.wait()


def _time_slice_call(compute, hbm_inputs, vmem_inputs, vmem_specs, n_out, *, N1, C, name):
    assert LANES >= 2
    any_spec = pl.BlockSpec(memory_space=pl.ANY)
    shape = jax.ShapeDtypeStruct((N1, LANES, C), F32)
    n_in = len(hbm_inputs)
    return pl.pallas_call(
        functools.partial(_time_slice_kernel, n_in=n_in, n_vmem=len(vmem_inputs), n_out=n_out, N1=N1,
                          compute=compute),
        out_shape=(shape,) * n_out, grid=(LANES,),
        in_specs=[any_spec] * n_in + list(vmem_specs),
        out_specs=(any_spec,) * n_out,
        scratch_shapes=[pltpu.VMEM((2, n_in, N1, C), F32), pltpu.VMEM((2, n_out, N1, C), F32),
                        pltpu.SemaphoreType.DMA((2,)), pltpu.SemaphoreType.DMA((2,))],
        compiler_params=_cparams(("arbitrary",)), name=name,
    )(*hbm_inputs, *vmem_inputs)


def _pack_complex(re, im):
    hi = lax.bitcast_convert_type(re.astype(BF16).astype(F32), jnp.uint32)
    lo = lax.bitcast_convert_type(im.astype(BF16).astype(F32), jnp.uint32)
    return lax.bitcast_convert_type(hi | (lo >> 16), F32)


def _unpack_complex(p):
    w = lax.bitcast_convert_type(p, jnp.uint32)
    re = lax.bitcast_convert_type(w & jnp.uint32(0xFFFF0000), F32)
    im = lax.bitcast_convert_type(w << 16, F32)
    return jnp.concatenate([re, im], axis=0).astype(BF16)


def _fft_a_compute(xs, vmem):
    m_ref, c_ref, s_ref = vmem
    r = _dot(m_ref[...], xs[0].astype(BF16))
    n1 = r.shape[0] // 2
    re, im = r[:n1], r[n1:]
    c, s = c_ref[...], s_ref[...]
    return (_pack_complex(re * c + im * s, im * c - re * s),)


def _fft_stage_a(x3, mat, tw, *, N1, C):
    twspec = pl.BlockSpec((None, N1, 1), lambda j: (j, 0, 0))
    return _time_slice_call(_fft_a_compute, [x3], [mat, tw["c_a"], tw["s_a"]],
                            [pl.BlockSpec(mat.shape, lambda j: (0, 0)), twspec, twspec], 1,
                            N1=N1, C=C, name="hyena_fft_a")[0]


def _fft_mid_kernel(a_ref, t_ref, ssq, f_ref, i_ref, c_ref, s_ref, b_ref):
    N2 = LANES
    x = _dot(f_ref[...], _unpack_complex(a_ref[...]))
    h = _dot(f_ref[...], _unpack_complex(t_ref[...]))
    scale = lax.rsqrt(ssq[0] + ssq[1] + EPS)
    xr, xi, hr, hi = x[:N2], x[N2:], h[:N2] * scale, h[N2:] * scale
    y = jnp.concatenate([xr * hr - xi * hi, xr * hi + xi * hr], axis=0)
    r = _dot(i_ref[...], y.astype(BF16))
    re, im = r[:N2], r[N2:]
    c, s = c_ref[...], s_ref[...]
    b_ref[...] = _pack_complex(re * c - im * s, im * c + re * s)


def _fft_mid(A, Tp, ssq, consts, *, C, ct):
    N1 = consts["N1"]
    blk = pl.BlockSpec((None, LANES, ct), lambda k1, c: (k1, 0, c))
    const = pl.BlockSpec((2 * LANES, 2 * LANES), lambda k1, c: (0, 0))
    twspec = pl.BlockSpec((None, LANES, 1), lambda k1, c: (k1, 0, 0))
    return pl.pallas_call(
        _fft_mid_kernel, out_shape=jax.ShapeDtypeStruct((N1, LANES, C), F32), grid=(N1, C // ct),
        in_specs=[blk, blk, pl.BlockSpec((2, 1, ct), lambda k1, c: (0, 0, c)),
                  const, const, twspec, twspec],
        out_specs=blk,
        compiler_params=_cparams(("parallel", "parallel")), name="hyena_fft_mid",
    )(A, Tp, ssq, consts["b_fwd"], consts["b_inv"], consts["tw"]["c_b"], consts["tw"]["s_b"])


def _fft_fin_compute(xs, vmem):
    b, x0, z = xs
    m_ref, bias = vmem
    r = _dot(m_ref[...], _unpack_complex(b))
    return (x0 * (r + bias[...] * z),)


def _fft_final(Bp, x0_3d, z_3d, bias, consts, *, C):
    mat = consts["fin"]
    return _time_slice_call(_fft_fin_compute, [Bp, x0_3d, z_3d], [mat, bias],
                            [pl.BlockSpec(mat.shape, lambda j: (0, 0)), pl.BlockSpec((1, C), lambda j: (0, 0))],
                            1, N1=consts["N1"], C=C, name="hyena_fft_final")[0]


def _hyena_latent(x0, z, taps, ssq, bias, consts, *, B, L, C):
    assert B == 2
    N1 = consts["N1"]
    ct = C
    T = x0.shape[0]
    z3 = z.reshape(T // LANES, LANES, C)
    x03 = x0.reshape(T // LANES, LANES, C)
    A = _fft_stage_a(z3, consts["a_data"], consts["tw"], N1=N1, C=C)
    Tp = _fft_stage_a(taps.reshape(N1, LANES, C), consts["a_taps"], consts["tw"], N1=N1, C=C)
    Bp = _fft_mid(A, Tp, ssq, consts, C=C, ct=ct)
    y = _fft_final(Bp, x03, z3, bias.reshape(1, C).astype(F32), consts, C=C)
    return y.reshape(B * L, C)


def _hyena_ctx_kernel(z, x0, taps, ssq, bias, dh, dl, th, tl, ih, il, y, *, Lc):
    n = 2 * Lc
    x = _dot3c(dh[...], dl[...], z[...])
    h = _dot3c(th[...], tl[...], taps[...])
    scale = lax.rsqrt(ssq[0] + ssq[1] + EPS)
    xr, xi, hr, hi = x[:n], x[n:], h[:n] * scale, h[n:] * scale
    r = _dot3c(ih[...], il[...], jnp.concatenate([xr * hr - xi * hi, xr * hi + xi * hr], axis=0))
    y[...] = x0[...] * (r + bias[...] * z[...])


def _hyena_ctx(x0, z, taps, ssq, bias, *, B, L, Lc, C):
    assert B == 2 and (B * L) % (2 * Lc) == 0
    n = 2 * Lc
    idx = np.arange(n)
    F = np.exp(-2j * np.pi * np.outer(idx, idx) / n)
    d = _np_split(_cblock(F[:, :Lc]))
    t = _np_split(np.concatenate([F.real, F.imag], axis=0))
    inv = _np_split(_cblock(np.conj(F)[:Lc, :]) / n)
    ct = 256
    r0 = (B * L) // n
    both = pl.BlockSpec((n, ct), lambda c: (r0, c))
    const = lambda m: pl.BlockSpec(m.shape, lambda c: (0, 0))
    return pl.pallas_call(
        functools.partial(_hyena_ctx_kernel, Lc=Lc),
        out_shape=jax.ShapeDtypeStruct((n, C), F32),
        grid=(C // ct,),
        in_specs=[both, both,
                  pl.BlockSpec((n, ct), lambda c: (0, c)),
                  pl.BlockSpec((2, 1, ct), lambda c: (0, 0, c)),
                  pl.BlockSpec((1, ct), lambda c: (0, c)),
                  const(d[0]), const(d[1]), const(t[0]), const(t[1]), const(inv[0]), const(inv[1])],
        out_specs=pl.BlockSpec((n, ct), lambda c: (0, c)),
        compiler_params=_cparams(("parallel",)), name="hyena_ctx",
    )(z, x0, taps, ssq, bias.reshape(1, C).astype(F32), *d, *t, *inv)


def _merge_kernel(*refs, with_ctx, n_lat_tiles):
    if with_ctx:
        (ym, ya, yh, ymc, yac, yhc, gm, ga, gh, mb, wm, wa, wh, wo, x_ref, g1, o_ref) = refs
        is_ctx = pl.program_id(0) >= n_lat_tiles
        pick = lambda lat, ctx: jnp.where(is_ctx, ctx[...], lat[...])
        m, a, h = pick(ym, ymc), pick(ya, yac), pick(yh, yhc)
    else:
        (ym, ya, yh, gm, ga, gh, mb, wm, wa, wh, wo, x_ref, g1, o_ref) = refs
        m, a, h = ym[...], ya[...], yh[...]
    D = x_ref.shape[1]
    bias = mb[...]
    gate = lambda g, k: _sigmoid(g[...].astype(F32) + bias[:, k * D:(k + 1) * D])
    y = (gate(gm, 0) * _dot(m, wm[...]) + gate(ga, 1) * _dot(a, wa[...])
         + gate(gh, 2) * _dot(h.astype(BF16), wh[...]))
    o_ref[...] = x_ref[...] + g1[...] * _dot(y.astype(BF16), wo[...])


def _merge(X, P, gcol0, branches, merge_b, weights, layer, mod, gate_idx, *, n_rows, B, L, with_ctx):
    D = X.shape[1]
    ctx_rows = X.shape[0] - B * L
    tm = 2 * ROW_TILE if L % (2 * ROW_TILE) == 0 and ctx_rows % (2 * ROW_TILE) == 0 else ROW_TILE
    nlt, tps = (B * L) // tm, L // tm
    grp = functools.partial(_group_of_tile, n_lat_tiles=nlt, tiles_per_seq=tps, n_batch=B)
    lat = pl.BlockSpec((tm, D), lambda i: (jnp.minimum(i, nlt - 1), 0))
    ctx = pl.BlockSpec((tm, D), lambda i: (jnp.maximum(i - nlt, 0), 0))
    gb = gcol0 // D
    gspec = lambda k: pl.BlockSpec((tm, D), lambda i: (i, gb + k))
    wspec = pl.BlockSpec((None, D, D), lambda i: (layer, 0, 0))
    in_specs = ([lat] * 3 + ([ctx] * 3 if with_ctx else []) + [gspec(0), gspec(1), gspec(2)]
                + [pl.BlockSpec((1, 3 * D), lambda i: (0, 0))] + [wspec] * 4
                + [pl.BlockSpec((tm, D), lambda i: (i, 0)),
                   pl.BlockSpec((None, None, 1, D), lambda i: (grp(i), gate_idx, 0, 0))])
    return pl.pallas_call(
        functools.partial(_merge_kernel, with_ctx=with_ctx, n_lat_tiles=nlt),
        out_shape=jax.ShapeDtypeStruct((n_rows, D), F32),
        grid=(n_rows // tm,), in_specs=in_specs,
        out_specs=pl.BlockSpec((tm, D), lambda i: (i, 0)),
        compiler_params=_cparams(("parallel",)), name="merge",
    )(*branches, P, P, P, merge_b.reshape(1, 3 * D).astype(F32), *weights, X, mod)


EXPERT_BITS = 5
assert 1 << EXPERT_BITS == N_EXPERTS


def _slot_kernel(pstart, code_ref, o_ref):
    code = code_ref[...]
    expert = code & (N_EXPERTS - 1)
    slot = code >> EXPERT_BITS
    for e in range(N_EXPERTS):
        slot = slot + jnp.where(expert == e, pstart[e], 0)
    o_ref[...] = slot


def _slot_table(codes, pstart):
    A = codes.shape[0]
    assert A % (8 * LANES) == 0
    shape = (A // LANES, LANES)
    whole = pl.BlockSpec(shape, lambda i, ps: (0, 0))
    return pl.pallas_call(
        _slot_kernel,
        out_shape=jax.ShapeDtypeStruct(shape, jnp.int32),
        grid_spec=pltpu.PrefetchScalarGridSpec(num_scalar_prefetch=1, grid=(1,), in_specs=[whole], out_specs=whole),
        compiler_params=_cparams(("arbitrary",)), name="moe_slots",
    )(pstart, codes.reshape(shape)).reshape(A)


def _dispatch_kernel(slots, tok_ref, xs_zero, xs_out, sem, *, n_tok):
    del xs_zero
    R = tok_ref.shape[0]
    base = pl.program_id(0) * R

    def copy(pos, r):
        return pltpu.make_async_copy(tok_ref.at[pl.ds(r, 1), :], xs_out.at[pl.ds(slots[pos], 1), :], sem)

    for wait in (False, True):
        for k in range(TOP_K):
            for r in range(R):
                cp = copy(k * n_tok + base + r, r)
                cp.wait() if wait else cp.start()


def _moe_dispatch(tok, slots, P):
    N, D = tok.shape
    tm = MOE_TOKEN_TILE
    return pl.pallas_call(
        functools.partial(_dispatch_kernel, n_tok=N),
        out_shape=jax.ShapeDtypeStruct((P, D), F32),
        grid_spec=pltpu.PrefetchScalarGridSpec(
            num_scalar_prefetch=1, grid=(N // tm,),
            in_specs=[pl.BlockSpec((tm, D), lambda i, sl: (i, 0)),
                      pl.BlockSpec(memory_space=pl.ANY)],
            out_specs=pl.BlockSpec(memory_space=pl.ANY),
            scratch_shapes=[pltpu.SemaphoreType.DMA]),
        input_output_aliases={2: 0},
        compiler_params=_cparams(("arbitrary",)), name="moe_dispatch",
    )(slots, tok, jnp.zeros((P, D), F32))


def _moe_kernel(blk_e, xs_ref, w1_ref, w3_ref, w2_ref, y_ref, w1b, w3b, w2b):
    i = pl.program_id(0)

    @pl.when((i == 0) | (blk_e[i] != blk_e[jnp.maximum(i - 1, 0)]))
    def _():
        w1b[...] = w1_ref[...].astype(BF16)
        w3b[...] = w3_ref[...].astype(BF16)
        w2b[...] = w2_ref[...].astype(BF16)

    x = xs_ref[...].astype(BF16)
    a = _dot(x, w1b[...])
    g = (a * _sigmoid(a)) * _dot(x, w3b[...])
    y_ref[...] = _dot(g.astype(BF16), w2b[...])


def _moe_experts(xs, blk_e, w1, w3, w2, layer):
    P, D = xs.shape
    De = w1.shape[3]
    return pl.pallas_call(
        _moe_kernel,
        out_shape=jax.ShapeDtypeStruct((P, D), F32),
        grid_spec=pltpu.PrefetchScalarGridSpec(
            num_scalar_prefetch=1, grid=(P // MOE_BLOCK,),
            in_specs=[pl.BlockSpec((MOE_BLOCK, D), lambda i, be: (i, 0)),
                      pl.BlockSpec((None, None, D, De), lambda i, be: (layer, be[i], 0, 0)),
                      pl.BlockSpec((None, None, D, De), lambda i, be: (layer, be[i], 0, 0)),
                      pl.BlockSpec((None, None, De, D), lambda i, be: (layer, be[i], 0, 0))],
            out_specs=pl.BlockSpec((MOE_BLOCK, D), lambda i, be: (i, 0)),
            scratch_shapes=[pltpu.VMEM((D, De), BF16), pltpu.VMEM((D, De), BF16), pltpu.VMEM((De, D), BF16)]),
        compiler_params=_cparams(("arbitrary",)), name="moe_experts",
    )(blk_e, xs, w1, w3, w2)


def _combine_kernel(slots, yb_hbm, x_ref, g2, wt_ref, o_ref, buf, sems, *, n_tok):
    i = pl.program_id(0)
    n = pl.num_programs(0)
    slot = i % 2
    R = x_ref.shape[0]

    def gather(tile, slot, wait):
        dst = buf.at[slot]
        for k in range(TOP_K):
            for r in range(R):
                src = slots[k * n_tok + tile * R + r]
                cp = pltpu.make_async_copy(yb_hbm.at[pl.ds(src, 1), :], dst.at[k, pl.ds(r, 1), :], sems.at[slot])
                cp.wait() if wait else cp.start()

    @pl.when(i == 0)
    def _():
        gather(0, 0, False)

    @pl.when(i + 1 < n)
    def _():
        gather(i + 1, 1 - slot, False)

    gather(i, slot, True)
    w = wt_ref[...]
    acc = w[:, 0:1] * buf[slot, 0]
    for k in range(1, TOP_K):
        acc = acc + w[:, k:k + 1] * buf[slot, k]
    o_ref[...] = x_ref[...] + g2[...] * acc


def _moe_combine(yb, slots, wts, X, mod, gate_idx, *, n_rows, B, L):
    D = X.shape[1]
    tm = MOE_TOKEN_TILE
    nlt, tps = (B * L) // tm, L // tm
    grp = functools.partial(_group_of_tile, n_lat_tiles=nlt, tiles_per_seq=tps, n_batch=B)
    return pl.pallas_call(
        functools.partial(_combine_kernel, n_tok=n_rows),
        out_shape=jax.ShapeDtypeStruct((n_rows, D), F32),
        grid_spec=pltpu.PrefetchScalarGridSpec(
            num_scalar_prefetch=1, grid=(n_rows // tm,),
            in_specs=[pl.BlockSpec(memory_space=pl.ANY),
                      pl.BlockSpec((tm, D), lambda i, sl: (i, 0)),
                      pl.BlockSpec((None, None, 1, D), lambda i, sl: (grp(i), gate_idx, 0, 0)),
                      pl.BlockSpec((tm, LANES), lambda i, sl: (i, 0))],
            out_specs=pl.BlockSpec((tm, D), lambda i, sl: (i, 0)),
            scratch_shapes=[pltpu.VMEM((2, TOP_K, tm, D), F32), pltpu.SemaphoreType.DMA((2,))]),
        compiler_params=_cparams(("arbitrary",)), name="moe_combine",
    )(slots, yb, X, mod, wts)


def _dispatch_tables(code, counts, n_tok):
    E, K = N_EXPERTS, TOP_K
    cnt = counts[0, :E].astype(jnp.int32)
    padded = (cnt + MOE_BLOCK - 1) // MOE_BLOCK * MOE_BLOCK
    pend = jnp.cumsum(padded)
    pstart = (pend - padded).astype(jnp.int32)
    P = -(-(n_tok * K + E * (MOE_BLOCK - 1)) // MOE_BLOCK) * MOE_BLOCK
    blk_row = jnp.arange(P // MOE_BLOCK, dtype=jnp.int32) * MOE_BLOCK
    blk_e = jnp.minimum(jnp.sum(pend[None, :] <= blk_row[:, None], axis=1), E - 1).astype(jnp.int32)
    return _slot_table(code[:, :K].T.reshape(-1), pstart), P, blk_e


def kernel(x, c, ctx, c_ctx, mod_w, mod_b, norm1_g, norm2_g, w_in, merge_b, m_conv_w, m_conv_b, m_gate_b, m_norm_g, a_qnorm_g, a_knorm_g, a_lambda, a_subln_g, h_conv_w, h_conv_b, h_ffn_w1, h_ffn_b1, h_ffn_w2, h_ffn_b2, h_ffn_w3, h_freq, h_decay, h_bias, w_br_m, w_br_a, w_br_h, w_out, router_w, router_b, e_w1, e_w3, e_w2):
    B, L, D = x.shape
    Lc = ctx.shape[1]
    depth = mod_w.shape[0]
    NL, NC = B * L, B * Lc
    T = NL + NC
    assert B + 1 <= 8 and Lc == ROW_TILE and L % min(L, 1024) == 0 and L % 512 == 0
    assert NL % Lc == 0 and D % LANES == 0
    width = D
    a_dh = D // (2 * A_HEADS)
    n_gates = 4 * M_HEADS
    c_mq, c_mv, c_mo = 0, 2 * width, 3 * width
    c_aq, c_ak, c_av = 4 * width, 5 * width, 6 * width
    c_hx = 7 * width
    c_gp = 10 * width

    X = jnp.concatenate([x.reshape(NL, D), ctx.reshape(NC, D)], axis=0)
    c8 = jnp.zeros((8, D), F32).at[:B].set(c).at[B].set(c_ctx)
    cos_tab, sin_tab = _rope_tables(L, a_dh, ROW_TILE)
    fft_consts = _dft_consts(L)
    g0 = 4 * width
    w_main = jnp.concatenate([w_in[:, :, :g0], w_in[:, :, g0 + n_gates:]], axis=2).astype(BF16)
    w_gate = jnp.pad(w_in[:, :, g0:g0 + n_gates], ((0, 0), (0, 0), (0, LANES - n_gates))).astype(BF16)
    merge_w = [w.astype(BF16) for w in (w_br_m, w_br_a, w_br_h, w_out)]

    for l in range(depth):
        need_ctx = l < depth - 1
        lam_init = 0.8 - 0.6 * math.exp(-0.3 * l)
        mod = _modulation(c8, mod_w, mod_b, l)[:B + 1].reshape(B + 1, 6, 1, D)

        P, G = _inproj(X, norm1_g[l], mod, w_main, w_gate, m_gate_b[l], l, L=L, B=B)

        QK = _mlstm_qk(P, m_conv_w[l], m_conv_b[l], L=L, Lc=Lc, NL=NL, width=width)
        ym, ymc = _mlstm(QK, P, G, m_norm_g[l], B=B, L=L, Lc=Lc, width=width)

        Qh = _attn_prep(P, c_aq, a_qnorm_g[l], cos_tab, sin_tab, a_dh ** -0.5 * LOG2E, L=L, NL=NL, dh=a_dh)
        Kh = _attn_prep(P, c_ak, a_knorm_g[l], cos_tab, sin_tab, 1.0, L=L, NL=NL, dh=a_dh)
        attn = functools.partial(_attention, Qh, Kh, P, c_av, a_lambda[l], a_subln_g[l], lam_init,
                                 B=B, L=L, Lc=Lc, dh=a_dh)
        ya = attn(latent=True)

        x0, z = _hyena_pre(P, c_hx, h_conv_w[l], h_conv_b[l], L=L, Lc=Lc, NL=NL, C=width)
        filt = (h_ffn_w1[l], h_ffn_b1[l], h_ffn_w2[l], h_ffn_b2[l], h_ffn_w3[l], h_freq[l], h_decay[l])
        taps, ssq = _hyena_filters(L, *filt, width)
        yh = _hyena_latent(x0, z, taps, ssq, h_bias[l], fft_consts, B=B, L=L, C=width)

        if need_ctx:
            yac = attn(latent=False)
            taps_c, ssq_c = _hyena_filters(Lc, *filt, width)
            yhc = _hyena_ctx(x0, z, taps_c, ssq_c, h_bias[l], B=B, L=L, Lc=Lc, C=width)
            branches, n_rows = [ym, ya, yh, ymc, yac, yhc], T
        else:
            branches, n_rows = [ym, ya, yh], NL
        X = _merge(X, P, c_gp, branches, merge_b[l], merge_w, l, mod, 2,
                   n_rows=n_rows, B=B, L=L, with_ctx=need_ctx)

        tok, code, wts, counts = _norm_mod(X, norm2_g[l], mod, 3, 4, n_rows=n_rows, L=L, B=B,
                                           router=(router_w, router_b))
        slots, n_slots, blk_e = _dispatch_tables(code, counts, n_rows)
        xs = _moe_dispatch(tok, slots, n_slots)
        yb = _moe_experts(xs, blk_e, e_w1, e_w3, e_w2, l)
        X = _moe_combine(yb, slots, wts, X, mod, 5, n_rows=n_rows, B=B, L=L)
    return X[:NL].reshape(B, L, D)
```

```python
import functools
import math

import numpy as np
import jax
import jax.numpy as jnp
from jax import lax
from jax.experimental import pallas as pl
from jax.experimental.pallas import tpu as pltpu

F32 = jnp.float32
BF16 = jnp.bfloat16

GRID_W = 64
EPS = 1e-6
M_HEADS = 4
M_CHUNK = 128
A_HEADS = 8
ROPE_BASE = 10000.0
H_EMB = 33
H_FFN = 64
SHORT_CONV = 3
N_EXPERTS = 32
N_GROUPS = 4
TOP_K = 2
MOE_BLOCK = 256
MOE_TOKEN_TILE = 128

LANES = 128
BF16_SUBLANES = 16
V7X_VMEM_BYTES = 64 * 1024 * 1024
VMEM_LIMIT = V7X_VMEM_BYTES * 7 // 8

ROW_TILE = 256
LOG2E = 1.4426950408889634


def _cparams(sem):
    return pltpu.CompilerParams(dimension_semantics=sem, vmem_limit_bytes=VMEM_LIMIT)


def _split(x):
    hi = x.astype(BF16)
    lo = (x - hi.astype(F32)).astype(BF16)
    return hi, lo


def _dot(a, b):
    return jnp.dot(a, b, preferred_element_type=F32)


def _dot3(a, b):
    ah, al = _split(a)
    bh, bl = _split(b)
    return _dot(ah, bh) + _dot(al, bh) + _dot(ah, bl)


def _dot3c(ch, cl, x):
    xh, xl = _split(x)
    return _dot(ch, xh) + _dot(cl, xh) + _dot(ch, xl)


def _np_split(a):
    a = jnp.asarray(np.asarray(a, np.float32))
    hi = a.astype(BF16)
    lo = (a - hi.astype(F32)).astype(BF16)
    return hi, lo


def _sigmoid(x):
    return 1.0 / (1.0 + jnp.exp(-x))


def _log_sigmoid(x):
    return jnp.minimum(x, 0.0) - jnp.log(1.0 + jnp.exp(-jnp.abs(x)))


def _mod_kernel(c_ref, w_ref, b_ref, o_ref):
    c = c_ref[...]
    o_ref[...] = _dot3(c * _sigmoid(c), w_ref[...]) + b_ref[...]


def _modulation(c8, w, b, layer):
    depth, D, N = w.shape
    tn = 1536
    return pl.pallas_call(
        _mod_kernel,
        out_shape=jax.ShapeDtypeStruct((8, N), F32),
        grid=(N // tn,),
        in_specs=[pl.BlockSpec((8, D), lambda j: (0, 0)),
                  pl.BlockSpec((None, D, tn), lambda j: (layer, 0, j)),
                  pl.BlockSpec((None, 1, tn), lambda j: (layer, 0, j))],
        out_specs=pl.BlockSpec((8, tn), lambda j: (0, j)),
        compiler_params=_cparams(("parallel",)),
        name="modulation",
    )(c8, w, b.reshape(depth, 1, N))


def _norm_mod_router_kernel(x_ref, g_ref, sh_ref, sc_ref, rw_ref, rb_ref, o_ref, code_ref, wt_ref, cnt_ref, carry):
    E, G = N_EXPERTS, N_GROUPS
    gs = E // G

    @pl.when(pl.program_id(0) == 0)
    def _():
        carry[...] = jnp.zeros_like(carry)

    x = x_ref[...]
    y = x * lax.rsqrt(jnp.mean(x * x, axis=-1, keepdims=True) + EPS) * g_ref[...]
    h = y * (1.0 + sc_ref[...]) + sh_ref[...]
    o_ref[...] = h
    tm = h.shape[0]
    s = _sigmoid(_dot3(h, rw_ref[...]))
    lane = lax.broadcasted_iota(jnp.int32, (1, LANES), 1)
    lane_f = lane.astype(F32)
    sb = jnp.where(lane < E, s + rb_ref[...], -jnp.inf)
    far = float(LANES)
    best = jnp.full((tm, 1), -jnp.inf, F32)
    e1 = jnp.zeros((tm, 1), F32)
    e2 = jnp.zeros((tm, 1), F32)
    for g in range(G):
        mg = jnp.where((lane >= g * gs) & (lane < (g + 1) * gs), sb, -jnp.inf)
        m1 = jnp.max(mg, axis=-1, keepdims=True)
        i1 = jnp.min(jnp.where(mg == m1, lane_f, far), axis=-1, keepdims=True)
        mg2 = jnp.where(lane_f == i1, -jnp.inf, mg)
        m2 = jnp.max(mg2, axis=-1, keepdims=True)
        i2 = jnp.min(jnp.where(mg2 == m2, lane_f, far), axis=-1, keepdims=True)
        score = m1 + m2
        take = score > best
        best = jnp.where(take, score, best)
        e1 = jnp.where(take, i1, e1)
        e2 = jnp.where(take, i2, e2)
    oh1 = lane_f == e1
    oh2 = lane_f == e2
    s1 = jnp.sum(jnp.where(oh1, s, 0.0), axis=-1, keepdims=True)
    s2 = jnp.sum(jnp.where(oh2, s, 0.0), axis=-1, keepdims=True)
    den = s1 + s2
    r = lax.broadcasted_iota(jnp.int32, (tm, tm), 0)
    c = lax.broadcasted_iota(jnp.int32, (tm, tm), 1)
    lower = (c < r).astype(BF16)
    o1 = oh1.astype(F32)
    o2 = oh2.astype(F32)
    cum1 = _dot(lower, o1.astype(BF16))
    cum2 = _dot(lower, o2.astype(BF16))
    tot1 = jnp.sum(o1, axis=0, keepdims=True)
    base = carry[...]
    rank1 = jnp.sum(jnp.where(oh1, base + cum1, 0.0), axis=-1, keepdims=True)
    rank2 = jnp.sum(jnp.where(oh2, base + tot1 + cum2, 0.0), axis=-1, keepdims=True)
    total = base + tot1 + jnp.sum(o2, axis=0, keepdims=True)
    carry[...] = total
    cnt_ref[...] = total
    code1 = (rank1 * E + e1).astype(jnp.int32)
    code2 = (rank2 * E + e2).astype(jnp.int32)
    code_ref[...] = jnp.where(lane == 0, code1, jnp.where(lane == 1, code2, 0))
    wt_ref[...] = jnp.where(lane == 0, s1 / den, jnp.where(lane == 1, s2 / den, 0.0))


def _group_of_tile(i, n_lat_tiles, tiles_per_seq, n_batch):
    return jnp.where(i < n_lat_tiles, i // tiles_per_seq, n_batch)


def _norm_mod(x, g, mod, shift_idx, scale_idx, *, n_rows, L, B, router):
    D = x.shape[1]
    tm = ROW_TILE
    nlt, tps = (B * L) // tm, L // tm
    grp = functools.partial(_group_of_tile, n_lat_tiles=nlt, tiles_per_seq=tps, n_batch=B)
    in_specs = [pl.BlockSpec((tm, D), lambda i: (i, 0)),
                pl.BlockSpec((1, D), lambda i: (0, 0)),
                pl.BlockSpec((None, None, 1, D), lambda i: (grp(i), shift_idx, 0, 0)),
                pl.BlockSpec((None, None, 1, D), lambda i: (grp(i), scale_idx, 0, 0))]
    args = [x, g.reshape(1, D), mod, mod]
    router_w, router_b = router
    E = router_w.shape[1]
    assert E == N_EXPERTS
    rw = jnp.pad(router_w, ((0, 0), (0, LANES - E)))
    rb = jnp.pad(router_b.astype(F32).reshape(1, E), ((0, 0), (0, LANES - E)))
    lanes = pl.BlockSpec((tm, LANES), lambda i: (i, 0))
    return pl.pallas_call(
        _norm_mod_router_kernel,
        out_shape=(jax.ShapeDtypeStruct((n_rows, D), F32),
                   jax.ShapeDtypeStruct((n_rows, LANES), jnp.int32),
                   jax.ShapeDtypeStruct((n_rows, LANES), F32),
                   jax.ShapeDtypeStruct((1, LANES), F32)),
        grid=(n_rows // tm,),
        in_specs=in_specs + [pl.BlockSpec((D, LANES), lambda i: (0, 0)),
                             pl.BlockSpec((1, LANES), lambda i: (0, 0))],
        out_specs=(pl.BlockSpec((tm, D), lambda i: (i, 0)), lanes, lanes,
                   pl.BlockSpec((1, LANES), lambda i: (0, 0))),
        scratch_shapes=[pltpu.VMEM((1, LANES), F32)],
        compiler_params=_cparams(("arbitrary",)), name="norm_mod_router",
    )(*args, rw, rb)


def _largest_tile(n, cap, step):
    return max(t for t in range(step, cap + 1, step) if n % t == 0)


def _inproj_kernel(x_ref, g_ref, mod_ref, w_ref, wg_ref, gb_ref, o_ref, gate_ref, h_scr, *, L, n_batch):
    tm = x_ref.shape[0]

    @pl.when(pl.program_id(1) == 0)
    def _():
        x = x_ref[...]
        y = x * lax.rsqrt(jnp.mean(x * x, axis=-1, keepdims=True) + EPS) * g_ref[...]
        row = pl.program_id(0) * tm + lax.broadcasted_iota(jnp.int32, (tm, 1), 0)
        shift, scale = mod_ref[n_batch, 0], mod_ref[n_batch, 1]
        for b in range(n_batch - 1, -1, -1):
            in_b = row < (b + 1) * L
            shift = jnp.where(in_b, mod_ref[b, 0], shift)
            scale = jnp.where(in_b, mod_ref[b, 1], scale)
        h_scr[...] = (y * (1.0 + scale) + shift).astype(h_scr.dtype)
        gates = _dot(h_scr[...], wg_ref[...]) + gb_ref[...]
        lane = lax.broadcasted_iota(jnp.int32, (1, LANES), 1)
        forget = (lane // M_HEADS) % 2 == 1
        gate_ref[...] = jnp.where(forget, _log_sigmoid(gates), gates)

    o_ref[...] = _dot(h_scr[...], w_ref[...]).astype(o_ref.dtype)


def _inproj(X, g, mod, w_main, w_gate, gate_b, layer, *, L, B):
    T, D = X.shape
    N = w_main.shape[2]
    tm, tn = _largest_tile(T, 1536, ROW_TILE), _largest_tile(N, 1024, LANES)
    return pl.pallas_call(
        functools.partial(_inproj_kernel, L=L, n_batch=B),
        out_shape=(jax.ShapeDtypeStruct((T, N), BF16), jax.ShapeDtypeStruct((T, LANES), F32)),
        grid=(T // tm, N // tn),
        in_specs=[pl.BlockSpec((tm, D), lambda i, j: (i, 0)),
                  pl.BlockSpec((1, D), lambda i, j: (0, 0)),
                  pl.BlockSpec(mod.shape, lambda i, j: (0, 0, 0, 0)),
                  pl.BlockSpec((None, D, tn), lambda i, j: (layer, 0, j)),
                  pl.BlockSpec((None, D, LANES), lambda i, j: (layer, 0, 0)),
                  pl.BlockSpec((1, LANES), lambda i, j: (0, 0))],
        out_specs=(pl.BlockSpec((tm, tn), lambda i, j: (i, j)),
                   pl.BlockSpec((tm, LANES), lambda i, j: (i, 0))),
        scratch_shapes=[pltpu.VMEM((tm, D), BF16)],
        compiler_params=_cparams(("parallel", "arbitrary")), name="inproj",
    )(X, g.reshape(1, D), mod, w_main, w_gate,
      jnp.pad(gate_b.astype(F32).reshape(1, -1), ((0, 0), (0, LANES - gate_b.shape[0]))))


def _seq_edge_flags(tm, L, Lc, NL):
    r0 = pl.program_id(0) * tm
    lat = r0 < NL
    start = jnp.where(lat, r0 % L == 0, (r0 - NL) % Lc == 0)
    end = jnp.where(lat, (r0 + tm) % L == 0, (r0 + tm - NL) % Lc == 0)
    return jnp.where(start, 0.0, 1.0), jnp.where(end, 0.0, 1.0)


def _conv3(cur_ref, prev_ref, next_ref, w_ref, b_ref, keep_prev, keep_next):
    cur = cur_ref[...].astype(F32)
    tm = cur.shape[0]
    prev_row = prev_ref[BF16_SUBLANES - 1:BF16_SUBLANES, :].astype(F32) * keep_prev
    next_row = next_ref[0:1, :].astype(F32) * keep_next
    row = lax.broadcasted_iota(jnp.int32, (tm, 1), 0)
    up = jnp.where(row == 0, prev_row, pltpu.roll(cur, 1, 0))
    dn = jnp.where(row == tm - 1, next_row, pltpu.roll(cur, tm - 1, 0))
    return b_ref[...] + up * w_ref[0:1, :] + cur * w_ref[1:2, :] + dn * w_ref[2:3, :]


def _conv_specs(tm, tc, T, col_block):
    per = tm // BF16_SUBLANES
    last = T // BF16_SUBLANES - 1
    return [pl.BlockSpec((tm, tc), lambda i, j: (i, col_block + j)),
            pl.BlockSpec((BF16_SUBLANES, tc), lambda i, j: (jnp.maximum(i * per - 1, 0), col_block + j)),
            pl.BlockSpec((BF16_SUBLANES, tc), lambda i, j: (jnp.minimum((i + 1) * per, last), col_block + j))]


def _mconv_kernel(cur_ref, prev_ref, next_ref, w_ref, b_ref, scale_ref, o_ref, *, tm, L, Lc, NL):
    kp, kn = _seq_edge_flags(tm, L, Lc, NL)
    y = _conv3(cur_ref, prev_ref, next_ref, w_ref, b_ref, kp, kn)
    o_ref[...] = (y * _sigmoid(y) * scale_ref[...]).astype(o_ref.dtype)


def _mlstm_qk(P, conv_w, conv_b, *, L, Lc, NL, width):
    T = P.shape[0]
    C = 2 * width
    tm, tc = ROW_TILE, C
    scale = jnp.concatenate([jnp.ones((1, width), F32),
                             jnp.full((1, width), (width // M_HEADS) ** -0.5, F32)], axis=1)
    vec = lambda r: pl.BlockSpec((r, tc), lambda i, j: (0, j))
    return pl.pallas_call(
        functools.partial(_mconv_kernel, tm=tm, L=L, Lc=Lc, NL=NL),
        out_shape=jax.ShapeDtypeStruct((T, C), BF16),
        grid=(T // tm, C // tc),
        in_specs=_conv_specs(tm, tc, T, 0) + [vec(SHORT_CONV), vec(1), vec(1)],
        out_specs=pl.BlockSpec((tm, tc), lambda i, j: (i, j)),
        compiler_params=_cparams(("parallel", "parallel")), name="mlstm_qk_conv",
    )(P, P, P, conv_w, conv_b.reshape(1, C), scale)


def _hyena_pre_kernel(c0, p0, n0, c1, p1, n1, c2, p2, n2, w0, w1, w2, b0, b1, b2,
                      x0_ref, z_ref, *, tm, L, Lc, NL):
    kp, kn = _seq_edge_flags(tm, L, Lc, NL)
    x0 = _conv3(c0, p0, n0, w0, b0, kp, kn)
    x1 = _conv3(c1, p1, n1, w1, b1, kp, kn)
    v = _conv3(c2, p2, n2, w2, b2, kp, kn)
    x0_ref[...] = x0
    z_ref[...] = v * x1


def _hyena_pre(P, col0, conv_w, conv_b, *, L, Lc, NL, C):
    T = P.shape[0]
    tm, tc = ROW_TILE, C
    nb = C // tc
    specs = []
    for part in range(3):
        specs += _conv_specs(tm, tc, T, col0 // tc + part * nb)
    wspecs = [pl.BlockSpec((SHORT_CONV, tc), lambda i, j, p=part: (0, p * nb + j)) for part in range(3)]
    bspecs = [pl.BlockSpec((1, tc), lambda i, j, p=part: (0, p * nb + j)) for part in range(3)]
    b2d = conv_b.reshape(1, 3 * C)
    out = pl.BlockSpec((tm, tc), lambda i, j: (i, j))
    return pl.pallas_call(
        functools.partial(_hyena_pre_kernel, tm=tm, L=L, Lc=Lc, NL=NL),
        out_shape=(jax.ShapeDtypeStruct((T, C), F32), jax.ShapeDtypeStruct((T, C), F32)),
        grid=(T // tm, nb),
        in_specs=specs + wspecs + bspecs,
        out_specs=(out, out),
        compiler_params=_cparams(("parallel", "parallel")), name="hyena_pre",
    )(*([P] * 9), conv_w, conv_w, conv_w, b2d, b2d, b2d)


def _mlstm_chunk(q, k, v, li_r, lf_r, li_c, lf_c, C_scr, n_scr, m_scr, fwd):
    Q = M_CHUNK
    row = lax.broadcasted_iota(jnp.int32, (Q, Q), 0)
    col = lax.broadcasted_iota(jnp.int32, (Q, Q), 1)
    mask = (col <= row) if fwd else (col >= row)
    tri_c = mask.astype(BF16)
    tri_r = ((row <= col) if fwd else (row >= col)).astype(BF16)
    lfc_h, lfc_l = _split(jnp.broadcast_to(lf_c, (Q, Q)))
    lfr_h, lfr_l = _split(jnp.broadcast_to(lf_r, (Q, Q)))
    b_cols = _dot(tri_c, lfc_h) + _dot(tri_c, lfc_l)
    b_rows = _dot(lfr_h, tri_r) + _dot(lfr_l, tri_r)
    dm = jnp.where(mask, b_cols - b_rows + li_r, -jnp.inf)
    m_prev = m_scr[0:1, 0:1]
    b_col = b_cols[:, 0:1]
    inter = b_col + m_prev
    mt = jnp.maximum(inter, jnp.max(dm, axis=-1, keepdims=True))
    s = lax.dot_general(q, k, (((1,), (1,)), ((), ())), preferred_element_type=F32) * jnp.exp(dm - mt)
    wi = jnp.exp(inter - mt)
    num = _dot(s.astype(BF16), v) + wi * _dot(q, C_scr[...].astype(BF16))
    qn = jnp.sum(q.astype(F32) * n_scr[...], axis=-1, keepdims=True)
    den = jnp.sum(s, axis=-1, keepdims=True) + wi * qn
    h = num / jnp.maximum(jnp.abs(den), jnp.exp(-mt))
    b_tot = b_cols[Q - 1:Q, 0:1] if fwd else b_cols[0:1, 0:1]
    ws = b_tot - b_col + li_c
    m_new = jnp.maximum(b_tot + m_prev, jnp.max(ws, axis=0, keepdims=True))
    decay = jnp.exp(b_tot + m_prev - m_new)
    kw = k.astype(F32) * jnp.exp(ws - m_new)
    C_scr[...] = decay * C_scr[...] + lax.dot_general(
        kw.astype(BF16), v, (((0,), (0,)), ((), ())), preferred_element_type=F32)
    n_scr[...] = decay * n_scr[...] + jnp.sum(kw, axis=0, keepdims=True)
    m_scr[...] = jnp.broadcast_to(m_new, m_scr.shape)
    return h


def _mlstm_kernel(*refs, fwd, SEG, Lc, H, dh):
    if fwd:
        (gr_ref, gc_ref, grc_ref, gcc_ref, q_ref, k_ref, v_ref, qc_ref, kc_ref, vc_ref,
         h_ref, hc_ref, *scr) = refs
    else:
        (gr_ref, gc_ref, grc_ref, gcc_ref, q_ref, k_ref, v_ref, qc_ref, kc_ref, vc_ref,
         o_ref, oc_ref, hf_ref, hfc_ref, ng_ref, y_ref, yc_ref, *scr) = refs
    C_scr, n_scr, m_scr = scr[0:H], scr[H:2 * H], scr[2 * H:3 * H]
    gi = 0 if fwd else 2

    def run(nchunks, qr, kr, vr, grr, gcr, emit):
        def body(j, carry):
            c = j if fwd else nchunks - 1 - j
            rows = pl.ds(pl.multiple_of(c * M_CHUNK, M_CHUNK), M_CHUNK)
            for hh in range(H):
                cols = slice(hh * dh, (hh + 1) * dh)
                gcs = gcr[hh, rows, :]
                h = _mlstm_chunk(qr[rows, cols], kr[rows, cols], vr[rows, cols],
                                 grr[hh, gi, pl.ds(c, 1), :], grr[hh, gi + 1, pl.ds(c, 1), :],
                                 gcs[:, gi:gi + 1], gcs[:, gi + 1:gi + 2],
                                 C_scr[hh], n_scr[hh], m_scr[hh], fwd)
                emit(rows, cols, h)
            return carry

        lax.fori_loop(0, nchunks, body, 0, unroll=2)

    def emitter(dst, hf=None, o=None):
        def emit(rows, cols, h):
            if not fwd:
                h = h + hf[rows, cols]
                y = h * lax.rsqrt(jnp.mean(h * h, axis=-1, keepdims=True) + EPS) * ng_ref[:, cols]
                h = (y * _sigmoid(o[rows, cols].astype(F32))).astype(dst.dtype)
            dst[rows, cols] = h
        return emit

    @pl.when(pl.program_id(1) == 0)
    def _():
        for r in scr:
            r[...] = jnp.zeros_like(r)
        run(Lc // M_CHUNK, qc_ref, kc_ref, vc_ref, grc_ref, gcc_ref,
            emitter(hc_ref) if fwd else emitter(yc_ref, hfc_ref, oc_ref))

    run(SEG // M_CHUNK, q_ref, k_ref, v_ref, gr_ref, gc_ref,
        emitter(h_ref) if fwd else emitter(y_ref, hf_ref, o_ref))


def _mlstm(QK, P, G, norm_g, *, B, L, Lc, width):
    H = M_HEADS
    dh = width // H
    NL = B * L
    SEG = min(L, 1024)
    S = L // SEG
    g = G[:, :4 * H]

    def gate_views(rows, n):
        a = rows.reshape(B, n, 4, H)
        return (a.transpose(0, 3, 2, 1).reshape(B, H, 4, n // M_CHUNK, M_CHUNK),
                a.transpose(0, 3, 1, 2))

    gr, gc = gate_views(g[:NL], L)
    grc, gcc = gate_views(g[NL:], Lc)
    scratch = ([pltpu.VMEM((dh, dh), F32)] * H + [pltpu.VMEM((1, dh), F32)] * H
               + [pltpu.VMEM((8, LANES), F32)] * H)

    def call(fwd, extra_in, extra_specs, out_dtype):
        seg_of = (lambda s: s) if fwd else (lambda s: S - 1 - s)
        lat = lambda cb: pl.BlockSpec((SEG, width), lambda b, s: (b * S + seg_of(s), cb))
        ctx = lambda cb: pl.BlockSpec((Lc, width), lambda b, s: (NL // Lc + b, cb))
        gate_specs = [
            pl.BlockSpec((None, H, 4, SEG // M_CHUNK, M_CHUNK), lambda b, s: (b, 0, 0, seg_of(s), 0)),
            pl.BlockSpec((None, H, SEG, 4), lambda b, s: (b, 0, seg_of(s), 0)),
            pl.BlockSpec((None, H, 4, Lc // M_CHUNK, M_CHUNK), lambda b, s: (b, 0, 0, 0, 0)),
            pl.BlockSpec((None, H, Lc, 4), lambda b, s: (b, 0, 0, 0))]
        lat_out = pl.BlockSpec((SEG, width), lambda b, s: (b * S + seg_of(s), 0))
        ctx_out = pl.BlockSpec((Lc, width), lambda b, s: (b, 0))
        return pl.pallas_call(
            functools.partial(_mlstm_kernel, fwd=fwd, SEG=SEG, Lc=Lc, H=H, dh=dh),
            out_shape=(jax.ShapeDtypeStruct((NL, width), out_dtype),
                       jax.ShapeDtypeStruct((B * Lc, width), out_dtype)),
            grid=(B, S),
            in_specs=(gate_specs + [lat(0), lat(1), lat(2), ctx(0), ctx(1), ctx(2)]
                      + extra_specs(lat, ctx, lat_out, ctx_out)),
            out_specs=(lat_out, ctx_out), scratch_shapes=scratch,
            compiler_params=_cparams(("parallel", "arbitrary")),
            name="mlstm_fwd" if fwd else "mlstm_bwd",
        )(gr, gc, grc, gcc, QK, QK, P, QK, QK, P, *extra_in)

    hf, hfc = call(True, [], lambda *_: [], F32)
    return call(False, [P, P, hf, hfc, jnp.tile(norm_g.astype(F32), H).reshape(1, width)],
                lambda lat, ctx, lat_out, ctx_out: [lat(3), ctx(3), lat_out, ctx_out,
                                                    pl.BlockSpec((1, width), lambda b, s: (0, 0))], BF16)


def _attn_prep_kernel(x_ref, g_ref, cos_ref, sin_ref, o_ref, *, scale, dh):
    n_blk = x_ref.shape[1] // LANES
    r = lax.broadcasted_iota(jnp.int32, (LANES, LANES), 0)
    c = lax.broadcasted_iota(jnp.int32, (LANES, LANES), 1)
    group = (r // dh == c // dh).astype(BF16)
    lane = lax.broadcasted_iota(jnp.int32, (1, LANES), 1)
    quarter = dh // 4
    first = (lane % (2 * quarter)) < quarter
    cos = cos_ref[...]
    sin = sin_ref[...]
    for hb in range(n_blk):
        cols = slice(hb * LANES, (hb + 1) * LANES)
        x = x_ref[:, cols].astype(F32)
        hi, lo = _split(x * x)
        ms = (_dot(hi, group) + _dot(lo, group)) * (1.0 / dh)
        y = x * lax.rsqrt(ms + EPS) * g_ref[:, cols]
        rot = jnp.where(first, -pltpu.roll(y, LANES - quarter, 1), pltpu.roll(y, quarter, 1))
        o_ref[:, cols] = ((y * cos + rot * sin) * scale).astype(o_ref.dtype)


def _attn_prep(P, col0, gain, cos_tab, sin_tab, scale, *, L, NL, dh):
    T = P.shape[0]
    W = A_HEADS * 2 * dh
    tm = ROW_TILE
    nlt, tps = NL // tm, L // tm
    tab = pl.BlockSpec((tm, LANES), lambda i: (jnp.where(i < nlt, i % tps, tps), 0))
    g = jnp.tile(gain.astype(F32), W // dh).reshape(1, W)
    return pl.pallas_call(
        functools.partial(_attn_prep_kernel, scale=scale, dh=dh),
        out_shape=jax.ShapeDtypeStruct((T, W), BF16),
        grid=(T // tm,),
        in_specs=[pl.BlockSpec((tm, W), lambda i: (i, col0 // W)),
                  pl.BlockSpec((1, W), lambda i: (0, 0)), tab, tab],
        out_specs=pl.BlockSpec((tm, W), lambda i: (i, 0)),
        compiler_params=_cparams(("parallel",)), name="attn_prep",
    )(P, g, cos_tab, sin_tab)


def _attn_kernel(*refs, n_lat, tk, L, Lc, dh, lam_init):
    if n_lat:
        lam_ref, q_ref, kl_ref, vl_ref, kc_ref, vc_ref, sg_ref, o_ref, vext, acc = refs
    else:
        lam_ref, q_ref, kc_ref, vc_ref, sg_ref, o_ref, vext, acc = refs
    dv = 2 * dh
    ctx0 = n_lat * tk

    @pl.when(pl.program_id(2) == 0)
    def _():
        if n_lat:
            vext[0:L, 0:dv] = vl_ref[...]
        vext[ctx0:ctx0 + Lc, 0:dv] = vc_ref[...]
        vext[:, dv:2 * dv] = jnp.ones((vext.shape[0], dv), BF16)

    q = q_ref[...]
    lane = lax.broadcasted_iota(jnp.int32, (1, dv), 1)
    qs = (jnp.where(lane < dh, q, jnp.zeros_like(q)), jnp.where(lane >= dh, q, jnp.zeros_like(q)))
    acc[...] = jnp.zeros_like(acc)
    tq = q.shape[0]

    def update(comp, kblk, vblk, m_old):
        s = lax.dot_general(qs[comp], kblk, (((1,), (1,)), ((), ())), preferred_element_type=F32)
        m_new = jnp.maximum(m_old, jnp.max(s, axis=-1, keepdims=True))
        p = jnp.exp2(s - m_new).astype(BF16)
        acc[comp] = jnp.exp2(m_old - m_new) * acc[comp] + _dot(p, vblk)
        return m_new

    m = (jnp.full((tq, 1), -jnp.inf, F32),) * 2
    for c in range(n_lat):
        kblk, vblk = kl_ref[c * tk:(c + 1) * tk, :], vext[c * tk:(c + 1) * tk, :]
        m = update(0, kblk, vblk, m[0]), update(1, kblk, vblk, m[1])
    kblk, vblk = kc_ref[...], vext[ctx0:ctx0 + Lc, :]
    update(0, kblk, vblk, m[0])
    update(1, kblk, vblk, m[1])

    lp = lam_ref[...]
    lam = (jnp.exp(jnp.sum(lp[0:1] * lp[1:2], axis=-1, keepdims=True))
           - jnp.exp(jnp.sum(lp[2:3] * lp[3:4], axis=-1, keepdims=True)) + lam_init)
    a0, a1 = acc[0], acc[1]
    o = a0[:, 0:dv] / a0[:, dv:dv + 1] - lam * (a1[:, 0:dv] / a1[:, dv:dv + 1])
    y = o * lax.rsqrt(jnp.mean(o * o, axis=-1, keepdims=True) + EPS) * sg_ref[...] * (1.0 - lam_init)
    o_ref[...] = y.astype(o_ref.dtype)


def _attention(Qh, Kh, P, vcol0, lam_p, sub_g, lam_init, *, B, L, Lc, dh, latent):
    NL = B * L
    dv = 2 * dh
    H = A_HEADS
    vb = vcol0 // dv
    if latent:
        tq, tk = 512, min(L, 1024)
        n_lat, nq, rows_out = L // tk, L // tq, NL
        q_spec = pl.BlockSpec((tq, dv), lambda b, h, i: (b * nq + i, h))
        lat_specs = [pl.BlockSpec((L, dv), lambda b, h, i: (b, h)),
                     pl.BlockSpec((L, dv), lambda b, h, i: (b, vb + h))]
        lat_args = [Kh, P]
        o_spec = pl.BlockSpec((tq, dv), lambda b, h, i: (b * nq + i, h))
        nkeys = L + Lc
    else:
        tq, tk = Lc, 512
        n_lat, nq, rows_out = 0, 1, B * Lc
        q_spec = pl.BlockSpec((Lc, dv), lambda b, h, i: (NL // Lc + b, h))
        lat_specs, lat_args = [], []
        o_spec = pl.BlockSpec((Lc, dv), lambda b, h, i: (b, h))
        nkeys = Lc
    ctx_specs = [pl.BlockSpec((Lc, dv), lambda b, h, i: (NL // Lc + b, h)),
                 pl.BlockSpec((Lc, dv), lambda b, h, i: (NL // Lc + b, vb + h))]
    return pl.pallas_call(
        functools.partial(_attn_kernel, n_lat=n_lat, tk=tk, L=L, Lc=Lc, dh=dh, lam_init=lam_init),
        out_shape=jax.ShapeDtypeStruct((rows_out, H * dv), BF16),
        grid=(B, H, nq),
        in_specs=[pl.BlockSpec((4, dh), lambda b, h, i: (0, 0)), q_spec] + lat_specs + ctx_specs
                 + [pl.BlockSpec((1, dv), lambda b, h, i: (0, 0))],
        out_specs=o_spec,
        scratch_shapes=[pltpu.VMEM((nkeys, 2 * dv), BF16), pltpu.VMEM((2, tq, 2 * dv), F32)],
        compiler_params=_cparams(("parallel", "parallel", "arbitrary")),
        name="diff_attn_latent" if latent else "diff_attn_ctx",
    )(lam_p.astype(F32), Qh, *lat_args, Kh, P, sub_g.reshape(1, dv).astype(F32))


def _rope_tables(L, dh, tm):
    rows = L // GRID_W
    row = np.repeat(np.arange(rows), GRID_W).astype(np.float64)
    col = np.tile(np.arange(GRID_W), rows).astype(np.float64)
    nf = dh // 4
    inv = (np.float32(ROPE_BASE) ** (-np.arange(nf, dtype=np.float32) / nf)).astype(np.float64)
    ang = np.concatenate([row[:, None] * inv] * 2 + [col[:, None] * inv] * 2, axis=-1)
    ang = np.tile(ang.astype(np.float32).astype(np.float64), (1, LANES // dh))
    cos = np.concatenate([np.cos(ang), np.ones((tm, LANES))], axis=0)
    sin = np.concatenate([np.sin(ang), np.zeros((tm, LANES))], axis=0)
    return jnp.asarray(cos, F32), jnp.asarray(sin, F32)


def _filter_kernel(f_ref, w1a_ref, w1b_ref, b1_ref, w2_ref, b2_ref, fr_ref, w3a_ref, w3b_ref, dl_ref,
                   taps_ref, ssq_ref):
    f = f_ref[...]
    half = f.shape[0] // 2
    freq = fr_ref[...]
    h = jnp.sin(freq * (_dot3(f[:half], w1a_ref[...]) + _dot3(f[half:], w1b_ref[...]) + b1_ref[...]))
    h = jnp.sin(freq * (_dot3(h, w2_ref[...]) + b2_ref[...]))
    h = jnp.concatenate([_dot3(h, w3a_ref[...]), _dot3(h, w3b_ref[...])], axis=0)
    h = h * jnp.exp(-f[:, 0:1] * jnp.abs(dl_ref[...]))

    @pl.when(pl.program_id(1) == 0)
    def _():
        ssq_ref[...] = jnp.zeros_like(ssq_ref)

    ssq_ref[...] += jnp.sum(h * h, axis=0, keepdims=True)
    taps_ref[...] = h * f[:, H_EMB:H_EMB + 1]


def _filter_features(L):
    t = np.linspace(0.0, 1.0, L, dtype=np.float32).astype(np.float64)[:, None]
    bands = (H_EMB - 1) // 2
    w = (np.float32(2.0 * math.pi) * np.arange(L, dtype=np.float32) / np.float32(L)).astype(np.float64)[:, None]
    f = np.linspace(1e-4, bands - 1, bands, dtype=np.float32).astype(np.float64)[None, :]
    fw = (f.astype(np.float32) * w.astype(np.float32)).astype(np.float64)
    z = np.concatenate([t, np.cos(fw), -np.sin(fw)], axis=-1)
    feat = np.zeros((2 * L, LANES), np.float64)
    feat[:L, :H_EMB] = z
    idx = (L - np.arange(L)) % L
    feat[L:, :H_EMB] = z[idx]
    feat[:, H_EMB] = 1.0
    feat[L, H_EMB] = 0.0
    return jnp.asarray(feat, F32)


def _hyena_filters(L, w1, b1, w2, b2, w3, freq, delta, C):
    assert 2 * H_FFN == LANES
    pad = LANES - H_FFN
    w1a = jnp.pad(w1, ((0, LANES - H_EMB), (0, pad)))
    w1b = jnp.pad(w1, ((0, LANES - H_EMB), (pad, 0)))
    w2d = jnp.pad(w2, ((0, pad), (0, pad))) + jnp.pad(w2, ((pad, 0), (pad, 0)))
    w3a = jnp.pad(w3, ((0, pad), (0, 0)))
    w3b = jnp.pad(w3, ((pad, 0), (0, 0)))
    row = lambda a: jnp.tile(a.reshape(1, H_FFN), (1, 2))
    tr = min(L, 512)
    nr = L // tr
    const = lambda shape: pl.BlockSpec(shape, lambda hf, r: (0, 0))
    w3spec = pl.BlockSpec((LANES, C), lambda hf, r: (0, hf))
    return pl.pallas_call(
        _filter_kernel,
        out_shape=(jax.ShapeDtypeStruct((2 * L, C), F32), jax.ShapeDtypeStruct((2, 1, C), F32)),
        grid=(2, nr),
        in_specs=[pl.BlockSpec((tr, LANES), lambda hf, r: (hf * nr + r, 0)),
                  const((LANES, LANES)), const((LANES, LANES)), const((1, LANES)),
                  const((LANES, LANES)), const((1, LANES)), const((1, LANES)),
                  w3spec, w3spec,
                  pl.BlockSpec((1, C), lambda hf, r: (0, hf))],
        out_specs=(pl.BlockSpec((tr, C), lambda hf, r: (hf * nr + r, 0)),
                   pl.BlockSpec((None, 1, C), lambda hf, r: (hf, 0, 0))),
        compiler_params=_cparams(("parallel", "arbitrary")), name="hyena_filter",
    )(_filter_features(L), w1a, w1b, row(b1), w2d, row(b2), row(freq), w3a, w3b, delta.reshape(1, 2 * C))


def _cblock(m):
    return np.block([[m.real, -m.imag], [m.imag, m.real]])


def _dft_consts(L):
    N = 2 * L
    N2 = LANES
    N1 = N // N2
    half = N1 // 2
    n1 = np.arange(N1)
    n2 = np.arange(N2)
    F1 = np.exp(-2j * np.pi * np.outer(n1, n1) / N1)
    F2 = np.exp(-2j * np.pi * np.outer(n2, n2) / N2)
    a_data = _cblock(F1[:, :half])
    a_taps = np.concatenate([F1.real, F1.imag], axis=0)
    b_fwd = _cblock(F2)
    b_inv = _cblock(np.conj(F2))
    fin = _cblock(np.conj(F1)[:half, :]) / N
    ang = 2.0 * np.pi * np.outer(n2, n1) / N
    tw = dict(c_a=np.cos(ang)[:, :, None], s_a=np.sin(ang)[:, :, None],
              c_b=np.cos(ang.T)[:, :, None], s_b=np.sin(ang.T)[:, :, None])
    as_bf16 = lambda m: jnp.asarray(np.asarray(m, np.float32)).astype(BF16)
    return dict(N1=N1, a_data=as_bf16(a_data), a_taps=as_bf16(a_taps),
                b_fwd=as_bf16(b_fwd), b_inv=as_bf16(b_inv), fin=as_bf16(fin),
                tw={k: jnp.asarray(v, F32) for k, v in tw.items()})


def _time_slice_kernel(*refs, n_in, n_vmem, n_out, N1, compute):
    ins, vmem = refs[:n_in], refs[n_in:n_in + n_vmem]
    outs = refs[n_in + n_vmem:n_in + n_vmem + n_out]
    in_buf, out_buf, in_sem, out_sem = refs[n_in + n_vmem + n_out:]
    j = pl.program_id(0)
    n = pl.num_programs(0)
    slot = j % 2

    def in_copies(step, sl):
        return [pltpu.make_async_copy(src.at[pl.ds(0, N1), step, :], in_buf.at[sl, i], in_sem.at[sl])
                for i, src in enumerate(ins)]

    def out_copies(step, sl):
        return [pltpu.make_async_copy(out_buf.at[sl, i], dst.at[:, step, :], out_sem.at[sl])
                for i, dst in enumerate(outs)]

    @pl.when(j == 0)
    def _():
        for cp in in_copies(0, 0):
            cp.start()

    @pl.when(j + 1 < n)
    def _():
        for cp in in_copies(j + 1, 1 - slot):
            cp.start()

    for cp in in_copies(j, slot):
        cp.wait()

    @pl.when(j >= 2)
    def _():
        for cp in out_copies(j - 2, slot):
            cp.wait()

    for i, r in enumerate(compute([in_buf[slot, i] for i in range(n_in)], vmem)):
        out_buf[slot, i] = r
    for cp in out_copies(j, slot):
        cp.start()

    @pl.when(j == n - 1)
    def _():
        for cp in out_copies(j - 1, 1 - slot) + out_copies(j, slot):
            cp.wait()


def _time_slice_call(compute, hbm_inputs, vmem_inputs, vmem_specs, n_out, *, N1, C, name):
    assert LANES >= 2
    any_spec = pl.BlockSpec(memory_space=pl.ANY)
    shape = jax.ShapeDtypeStruct((N1, LANES, C), F32)
    n_in = len(hbm_inputs)
    return pl.pallas_call(
        functools.partial(_time_slice_kernel, n_in=n_in, n_vmem=len(vmem_inputs), n_out=n_out, N1=N1,
                          compute=compute),
        out_shape=(shape,) * n_out, grid=(LANES,),
        in_specs=[any_spec] * n_in + list(vmem_specs),
        out_specs=(any_spec,) * n_out,
        scratch_shapes=[pltpu.VMEM((2, n_in, N1, C), F32), pltpu.VMEM((2, n_out, N1, C), F32),
                        pltpu.SemaphoreType.DMA((2,)), pltpu.SemaphoreType.DMA((2,))],
        compiler_params=_cparams(("arbitrary",)), name=name,
    )(*hbm_inputs, *vmem_inputs)


def _pack_complex(re, im):
    hi = lax.bitcast_convert_type(re.astype(BF16).astype(F32), jnp.uint32)
    lo = lax.bitcast_convert_type(im.astype(BF16).astype(F32), jnp.uint32)
    return lax.bitcast_convert_type(hi | (lo >> 16), F32)


def _unpack_complex(p):
    w = lax.bitcast_convert_type(p, jnp.uint32)
    re = lax.bitcast_convert_type(w & jnp.uint32(0xFFFF0000), F32)
    im = lax.bitcast_convert_type(w << 16, F32)
    return jnp.concatenate([re, im], axis=0).astype(BF16)


def _fft_a_compute(xs, vmem):
    m_ref, c_ref, s_ref = vmem
    r = _dot(m_ref[...], xs[0].astype(BF16))
    n1 = r.shape[0] // 2
    re, im = r[:n1], r[n1:]
    c, s = c_ref[...], s_ref[...]
    return (_pack_complex(re * c + im * s, im * c - re * s),)


def _fft_stage_a(x3, mat, tw, *, N1, C):
    twspec = pl.BlockSpec((None, N1, 1), lambda j: (j, 0, 0))
    return _time_slice_call(_fft_a_compute, [x3], [mat, tw["c_a"], tw["s_a"]],
                            [pl.BlockSpec(mat.shape, lambda j: (0, 0)), twspec, twspec], 1,
                            N1=N1, C=C, name="hyena_fft_a")[0]


def _fft_mid_kernel(a_ref, t_ref, ssq, f_ref, i_ref, c_ref, s_ref, b_ref):
    N2 = LANES
    x = _dot(f_ref[...], _unpack_complex(a_ref[...]))
    h = _dot(f_ref[...], _unpack_complex(t_ref[...]))
    scale = lax.rsqrt(ssq[0] + ssq[1] + EPS)
    xr, xi, hr, hi = x[:N2], x[N2:], h[:N2] * scale, h[N2:] * scale
    y = jnp.concatenate([xr * hr - xi * hi, xr * hi + xi * hr], axis=0)
    r = _dot(i_ref[...], y.astype(BF16))
    re, im = r[:N2], r[N2:]
    c, s = c_ref[...], s_ref[...]
    b_ref[...] = _pack_complex(re * c - im * s, im * c + re * s)


def _fft_mid(A, Tp, ssq, consts, *, C, ct):
    N1 = consts["N1"]
    blk = pl.BlockSpec((None, LANES, ct), lambda k1, c: (k1, 0, c))
    const = pl.BlockSpec((2 * LANES, 2 * LANES), lambda k1, c: (0, 0))
    twspec = pl.BlockSpec((None, LANES, 1), lambda k1, c: (k1, 0, 0))
    return pl.pallas_call(
        _fft_mid_kernel, out_shape=jax.ShapeDtypeStruct((N1, LANES, C), F32), grid=(N1, C // ct),
        in_specs=[blk, blk, pl.BlockSpec((2, 1, ct), lambda k1, c: (0, 0, c)),
                  const, const, twspec, twspec],
        out_specs=blk,
        compiler_params=_cparams(("parallel", "parallel")), name="hyena_fft_mid",
    )(A, Tp, ssq, consts["b_fwd"], consts["b_inv"], consts["tw"]["c_b"], consts["tw"]["s_b"])


def _fft_fin_compute(xs, vmem):
    b, x0, z = xs
    m_ref, bias = vmem
    r = _dot(m_ref[...], _unpack_complex(b))
    return (x0 * (r + bias[...] * z),)


def _fft_final(Bp, x0_3d, z_3d, bias, consts, *, C):
    mat = consts["fin"]
    return _time_slice_call(_fft_fin_compute, [Bp, x0_3d, z_3d], [mat, bias],
                            [pl.BlockSpec(mat.shape, lambda j: (0, 0)), pl.BlockSpec((1, C), lambda j: (0, 0))],
                            1, N1=consts["N1"], C=C, name="hyena_fft_final")[0]


def _hyena_latent(x0, z, taps, ssq, bias, consts, *, B, L, C):
    assert B == 2
    N1 = consts["N1"]
    ct = C
    T = x0.shape[0]
    z3 = z.reshape(T // LANES, LANES, C)
    x03 = x0.reshape(T // LANES, LANES, C)
    A = _fft_stage_a(z3, consts["a_data"], consts["tw"], N1=N1, C=C)
    Tp = _fft_stage_a(taps.reshape(N1, LANES, C), consts["a_taps"], consts["tw"], N1=N1, C=C)
    Bp = _fft_mid(A, Tp, ssq, consts, C=C, ct=ct)
    y = _fft_final(Bp, x03, z3, bias.reshape(1, C).astype(F32), consts, C=C)
    return y.reshape(B * L, C)


def _hyena_ctx_kernel(z, x0, taps, ssq, bias, dh, dl, th, tl, ih, il, y, *, Lc):
    n = 2 * Lc
    x = _dot3c(dh[...], dl[...], z[...])
    h = _dot3c(th[...], tl[...], taps[...])
    scale = lax.rsqrt(ssq[0] + ssq[1] + EPS)
    xr, xi, hr, hi = x[:n], x[n:], h[:n] * scale, h[n:] * scale
    r = _dot3c(ih[...], il[...], jnp.concatenate([xr * hr - xi * hi, xr * hi + xi * hr], axis=0))
    y[...] = x0[...] * (r + bias[...] * z[...])


def _hyena_ctx(x0, z, taps, ssq, bias, *, B, L, Lc, C):
    assert B == 2 and (B * L) % (2 * Lc) == 0
    n = 2 * Lc
    idx = np.arange(n)
    F = np.exp(-2j * np.pi * np.outer(idx, idx) / n)
    d = _np_split(_cblock(F[:, :Lc]))
    t = _np_split(np.concatenate([F.real, F.imag], axis=0))
    inv = _np_split(_cblock(np.conj(F)[:Lc, :]) / n)
    ct = 256
    r0 = (B * L) // n
    both = pl.BlockSpec((n, ct), lambda c: (r0, c))
    const = lambda m: pl.BlockSpec(m.shape, lambda c: (0, 0))
    return pl.pallas_call(
        functools.partial(_hyena_ctx_kernel, Lc=Lc),
        out_shape=jax.ShapeDtypeStruct((n, C), F32),
        grid=(C // ct,),
        in_specs=[both, both,
                  pl.BlockSpec((n, ct), lambda c: (0, c)),
                  pl.BlockSpec((2, 1, ct), lambda c: (0, 0, c)),
                  pl.BlockSpec((1, ct), lambda c: (0, c)),
                  const(d[0]), const(d[1]), const(t[0]), const(t[1]), const(inv[0]), const(inv[1])],
        out_specs=pl.BlockSpec((n, ct), lambda c: (0, c)),
        compiler_params=_cparams(("parallel",)), name="hyena_ctx",
    )(z, x0, taps, ssq, bias.reshape(1, C).astype(F32), *d, *t, *inv)


def _merge_kernel(*refs, with_ctx, n_lat_tiles):
    if with_ctx:
        (ym, ya, yh, ymc, yac, yhc, gm, ga, gh, mb, wm, wa, wh, wo, x_ref, g1, o_ref) = refs
        is_ctx = pl.program_id(0) >= n_lat_tiles
        pick = lambda lat, ctx: jnp.where(is_ctx, ctx[...], lat[...])
        m, a, h = pick(ym, ymc), pick(ya, yac), pick(yh, yhc)
    else:
        (ym, ya, yh, gm, ga, gh, mb, wm, wa, wh, wo, x_ref, g1, o_ref) = refs
        m, a, h = ym[...], ya[...], yh[...]
    D = x_ref.shape[1]
    bias = mb[...]
    gate = lambda g, k: _sigmoid(g[...].astype(F32) + bias[:, k * D:(k + 1) * D])
    y = (gate(gm, 0) * _dot(m, wm[...]) + gate(ga, 1) * _dot(a, wa[...])
         + gate(gh, 2) * _dot(h.astype(BF16), wh[...]))
    o_ref[...] = x_ref[...] + g1[...] * _dot(y.astype(BF16), wo[...])


def _merge(X, P, gcol0, branches, merge_b, weights, layer, mod, gate_idx, *, n_rows, B, L, with_ctx):
    D = X.shape[1]
    ctx_rows = X.shape[0] - B * L
    tm = 2 * ROW_TILE if L % (2 * ROW_TILE) == 0 and ctx_rows % (2 * ROW_TILE) == 0 else ROW_TILE
    nlt, tps = (B * L) // tm, L // tm
    grp = functools.partial(_group_of_tile, n_lat_tiles=nlt, tiles_per_seq=tps, n_batch=B)
    lat = pl.BlockSpec((tm, D), lambda i: (jnp.minimum(i, nlt - 1), 0))
    ctx = pl.BlockSpec((tm, D), lambda i: (jnp.maximum(i - nlt, 0), 0))
    gb = gcol0 // D
    gspec = lambda k: pl.BlockSpec((tm, D), lambda i: (i, gb + k))
    wspec = pl.BlockSpec((None, D, D), lambda i: (layer, 0, 0))
    in_specs = ([lat] * 3 + ([ctx] * 3 if with_ctx else []) + [gspec(0), gspec(1), gspec(2)]
                + [pl.BlockSpec((1, 3 * D), lambda i: (0, 0))] + [wspec] * 4
                + [pl.BlockSpec((tm, D), lambda i: (i, 0)),
                   pl.BlockSpec((None, None, 1, D), lambda i: (grp(i), gate_idx, 0, 0))])
    return pl.pallas_call(
        functools.partial(_merge_kernel, with_ctx=with_ctx, n_lat_tiles=nlt),
        out_shape=jax.ShapeDtypeStruct((n_rows, D), F32),
        grid=(n_rows // tm,), in_specs=in_specs,
        out_specs=pl.BlockSpec((tm, D), lambda i: (i, 0)),
        compiler_params=_cparams(("parallel",)), name="merge",
    )(*branches, P, P, P, merge_b.reshape(1, 3 * D).astype(F32), *weights, X, mod)


EXPERT_BITS = 5
assert 1 << EXPERT_BITS == N_EXPERTS


def _slot_kernel(pstart, code_ref, o_ref):
    code = code_ref[...]
    expert = code & (N_EXPERTS - 1)
    slot = code >> EXPERT_BITS
    for e in range(N_EXPERTS):
        slot = slot + jnp.where(expert == e, pstart[e], 0)
    o_ref[...] = slot


def _slot_table(codes, pstart):
    A = codes.shape[0]
    assert A % (8 * LANES) == 0
    shape = (A // LANES, LANES)
    whole = pl.BlockSpec(shape, lambda i, ps: (0, 0))
    return pl.pallas_call(
        _slot_kernel,
        out_shape=jax.ShapeDtypeStruct(shape, jnp.int32),
        grid_spec=pltpu.PrefetchScalarGridSpec(num_scalar_prefetch=1, grid=(1,), in_specs=[whole], out_specs=whole),
        compiler_params=_cparams(("arbitrary",)), name="moe_slots",
    )(pstart, codes.reshape(shape)).reshape(A)


def _dispatch_kernel(slots, tok_ref, xs_zero, xs_out, sem, *, n_tok):
    del xs_zero
    R = tok_ref.shape[0]
    base = pl.program_id(0) * R

    def copy(pos, r):
        return pltpu.make_async_copy(tok_ref.at[pl.ds(r, 1), :], xs_out.at[pl.ds(slots[pos], 1), :], sem)

    for wait in (False, True):
        for k in range(TOP_K):
            for r in range(R):
                cp = copy(k * n_tok + base + r, r)
                cp.wait() if wait else cp.start()


def _moe_dispatch(tok, slots, P):
    N, D = tok.shape
    tm = MOE_TOKEN_TILE
    return pl.pallas_call(
        functools.partial(_dispatch_kernel, n_tok=N),
        out_shape=jax.ShapeDtypeStruct((P, D), F32),
        grid_spec=pltpu.PrefetchScalarGridSpec(
            num_scalar_prefetch=1, grid=(N // tm,),
            in_specs=[pl.BlockSpec((tm, D), lambda i, sl: (i, 0)),
                      pl.BlockSpec(memory_space=pl.ANY)],
            out_specs=pl.BlockSpec(memory_space=pl.ANY),
            scratch_shapes=[pltpu.SemaphoreType.DMA]),
        input_output_aliases={2: 0},
        compiler_params=_cparams(("arbitrary",)), name="moe_dispatch",
    )(slots, tok, jnp.zeros((P, D), F32))


def _moe_kernel(blk_e, xs_ref, w1_ref, w3_ref, w2_ref, y_ref, w1b, w3b, w2b):
    i = pl.program_id(0)

    @pl.when((i == 0) | (blk_e[i] != blk_e[jnp.maximum(i - 1, 0)]))
    def _():
        w1b[...] = w1_ref[...].astype(BF16)
        w3b[...] = w3_ref[...].astype(BF16)
        w2b[...] = w2_ref[...].astype(BF16)

    x = xs_ref[...].astype(BF16)
    a = _dot(x, w1b[...])
    g = (a * _sigmoid(a)) * _dot(x, w3b[...])
    y_ref[...] = _dot(g.astype(BF16), w2b[...])


def _moe_experts(xs, blk_e, w1, w3, w2, layer):
    P, D = xs.shape
    De = w1.shape[3]
    return pl.pallas_call(
        _moe_kernel,
        out_shape=jax.ShapeDtypeStruct((P, D), F32),
        grid_spec=pltpu.PrefetchScalarGridSpec(
            num_scalar_prefetch=1, grid=(P // MOE_BLOCK,),
            in_specs=[pl.BlockSpec((MOE_BLOCK, D), lambda i, be: (i, 0)),
                      pl.BlockSpec((None, None, D, De), lambda i, be: (layer, be[i], 0, 0)),
                      pl.BlockSpec((None, None, D, De), lambda i, be: (layer, be[i], 0, 0)),
                      pl.BlockSpec((None, None, De, D), lambda i, be: (layer, be[i], 0, 0))],
            out_specs=pl.BlockSpec((MOE_BLOCK, D), lambda i, be: (i, 0)),
            scratch_shapes=[pltpu.VMEM((D, De), BF16), pltpu.VMEM((D, De), BF16), pltpu.VMEM((De, D), BF16)]),
        compiler_params=_cparams(("arbitrary",)), name="moe_experts",
    )(blk_e, xs, w1, w3, w2)


def _combine_kernel(slots, yb_hbm, x_ref, g2, wt_ref, o_ref, buf, sems, *, n_tok):
    i = pl.program_id(0)
    n = pl.num_programs(0)
    slot = i % 2
    R = x_ref.shape[0]

    def gather(tile, slot, wait):
        dst = buf.at[slot]
        for k in range(TOP_K):
            for r in range(R):
                src = slots[k * n_tok + tile * R + r]
                cp = pltpu.make_async_copy(yb_hbm.at[pl.ds(src, 1), :], dst.at[k, pl.ds(r, 1), :], sems.at[slot])
                cp.wait() if wait else cp.start()

    @pl.when(i == 0)
    def _():
        gather(0, 0, False)

    @pl.when(i + 1 < n)
    def _():
        gather(i + 1, 1 - slot, False)

    gather(i, slot, True)
    w = wt_ref[...]
    acc = w[:, 0:1] * buf[slot, 0]
    for k in range(1, TOP_K):
        acc = acc + w[:, k:k + 1] * buf[slot, k]
    o_ref[...] = x_ref[...] + g2[...] * acc


def _moe_combine(yb, slots, wts, X, mod, gate_idx, *, n_rows, B, L):
    D = X.shape[1]
    tm = MOE_TOKEN_TILE
    nlt, tps = (B * L) // tm, L // tm
    grp = functools.partial(_group_of_tile, n_lat_tiles=nlt, tiles_per_seq=tps, n_batch=B)
    return pl.pallas_call(
        functools.partial(_combine_kernel, n_tok=n_rows),
        out_shape=jax.ShapeDtypeStruct((n_rows, D), F32),
        grid_spec=pltpu.PrefetchScalarGridSpec(
            num_scalar_prefetch=1, grid=(n_rows // tm,),
            in_specs=[pl.BlockSpec(memory_space=pl.ANY),
                      pl.BlockSpec((tm, D), lambda i, sl: (i, 0)),
                      pl.BlockSpec((None, None, 1, D), lambda i, sl: (grp(i), gate_idx, 0, 0)),
                      pl.BlockSpec((tm, LANES), lambda i, sl: (i, 0))],
            out_specs=pl.BlockSpec((tm, D), lambda i, sl: (i, 0)),
            scratch_shapes=[pltpu.VMEM((2, TOP_K, tm, D), F32), pltpu.SemaphoreType.DMA((2,))]),
        compiler_params=_cparams(("arbitrary",)), name="moe_combine",
    )(slots, yb, X, mod, wts)


def _dispatch_tables(code, counts, n_tok):
    E, K = N_EXPERTS, TOP_K
    cnt = counts[0, :E].astype(jnp.int32)
    padded = (cnt + MOE_BLOCK - 1) // MOE_BLOCK * MOE_BLOCK
    pend = jnp.cumsum(padded)
    pstart = (pend - padded).astype(jnp.int32)
    P = -(-(n_tok * K + E * (MOE_BLOCK - 1)) // MOE_BLOCK) * MOE_BLOCK
    blk_row = jnp.arange(P // MOE_BLOCK, dtype=jnp.int32) * MOE_BLOCK
    blk_e = jnp.minimum(jnp.sum(pend[None, :] <= blk_row[:, None], axis=1), E - 1).astype(jnp.int32)
    return _slot_table(code[:, :K].T.reshape(-1), pstart), P, blk_e


def kernel(x, c, ctx, c_ctx, mod_w, mod_b, norm1_g, norm2_g, w_in, merge_b, m_conv_w, m_conv_b, m_gate_b, m_norm_g, a_qnorm_g, a_knorm_g, a_lambda, a_subln_g, h_conv_w, h_conv_b, h_ffn_w1, h_ffn_b1, h_ffn_w2, h_ffn_b2, h_ffn_w3, h_freq, h_decay, h_bias, w_br_m, w_br_a, w_br_h, w_out, router_w, router_b, e_w1, e_w3, e_w2):
    B, L, D = x.shape
    Lc = ctx.shape[1]
    depth = mod_w.shape[0]
    NL, NC = B * L, B * Lc
    T = NL + NC
    assert B + 1 <= 8 and Lc == ROW_TILE and L % min(L, 1024) == 0 and L % 512 == 0
    assert NL % Lc == 0 and D % LANES == 0
    width = D
    a_dh = D // (2 * A_HEADS)
    n_gates = 4 * M_HEADS
    c_mq, c_mv, c_mo = 0, 2 * width, 3 * width
    c_aq, c_ak, c_av = 4 * width, 5 * width, 6 * width
    c_hx = 7 * width
    c_gp = 10 * width

    X = jnp.concatenate([x.reshape(NL, D), ctx.reshape(NC, D)], axis=0)
    c8 = jnp.zeros((8, D), F32).at[:B].set(c).at[B].set(c_ctx)
    cos_tab, sin_tab = _rope_tables(L, a_dh, ROW_TILE)
    fft_consts = _dft_consts(L)
    g0 = 4 * width
    w_main = jnp.concatenate([w_in[:, :, :g0], w_in[:, :, g0 + n_gates:]], axis=2).astype(BF16)
    w_gate = jnp.pad(w_in[:, :, g0:g0 + n_gates], ((0, 0), (0, 0), (0, LANES - n_gates))).astype(BF16)
    merge_w = [w.astype(BF16) for w in (w_br_m, w_br_a, w_br_h, w_out)]

    for l in range(depth):
        need_ctx = l < depth - 1
        lam_init = 0.8 - 0.6 * math.exp(-0.3 * l)
        mod = _modulation(c8, mod_w, mod_b, l)[:B + 1].reshape(B + 1, 6, 1, D)

        P, G = _inproj(X, norm1_g[l], mod, w_main, w_gate, m_gate_b[l], l, L=L, B=B)

        QK = _mlstm_qk(P, m_conv_w[l], m_conv_b[l], L=L, Lc=Lc, NL=NL, width=width)
        ym, ymc = _mlstm(QK, P, G, m_norm_g[l], B=B, L=L, Lc=Lc, width=width)

        Qh = _attn_prep(P, c_aq, a_qnorm_g[l], cos_tab, sin_tab, a_dh ** -0.5 * LOG2E, L=L, NL=NL, dh=a_dh)
        Kh = _attn_prep(P, c_ak, a_knorm_g[l], cos_tab, sin_tab, 1.0, L=L, NL=NL, dh=a_dh)
        attn = functools.partial(_attention, Qh, Kh, P, c_av, a_lambda[l], a_subln_g[l], lam_init,
                                 B=B, L=L, Lc=Lc, dh=a_dh)
        ya = attn(latent=True)

        x0, z = _hyena_pre(P, c_hx, h_conv_w[l], h_conv_b[l], L=L, Lc=Lc, NL=NL, C=width)
        filt = (h_ffn_w1[l], h_ffn_b1[l], h_ffn_w2[l], h_ffn_b2[l], h_ffn_w3[l], h_freq[l], h_decay[l])
        taps, ssq = _hyena_filters(L, *filt, width)
        yh = _hyena_latent(x0, z, taps, ssq, h_bias[l], fft_consts, B=B, L=L, C=width)

        if need_ctx:
            yac = attn(latent=False)
            taps_c, ssq_c = _hyena_filters(Lc, *filt, width)
            yhc = _hyena_ctx(x0, z, taps_c, ssq_c, h_bias[l], B=B, L=L, Lc=Lc, C=width)
            branches, n_rows = [ym, ya, yh, ymc, yac, yhc], T
        else:
            branches, n_rows = [ym, ya, yh], NL
        X = _merge(X, P, c_gp, branches, merge_b[l], merge_w, l, mod, 2,
                   n_rows=n_rows, B=B, L=L, with_ctx=need_ctx)

        tok, code, wts, counts = _norm_mod(X, norm2_g[l], mod, 3, 4, n_rows=n_rows, L=L, B=B,
                                           router=(router_w, router_b))
        slots, n_slots, blk_e = _dispatch_tables(code, counts, n_rows)
        xs = _moe_dispatch(tok, slots, n_slots)
        yb = _moe_experts(xs, blk_e, e_w1, e_w3, e_w2, l)
        X = _moe_combine(yb, slots, wts, X, mod, 5, n_rows=n_rows, B=B, L=L)
    return X[:NL].reshape(B, L, D)
```

```python
import functools
import math

import numpy as np
import jax
import jax.numpy as jnp
from jax import lax
from jax.experimental import pallas as pl
from jax.experimental.pallas import tpu as pltpu

F32 = jnp.float32
BF16 = jnp.bfloat16

GRID_W = 64
EPS = 1e-6
M_HEADS = 4
M_CHUNK = 128
A_HEADS = 8
ROPE_BASE = 10000.0
H_EMB = 33
H_FFN = 64
SHORT_CONV = 3
N_EXPERTS = 32
N_GROUPS = 4
TOP_K = 2
MOE_BLOCK = 256
MOE_TOKEN_TILE = 256

LANES = 128
BF16_SUBLANES = 16
V7X_VMEM_BYTES = 64 * 1024 * 1024
VMEM_LIMIT = V7X_VMEM_BYTES * 7 // 8

ROW_TILE = 256
LOG2E = 1.4426950408889634


def _cparams(sem):
    return pltpu.CompilerParams(dimension_semantics=sem, vmem_limit_bytes=VMEM_LIMIT)


def _split(x):
    hi = x.astype(BF16)
    lo = (x - hi.astype(F32)).astype(BF16)
    return hi, lo


def _dot(a, b):
    return jnp.dot(a, b, preferred_element_type=F32)


def _dot3(a, b):
    ah, al = _split(a)
    bh, bl = _split(b)
    return _dot(ah, bh) + _dot(al, bh) + _dot(ah, bl)


def _dot3c(ch, cl, x):
    xh, xl = _split(x)
    return _dot(ch, xh) + _dot(cl, xh) + _dot(ch, xl)


def _np_split(a):
    a = jnp.asarray(np.asarray(a, np.float32))
    hi = a.astype(BF16)
    lo = (a - hi.astype(F32)).astype(BF16)
    return hi, lo


def _sigmoid(x):
    return 1.0 / (1.0 + jnp.exp(-x))


def _log_sigmoid(x):
    return jnp.minimum(x, 0.0) - jnp.log(1.0 + jnp.exp(-jnp.abs(x)))


def _mod_kernel(c_ref, w_ref, b_ref, o_ref):
    c = c_ref[...]
    o_ref[...] = _dot3(c * _sigmoid(c), w_ref[...]) + b_ref[...]


def _modulation(c8, w, b, layer):
    depth, D, N = w.shape
    tn = 1536
    return pl.pallas_call(
        _mod_kernel,
        out_shape=jax.ShapeDtypeStruct((8, N), F32),
        grid=(N // tn,),
        in_specs=[pl.BlockSpec((8, D), lambda j: (0, 0)),
                  pl.BlockSpec((None, D, tn), lambda j: (layer, 0, j)),
                  pl.BlockSpec((None, 1, tn), lambda j: (layer, 0, j))],
        out_specs=pl.BlockSpec((8, tn), lambda j: (0, j)),
        compiler_params=_cparams(("parallel",)),
        name="modulation",
    )(c8, w, b.reshape(depth, 1, N))


def _norm_mod_router_kernel(x_ref, g_ref, sh_ref, sc_ref, rw_ref, rb_ref, o_ref, code_ref, wt_ref, cnt_ref, carry):
    E, G = N_EXPERTS, N_GROUPS
    gs = E // G

    @pl.when(pl.program_id(0) == 0)
    def _():
        carry[...] = jnp.zeros_like(carry)

    x = x_ref[...]
    y = x * lax.rsqrt(jnp.mean(x * x, axis=-1, keepdims=True) + EPS) * g_ref[...]
    h = y * (1.0 + sc_ref[...]) + sh_ref[...]
    o_ref[...] = h
    tm = h.shape[0]
    s = _sigmoid(_dot3(h, rw_ref[...]))
    lane = lax.broadcasted_iota(jnp.int32, (1, LANES), 1)
    lane_f = lane.astype(F32)
    sb = jnp.where(lane < E, s + rb_ref[...], -jnp.inf)
    far = float(LANES)
    best = jnp.full((tm, 1), -jnp.inf, F32)
    e1 = jnp.zeros((tm, 1), F32)
    e2 = jnp.zeros((tm, 1), F32)
    for g in range(G):
        mg = jnp.where((lane >= g * gs) & (lane < (g + 1) * gs), sb, -jnp.inf)
        m1 = jnp.max(mg, axis=-1, keepdims=True)
        i1 = jnp.min(jnp.where(mg == m1, lane_f, far), axis=-1, keepdims=True)
        mg2 = jnp.where(lane_f == i1, -jnp.inf, mg)
        m2 = jnp.max(mg2, axis=-1, keepdims=True)
        i2 = jnp.min(jnp.where(mg2 == m2, lane_f, far), axis=-1, keepdims=True)
        score = m1 + m2
        take = score > best
        best = jnp.where(take, score, best)
        e1 = jnp.where(take, i1, e1)
        e2 = jnp.where(take, i2, e2)
    oh1 = lane_f == e1
    oh2 = lane_f == e2
    s1 = jnp.sum(jnp.where(oh1, s, 0.0), axis=-1, keepdims=True)
    s2 = jnp.sum(jnp.where(oh2, s, 0.0), axis=-1, keepdims=True)
    den = s1 + s2
    r = lax.broadcasted_iota(jnp.int32, (tm, tm), 0)
    c = lax.broadcasted_iota(jnp.int32, (tm, tm), 1)
    lower = (c < r).astype(BF16)
    o1 = oh1.astype(F32)
    o2 = oh2.astype(F32)
    cum1 = _dot(lower, o1.astype(BF16))
    cum2 = _dot(lower, o2.astype(BF16))
    tot1 = jnp.sum(o1, axis=0, keepdims=True)
    base = carry[...]
    rank1 = jnp.sum(jnp.where(oh1, base + cum1, 0.0), axis=-1, keepdims=True)
    rank2 = jnp.sum(jnp.where(oh2, base + tot1 + cum2, 0.0), axis=-1, keepdims=True)
    total = base + tot1 + jnp.sum(o2, axis=0, keepdims=True)
    carry[...] = total
    cnt_ref[...] = total
    code1 = (rank1 * E + e1).astype(jnp.int32)
    code2 = (rank2 * E + e2).astype(jnp.int32)
    code_ref[...] = jnp.where(lane == 0, code1, jnp.where(lane == 1, code2, 0))
    wt_ref[...] = jnp.where(lane == 0, s1 / den, jnp.where(lane == 1, s2 / den, 0.0))


def _group_of_tile(i, n_lat_tiles, tiles_per_seq, n_batch):
    return jnp.where(i < n_lat_tiles, i // tiles_per_seq, n_batch)


def _norm_mod(x, g, mod, shift_idx, scale_idx, *, n_rows, L, B, router):
    D = x.shape[1]
    tm = 2 * ROW_TILE if L % (2 * ROW_TILE) == 0 and (n_rows - B * L) % (2 * ROW_TILE) == 0 else ROW_TILE
    nlt, tps = (B * L) // tm, L // tm
    grp = functools.partial(_group_of_tile, n_lat_tiles=nlt, tiles_per_seq=tps, n_batch=B)
    in_specs = [pl.BlockSpec((tm, D), lambda i: (i, 0)),
                pl.BlockSpec((1, D), lambda i: (0, 0)),
                pl.BlockSpec((None, None, 1, D), lambda i: (grp(i), shift_idx, 0, 0)),
                pl.BlockSpec((None, None, 1, D), lambda i: (grp(i), scale_idx, 0, 0))]
    args = [x, g.reshape(1, D), mod, mod]
    router_w, router_b = router
    E = router_w.shape[1]
    assert E == N_EXPERTS
    rw = jnp.pad(router_w, ((0, 0), (0, LANES - E)))
    rb = jnp.pad(router_b.astype(F32).reshape(1, E), ((0, 0), (0, LANES - E)))
    lanes = pl.BlockSpec((tm, LANES), lambda i: (i, 0))
    return pl.pallas_call(
        _norm_mod_router_kernel,
        out_shape=(jax.ShapeDtypeStruct((n_rows, D), F32),
                   jax.ShapeDtypeStruct((n_rows, LANES), jnp.int32),
                   jax.ShapeDtypeStruct((n_rows, LANES), F32),
                   jax.ShapeDtypeStruct((1, LANES), F32)),
        grid=(n_rows // tm,),
        in_specs=in_specs + [pl.BlockSpec((D, LANES), lambda i: (0, 0)),
                             pl.BlockSpec((1, LANES), lambda i: (0, 0))],
        out_specs=(pl.BlockSpec((tm, D), lambda i: (i, 0)), lanes, lanes,
                   pl.BlockSpec((1, LANES), lambda i: (0, 0))),
        scratch_shapes=[pltpu.VMEM((1, LANES), F32)],
        compiler_params=_cparams(("arbitrary",)), name="norm_mod_router",
    )(*args, rw, rb)


def _largest_tile(n, cap, step):
    return max(t for t in range(step, cap + 1, step) if n % t == 0)


def _inproj_kernel(x_ref, g_ref, mod_ref, w_ref, wg_ref, gb_ref, o_ref, gate_ref, h_scr, *, L, n_batch):
    tm = x_ref.shape[0]

    @pl.when(pl.program_id(1) == 0)
    def _():
        x = x_ref[...]
        y = x * lax.rsqrt(jnp.mean(x * x, axis=-1, keepdims=True) + EPS) * g_ref[...]
        row = pl.program_id(0) * tm + lax.broadcasted_iota(jnp.int32, (tm, 1), 0)
        shift, scale = mod_ref[n_batch, 0], mod_ref[n_batch, 1]
        for b in range(n_batch - 1, -1, -1):
            in_b = row < (b + 1) * L
            shift = jnp.where(in_b, mod_ref[b, 0], shift)
            scale = jnp.where(in_b, mod_ref[b, 1], scale)
        h_scr[...] = (y * (1.0 + scale) + shift).astype(h_scr.dtype)
        gates = _dot(h_scr[...], wg_ref[...]) + gb_ref[...]
        lane = lax.broadcasted_iota(jnp.int32, (1, LANES), 1)
        forget = (lane // M_HEADS) % 2 == 1
        gate_ref[...] = jnp.where(forget, _log_sigmoid(gates), gates)

    o_ref[...] = _dot(h_scr[...], w_ref[...]).astype(o_ref.dtype)


def _inproj(X, g, mod, w_main, w_gate, gate_b, layer, *, L, B):
    T, D = X.shape
    N = w_main.shape[2]
    tm, tn = _largest_tile(T, 1536, ROW_TILE), _largest_tile(N, 1024, LANES)
    return pl.pallas_call(
        functools.partial(_inproj_kernel, L=L, n_batch=B),
        out_shape=(jax.ShapeDtypeStruct((T, N), BF16), jax.ShapeDtypeStruct((T, LANES), F32)),
        grid=(T // tm, N // tn),
        in_specs=[pl.BlockSpec((tm, D), lambda i, j: (i, 0)),
                  pl.BlockSpec((1, D), lambda i, j: (0, 0)),
                  pl.BlockSpec(mod.shape, lambda i, j: (0, 0, 0, 0)),
                  pl.BlockSpec((None, D, tn), lambda i, j: (layer, 0, j)),
                  pl.BlockSpec((None, D, LANES), lambda i, j: (layer, 0, 0)),
                  pl.BlockSpec((1, LANES), lambda i, j: (0, 0))],
        out_specs=(pl.BlockSpec((tm, tn), lambda i, j: (i, j)),
                   pl.BlockSpec((tm, LANES), lambda i, j: (i, 0))),
        scratch_shapes=[pltpu.VMEM((tm, D), BF16)],
        compiler_params=_cparams(("parallel", "arbitrary")), name="inproj",
    )(X, g.reshape(1, D), mod, w_main, w_gate,
      jnp.pad(gate_b.astype(F32).reshape(1, -1), ((0, 0), (0, LANES - gate_b.shape[0]))))


def _seq_edge_flags(tm, L, Lc, NL):
    r0 = pl.program_id(0) * tm
    lat = r0 < NL
    start = jnp.where(lat, r0 % L == 0, (r0 - NL) % Lc == 0)
    end = jnp.where(lat, (r0 + tm) % L == 0, (r0 + tm - NL) % Lc == 0)
    return jnp.where(start, 0.0, 1.0), jnp.where(end, 0.0, 1.0)


def _conv3(cur_ref, prev_ref, next_ref, w_ref, b_ref, keep_prev, keep_next):
    cur = cur_ref[...].astype(F32)
    tm = cur.shape[0]
    prev_row = prev_ref[BF16_SUBLANES - 1:BF16_SUBLANES, :].astype(F32) * keep_prev
    next_row = next_ref[0:1, :].astype(F32) * keep_next
    row = lax.broadcasted_iota(jnp.int32, (tm, 1), 0)
    up = jnp.where(row == 0, prev_row, pltpu.roll(cur, 1, 0))
    dn = jnp.where(row == tm - 1, next_row, pltpu.roll(cur, tm - 1, 0))
    return b_ref[...] + up * w_ref[0:1, :] + cur * w_ref[1:2, :] + dn * w_ref[2:3, :]


def _conv_specs(tm, tc, T, col_block):
    per = tm // BF16_SUBLANES
    last = T // BF16_SUBLANES - 1
    return [pl.BlockSpec((tm, tc), lambda i, j: (i, col_block + j)),
            pl.BlockSpec((BF16_SUBLANES, tc), lambda i, j: (jnp.maximum(i * per - 1, 0), col_block + j)),
            pl.BlockSpec((BF16_SUBLANES, tc), lambda i, j: (jnp.minimum((i + 1) * per, last), col_block + j))]


def _mconv_kernel(cur_ref, prev_ref, next_ref, w_ref, b_ref, scale_ref, o_ref, *, tm, L, Lc, NL):
    kp, kn = _seq_edge_flags(tm, L, Lc, NL)
    y = _conv3(cur_ref, prev_ref, next_ref, w_ref, b_ref, kp, kn)
    o_ref[...] = (y * _sigmoid(y) * scale_ref[...]).astype(o_ref.dtype)


def _mlstm_qk(P, conv_w, conv_b, *, L, Lc, NL, width):
    T = P.shape[0]
    C = 2 * width
    tm, tc = ROW_TILE, C
    scale = jnp.concatenate([jnp.ones((1, width), F32),
                             jnp.full((1, width), (width // M_HEADS) ** -0.5, F32)], axis=1)
    vec = lambda r: pl.BlockSpec((r, tc), lambda i, j: (0, j))
    return pl.pallas_call(
        functools.partial(_mconv_kernel, tm=tm, L=L, Lc=Lc, NL=NL),
        out_shape=jax.ShapeDtypeStruct((T, C), BF16),
        grid=(T // tm, C // tc),
        in_specs=_conv_specs(tm, tc, T, 0) + [vec(SHORT_CONV), vec(1), vec(1)],
        out_specs=pl.BlockSpec((tm, tc), lambda i, j: (i, j)),
        compiler_params=_cparams(("parallel", "parallel")), name="mlstm_qk_conv",
    )(P, P, P, conv_w, conv_b.reshape(1, C), scale)


def _hyena_pre_kernel(c0, p0, n0, c1, p1, n1, c2, p2, n2, w0, w1, w2, b0, b1, b2,
                      x0_ref, z_ref, *, tm, L, Lc, NL):
    kp, kn = _seq_edge_flags(tm, L, Lc, NL)
    x0 = _conv3(c0, p0, n0, w0, b0, kp, kn)
    x1 = _conv3(c1, p1, n1, w1, b1, kp, kn)
    v = _conv3(c2, p2, n2, w2, b2, kp, kn)
    x0_ref[...] = x0
    z_ref[...] = v * x1


def _hyena_pre(P, col0, conv_w, conv_b, *, L, Lc, NL, C):
    T = P.shape[0]
    tm, tc = ROW_TILE, C
    nb = C // tc
    specs = []
    for part in range(3):
        specs += _conv_specs(tm, tc, T, col0 // tc + part * nb)
    wspecs = [pl.BlockSpec((SHORT_CONV, tc), lambda i, j, p=part: (0, p * nb + j)) for part in range(3)]
    bspecs = [pl.BlockSpec((1, tc), lambda i, j, p=part: (0, p * nb + j)) for part in range(3)]
    b2d = conv_b.reshape(1, 3 * C)
    out = pl.BlockSpec((tm, tc), lambda i, j: (i, j))
    return pl.pallas_call(
        functools.partial(_hyena_pre_kernel, tm=tm, L=L, Lc=Lc, NL=NL),
        out_shape=(jax.ShapeDtypeStruct((T, C), F32), jax.ShapeDtypeStruct((T, C), F32)),
        grid=(T // tm, nb),
        in_specs=specs + wspecs + bspecs,
        out_specs=(out, out),
        compiler_params=_cparams(("parallel", "parallel")), name="hyena_pre",
    )(*([P] * 9), conv_w, conv_w, conv_w, b2d, b2d, b2d)


def _mlstm_chunk(q, k, v, li_r, lf_r, li_c, lf_c, C_scr, n_scr, m_scr, fwd):
    Q = M_CHUNK
    row = lax.broadcasted_iota(jnp.int32, (Q, Q), 0)
    col = lax.broadcasted_iota(jnp.int32, (Q, Q), 1)
    mask = (col <= row) if fwd else (col >= row)
    tri_c = mask.astype(BF16)
    tri_r = ((row <= col) if fwd else (row >= col)).astype(BF16)
    lfc_h, lfc_l = _split(jnp.broadcast_to(lf_c, (Q, Q)))
    lfr_h, lfr_l = _split(jnp.broadcast_to(lf_r, (Q, Q)))
    b_cols = _dot(tri_c, lfc_h) + _dot(tri_c, lfc_l)
    b_rows = _dot(lfr_h, tri_r) + _dot(lfr_l, tri_r)
    dm = jnp.where(mask, b_cols - b_rows + li_r, -jnp.inf)
    m_prev = m_scr[0:1, 0:1]
    b_col = b_cols[:, 0:1]
    inter = b_col + m_prev
    mt = jnp.maximum(inter, jnp.max(dm, axis=-1, keepdims=True))
    s = lax.dot_general(q, k, (((1,), (1,)), ((), ())), preferred_element_type=F32) * jnp.exp(dm - mt)
    wi = jnp.exp(inter - mt)
    num = _dot(s.astype(BF16), v) + wi * _dot(q, C_scr[...].astype(BF16))
    qn = jnp.sum(q.astype(F32) * n_scr[...], axis=-1, keepdims=True)
    den = jnp.sum(s, axis=-1, keepdims=True) + wi * qn
    h = num / jnp.maximum(jnp.abs(den), jnp.exp(-mt))
    b_tot = b_cols[Q - 1:Q, 0:1] if fwd else b_cols[0:1, 0:1]
    ws = b_tot - b_col + li_c
    m_new = jnp.maximum(b_tot + m_prev, jnp.max(ws, axis=0, keepdims=True))
    decay = jnp.exp(b_tot + m_prev - m_new)
    kw = k.astype(F32) * jnp.exp(ws - m_new)
    C_scr[...] = decay * C_scr[...] + lax.dot_general(
        kw.astype(BF16), v, (((0,), (0,)), ((), ())), preferred_element_type=F32)
    n_scr[...] = decay * n_scr[...] + jnp.sum(kw, axis=0, keepdims=True)
    m_scr[...] = jnp.broadcast_to(m_new, m_scr.shape)
    return h


def _mlstm_kernel(*refs, fwd, SEG, Lc, H, dh):
    if fwd:
        (gr_ref, gc_ref, grc_ref, gcc_ref, q_ref, k_ref, v_ref, qc_ref, kc_ref, vc_ref,
         h_ref, hc_ref, *scr) = refs
    else:
        (gr_ref, gc_ref, grc_ref, gcc_ref, q_ref, k_ref, v_ref, qc_ref, kc_ref, vc_ref,
         o_ref, oc_ref, hf_ref, hfc_ref, ng_ref, y_ref, yc_ref, *scr) = refs
    C_scr, n_scr, m_scr = scr[0:H], scr[H:2 * H], scr[2 * H:3 * H]
    gi = 0 if fwd else 2

    def run(nchunks, qr, kr, vr, grr, gcr, emit):
        def body(j, carry):
            c = j if fwd else nchunks - 1 - j
            rows = pl.ds(pl.multiple_of(c * M_CHUNK, M_CHUNK), M_CHUNK)
            for hh in range(H):
                cols = slice(hh * dh, (hh + 1) * dh)
                gcs = gcr[hh, rows, :]
                h = _mlstm_chunk(qr[rows, cols], kr[rows, cols], vr[rows, cols],
                                 grr[hh, gi, pl.ds(c, 1), :], grr[hh, gi + 1, pl.ds(c, 1), :],
                                 gcs[:, gi:gi + 1], gcs[:, gi + 1:gi + 2],
                                 C_scr[hh], n_scr[hh], m_scr[hh], fwd)
                emit(rows, cols, h)
            return carry

        lax.fori_loop(0, nchunks, body, 0, unroll=2)

    def emitter(dst, hf=None, o=None):
        def emit(rows, cols, h):
            if not fwd:
                h = h + hf[rows, cols]
                y = h * lax.rsqrt(jnp.mean(h * h, axis=-1, keepdims=True) + EPS) * ng_ref[:, cols]
                h = (y * _sigmoid(o[rows, cols].astype(F32))).astype(dst.dtype)
            dst[rows, cols] = h
        return emit

    @pl.when(pl.program_id(1) == 0)
    def _():
        for r in scr:
            r[...] = jnp.zeros_like(r)
        run(Lc // M_CHUNK, qc_ref, kc_ref, vc_ref, grc_ref, gcc_ref,
            emitter(hc_ref) if fwd else emitter(yc_ref, hfc_ref, oc_ref))

    run(SEG // M_CHUNK, q_ref, k_ref, v_ref, gr_ref, gc_ref,
        emitter(h_ref) if fwd else emitter(y_ref, hf_ref, o_ref))


def _mlstm(QK, P, G, norm_g, *, B, L, Lc, width):
    H = M_HEADS
    dh = width // H
    NL = B * L
    SEG = min(L, 1024)
    S = L // SEG
    g = G[:, :4 * H]

    def gate_views(rows, n):
        a = rows.reshape(B, n, 4, H)
        return (a.transpose(0, 3, 2, 1).reshape(B, H, 4, n // M_CHUNK, M_CHUNK),
                a.transpose(0, 3, 1, 2))

    gr, gc = gate_views(g[:NL], L)
    grc, gcc = gate_views(g[NL:], Lc)
    scratch = ([pltpu.VMEM((dh, dh), F32)] * H + [pltpu.VMEM((1, dh), F32)] * H
               + [pltpu.VMEM((8, LANES), F32)] * H)

    def call(fwd, extra_in, extra_specs, out_dtype):
        seg_of = (lambda s: s) if fwd else (lambda s: S - 1 - s)
        lat = lambda cb: pl.BlockSpec((SEG, width), lambda b, s: (b * S + seg_of(s), cb))
        ctx = lambda cb: pl.BlockSpec((Lc, width), lambda b, s: (NL // Lc + b, cb))
        gate_specs = [
            pl.BlockSpec((None, H, 4, SEG // M_CHUNK, M_CHUNK), lambda b, s: (b, 0, 0, seg_of(s), 0)),
            pl.BlockSpec((None, H, SEG, 4), lambda b, s: (b, 0, seg_of(s), 0)),
            pl.BlockSpec((None, H, 4, Lc // M_CHUNK, M_CHUNK), lambda b, s: (b, 0, 0, 0, 0)),
            pl.BlockSpec((None, H, Lc, 4), lambda b, s: (b, 0, 0, 0))]
        lat_out = pl.BlockSpec((SEG, width), lambda b, s: (b * S + seg_of(s), 0))
        ctx_out = pl.BlockSpec((Lc, width), lambda b, s: (b, 0))
        return pl.pallas_call(
            functools.partial(_mlstm_kernel, fwd=fwd, SEG=SEG, Lc=Lc, H=H, dh=dh),
            out_shape=(jax.ShapeDtypeStruct((NL, width), out_dtype),
                       jax.ShapeDtypeStruct((B * Lc, width), out_dtype)),
            grid=(B, S),
            in_specs=(gate_specs + [lat(0), lat(1), lat(2), ctx(0), ctx(1), ctx(2)]
                      + extra_specs(lat, ctx, lat_out, ctx_out)),
            out_specs=(lat_out, ctx_out), scratch_shapes=scratch,
            compiler_params=_cparams(("parallel", "arbitrary")),
            name="mlstm_fwd" if fwd else "mlstm_bwd",
        )(gr, gc, grc, gcc, QK, QK, P, QK, QK, P, *extra_in)

    hf, hfc = call(True, [], lambda *_: [], F32)
    return call(False, [P, P, hf, hfc, jnp.tile(norm_g.astype(F32), H).reshape(1, width)],
                lambda lat, ctx, lat_out, ctx_out: [lat(3), ctx(3), lat_out, ctx_out,
                                                    pl.BlockSpec((1, width), lambda b, s: (0, 0))], BF16)


def _attn_prep_kernel(x_ref, g_ref, cos_ref, sin_ref, o_ref, *, scale, dh):
    n_blk = x_ref.shape[1] // LANES
    r = lax.broadcasted_iota(jnp.int32, (LANES, LANES), 0)
    c = lax.broadcasted_iota(jnp.int32, (LANES, LANES), 1)
    group = (r // dh == c // dh).astype(BF16)
    lane = lax.broadcasted_iota(jnp.int32, (1, LANES), 1)
    quarter = dh // 4
    first = (lane % (2 * quarter)) < quarter
    cos = cos_ref[...]
    sin = sin_ref[...]
    for hb in range(n_blk):
        cols = slice(hb * LANES, (hb + 1) * LANES)
        x = x_ref[:, cols].astype(F32)
        hi, lo = _split(x * x)
        ms = (_dot(hi, group) + _dot(lo, group)) * (1.0 / dh)
        y = x * lax.rsqrt(ms + EPS) * g_ref[:, cols]
        rot = jnp.where(first, -pltpu.roll(y, LANES - quarter, 1), pltpu.roll(y, quarter, 1))
        o_ref[:, cols] = ((y * cos + rot * sin) * scale).astype(o_ref.dtype)


def _attn_prep(P, col0, gain, cos_tab, sin_tab, scale, *, L, NL, dh):
    T = P.shape[0]
    W = A_HEADS * 2 * dh
    tm = ROW_TILE
    nlt, tps = NL // tm, L // tm
    tab = pl.BlockSpec((tm, LANES), lambda i: (jnp.where(i < nlt, i % tps, tps), 0))
    g = jnp.tile(gain.astype(F32), W // dh).reshape(1, W)
    return pl.pallas_call(
        functools.partial(_attn_prep_kernel, scale=scale, dh=dh),
        out_shape=jax.ShapeDtypeStruct((T, W), BF16),
        grid=(T // tm,),
        in_specs=[pl.BlockSpec((tm, W), lambda i: (i, col0 // W)),
                  pl.BlockSpec((1, W), lambda i: (0, 0)), tab, tab],
        out_specs=pl.BlockSpec((tm, W), lambda i: (i, 0)),
        compiler_params=_cparams(("parallel",)), name="attn_prep",
    )(P, g, cos_tab, sin_tab)


def _attn_kernel(*refs, n_lat, tk, L, Lc, dh, lam_init):
    if n_lat:
        lam_ref, q_ref, kl_ref, vl_ref, kc_ref, vc_ref, sg_ref, o_ref, vext, acc = refs
    else:
        lam_ref, q_ref, kc_ref, vc_ref, sg_ref, o_ref, vext, acc = refs
    dv = 2 * dh
    ctx0 = n_lat * tk

    @pl.when(pl.program_id(2) == 0)
    def _():
        if n_lat:
            vext[0:L, 0:dv] = vl_ref[...]
        vext[ctx0:ctx0 + Lc, 0:dv] = vc_ref[...]
        vext[:, dv:2 * dv] = jnp.ones((vext.shape[0], dv), BF16)

    q = q_ref[...]
    lane = lax.broadcasted_iota(jnp.int32, (1, dv), 1)
    qs = (jnp.where(lane < dh, q, jnp.zeros_like(q)), jnp.where(lane >= dh, q, jnp.zeros_like(q)))
    acc[...] = jnp.zeros_like(acc)
    tq = q.shape[0]

    def update(comp, kblk, vblk, m_old):
        s = lax.dot_general(qs[comp], kblk, (((1,), (1,)), ((), ())), preferred_element_type=F32)
        m_new = jnp.maximum(m_old, jnp.max(s, axis=-1, keepdims=True))
        p = jnp.exp2(s - m_new).astype(BF16)
        acc[comp] = jnp.exp2(m_old - m_new) * acc[comp] + _dot(p, vblk)
        return m_new

    m = (jnp.full((tq, 1), -jnp.inf, F32),) * 2
    for c in range(n_lat):
        kblk, vblk = kl_ref[c * tk:(c + 1) * tk, :], vext[c * tk:(c + 1) * tk, :]
        m = update(0, kblk, vblk, m[0]), update(1, kblk, vblk, m[1])
    kblk, vblk = kc_ref[...], vext[ctx0:ctx0 + Lc, :]
    update(0, kblk, vblk, m[0])
    update(1, kblk, vblk, m[1])

    lp = lam_ref[...]
    lam = (jnp.exp(jnp.sum(lp[0:1] * lp[1:2], axis=-1, keepdims=True))
           - jnp.exp(jnp.sum(lp[2:3] * lp[3:4], axis=-1, keepdims=True)) + lam_init)
    a0, a1 = acc[0], acc[1]
    o = a0[:, 0:dv] / a0[:, dv:dv + 1] - lam * (a1[:, 0:dv] / a1[:, dv:dv + 1])
    y = o * lax.rsqrt(jnp.mean(o * o, axis=-1, keepdims=True) + EPS) * sg_ref[...] * (1.0 - lam_init)
    o_ref[...] = y.astype(o_ref.dtype)


def _attention(Qh, Kh, P, vcol0, lam_p, sub_g, lam_init, *, B, L, Lc, dh, latent):
    NL = B * L
    dv = 2 * dh
    H = A_HEADS
    vb = vcol0 // dv
    if latent:
        tq, tk = 512, min(L, 1024)
        n_lat, nq, rows_out = L // tk, L // tq, NL
        q_spec = pl.BlockSpec((tq, dv), lambda b, h, i: (b * nq + i, h))
        lat_specs = [pl.BlockSpec((L, dv), lambda b, h, i: (b, h)),
                     pl.BlockSpec((L, dv), lambda b, h, i: (b, vb + h))]
        lat_args = [Kh, P]
        o_spec = pl.BlockSpec((tq, dv), lambda b, h, i: (b * nq + i, h))
        nkeys = L + Lc
    else:
        tq, tk = Lc, 512
        n_lat, nq, rows_out = 0, 1, B * Lc
        q_spec = pl.BlockSpec((Lc, dv), lambda b, h, i: (NL // Lc + b, h))
        lat_specs, lat_args = [], []
        o_spec = pl.BlockSpec((Lc, dv), lambda b, h, i: (b, h))
        nkeys = Lc
    ctx_specs = [pl.BlockSpec((Lc, dv), lambda b, h, i: (NL // Lc + b, h)),
                 pl.BlockSpec((Lc, dv), lambda b, h, i: (NL // Lc + b, vb + h))]
    return pl.pallas_call(
        functools.partial(_attn_kernel, n_lat=n_lat, tk=tk, L=L, Lc=Lc, dh=dh, lam_init=lam_init),
        out_shape=jax.ShapeDtypeStruct((rows_out, H * dv), BF16),
        grid=(B, H, nq),
        in_specs=[pl.BlockSpec((4, dh), lambda b, h, i: (0, 0)), q_spec] + lat_specs + ctx_specs
                 + [pl.BlockSpec((1, dv), lambda b, h, i: (0, 0))],
        out_specs=o_spec,
        scratch_shapes=[pltpu.VMEM((nkeys, 2 * dv), BF16), pltpu.VMEM((2, tq, 2 * dv), F32)],
        compiler_params=_cparams(("parallel", "parallel", "arbitrary")),
        name="diff_attn_latent" if latent else "diff_attn_ctx",
    )(lam_p.astype(F32), Qh, *lat_args, Kh, P, sub_g.reshape(1, dv).astype(F32))


def _rope_tables(L, dh, tm):
    rows = L // GRID_W
    row = np.repeat(np.arange(rows), GRID_W).astype(np.float64)
    col = np.tile(np.arange(GRID_W), rows).astype(np.float64)
    nf = dh // 4
    inv = (np.float32(ROPE_BASE) ** (-np.arange(nf, dtype=np.float32) / nf)).astype(np.float64)
    ang = np.concatenate([row[:, None] * inv] * 2 + [col[:, None] * inv] * 2, axis=-1)
    ang = np.tile(ang.astype(np.float32).astype(np.float64), (1, LANES // dh))
    cos = np.concatenate([np.cos(ang), np.ones((tm, LANES))], axis=0)
    sin = np.concatenate([np.sin(ang), np.zeros((tm, LANES))], axis=0)
    return jnp.asarray(cos, F32), jnp.asarray(sin, F32)


def _filter_kernel(f_ref, w1a_ref, w1b_ref, b1_ref, w2_ref, b2_ref, fr_ref, w3a_ref, w3b_ref, dl_ref,
                   taps_ref, ssq_ref):
    f = f_ref[...]
    half = f.shape[0] // 2
    freq = fr_ref[...]
    h = jnp.sin(freq * (_dot3(f[:half], w1a_ref[...]) + _dot3(f[half:], w1b_ref[...]) + b1_ref[...]))
    h = jnp.sin(freq * (_dot3(h, w2_ref[...]) + b2_ref[...]))
    h = jnp.concatenate([_dot3(h, w3a_ref[...]), _dot3(h, w3b_ref[...])], axis=0)
    h = h * jnp.exp(-f[:, 0:1] * jnp.abs(dl_ref[...]))

    @pl.when(pl.program_id(1) == 0)
    def _():
        ssq_ref[...] = jnp.zeros_like(ssq_ref)

    ssq_ref[...] += jnp.sum(h * h, axis=0, keepdims=True)
    taps_ref[...] = h * f[:, H_EMB:H_EMB + 1]


def _filter_features(L):
    t = np.linspace(0.0, 1.0, L, dtype=np.float32).astype(np.float64)[:, None]
    bands = (H_EMB - 1) // 2
    w = (np.float32(2.0 * math.pi) * np.arange(L, dtype=np.float32) / np.float32(L)).astype(np.float64)[:, None]
    f = np.linspace(1e-4, bands - 1, bands, dtype=np.float32).astype(np.float64)[None, :]
    fw = (f.astype(np.float32) * w.astype(np.float32)).astype(np.float64)
    z = np.concatenate([t, np.cos(fw), -np.sin(fw)], axis=-1)
    feat = np.zeros((2 * L, LANES), np.float64)
    feat[:L, :H_EMB] = z
    idx = (L - np.arange(L)) % L
    feat[L:, :H_EMB] = z[idx]
    feat[:, H_EMB] = 1.0
    feat[L, H_EMB] = 0.0
    return jnp.asarray(feat, F32)


def _hyena_filters(L, w1, b1, w2, b2, w3, freq, delta, C):
    assert 2 * H_FFN == LANES
    pad = LANES - H_FFN
    w1a = jnp.pad(w1, ((0, LANES - H_EMB), (0, pad)))
    w1b = jnp.pad(w1, ((0, LANES - H_EMB), (pad, 0)))
    w2d = jnp.pad(w2, ((0, pad), (0, pad))) + jnp.pad(w2, ((pad, 0), (pad, 0)))
    w3a = jnp.pad(w3, ((0, pad), (0, 0)))
    w3b = jnp.pad(w3, ((pad, 0), (0, 0)))
    row = lambda a: jnp.tile(a.reshape(1, H_FFN), (1, 2))
    tr = min(L, 512)
    nr = L // tr
    const = lambda shape: pl.BlockSpec(shape, lambda hf, r: (0, 0))
    w3spec = pl.BlockSpec((LANES, C), lambda hf, r: (0, hf))
    return pl.pallas_call(
        _filter_kernel,
        out_shape=(jax.ShapeDtypeStruct((2 * L, C), F32), jax.ShapeDtypeStruct((2, 1, C), F32)),
        grid=(2, nr),
        in_specs=[pl.BlockSpec((tr, LANES), lambda hf, r: (hf * nr + r, 0)),
                  const((LANES, LANES)), const((LANES, LANES)), const((1, LANES)),
                  const((LANES, LANES)), const((1, LANES)), const((1, LANES)),
                  w3spec, w3spec,
                  pl.BlockSpec((1, C), lambda hf, r: (0, hf))],
        out_specs=(pl.BlockSpec((tr, C), lambda hf, r: (hf * nr + r, 0)),
                   pl.BlockSpec((None, 1, C), lambda hf, r: (hf, 0, 0))),
        compiler_params=_cparams(("parallel", "arbitrary")), name="hyena_filter",
    )(_filter_features(L), w1a, w1b, row(b1), w2d, row(b2), row(freq), w3a, w3b, delta.reshape(1, 2 * C))


def _cblock(m):
    return np.block([[m.real, -m.imag], [m.imag, m.real]])


def _dft_consts(L):
    N = 2 * L
    N2 = LANES
    N1 = N // N2
    half = N1 // 2
    n1 = np.arange(N1)
    n2 = np.arange(N2)
    F1 = np.exp(-2j * np.pi * np.outer(n1, n1) / N1)
    F2 = np.exp(-2j * np.pi * np.outer(n2, n2) / N2)
    a_data = _cblock(F1[:, :half])
    a_taps = np.concatenate([F1.real, F1.imag], axis=0)
    b_fwd = _cblock(F2)
    b_inv = _cblock(np.conj(F2))
    fin = _cblock(np.conj(F1)[:half, :]) / N
    ang = 2.0 * np.pi * np.outer(n2, n1) / N
    tw = dict(c_a=np.cos(ang)[:, :, None], s_a=np.sin(ang)[:, :, None],
              c_b=np.cos(ang.T)[:, :, None], s_b=np.sin(ang.T)[:, :, None])
    as_bf16 = lambda m: jnp.asarray(np.asarray(m, np.float32)).astype(BF16)
    return dict(N1=N1, a_data=as_bf16(a_data), a_taps=as_bf16(a_taps),
                b_fwd=as_bf16(b_fwd), b_inv=as_bf16(b_inv), fin=as_bf16(fin),
                tw={k: jnp.asarray(v, F32) for k, v in tw.items()})


def _time_slice_kernel(*refs, n_in, n_vmem, n_out, N1, compute):
    ins, vmem = refs[:n_in], refs[n_in:n_in + n_vmem]
    outs = refs[n_in + n_vmem:n_in + n_vmem + n_out]
    in_buf, out_buf, in_sem, out_sem = refs[n_in + n_vmem + n_out:]
    j = pl.program_id(0)
    n = pl.num_programs(0)
    slot = j % 2

    def in_copies(step, sl):
        return [pltpu.make_async_copy(src.at[pl.ds(0, N1), step, :], in_buf.at[sl, i], in_sem.at[sl])
                for i, src in enumerate(ins)]

    def out_copies(step, sl):
        return [pltpu.make_async_copy(out_buf.at[sl, i], dst.at[:, step, :], out_sem.at[sl])
                for i, dst in enumerate(outs)]

    @pl.when(j == 0)
    def _():
        for cp in in_copies(0, 0):
            cp.start()

    @pl.when(j + 1 < n)
    def _():
        for cp in in_copies(j + 1, 1 - slot):
            cp.start()

    for cp in in_copies(j, slot):
        cp.wait()

    @pl.when(j >= 2)
    def _():
        for cp in out_copies(j - 2, slot):
            cp.wait()

    for i, r in enumerate(compute([in_buf[slot, i] for i in range(n_in)], vmem)):
        out_buf[slot, i] = r
    for cp in out_copies(j, slot):
        cp.start()

    @pl.when(j == n - 1)
    def _():
        for cp in out_copies(j - 1, 1 - slot) + out_copies(j, slot):
            cp.wait()


def _time_slice_call(compute, hbm_inputs, vmem_inputs, vmem_specs, n_out, *, N1, C, name):
    assert LANES >= 2
    any_spec = pl.BlockSpec(memory_space=pl.ANY)
    shape = jax.ShapeDtypeStruct((N1, LANES, C), F32)
    n_in = len(hbm_inputs)
    return pl.pallas_call(
        functools.partial(_time_slice_kernel, n_in=n_in, n_vmem=len(vmem_inputs), n_out=n_out, N1=N1,
                          compute=compute),
        out_shape=(shape,) * n_out, grid=(LANES,),
        in_specs=[any_spec] * n_in + list(vmem_specs),
        out_specs=(any_spec,) * n_out,
        scratch_shapes=[pltpu.VMEM((2, n_in, N1, C), F32), pltpu.VMEM((2, n_out, N1, C), F32),
                        pltpu.SemaphoreType.DMA((2,)), pltpu.SemaphoreType.DMA((2,))],
        compiler_params=_cparams(("arbitrary",)), name=name,
    )(*hbm_inputs, *vmem_inputs)


def _pack_complex(re, im):
    hi = lax.bitcast_convert_type(re.astype(BF16).astype(F32), jnp.uint32)
    lo = lax.bitcast_convert_type(im.astype(BF16).astype(F32), jnp.uint32)
    return lax.bitcast_convert_type(hi | (lo >> 16), F32)


def _unpack_complex(p):
    w = lax.bitcast_convert_type(p, jnp.uint32)
    re = lax.bitcast_convert_type(w & jnp.uint32(0xFFFF0000), F32)
    im = lax.bitcast_convert_type(w << 16, F32)
    return jnp.concatenate([re, im], axis=0).astype(BF16)


def _fft_a_compute(xs, vmem):
    m_ref, c_ref, s_ref = vmem
    r = _dot(m_ref[...], xs[0].astype(BF16))
    n1 = r.shape[0] // 2
    re, im = r[:n1], r[n1:]
    c, s = c_ref[...], s_ref[...]
    return (_pack_complex(re * c + im * s, im * c - re * s),)


def _fft_stage_a(x3, mat, tw, *, N1, C):
    twspec = pl.BlockSpec((None, N1, 1), lambda j: (j, 0, 0))
    return _time_slice_call(_fft_a_compute, [x3], [mat, tw["c_a"], tw["s_a"]],
                            [pl.BlockSpec(mat.shape, lambda j: (0, 0)), twspec, twspec], 1,
                            N1=N1, C=C, name="hyena_fft_a")[0]


def _fft_mid_kernel(a_ref, t_ref, ssq, f_ref, i_ref, c_ref, s_ref, b_ref):
    N2 = LANES
    x = _dot(f_ref[...], _unpack_complex(a_ref[...]))
    h = _dot(f_ref[...], _unpack_complex(t_ref[...]))
    scale = lax.rsqrt(ssq[0] + ssq[1] + EPS)
    xr, xi, hr, hi = x[:N2], x[N2:], h[:N2] * scale, h[N2:] * scale
    y = jnp.concatenate([xr * hr - xi * hi, xr * hi + xi * hr], axis=0)
    r = _dot(i_ref[...], y.astype(BF16))
    re, im = r[:N2], r[N2:]
    c, s = c_ref[...], s_ref[...]
    b_ref[...] = _pack_complex(re * c - im * s, im * c + re * s)


def _fft_mid(A, Tp, ssq, consts, *, C, ct):
    N1 = consts["N1"]
    blk = pl.BlockSpec((None, LANES, ct), lambda k1, c: (k1, 0, c))
    const = pl.BlockSpec((2 * LANES, 2 * LANES), lambda k1, c: (0, 0))
    twspec = pl.BlockSpec((None, LANES, 1), lambda k1, c: (k1, 0, 0))
    return pl.pallas_call(
        _fft_mid_kernel, out_shape=jax.ShapeDtypeStruct((N1, LANES, C), F32), grid=(N1, C // ct),
        in_specs=[blk, blk, pl.BlockSpec((2, 1, ct), lambda k1, c: (0, 0, c)),
                  const, const, twspec, twspec],
        out_specs=blk,
        compiler_params=_cparams(("parallel", "parallel")), name="hyena_fft_mid",
    )(A, Tp, ssq, consts["b_fwd"], consts["b_inv"], consts["tw"]["c_b"], consts["tw"]["s_b"])


def _fft_fin_compute(xs, vmem):
    b, x0, z = xs
    m_ref, bias = vmem
    r = _dot(m_ref[...], _unpack_complex(b))
    return (x0 * (r + bias[...] * z),)


def _fft_final(Bp, x0_3d, z_3d, bias, consts, *, C):
    mat = consts["fin"]
    return _time_slice_call(_fft_fin_compute, [Bp, x0_3d, z_3d], [mat, bias],
                            [pl.BlockSpec(mat.shape, lambda j: (0, 0)), pl.BlockSpec((1, C), lambda j: (0, 0))],
                            1, N1=consts["N1"], C=C, name="hyena_fft_final")[0]


def _hyena_latent(x0, z, taps, ssq, bias, consts, *, B, L, C):
    assert B == 2
    N1 = consts["N1"]
    ct = C
    T = x0.shape[0]
    z3 = z.reshape(T // LANES, LANES, C)
    x03 = x0.reshape(T // LANES, LANES, C)
    A = _fft_stage_a(z3, consts["a_data"], consts["tw"], N1=N1, C=C)
    Tp = _fft_stage_a(taps.reshape(N1, LANES, C), consts["a_taps"], consts["tw"], N1=N1, C=C)
    Bp = _fft_mid(A, Tp, ssq, consts, C=C, ct=ct)
    y = _fft_final(Bp, x03, z3, bias.reshape(1, C).astype(F32), consts, C=C)
    return y.reshape(B * L, C)


def _hyena_ctx_kernel(z, x0, taps, ssq, bias, dh, dl, th, tl, ih, il, y, *, Lc):
    n = 2 * Lc
    x = _dot3c(dh[...], dl[...], z[...])
    h = _dot3c(th[...], tl[...], taps[...])
    scale = lax.rsqrt(ssq[0] + ssq[1] + EPS)
    xr, xi, hr, hi = x[:n], x[n:], h[:n] * scale, h[n:] * scale
    r = _dot3c(ih[...], il[...], jnp.concatenate([xr * hr - xi * hi, xr * hi + xi * hr], axis=0))
    y[...] = x0[...] * (r + bias[...] * z[...])


def _hyena_ctx(x0, z, taps, ssq, bias, *, B, L, Lc, C):
    assert B == 2 and (B * L) % (2 * Lc) == 0
    n = 2 * Lc
    idx = np.arange(n)
    F = np.exp(-2j * np.pi * np.outer(idx, idx) / n)
    d = _np_split(_cblock(F[:, :Lc]))
    t = _np_split(np.concatenate([F.real, F.imag], axis=0))
    inv = _np_split(_cblock(np.conj(F)[:Lc, :]) / n)
    ct = 256
    r0 = (B * L) // n
    both = pl.BlockSpec((n, ct), lambda c: (r0, c))
    const = lambda m: pl.BlockSpec(m.shape, lambda c: (0, 0))
    return pl.pallas_call(
        functools.partial(_hyena_ctx_kernel, Lc=Lc),
        out_shape=jax.ShapeDtypeStruct((n, C), F32),
        grid=(C // ct,),
        in_specs=[both, both,
                  pl.BlockSpec((n, ct), lambda c: (0, c)),
                  pl.BlockSpec((2, 1, ct), lambda c: (0, 0, c)),
                  pl.BlockSpec((1, ct), lambda c: (0, c)),
                  const(d[0]), const(d[1]), const(t[0]), const(t[1]), const(inv[0]), const(inv[1])],
        out_specs=pl.BlockSpec((n, ct), lambda c: (0, c)),
        compiler_params=_cparams(("parallel",)), name="hyena_ctx",
    )(z, x0, taps, ssq, bias.reshape(1, C).astype(F32), *d, *t, *inv)


def _merge_kernel(*refs, with_ctx, n_lat_tiles):
    if with_ctx:
        (ym, ya, yh, ymc, yac, yhc, gm, ga, gh, mb, wm, wa, wh, wo, x_ref, g1, o_ref) = refs
        is_ctx = pl.program_id(0) >= n_lat_tiles
        pick = lambda lat, ctx: jnp.where(is_ctx, ctx[...], lat[...])
        m, a, h = pick(ym, ymc), pick(ya, yac), pick(yh, yhc)
    else:
        (ym, ya, yh, gm, ga, gh, mb, wm, wa, wh, wo, x_ref, g1, o_ref) = refs
        m, a, h = ym[...], ya[...], yh[...]
    D = x_ref.shape[1]
    bias = mb[...]
    gate = lambda g, k: _sigmoid(g[...].astype(F32) + bias[:, k * D:(k + 1) * D])
    y = (gate(gm, 0) * _dot(m, wm[...]) + gate(ga, 1) * _dot(a, wa[...])
         + gate(gh, 2) * _dot(h.astype(BF16), wh[...]))
    o_ref[...] = x_ref[...] + g1[...] * _dot(y.astype(BF16), wo[...])


def _merge(X, P, gcol0, branches, merge_b, weights, layer, mod, gate_idx, *, n_rows, B, L, with_ctx):
    D = X.shape[1]
    ctx_rows = X.shape[0] - B * L
    tm = 2 * ROW_TILE if L % (2 * ROW_TILE) == 0 and ctx_rows % (2 * ROW_TILE) == 0 else ROW_TILE
    nlt, tps = (B * L) // tm, L // tm
    grp = functools.partial(_group_of_tile, n_lat_tiles=nlt, tiles_per_seq=tps, n_batch=B)
    lat = pl.BlockSpec((tm, D), lambda i: (jnp.minimum(i, nlt - 1), 0))
    ctx = pl.BlockSpec((tm, D), lambda i: (jnp.maximum(i - nlt, 0), 0))
    gb = gcol0 // D
    gspec = lambda k: pl.BlockSpec((tm, D), lambda i: (i, gb + k))
    wspec = pl.BlockSpec((None, D, D), lambda i: (layer, 0, 0))
    in_specs = ([lat] * 3 + ([ctx] * 3 if with_ctx else []) + [gspec(0), gspec(1), gspec(2)]
                + [pl.BlockSpec((1, 3 * D), lambda i: (0, 0))] + [wspec] * 4
                + [pl.BlockSpec((tm, D), lambda i: (i, 0)),
                   pl.BlockSpec((None, None, 1, D), lambda i: (grp(i), gate_idx, 0, 0))])
    return pl.pallas_call(
        functools.partial(_merge_kernel, with_ctx=with_ctx, n_lat_tiles=nlt),
        out_shape=jax.ShapeDtypeStruct((n_rows, D), F32),
        grid=(n_rows // tm,), in_specs=in_specs,
        out_specs=pl.BlockSpec((tm, D), lambda i: (i, 0)),
        compiler_params=_cparams(("parallel",)), name="merge",
    )(*branches, P, P, P, merge_b.reshape(1, 3 * D).astype(F32), *weights, X, mod)


EXPERT_BITS = 5
assert 1 << EXPERT_BITS == N_EXPERTS


def _slot_kernel(pstart, code_ref, o_ref):
    code = code_ref[...]
    expert = code & (N_EXPERTS - 1)
    slot = code >> EXPERT_BITS
    for e in range(N_EXPERTS):
        slot = slot + jnp.where(expert == e, pstart[e], 0)
    o_ref[...] = slot


def _slot_table(codes, pstart):
    A = codes.shape[0]
    assert A % (8 * LANES) == 0
    shape = (A // LANES, LANES)
    whole = pl.BlockSpec(shape, lambda i, ps: (0, 0))
    return pl.pallas_call(
        _slot_kernel,
        out_shape=jax.ShapeDtypeStruct(shape, jnp.int32),
        grid_spec=pltpu.PrefetchScalarGridSpec(num_scalar_prefetch=1, grid=(1,), in_specs=[whole], out_specs=whole),
        compiler_params=_cparams(("arbitrary",)), name="moe_slots",
    )(pstart, codes.reshape(shape)).reshape(A)


def _dispatch_kernel(slots, tok_ref, xs_zero, xs_out, sem, *, n_tok):
    del xs_zero
    R = tok_ref.shape[0]
    base = pl.program_id(0) * R

    def copy(pos, r):
        return pltpu.make_async_copy(tok_ref.at[pl.ds(r, 1), :], xs_out.at[pl.ds(slots[pos], 1), :], sem)

    for wait in (False, True):
        for k in range(TOP_K):
            for r in range(R):
                cp = copy(k * n_tok + base + r, r)
                cp.wait() if wait else cp.start()


def _moe_dispatch(tok, slots, P):
    N, D = tok.shape
    tm = MOE_TOKEN_TILE
    return pl.pallas_call(
        functools.partial(_dispatch_kernel, n_tok=N),
        out_shape=jax.ShapeDtypeStruct((P, D), F32),
        grid_spec=pltpu.PrefetchScalarGridSpec(
            num_scalar_prefetch=1, grid=(N // tm,),
            in_specs=[pl.BlockSpec((tm, D), lambda i, sl: (i, 0)),
                      pl.BlockSpec(memory_space=pl.ANY)],
            out_specs=pl.BlockSpec(memory_space=pl.ANY),
            scratch_shapes=[pltpu.SemaphoreType.DMA]),
        input_output_aliases={2: 0},
        compiler_params=_cparams(("arbitrary",)), name="moe_dispatch",
    )(slots, tok, jnp.zeros((P, D), F32))


def _moe_kernel(blk_e, xs_ref, w1_ref, w3_ref, w2_ref, y_ref, w1b, w3b, w2b):
    i = pl.program_id(0)

    @pl.when((i == 0) | (blk_e[i] != blk_e[jnp.maximum(i - 1, 0)]))
    def _():
        w1b[...] = w1_ref[...].astype(BF16)
        w3b[...] = w3_ref[...].astype(BF16)
        w2b[...] = w2_ref[...].astype(BF16)

    x = xs_ref[...].astype(BF16)
    a = _dot(x, w1b[...])
    g = (a * _sigmoid(a)) * _dot(x, w3b[...])
    y_ref[...] = _dot(g.astype(BF16), w2b[...])


def _moe_experts(xs, blk_e, w1, w3, w2, layer):
    P, D = xs.shape
    De = w1.shape[3]
    return pl.pallas_call(
        _moe_kernel,
        out_shape=jax.ShapeDtypeStruct((P, D), F32),
        grid_spec=pltpu.PrefetchScalarGridSpec(
            num_scalar_prefetch=1, grid=(P // MOE_BLOCK,),
            in_specs=[pl.BlockSpec((MOE_BLOCK, D), lambda i, be: (i, 0)),
                      pl.BlockSpec((None, None, D, De), lambda i, be: (layer, be[i], 0, 0)),
                      pl.BlockSpec((None, None, D, De), lambda i, be: (layer, be[i], 0, 0)),
                      pl.BlockSpec((None, None, De, D), lambda i, be: (layer, be[i], 0, 0))],
            out_specs=pl.BlockSpec((MOE_BLOCK, D), lambda i, be: (i, 0)),
            scratch_shapes=[pltpu.VMEM((D, De), BF16), pltpu.VMEM((D, De), BF16), pltpu.VMEM((De, D), BF16)]),
        compiler_params=_cparams(("arbitrary",)), name="moe_experts",
    )(blk_e, xs, w1, w3, w2)


def _combine_kernel(slots, yb_hbm, x_ref, g2, wt_ref, o_ref, buf, sems, *, n_tok):
    i = pl.program_id(0)
    n = pl.num_programs(0)
    slot = i % 2
    R = x_ref.shape[0]

    def gather(tile, slot, wait):
        dst = buf.at[slot]
        for k in range(TOP_K):
            for r in range(R):
                src = slots[k * n_tok + tile * R + r]
                cp = pltpu.make_async_copy(yb_hbm.at[pl.ds(src, 1), :], dst.at[k, pl.ds(r, 1), :], sems.at[slot])
                cp.wait() if wait else cp.start()

    @pl.when(i == 0)
    def _():
        gather(0, 0, False)

    @pl.when(i + 1 < n)
    def _():
        gather(i + 1, 1 - slot, False)

    gather(i, slot, True)
    w = wt_ref[...]
    acc = w[:, 0:1] * buf[slot, 0]
    for k in range(1, TOP_K):
        acc = acc + w[:, k:k + 1] * buf[slot, k]
    o_ref[...] = x_ref[...] + g2[...] * acc


def _moe_combine(yb, slots, wts, X, mod, gate_idx, *, n_rows, B, L):
    D = X.shape[1]
    tm = MOE_TOKEN_TILE
    nlt, tps = (B * L) // tm, L // tm
    grp = functools.partial(_group_of_tile, n_lat_tiles=nlt, tiles_per_seq=tps, n_batch=B)
    return pl.pallas_call(
        functools.partial(_combine_kernel, n_tok=n_rows),
        out_shape=jax.ShapeDtypeStruct((n_rows, D), F32),
        grid_spec=pltpu.PrefetchScalarGridSpec(
            num_scalar_prefetch=1, grid=(n_rows // tm,),
            in_specs=[pl.BlockSpec(memory_space=pl.ANY),
                      pl.BlockSpec((tm, D), lambda i, sl: (i, 0)),
                      pl.BlockSpec((None, None, 1, D), lambda i, sl: (grp(i), gate_idx, 0, 0)),
                      pl.BlockSpec((tm, LANES), lambda i, sl: (i, 0))],
            out_specs=pl.BlockSpec((tm, D), lambda i, sl: (i, 0)),
            scratch_shapes=[pltpu.VMEM((2, TOP_K, tm, D), F32), pltpu.SemaphoreType.DMA((2,))]),
        compiler_params=_cparams(("arbitrary",)), name="moe_combine",
    )(slots, yb, X, mod, wts)


def _dispatch_tables(code, counts, n_tok):
    E, K = N_EXPERTS, TOP_K
    cnt = counts[0, :E].astype(jnp.int32)
    padded = (cnt + MOE_BLOCK - 1) // MOE_BLOCK * MOE_BLOCK
    pend = jnp.cumsum(padded)
    pstart = (pend - padded).astype(jnp.int32)
    P = -(-(n_tok * K + E * (MOE_BLOCK - 1)) // MOE_BLOCK) * MOE_BLOCK
    blk_row = jnp.arange(P // MOE_BLOCK, dtype=jnp.int32) * MOE_BLOCK
    blk_e = jnp.minimum(jnp.sum(pend[None, :] <= blk_row[:, None], axis=1), E - 1).astype(jnp.int32)
    return _slot_table(code[:, :K].T.reshape(-1), pstart), P, blk_e


def kernel(x, c, ctx, c_ctx, mod_w, mod_b, norm1_g, norm2_g, w_in, merge_b, m_conv_w, m_conv_b, m_gate_b, m_norm_g, a_qnorm_g, a_knorm_g, a_lambda, a_subln_g, h_conv_w, h_conv_b, h_ffn_w1, h_ffn_b1, h_ffn_w2, h_ffn_b2, h_ffn_w3, h_freq, h_decay, h_bias, w_br_m, w_br_a, w_br_h, w_out, router_w, router_b, e_w1, e_w3, e_w2):
    B, L, D = x.shape
    Lc = ctx.shape[1]
    depth = mod_w.shape[0]
    NL, NC = B * L, B * Lc
    T = NL + NC
    assert B + 1 <= 8 and Lc == ROW_TILE and L % min(L, 1024) == 0 and L % 512 == 0
    assert NL % Lc == 0 and D % LANES == 0
    width = D
    a_dh = D // (2 * A_HEADS)
    n_gates = 4 * M_HEADS
    c_mq, c_mv, c_mo = 0, 2 * width, 3 * width
    c_aq, c_ak, c_av = 4 * width, 5 * width, 6 * width
    c_hx = 7 * width
    c_gp = 10 * width

    X = jnp.concatenate([x.reshape(NL, D), ctx.reshape(NC, D)], axis=0)
    c8 = jnp.zeros((8, D), F32).at[:B].set(c).at[B].set(c_ctx)
    cos_tab, sin_tab = _rope_tables(L, a_dh, ROW_TILE)
    fft_consts = _dft_consts(L)
    g0 = 4 * width
    w_main = jnp.concatenate([w_in[:, :, :g0], w_in[:, :, g0 + n_gates:]], axis=2).astype(BF16)
    w_gate = jnp.pad(w_in[:, :, g0:g0 + n_gates], ((0, 0), (0, 0), (0, LANES - n_gates))).astype(BF16)
    merge_w = [w.astype(BF16) for w in (w_br_m, w_br_a, w_br_h, w_out)]

    for l in range(depth):
        need_ctx = l < depth - 1
        lam_init = 0.8 - 0.6 * math.exp(-0.3 * l)
        mod = _modulation(c8, mod_w, mod_b, l)[:B + 1].reshape(B + 1, 6, 1, D)

        P, G = _inproj(X, norm1_g[l], mod, w_main, w_gate, m_gate_b[l], l, L=L, B=B)

        QK = _mlstm_qk(P, m_conv_w[l], m_conv_b[l], L=L, Lc=Lc, NL=NL, width=width)
        ym, ymc = _mlstm(QK, P, G, m_norm_g[l], B=B, L=L, Lc=Lc, width=width)

        Qh = _attn_prep(P, c_aq, a_qnorm_g[l], cos_tab, sin_tab, a_dh ** -0.5 * LOG2E, L=L, NL=NL, dh=a_dh)
        Kh = _attn_prep(P, c_ak, a_knorm_g[l], cos_tab, sin_tab, 1.0, L=L, NL=NL, dh=a_dh)
        attn = functools.partial(_attention, Qh, Kh, P, c_av, a_lambda[l], a_subln_g[l], lam_init,
                                 B=B, L=L, Lc=Lc, dh=a_dh)
        ya = attn(latent=True)

        x0, z = _hyena_pre(P, c_hx, h_conv_w[l], h_conv_b[l], L=L, Lc=Lc, NL=NL, C=width)
        filt = (h_ffn_w1[l], h_ffn_b1[l], h_ffn_w2[l], h_ffn_b2[l], h_ffn_w3[l], h_freq[l], h_decay[l])
        taps, ssq = _hyena_filters(L, *filt, width)
        yh = _hyena_latent(x0, z, taps, ssq, h_bias[l], fft_consts, B=B, L=L, C=width)

        if need_ctx:
            yac = attn(latent=False)
            taps_c, ssq_c = _hyena_filters(Lc, *filt, width)
            yhc = _hyena_ctx(x0, z, taps_c, ssq_c, h_bias[l], B=B, L=L, Lc=Lc, C=width)
            branches, n_rows = [ym, ya, yh, ymc, yac, yhc], T
        else:
            branches, n_rows = [ym, ya, yh], NL
        X = _merge(X, P, c_gp, branches, merge_b[l], merge_w, l, mod, 2,
                   n_rows=n_rows, B=B, L=L, with_ctx=need_ctx)

        tok, code, wts, counts = _norm_mod(X, norm2_g[l], mod, 3, 4, n_rows=n_rows, L=L, B=B,
                                           router=(router_w, router_b))
        slots, n_slots, blk_e = _dispatch_tables(code, counts, n_rows)
        xs = _moe_dispatch(tok, slots, n_slots)
        yb = _moe_experts(xs, blk_e, e_w1, e_w3, e_w2, l)
        X = _moe_combine(yb, slots, wts, X, mod, 5, n_rows=n_rows, B=B, L=L)
    return X[:NL].reshape(B, L, D)
```

```python
import functools
import math

import numpy as np
import jax
import jax.numpy as jnp
from jax import lax
from jax.experimental import pallas as pl
from jax.experimental.pallas import tpu as pltpu

F32 = jnp.float32
BF16 = jnp.bfloat16

GRID_W = 64
EPS = 1e-6
M_HEADS = 4
M_CHUNK = 128
A_HEADS = 8
ROPE_BASE = 10000.0
H_EMB = 33
H_FFN = 64
SHORT_CONV = 3
N_EXPERTS = 32
N_GROUPS = 4
TOP_K = 2
MOE_BLOCK = 256
MOE_TOKEN_TILE = 512

LANES = 128
BF16_SUBLANES = 16
V7X_VMEM_BYTES = 64 * 1024 * 1024
VMEM_LIMIT = V7X_VMEM_BYTES * 7 // 8

ROW_TILE = 256
LOG2E = 1.4426950408889634


def _cparams(sem):
    return pltpu.CompilerParams(dimension_semantics=sem, vmem_limit_bytes=VMEM_LIMIT)


def _split(x):
    hi = x.astype(BF16)
    lo = (x - hi.astype(F32)).astype(BF16)
    return hi, lo


def _dot(a, b):
    return jnp.dot(a, b, preferred_element_type=F32)


def _dot3(a, b):
    ah, al = _split(a)
    bh, bl = _split(b)
    return _dot(ah, bh) + _dot(al, bh) + _dot(ah, bl)


def _dot3c(ch, cl, x):
    xh, xl = _split(x)
    return _dot(ch, xh) + _dot(cl, xh) + _dot(ch, xl)


def _np_split(a):
    a = jnp.asarray(np.asarray(a, np.float32))
    hi = a.astype(BF16)
    lo = (a - hi.astype(F32)).astype(BF16)
    return hi, lo


def _sigmoid(x):
    return 1.0 / (1.0 + jnp.exp(-x))


def _log_sigmoid(x):
    return jnp.minimum(x, 0.0) - jnp.log(1.0 + jnp.exp(-jnp.abs(x)))


def _mod_kernel(c_ref, w_ref, b_ref, o_ref):
    c = c_ref[...]
    o_ref[...] = _dot3(c * _sigmoid(c), w_ref[...]) + b_ref[...]


def _modulation(c8, w, b, layer):
    depth, D, N = w.shape
    tn = 1536
    return pl.pallas_call(
        _mod_kernel,
        out_shape=jax.ShapeDtypeStruct((8, N), F32),
        grid=(N // tn,),
        in_specs=[pl.BlockSpec((8, D), lambda j: (0, 0)),
                  pl.BlockSpec((None, D, tn), lambda j: (layer, 0, j)),
                  pl.BlockSpec((None, 1, tn), lambda j: (layer, 0, j))],
        out_specs=pl.BlockSpec((8, tn), lambda j: (0, j)),
        compiler_params=_cparams(("parallel",)),
        name="modulation",
    )(c8, w, b.reshape(depth, 1, N))


def _norm_mod_router_kernel(x_ref, g_ref, sh_ref, sc_ref, rw_ref, rb_ref, o_ref, code_ref, wt_ref, cnt_ref, carry):
    E, G = N_EXPERTS, N_GROUPS
    gs = E // G

    @pl.when(pl.program_id(0) == 0)
    def _():
        carry[...] = jnp.zeros_like(carry)

    x = x_ref[...]
    y = x * lax.rsqrt(jnp.mean(x * x, axis=-1, keepdims=True) + EPS) * g_ref[...]
    h = y * (1.0 + sc_ref[...]) + sh_ref[...]
    o_ref[...] = h
    tm = h.shape[0]
    s = _sigmoid(_dot3(h, rw_ref[...]))
    lane = lax.broadcasted_iota(jnp.int32, (1, LANES), 1)
    lane_f = lane.astype(F32)
    sb = jnp.where(lane < E, s + rb_ref[...], -jnp.inf)
    far = float(LANES)
    best = jnp.full((tm, 1), -jnp.inf, F32)
    e1 = jnp.zeros((tm, 1), F32)
    e2 = jnp.zeros((tm, 1), F32)
    for g in range(G):
        mg = jnp.where((lane >= g * gs) & (lane < (g + 1) * gs), sb, -jnp.inf)
        m1 = jnp.max(mg, axis=-1, keepdims=True)
        i1 = jnp.min(jnp.where(mg == m1, lane_f, far), axis=-1, keepdims=True)
        mg2 = jnp.where(lane_f == i1, -jnp.inf, mg)
        m2 = jnp.max(mg2, axis=-1, keepdims=True)
        i2 = jnp.min(jnp.where(mg2 == m2, lane_f, far), axis=-1, keepdims=True)
        score = m1 + m2
        take = score > best
        best = jnp.where(take, score, best)
        e1 = jnp.where(take, i1, e1)
        e2 = jnp.where(take, i2, e2)
    oh1 = lane_f == e1
    oh2 = lane_f == e2
    s1 = jnp.sum(jnp.where(oh1, s, 0.0), axis=-1, keepdims=True)
    s2 = jnp.sum(jnp.where(oh2, s, 0.0), axis=-1, keepdims=True)
    den = s1 + s2
    r = lax.broadcasted_iota(jnp.int32, (tm, tm), 0)
    c = lax.broadcasted_iota(jnp.int32, (tm, tm), 1)
    lower = (c < r).astype(BF16)
    o1 = oh1.astype(F32)
    o2 = oh2.astype(F32)
    cum1 = _dot(lower, o1.astype(BF16))
    cum2 = _dot(lower, o2.astype(BF16))
    tot1 = jnp.sum(o1, axis=0, keepdims=True)
    base = carry[...]
    rank1 = jnp.sum(jnp.where(oh1, base + cum1, 0.0), axis=-1, keepdims=True)
    rank2 = jnp.sum(jnp.where(oh2, base + tot1 + cum2, 0.0), axis=-1, keepdims=True)
    total = base + tot1 + jnp.sum(o2, axis=0, keepdims=True)
    carry[...] = total
    cnt_ref[...] = total
    code1 = (rank1 * E + e1).astype(jnp.int32)
    code2 = (rank2 * E + e2).astype(jnp.int32)
    code_ref[...] = jnp.where(lane == 0, code1, jnp.where(lane == 1, code2, 0))
    wt_ref[...] = jnp.where(lane == 0, s1 / den, jnp.where(lane == 1, s2 / den, 0.0))


def _group_of_tile(i, n_lat_tiles, tiles_per_seq, n_batch):
    return jnp.where(i < n_lat_tiles, i // tiles_per_seq, n_batch)


def _norm_mod(x, g, mod, shift_idx, scale_idx, *, n_rows, L, B, router):
    D = x.shape[1]
    tm = 2 * ROW_TILE if L % (2 * ROW_TILE) == 0 and (n_rows - B * L) % (2 * ROW_TILE) == 0 else ROW_TILE
    nlt, tps = (B * L) // tm, L // tm
    grp = functools.partial(_group_of_tile, n_lat_tiles=nlt, tiles_per_seq=tps, n_batch=B)
    in_specs = [pl.BlockSpec((tm, D), lambda i: (i, 0)),
                pl.BlockSpec((1, D), lambda i: (0, 0)),
                pl.BlockSpec((None, None, 1, D), lambda i: (grp(i), shift_idx, 0, 0)),
                pl.BlockSpec((None, None, 1, D), lambda i: (grp(i), scale_idx, 0, 0))]
    args = [x, g.reshape(1, D), mod, mod]
    router_w, router_b = router
    E = router_w.shape[1]
    assert E == N_EXPERTS
    rw = jnp.pad(router_w, ((0, 0), (0, LANES - E)))
    rb = jnp.pad(router_b.astype(F32).reshape(1, E), ((0, 0), (0, LANES - E)))
    lanes = pl.BlockSpec((tm, LANES), lambda i: (i, 0))
    return pl.pallas_call(
        _norm_mod_router_kernel,
        out_shape=(jax.ShapeDtypeStruct((n_rows, D), F32),
                   jax.ShapeDtypeStruct((n_rows, LANES), jnp.int32),
                   jax.ShapeDtypeStruct((n_rows, LANES), F32),
                   jax.ShapeDtypeStruct((1, LANES), F32)),
        grid=(n_rows // tm,),
        in_specs=in_specs + [pl.BlockSpec((D, LANES), lambda i: (0, 0)),
                             pl.BlockSpec((1, LANES), lambda i: (0, 0))],
        out_specs=(pl.BlockSpec((tm, D), lambda i: (i, 0)), lanes, lanes,
                   pl.BlockSpec((1, LANES), lambda i: (0, 0))),
        scratch_shapes=[pltpu.VMEM((1, LANES), F32)],
        compiler_params=_cparams(("arbitrary",)), name="norm_mod_router",
    )(*args, rw, rb)


def _largest_tile(n, cap, step):
    return max(t for t in range(step, cap + 1, step) if n % t == 0)


def _inproj_kernel(x_ref, g_ref, mod_ref, w_ref, wg_ref, gb_ref, o_ref, gate_ref, h_scr, *, L, n_batch):
    tm = x_ref.shape[0]

    @pl.when(pl.program_id(1) == 0)
    def _():
        x = x_ref[...]
        y = x * lax.rsqrt(jnp.mean(x * x, axis=-1, keepdims=True) + EPS) * g_ref[...]
        row = pl.program_id(0) * tm + lax.broadcasted_iota(jnp.int32, (tm, 1), 0)
        shift, scale = mod_ref[n_batch, 0], mod_ref[n_batch, 1]
        for b in range(n_batch - 1, -1, -1):
            in_b = row < (b + 1) * L
            shift = jnp.where(in_b, mod_ref[b, 0], shift)
            scale = jnp.where(in_b, mod_ref[b, 1], scale)
        h_scr[...] = (y * (1.0 + scale) + shift).astype(h_scr.dtype)
        gates = _dot(h_scr[...], wg_ref[...]) + gb_ref[...]
        lane = lax.broadcasted_iota(jnp.int32, (1, LANES), 1)
        forget = (lane // M_HEADS) % 2 == 1
        gate_ref[...] = jnp.where(forget, _log_sigmoid(gates), gates)

    o_ref[...] = _dot(h_scr[...], w_ref[...]).astype(o_ref.dtype)


def _inproj(X, g, mod, w_main, w_gate, gate_b, layer, *, L, B):
    T, D = X.shape
    N = w_main.shape[2]
    tm, tn = _largest_tile(T, 1536, ROW_TILE), _largest_tile(N, 1024, LANES)
    return pl.pallas_call(
        functools.partial(_inproj_kernel, L=L, n_batch=B),
        out_shape=(jax.ShapeDtypeStruct((T, N), BF16), jax.ShapeDtypeStruct((T, LANES), F32)),
        grid=(T // tm, N // tn),
        in_specs=[pl.BlockSpec((tm, D), lambda i, j: (i, 0)),
                  pl.BlockSpec((1, D), lambda i, j: (0, 0)),
                  pl.BlockSpec(mod.shape, lambda i, j: (0, 0, 0, 0)),
                  pl.BlockSpec((None, D, tn), lambda i, j: (layer, 0, j)),
                  pl.BlockSpec((None, D, LANES), lambda i, j: (layer, 0, 0)),
                  pl.BlockSpec((1, LANES), lambda i, j: (0, 0))],
        out_specs=(pl.BlockSpec((tm, tn), lambda i, j: (i, j)),
                   pl.BlockSpec((tm, LANES), lambda i, j: (i, 0))),
        scratch_shapes=[pltpu.VMEM((tm, D), BF16)],
        compiler_params=_cparams(("parallel", "arbitrary")), name="inproj",
    )(X, g.reshape(1, D), mod, w_main, w_gate,
      jnp.pad(gate_b.astype(F32).reshape(1, -1), ((0, 0), (0, LANES - gate_b.shape[0]))))


def _seq_edge_flags(tm, L, Lc, NL):
    r0 = pl.program_id(0) * tm
    lat = r0 < NL
    start = jnp.where(lat, r0 % L == 0, (r0 - NL) % Lc == 0)
    end = jnp.where(lat, (r0 + tm) % L == 0, (r0 + tm - NL) % Lc == 0)
    return jnp.where(start, 0.0, 1.0), jnp.where(end, 0.0, 1.0)


def _conv3(cur_ref, prev_ref, next_ref, w_ref, b_ref, keep_prev, keep_next):
    cur = cur_ref[...].astype(F32)
    tm = cur.shape[0]
    prev_row = prev_ref[BF16_SUBLANES - 1:BF16_SUBLANES, :].astype(F32) * keep_prev
    next_row = next_ref[0:1, :].astype(F32) * keep_next
    row = lax.broadcasted_iota(jnp.int32, (tm, 1), 0)
    up = jnp.where(row == 0, prev_row, pltpu.roll(cur, 1, 0))
    dn = jnp.where(row == tm - 1, next_row, pltpu.roll(cur, tm - 1, 0))
    return b_ref[...] + up * w_ref[0:1, :] + cur * w_ref[1:2, :] + dn * w_ref[2:3, :]


def _conv_specs(tm, tc, T, col_block):
    per = tm // BF16_SUBLANES
    last = T // BF16_SUBLANES - 1
    return [pl.BlockSpec((tm, tc), lambda i, j: (i, col_block + j)),
            pl.BlockSpec((BF16_SUBLANES, tc), lambda i, j: (jnp.maximum(i * per - 1, 0), col_block + j)),
            pl.BlockSpec((BF16_SUBLANES, tc), lambda i, j: (jnp.minimum((i + 1) * per, last), col_block + j))]


def _mconv_kernel(cur_ref, prev_ref, next_ref, w_ref, b_ref, scale_ref, o_ref, *, tm, L, Lc, NL):
    kp, kn = _seq_edge_flags(tm, L, Lc, NL)
    y = _conv3(cur_ref, prev_ref, next_ref, w_ref, b_ref, kp, kn)
    o_ref[...] = (y * _sigmoid(y) * scale_ref[...]).astype(o_ref.dtype)


def _mlstm_qk(P, conv_w, conv_b, *, L, Lc, NL, width):
    T = P.shape[0]
    C = 2 * width
    tm, tc = ROW_TILE, C
    scale = jnp.concatenate([jnp.ones((1, width), F32),
                             jnp.full((1, width), (width // M_HEADS) ** -0.5, F32)], axis=1)
    vec = lambda r: pl.BlockSpec((r, tc), lambda i, j: (0, j))
    return pl.pallas_call(
        functools.partial(_mconv_kernel, tm=tm, L=L, Lc=Lc, NL=NL),
        out_shape=jax.ShapeDtypeStruct((T, C), BF16),
        grid=(T // tm, C // tc),
        in_specs=_conv_specs(tm, tc, T, 0) + [vec(SHORT_CONV), vec(1), vec(1)],
        out_specs=pl.BlockSpec((tm, tc), lambda i, j: (i, j)),
        compiler_params=_cparams(("parallel", "parallel")), name="mlstm_qk_conv",
    )(P, P, P, conv_w, conv_b.reshape(1, C), scale)


def _hyena_pre_kernel(c0, p0, n0, c1, p1, n1, c2, p2, n2, w0, w1, w2, b0, b1, b2,
                      x0_ref, z_ref, *, tm, L, Lc, NL):
    kp, kn = _seq_edge_flags(tm, L, Lc, NL)
    x0 = _conv3(c0, p0, n0, w0, b0, kp, kn)
    x1 = _conv3(c1, p1, n1, w1, b1, kp, kn)
    v = _conv3(c2, p2, n2, w2, b2, kp, kn)
    x0_ref[...] = x0
    z_ref[...] = v * x1


def _hyena_pre(P, col0, conv_w, conv_b, *, L, Lc, NL, C):
    T = P.shape[0]
    tm, tc = ROW_TILE, C
    nb = C // tc
    specs = []
    for part in range(3):
        specs += _conv_specs(tm, tc, T, col0 // tc + part * nb)
    wspecs = [pl.BlockSpec((SHORT_CONV, tc), lambda i, j, p=part: (0, p * nb + j)) for part in range(3)]
    bspecs = [pl.BlockSpec((1, tc), lambda i, j, p=part: (0, p * nb + j)) for part in range(3)]
    b2d = conv_b.reshape(1, 3 * C)
    out = pl.BlockSpec((tm, tc), lambda i, j: (i, j))
    return pl.pallas_call(
        functools.partial(_hyena_pre_kernel, tm=tm, L=L, Lc=Lc, NL=NL),
        out_shape=(jax.ShapeDtypeStruct((T, C), F32), jax.ShapeDtypeStruct((T, C), F32)),
        grid=(T // tm, nb),
        in_specs=specs + wspecs + bspecs,
        out_specs=(out, out),
        compiler_params=_cparams(("parallel", "parallel")), name="hyena_pre",
    )(*([P] * 9), conv_w, conv_w, conv_w, b2d, b2d, b2d)


def _mlstm_chunk(q, k, v, li_r, lf_r, li_c, lf_c, C_scr, n_scr, m_scr, fwd):
    Q = M_CHUNK
    row = lax.broadcasted_iota(jnp.int32, (Q, Q), 0)
    col = lax.broadcasted_iota(jnp.int32, (Q, Q), 1)
    mask = (col <= row) if fwd else (col >= row)
    tri_c = mask.astype(BF16)
    tri_r = ((row <= col) if fwd else (row >= col)).astype(BF16)
    lfc_h, lfc_l = _split(jnp.broadcast_to(lf_c, (Q, Q)))
    lfr_h, lfr_l = _split(jnp.broadcast_to(lf_r, (Q, Q)))
    b_cols = _dot(tri_c, lfc_h) + _dot(tri_c, lfc_l)
    b_rows = _dot(lfr_h, tri_r) + _dot(lfr_l, tri_r)
    dm = jnp.where(mask, b_cols - b_rows + li_r, -jnp.inf)
    m_prev = m_scr[0:1, 0:1]
    b_col = b_cols[:, 0:1]
    inter = b_col + m_prev
    mt = jnp.maximum(inter, jnp.max(dm, axis=-1, keepdims=True))
    s = lax.dot_general(q, k, (((1,), (1,)), ((), ())), preferred_element_type=F32) * jnp.exp(dm - mt)
    wi = jnp.exp(inter - mt)
    num = _dot(s.astype(BF16), v) + wi * _dot(q, C_scr[...].astype(BF16))
    qn = jnp.sum(q.astype(F32) * n_scr[...], axis=-1, keepdims=True)
    den = jnp.sum(s, axis=-1, keepdims=True) + wi * qn
    h = num / jnp.maximum(jnp.abs(den), jnp.exp(-mt))
    b_tot = b_cols[Q - 1:Q, 0:1] if fwd else b_cols[0:1, 0:1]
    ws = b_tot - b_col + li_c
    m_new = jnp.maximum(b_tot + m_prev, jnp.max(ws, axis=0, keepdims=True))
    decay = jnp.exp(b_tot + m_prev - m_new)
    kw = k.astype(F32) * jnp.exp(ws - m_new)
    C_scr[...] = decay * C_scr[...] + lax.dot_general(
        kw.astype(BF16), v, (((0,), (0,)), ((), ())), preferred_element_type=F32)
    n_scr[...] = decay * n_scr[...] + jnp.sum(kw, axis=0, keepdims=True)
    m_scr[...] = jnp.broadcast_to(m_new, m_scr.shape)
    return h


def _mlstm_kernel(*refs, fwd, SEG, Lc, H, dh):
    if fwd:
        (gr_ref, gc_ref, grc_ref, gcc_ref, q_ref, k_ref, v_ref, qc_ref, kc_ref, vc_ref,
         h_ref, hc_ref, *scr) = refs
    else:
        (gr_ref, gc_ref, grc_ref, gcc_ref, q_ref, k_ref, v_ref, qc_ref, kc_ref, vc_ref,
         o_ref, oc_ref, hf_ref, hfc_ref, ng_ref, y_ref, yc_ref, *scr) = refs
    C_scr, n_scr, m_scr = scr[0:H], scr[H:2 * H], scr[2 * H:3 * H]
    gi = 0 if fwd else 2

    def run(nchunks, qr, kr, vr, grr, gcr, emit):
        def body(j, carry):
            c = j if fwd else nchunks - 1 - j
            rows = pl.ds(pl.multiple_of(c * M_CHUNK, M_CHUNK), M_CHUNK)
            for hh in range(H):
                cols = slice(hh * dh, (hh + 1) * dh)
                gcs = gcr[hh, rows, :]
                h = _mlstm_chunk(qr[rows, cols], kr[rows, cols], vr[rows, cols],
                                 grr[hh, gi, pl.ds(c, 1), :], grr[hh, gi + 1, pl.ds(c, 1), :],
                                 gcs[:, gi:gi + 1], gcs[:, gi + 1:gi + 2],
                                 C_scr[hh], n_scr[hh], m_scr[hh], fwd)
                emit(rows, cols, h)
            return carry

        lax.fori_loop(0, nchunks, body, 0, unroll=2)

    def emitter(dst, hf=None, o=None):
        def emit(rows, cols, h):
            if not fwd:
                h = h + hf[rows, cols]
                y = h * lax.rsqrt(jnp.mean(h * h, axis=-1, keepdims=True) + EPS) * ng_ref[:, cols]
                h = (y * _sigmoid(o[rows, cols].astype(F32))).astype(dst.dtype)
            dst[rows, cols] = h
        return emit

    @pl.when(pl.program_id(1) == 0)
    def _():
        for r in scr:
            r[...] = jnp.zeros_like(r)
        run(Lc // M_CHUNK, qc_ref, kc_ref, vc_ref, grc_ref, gcc_ref,
            emitter(hc_ref) if fwd else emitter(yc_ref, hfc_ref, oc_ref))

    run(SEG // M_CHUNK, q_ref, k_ref, v_ref, gr_ref, gc_ref,
        emitter(h_ref) if fwd else emitter(y_ref, hf_ref, o_ref))


def _mlstm(QK, P, G, norm_g, *, B, L, Lc, width):
    H = M_HEADS
    dh = width // H
    NL = B * L
    SEG = min(L, 1024)
    S = L // SEG
    g = G[:, :4 * H]

    def gate_views(rows, n):
        a = rows.reshape(B, n, 4, H)
        return (a.transpose(0, 3, 2, 1).reshape(B, H, 4, n // M_CHUNK, M_CHUNK),
                a.transpose(0, 3, 1, 2))

    gr, gc = gate_views(g[:NL], L)
    grc, gcc = gate_views(g[NL:], Lc)
    scratch = ([pltpu.VMEM((dh, dh), F32)] * H + [pltpu.VMEM((1, dh), F32)] * H
               + [pltpu.VMEM((8, LANES), F32)] * H)

    def call(fwd, extra_in, extra_specs, out_dtype):
        seg_of = (lambda s: s) if fwd else (lambda s: S - 1 - s)
        lat = lambda cb: pl.BlockSpec((SEG, width), lambda b, s: (b * S + seg_of(s), cb))
        ctx = lambda cb: pl.BlockSpec((Lc, width), lambda b, s: (NL // Lc + b, cb))
        gate_specs = [
            pl.BlockSpec((None, H, 4, SEG // M_CHUNK, M_CHUNK), lambda b, s: (b, 0, 0, seg_of(s), 0)),
            pl.BlockSpec((None, H, SEG, 4), lambda b, s: (b, 0, seg_of(s), 0)),
            pl.BlockSpec((None, H, 4, Lc // M_CHUNK, M_CHUNK), lambda b, s: (b, 0, 0, 0, 0)),
            pl.BlockSpec((None, H, Lc, 4), lambda b, s: (b, 0, 0, 0))]
        lat_out = pl.BlockSpec((SEG, width), lambda b, s: (b * S + seg_of(s), 0))
        ctx_out = pl.BlockSpec((Lc, width), lambda b, s: (b, 0))
        return pl.pallas_call(
            functools.partial(_mlstm_kernel, fwd=fwd, SEG=SEG, Lc=Lc, H=H, dh=dh),
            out_shape=(jax.ShapeDtypeStruct((NL, width), out_dtype),
                       jax.ShapeDtypeStruct((B * Lc, width), out_dtype)),
            grid=(B, S),
            in_specs=(gate_specs + [lat(0), lat(1), lat(2), ctx(0), ctx(1), ctx(2)]
                      + extra_specs(lat, ctx, lat_out, ctx_out)),
            out_specs=(lat_out, ctx_out), scratch_shapes=scratch,
            compiler_params=_cparams(("parallel", "arbitrary")),
            name="mlstm_fwd" if fwd else "mlstm_bwd",
        )(gr, gc, grc, gcc, QK, QK, P, QK, QK, P, *extra_in)

    hf, hfc = call(True, [], lambda *_: [], F32)
    return call(False, [P, P, hf, hfc, jnp.tile(norm_g.astype(F32), H).reshape(1, width)],
                lambda lat, ctx, lat_out, ctx_out: [lat(3), ctx(3), lat_out, ctx_out,
                                                    pl.BlockSpec((1, width), lambda b, s: (0, 0))], BF16)


def _attn_prep_kernel(x_ref, g_ref, cos_ref, sin_ref, o_ref, *, scale, dh):
    n_blk = x_ref.shape[1] // LANES
    r = lax.broadcasted_iota(jnp.int32, (LANES, LANES), 0)
    c = lax.broadcasted_iota(jnp.int32, (LANES, LANES), 1)
    group = (r // dh == c // dh).astype(BF16)
    lane = lax.broadcasted_iota(jnp.int32, (1, LANES), 1)
    quarter = dh // 4
    first = (lane % (2 * quarter)) < quarter
    cos = cos_ref[...]
    sin = sin_ref[...]
    for hb in range(n_blk):
        cols = slice(hb * LANES, (hb + 1) * LANES)
        x = x_ref[:, cols].astype(F32)
        hi, lo = _split(x * x)
        ms = (_dot(hi, group) + _dot(lo, group)) * (1.0 / dh)
        y = x * lax.rsqrt(ms + EPS) * g_ref[:, cols]
        rot = jnp.where(first, -pltpu.roll(y, LANES - quarter, 1), pltpu.roll(y, quarter, 1))
        o_ref[:, cols] = ((y * cos + rot * sin) * scale).astype(o_ref.dtype)


def _attn_prep(P, col0, gain, cos_tab, sin_tab, scale, *, L, NL, dh):
    T = P.shape[0]
    W = A_HEADS * 2 * dh
    tm = cos_tab.shape[0] - L
    assert L % tm == 0 and (T - NL) % tm == 0
    nlt, tps = NL // tm, L // tm
    tab = pl.BlockSpec((tm, LANES), lambda i: (jnp.where(i < nlt, i % tps, tps), 0))
    g = jnp.tile(gain.astype(F32), W // dh).reshape(1, W)
    return pl.pallas_call(
        functools.partial(_attn_prep_kernel, scale=scale, dh=dh),
        out_shape=jax.ShapeDtypeStruct((T, W), BF16),
        grid=(T // tm,),
        in_specs=[pl.BlockSpec((tm, W), lambda i: (i, col0 // W)),
                  pl.BlockSpec((1, W), lambda i: (0, 0)), tab, tab],
        out_specs=pl.BlockSpec((tm, W), lambda i: (i, 0)),
        compiler_params=_cparams(("parallel",)), name="attn_prep",
    )(P, g, cos_tab, sin_tab)


def _attn_kernel(*refs, n_lat, tk, L, Lc, dh, lam_init):
    if n_lat:
        lam_ref, q_ref, kl_ref, vl_ref, kc_ref, vc_ref, sg_ref, o_ref, vext, acc = refs
    else:
        lam_ref, q_ref, kc_ref, vc_ref, sg_ref, o_ref, vext, acc = refs
    dv = 2 * dh
    ctx0 = n_lat * tk

    @pl.when(pl.program_id(2) == 0)
    def _():
        if n_lat:
            vext[0:L, 0:dv] = vl_ref[...]
        vext[ctx0:ctx0 + Lc, 0:dv] = vc_ref[...]
        vext[:, dv:2 * dv] = jnp.ones((vext.shape[0], dv), BF16)

    q = q_ref[...]
    lane = lax.broadcasted_iota(jnp.int32, (1, dv), 1)
    qs = (jnp.where(lane < dh, q, jnp.zeros_like(q)), jnp.where(lane >= dh, q, jnp.zeros_like(q)))
    acc[...] = jnp.zeros_like(acc)
    tq = q.shape[0]

    def update(comp, kblk, vblk, m_old):
        s = lax.dot_general(qs[comp], kblk, (((1,), (1,)), ((), ())), preferred_element_type=F32)
        m_new = jnp.maximum(m_old, jnp.max(s, axis=-1, keepdims=True))
        p = jnp.exp2(s - m_new).astype(BF16)
        acc[comp] = jnp.exp2(m_old - m_new) * acc[comp] + _dot(p, vblk)
        return m_new

    m = (jnp.full((tq, 1), -jnp.inf, F32),) * 2
    for c in range(n_lat):
        kblk, vblk = kl_ref[c * tk:(c + 1) * tk, :], vext[c * tk:(c + 1) * tk, :]
        m = update(0, kblk, vblk, m[0]), update(1, kblk, vblk, m[1])
    kblk, vblk = kc_ref[...], vext[ctx0:ctx0 + Lc, :]
    update(0, kblk, vblk, m[0])
    update(1, kblk, vblk, m[1])

    lp = lam_ref[...]
    lam = (jnp.exp(jnp.sum(lp[0:1] * lp[1:2], axis=-1, keepdims=True))
           - jnp.exp(jnp.sum(lp[2:3] * lp[3:4], axis=-1, keepdims=True)) + lam_init)
    a0, a1 = acc[0], acc[1]
    o = a0[:, 0:dv] / a0[:, dv:dv + 1] - lam * (a1[:, 0:dv] / a1[:, dv:dv + 1])
    y = o * lax.rsqrt(jnp.mean(o * o, axis=-1, keepdims=True) + EPS) * sg_ref[...] * (1.0 - lam_init)
    o_ref[...] = y.astype(o_ref.dtype)


def _attention(Qh, Kh, P, vcol0, lam_p, sub_g, lam_init, *, B, L, Lc, dh, latent):
    NL = B * L
    dv = 2 * dh
    H = A_HEADS
    vb = vcol0 // dv
    if latent:
        tq, tk = 512, min(L, 1024)
        n_lat, nq, rows_out = L // tk, L // tq, NL
        q_spec = pl.BlockSpec((tq, dv), lambda b, h, i: (b * nq + i, h))
        lat_specs = [pl.BlockSpec((L, dv), lambda b, h, i: (b, h)),
                     pl.BlockSpec((L, dv), lambda b, h, i: (b, vb + h))]
        lat_args = [Kh, P]
        o_spec = pl.BlockSpec((tq, dv), lambda b, h, i: (b * nq + i, h))
        nkeys = L + Lc
    else:
        tq, tk = Lc, 512
        n_lat, nq, rows_out = 0, 1, B * Lc
        q_spec = pl.BlockSpec((Lc, dv), lambda b, h, i: (NL // Lc + b, h))
        lat_specs, lat_args = [], []
        o_spec = pl.BlockSpec((Lc, dv), lambda b, h, i: (b, h))
        nkeys = Lc
    ctx_specs = [pl.BlockSpec((Lc, dv), lambda b, h, i: (NL // Lc + b, h)),
                 pl.BlockSpec((Lc, dv), lambda b, h, i: (NL // Lc + b, vb + h))]
    return pl.pallas_call(
        functools.partial(_attn_kernel, n_lat=n_lat, tk=tk, L=L, Lc=Lc, dh=dh, lam_init=lam_init),
        out_shape=jax.ShapeDtypeStruct((rows_out, H * dv), BF16),
        grid=(B, H, nq),
        in_specs=[pl.BlockSpec((4, dh), lambda b, h, i: (0, 0)), q_spec] + lat_specs + ctx_specs
                 + [pl.BlockSpec((1, dv), lambda b, h, i: (0, 0))],
        out_specs=o_spec,
        scratch_shapes=[pltpu.VMEM((nkeys, 2 * dv), BF16), pltpu.VMEM((2, tq, 2 * dv), F32)],
        compiler_params=_cparams(("parallel", "parallel", "arbitrary")),
        name="diff_attn_latent" if latent else "diff_attn_ctx",
    )(lam_p.astype(F32), Qh, *lat_args, Kh, P, sub_g.reshape(1, dv).astype(F32))


def _rope_tables(L, dh, tm):
    rows = L // GRID_W
    row = np.repeat(np.arange(rows), GRID_W).astype(np.float64)
    col = np.tile(np.arange(GRID_W), rows).astype(np.float64)
    nf = dh // 4
    inv = (np.float32(ROPE_BASE) ** (-np.arange(nf, dtype=np.float32) / nf)).astype(np.float64)
    ang = np.concatenate([row[:, None] * inv] * 2 + [col[:, None] * inv] * 2, axis=-1)
    ang = np.tile(ang.astype(np.float32).astype(np.float64), (1, LANES // dh))
    cos = np.concatenate([np.cos(ang), np.ones((tm, LANES))], axis=0)
    sin = np.concatenate([np.sin(ang), np.zeros((tm, LANES))], axis=0)
    return jnp.asarray(cos, F32), jnp.asarray(sin, F32)


def _filter_kernel(f_ref, w1a_ref, w1b_ref, b1_ref, w2_ref, b2_ref, fr_ref, w3a_ref, w3b_ref, dl_ref,
                   taps_ref, ssq_ref):
    f = f_ref[...]
    half = f.shape[0] // 2
    freq = fr_ref[...]
    h = jnp.sin(freq * (_dot3(f[:half], w1a_ref[...]) + _dot3(f[half:], w1b_ref[...]) + b1_ref[...]))
    h = jnp.sin(freq * (_dot3(h, w2_ref[...]) + b2_ref[...]))
    h = jnp.concatenate([_dot3(h, w3a_ref[...]), _dot3(h, w3b_ref[...])], axis=0)
    h = h * jnp.exp(-f[:, 0:1] * jnp.abs(dl_ref[...]))

    @pl.when(pl.program_id(1) == 0)
    def _():
        ssq_ref[...] = jnp.zeros_like(ssq_ref)

    ssq_ref[...] += jnp.sum(h * h, axis=0, keepdims=True)
    taps_ref[...] = h * f[:, H_EMB:H_EMB + 1]


def _filter_features(L):
    t = np.linspace(0.0, 1.0, L, dtype=np.float32).astype(np.float64)[:, None]
    bands = (H_EMB - 1) // 2
    w = (np.float32(2.0 * math.pi) * np.arange(L, dtype=np.float32) / np.float32(L)).astype(np.float64)[:, None]
    f = np.linspace(1e-4, bands - 1, bands, dtype=np.float32).astype(np.float64)[None, :]
    fw = (f.astype(np.float32) * w.astype(np.float32)).astype(np.float64)
    z = np.concatenate([t, np.cos(fw), -np.sin(fw)], axis=-1)
    feat = np.zeros((2 * L, LANES), np.float64)
    feat[:L, :H_EMB] = z
    idx = (L - np.arange(L)) % L
    feat[L:, :H_EMB] = z[idx]
    feat[:, H_EMB] = 1.0
    feat[L, H_EMB] = 0.0
    return jnp.asarray(feat, F32)


def _hyena_filters(L, w1, b1, w2, b2, w3, freq, delta, C):
    assert 2 * H_FFN == LANES
    pad = LANES - H_FFN
    w1a = jnp.pad(w1, ((0, LANES - H_EMB), (0, pad)))
    w1b = jnp.pad(w1, ((0, LANES - H_EMB), (pad, 0)))
    w2d = jnp.pad(w2, ((0, pad), (0, pad))) + jnp.pad(w2, ((pad, 0), (pad, 0)))
    w3a = jnp.pad(w3, ((0, pad), (0, 0)))
    w3b = jnp.pad(w3, ((pad, 0), (0, 0)))
    row = lambda a: jnp.tile(a.reshape(1, H_FFN), (1, 2))
    tr = min(L, 512)
    nr = L // tr
    const = lambda shape: pl.BlockSpec(shape, lambda hf, r: (0, 0))
    w3spec = pl.BlockSpec((LANES, C), lambda hf, r: (0, hf))
    return pl.pallas_call(
        _filter_kernel,
        out_shape=(jax.ShapeDtypeStruct((2 * L, C), F32), jax.ShapeDtypeStruct((2, 1, C), F32)),
        grid=(2, nr),
        in_specs=[pl.BlockSpec((tr, LANES), lambda hf, r: (hf * nr + r, 0)),
                  const((LANES, LANES)), const((LANES, LANES)), const((1, LANES)),
                  const((LANES, LANES)), const((1, LANES)), const((1, LANES)),
                  w3spec, w3spec,
                  pl.BlockSpec((1, C), lambda hf, r: (0, hf))],
        out_specs=(pl.BlockSpec((tr, C), lambda hf, r: (hf * nr + r, 0)),
                   pl.BlockSpec((None, 1, C), lambda hf, r: (hf, 0, 0))),
        compiler_params=_cparams(("parallel", "arbitrary")), name="hyena_filter",
    )(_filter_features(L), w1a, w1b, row(b1), w2d, row(b2), row(freq), w3a, w3b, delta.reshape(1, 2 * C))


def _cblock(m):
    return np.block([[m.real, -m.imag], [m.imag, m.real]])


def _dft_consts(L):
    N = 2 * L
    N2 = LANES
    N1 = N // N2
    half = N1 // 2
    n1 = np.arange(N1)
    n2 = np.arange(N2)
    F1 = np.exp(-2j * np.pi * np.outer(n1, n1) / N1)
    F2 = np.exp(-2j * np.pi * np.outer(n2, n2) / N2)
    a_data = _cblock(F1[:, :half])
    a_taps = np.concatenate([F1.real, F1.imag], axis=0)
    b_fwd = _cblock(F2)
    b_inv = _cblock(np.conj(F2))
    fin = _cblock(np.conj(F1)[:half, :]) / N
    ang = 2.0 * np.pi * np.outer(n2, n1) / N
    tw = dict(c_a=np.cos(ang)[:, :, None], s_a=np.sin(ang)[:, :, None],
              c_b=np.cos(ang.T)[:, :, None], s_b=np.sin(ang.T)[:, :, None])
    as_bf16 = lambda m: jnp.asarray(np.asarray(m, np.float32)).astype(BF16)
    return dict(N1=N1, a_data=as_bf16(a_data), a_taps=as_bf16(a_taps),
                b_fwd=as_bf16(b_fwd), b_inv=as_bf16(b_inv), fin=as_bf16(fin),
                tw={k: jnp.asarray(v, F32) for k, v in tw.items()})


def _time_slice_kernel(*refs, n_in, n_vmem, n_out, N1, compute):
    ins, vmem = refs[:n_in], refs[n_in:n_in + n_vmem]
    outs = refs[n_in + n_vmem:n_in + n_vmem + n_out]
    in_buf, out_buf, in_sem, out_sem = refs[n_in + n_vmem + n_out:]
    j = pl.program_id(0)
    n = pl.num_programs(0)
    slot = j % 2

    def in_copies(step, sl):
        return [pltpu.make_async_copy(src.at[pl.ds(0, N1), step, :], in_buf.at[sl, i], in_sem.at[sl])
                for i, src in enumerate(ins)]

    def out_copies(step, sl):
        return [pltpu.make_async_copy(out_buf.at[sl, i], dst.at[:, step, :], out_sem.at[sl])
                for i, dst in enumerate(outs)]

    @pl.when(j == 0)
    def _():
        for cp in in_copies(0, 0):
            cp.start()

    @pl.when(j + 1 < n)
    def _():
        for cp in in_copies(j + 1, 1 - slot):
            cp.start()

    for cp in in_copies(j, slot):
        cp.wait()

    @pl.when(j >= 2)
    def _():
        for cp in out_copies(j - 2, slot):
            cp.wait()

    for i, r in enumerate(compute([in_buf[slot, i] for i in range(n_in)], vmem)):
        out_buf[slot, i] = r
    for cp in out_copies(j, slot):
        cp.start()

    @pl.when(j == n - 1)
    def _():
        for cp in out_copies(j - 1, 1 - slot) + out_copies(j, slot):
            cp.wait()


def _time_slice_call(compute, hbm_inputs, vmem_inputs, vmem_specs, n_out, *, N1, C, name):
    assert LANES >= 2
    any_spec = pl.BlockSpec(memory_space=pl.ANY)
    shape = jax.ShapeDtypeStruct((N1, LANES, C), F32)
    n_in = len(hbm_inputs)
    return pl.pallas_call(
        functools.partial(_time_slice_kernel, n_in=n_in, n_vmem=len(vmem_inputs), n_out=n_out, N1=N1,
                          compute=compute),
        out_shape=(shape,) * n_out, grid=(LANES,),
        in_specs=[any_spec] * n_in + list(vmem_specs),
        out_specs=(any_spec,) * n_out,
        scratch_shapes=[pltpu.VMEM((2, n_in, N1, C), F32), pltpu.VMEM((2, n_out, N1, C), F32),
                        pltpu.SemaphoreType.DMA((2,)), pltpu.SemaphoreType.DMA((2,))],
        compiler_params=_cparams(("arbitrary",)), name=name,
    )(*hbm_inputs, *vmem_inputs)


def _pack_complex(re, im):
    hi = lax.bitcast_convert_type(re.astype(BF16).astype(F32), jnp.uint32)
    lo = lax.bitcast_convert_type(im.astype(BF16).astype(F32), jnp.uint32)
    return lax.bitcast_convert_type(hi | (lo >> 16), F32)


def _unpack_complex(p):
    w = lax.bitcast_convert_type(p, jnp.uint32)
    re = lax.bitcast_convert_type(w & jnp.uint32(0xFFFF0000), F32)
    im = lax.bitcast_convert_type(w << 16, F32)
    return jnp.concatenate([re, im], axis=0).astype(BF16)


def _fft_a_compute(xs, vmem):
    m_ref, c_ref, s_ref = vmem
    r = _dot(m_ref[...], xs[0].astype(BF16))
    n1 = r.shape[0] // 2
    re, im = r[:n1], r[n1:]
    c, s = c_ref[...], s_ref[...]
    return (_pack_complex(re * c + im * s, im * c - re * s),)


def _fft_stage_a(x3, mat, tw, *, N1, C):
    twspec = pl.BlockSpec((None, N1, 1), lambda j: (j, 0, 0))
    return _time_slice_call(_fft_a_compute, [x3], [mat, tw["c_a"], tw["s_a"]],
                            [pl.BlockSpec(mat.shape, lambda j: (0, 0)), twspec, twspec], 1,
                            N1=N1, C=C, name="hyena_fft_a")[0]


def _fft_mid_kernel(a_ref, t_ref, ssq, f_ref, i_ref, c_ref, s_ref, b_ref):
    N2 = LANES
    x = _dot(f_ref[...], _unpack_complex(a_ref[...]))
    h = _dot(f_ref[...], _unpack_complex(t_ref[...]))
    scale = lax.rsqrt(ssq[0] + ssq[1] + EPS)
    xr, xi, hr, hi = x[:N2], x[N2:], h[:N2] * scale, h[N2:] * scale
    y = jnp.concatenate([xr * hr - xi * hi, xr * hi + xi * hr], axis=0)
    r = _dot(i_ref[...], y.astype(BF16))
    re, im = r[:N2], r[N2:]
    c, s = c_ref[...], s_ref[...]
    b_ref[...] = _pack_complex(re * c - im * s, im * c + re * s)


def _fft_mid(A, Tp, ssq, consts, *, C, ct):
    N1 = consts["N1"]
    blk = pl.BlockSpec((None, LANES, ct), lambda k1, c: (k1, 0, c))
    const = pl.BlockSpec((2 * LANES, 2 * LANES), lambda k1, c: (0, 0))
    twspec = pl.BlockSpec((None, LANES, 1), lambda k1, c: (k1, 0, 0))
    return pl.pallas_call(
        _fft_mid_kernel, out_shape=jax.ShapeDtypeStruct((N1, LANES, C), F32), grid=(N1, C // ct),
        in_specs=[blk, blk, pl.BlockSpec((2, 1, ct), lambda k1, c: (0, 0, c)),
                  const, const, twspec, twspec],
        out_specs=blk,
        compiler_params=_cparams(("parallel", "parallel")), name="hyena_fft_mid",
    )(A, Tp, ssq, consts["b_fwd"], consts["b_inv"], consts["tw"]["c_b"], consts["tw"]["s_b"])


def _fft_fin_compute(xs, vmem):
    b, x0, z = xs
    m_ref, bias = vmem
    r = _dot(m_ref[...], _unpack_complex(b))
    return (x0 * (r + bias[...] * z),)


def _fft_final(Bp, x0_3d, z_3d, bias, consts, *, C):
    mat = consts["fin"]
    return _time_slice_call(_fft_fin_compute, [Bp, x0_3d, z_3d], [mat, bias],
                            [pl.BlockSpec(mat.shape, lambda j: (0, 0)), pl.BlockSpec((1, C), lambda j: (0, 0))],
                            1, N1=consts["N1"], C=C, name="hyena_fft_final")[0]


def _hyena_latent(x0, z, taps, ssq, bias, consts, *, B, L, C):
    assert B == 2
    N1 = consts["N1"]
    ct = C
    T = x0.shape[0]
    z3 = z.reshape(T // LANES, LANES, C)
    x03 = x0.reshape(T // LANES, LANES, C)
    A = _fft_stage_a(z3, consts["a_data"], consts["tw"], N1=N1, C=C)
    Tp = _fft_stage_a(taps.reshape(N1, LANES, C), consts["a_taps"], consts["tw"], N1=N1, C=C)
    Bp = _fft_mid(A, Tp, ssq, consts, C=C, ct=ct)
    y = _fft_final(Bp, x03, z3, bias.reshape(1, C).astype(F32), consts, C=C)
    return y.reshape(B * L, C)


def _hyena_ctx_kernel(z, x0, taps, ssq, bias, dh, dl, th, tl, ih, il, y, *, Lc):
    n = 2 * Lc
    x = _dot3c(dh[...], dl[...], z[...])
    h = _dot3c(th[...], tl[...], taps[...])
    scale = lax.rsqrt(ssq[0] + ssq[1] + EPS)
    xr, xi, hr, hi = x[:n], x[n:], h[:n] * scale, h[n:] * scale
    r = _dot3c(ih[...], il[...], jnp.concatenate([xr * hr - xi * hi, xr * hi + xi * hr], axis=0))
    y[...] = x0[...] * (r + bias[...] * z[...])


def _hyena_ctx(x0, z, taps, ssq, bias, *, B, L, Lc, C):
    assert B == 2 and (B * L) % (2 * Lc) == 0
    n = 2 * Lc
    idx = np.arange(n)
    F = np.exp(-2j * np.pi * np.outer(idx, idx) / n)
    d = _np_split(_cblock(F[:, :Lc]))
    t = _np_split(np.concatenate([F.real, F.imag], axis=0))
    inv = _np_split(_cblock(np.conj(F)[:Lc, :]) / n)
    ct = 256
    r0 = (B * L) // n
    both = pl.BlockSpec((n, ct), lambda c: (r0, c))
    const = lambda m: pl.BlockSpec(m.shape, lambda c: (0, 0))
    return pl.pallas_call(
        functools.partial(_hyena_ctx_kernel, Lc=Lc),
        out_shape=jax.ShapeDtypeStruct((n, C), F32),
        grid=(C // ct,),
        in_specs=[both, both,
                  pl.BlockSpec((n, ct), lambda c: (0, c)),
                  pl.BlockSpec((2, 1, ct), lambda c: (0, 0, c)),
                  pl.BlockSpec((1, ct), lambda c: (0, c)),
                  const(d[0]), const(d[1]), const(t[0]), const(t[1]), const(inv[0]), const(inv[1])],
        out_specs=pl.BlockSpec((n, ct), lambda c: (0, c)),
        compiler_params=_cparams(("parallel",)), name="hyena_ctx",
    )(z, x0, taps, ssq, bias.reshape(1, C).astype(F32), *d, *t, *inv)


def _merge_kernel(*refs, with_ctx, n_lat_tiles):
    if with_ctx:
        (ym, ya, yh, ymc, yac, yhc, gm, ga, gh, mb, wm, wa, wh, wo, x_ref, g1, o_ref) = refs
        is_ctx = pl.program_id(0) >= n_lat_tiles
        pick = lambda lat, ctx: jnp.where(is_ctx, ctx[...], lat[...])
        m, a, h = pick(ym, ymc), pick(ya, yac), pick(yh, yhc)
    else:
        (ym, ya, yh, gm, ga, gh, mb, wm, wa, wh, wo, x_ref, g1, o_ref) = refs
        m, a, h = ym[...], ya[...], yh[...]
    D = x_ref.shape[1]
    bias = mb[...]
    gate = lambda g, k: _sigmoid(g[...].astype(F32) + bias[:, k * D:(k + 1) * D])
    y = (gate(gm, 0) * _dot(m, wm[...]) + gate(ga, 1) * _dot(a, wa[...])
         + gate(gh, 2) * _dot(h.astype(BF16), wh[...]))
    o_ref[...] = x_ref[...] + g1[...] * _dot(y.astype(BF16), wo[...])


def _merge(X, P, gcol0, branches, merge_b, weights, layer, mod, gate_idx, *, n_rows, B, L, with_ctx):
    D = X.shape[1]
    ctx_rows = X.shape[0] - B * L
    tm = 2 * ROW_TILE if L % (2 * ROW_TILE) == 0 and ctx_rows % (2 * ROW_TILE) == 0 else ROW_TILE
    nlt, tps = (B * L) // tm, L // tm
    grp = functools.partial(_group_of_tile, n_lat_tiles=nlt, tiles_per_seq=tps, n_batch=B)
    lat = pl.BlockSpec((tm, D), lambda i: (jnp.minimum(i, nlt - 1), 0))
    ctx = pl.BlockSpec((tm, D), lambda i: (jnp.maximum(i - nlt, 0), 0))
    gb = gcol0 // D
    gspec = lambda k: pl.BlockSpec((tm, D), lambda i: (i, gb + k))
    wspec = pl.BlockSpec((None, D, D), lambda i: (layer, 0, 0))
    in_specs = ([lat] * 3 + ([ctx] * 3 if with_ctx else []) + [gspec(0), gspec(1), gspec(2)]
                + [pl.BlockSpec((1, 3 * D), lambda i: (0, 0))] + [wspec] * 4
                + [pl.BlockSpec((tm, D), lambda i: (i, 0)),
                   pl.BlockSpec((None, None, 1, D), lambda i: (grp(i), gate_idx, 0, 0))])
    return pl.pallas_call(
        functools.partial(_merge_kernel, with_ctx=with_ctx, n_lat_tiles=nlt),
        out_shape=jax.ShapeDtypeStruct((n_rows, D), F32),
        grid=(n_rows // tm,), in_specs=in_specs,
        out_specs=pl.BlockSpec((tm, D), lambda i: (i, 0)),
        compiler_params=_cparams(("parallel",)), name="merge",
    )(*branches, P, P, P, merge_b.reshape(1, 3 * D).astype(F32), *weights, X, mod)


EXPERT_BITS = 5
assert 1 << EXPERT_BITS == N_EXPERTS


def _slot_kernel(pstart, code_ref, o_ref):
    code = code_ref[...]
    expert = code & (N_EXPERTS - 1)
    slot = code >> EXPERT_BITS
    for e in range(N_EXPERTS):
        slot = slot + jnp.where(expert == e, pstart[e], 0)
    o_ref[...] = slot


def _slot_table(codes, pstart):
    A = codes.shape[0]
    assert A % (8 * LANES) == 0
    shape = (A // LANES, LANES)
    whole = pl.BlockSpec(shape, lambda i, ps: (0, 0))
    return pl.pallas_call(
        _slot_kernel,
        out_shape=jax.ShapeDtypeStruct(shape, jnp.int32),
        grid_spec=pltpu.PrefetchScalarGridSpec(num_scalar_prefetch=1, grid=(1,), in_specs=[whole], out_specs=whole),
        compiler_params=_cparams(("arbitrary",)), name="moe_slots",
    )(pstart, codes.reshape(shape)).reshape(A)


def _dispatch_kernel(slots, tok_ref, xs_zero, xs_out, sem, *, n_tok):
    del xs_zero
    R = tok_ref.shape[0]
    base = pl.program_id(0) * R

    def copy(pos, r):
        return pltpu.make_async_copy(tok_ref.at[pl.ds(r, 1), :], xs_out.at[pl.ds(slots[pos], 1), :], sem)

    for wait in (False, True):
        for k in range(TOP_K):
            for r in range(R):
                cp = copy(k * n_tok + base + r, r)
                cp.wait() if wait else cp.start()


def _moe_dispatch(tok, slots, P):
    N, D = tok.shape
    tm = MOE_TOKEN_TILE
    return pl.pallas_call(
        functools.partial(_dispatch_kernel, n_tok=N),
        out_shape=jax.ShapeDtypeStruct((P, D), F32),
        grid_spec=pltpu.PrefetchScalarGridSpec(
            num_scalar_prefetch=1, grid=(N // tm,),
            in_specs=[pl.BlockSpec((tm, D), lambda i, sl: (i, 0)),
                      pl.BlockSpec(memory_space=pl.ANY)],
            out_specs=pl.BlockSpec(memory_space=pl.ANY),
            scratch_shapes=[pltpu.SemaphoreType.DMA]),
        input_output_aliases={2: 0},
        compiler_params=_cparams(("arbitrary",)), name="moe_dispatch",
    )(slots, tok, jnp.zeros((P, D), F32))


def _moe_kernel(blk_e, xs_ref, w1_ref, w3_ref, w2_ref, y_ref, w1b, w3b, w2b):
    i = pl.program_id(0)

    @pl.when((i == 0) | (blk_e[i] != blk_e[jnp.maximum(i - 1, 0)]))
    def _():
        w1b[...] = w1_ref[...].astype(BF16)
        w3b[...] = w3_ref[...].astype(BF16)
        w2b[...] = w2_ref[...].astype(BF16)

    x = xs_ref[...].astype(BF16)
    a = _dot(x, w1b[...])
    g = (a * _sigmoid(a)) * _dot(x, w3b[...])
    y_ref[...] = _dot(g.astype(BF16), w2b[...])


def _moe_experts(xs, blk_e, w1, w3, w2, layer):
    P, D = xs.shape
    De = w1.shape[3]
    return pl.pallas_call(
        _moe_kernel,
        out_shape=jax.ShapeDtypeStruct((P, D), F32),
        grid_spec=pltpu.PrefetchScalarGridSpec(
            num_scalar_prefetch=1, grid=(P // MOE_BLOCK,),
            in_specs=[pl.BlockSpec((MOE_BLOCK, D), lambda i, be: (i, 0)),
                      pl.BlockSpec((None, None, D, De), lambda i, be: (layer, be[i], 0, 0)),
                      pl.BlockSpec((None, None, D, De), lambda i, be: (layer, be[i], 0, 0)),
                      pl.BlockSpec((None, None, De, D), lambda i, be: (layer, be[i], 0, 0))],
            out_specs=pl.BlockSpec((MOE_BLOCK, D), lambda i, be: (i, 0)),
            scratch_shapes=[pltpu.VMEM((D, De), BF16), pltpu.VMEM((D, De), BF16), pltpu.VMEM((De, D), BF16)]),
        compiler_params=_cparams(("arbitrary",)), name="moe_experts",
    )(blk_e, xs, w1, w3, w2)


def _combine_kernel(slots, yb_hbm, x_ref, g2, wt_ref, o_ref, buf, sems, *, n_tok):
    i = pl.program_id(0)
    n = pl.num_programs(0)
    slot = i % 2
    R = x_ref.shape[0]

    def gather(tile, slot, wait):
        dst = buf.at[slot]
        for k in range(TOP_K):
            for r in range(R):
                src = slots[k * n_tok + tile * R + r]
                cp = pltpu.make_async_copy(yb_hbm.at[pl.ds(src, 1), :], dst.at[k, pl.ds(r, 1), :], sems.at[slot])
                cp.wait() if wait else cp.start()

    @pl.when(i == 0)
    def _():
        gather(0, 0, False)

    @pl.when(i + 1 < n)
    def _():
        gather(i + 1, 1 - slot, False)

    gather(i, slot, True)
    w = wt_ref[...]
    acc = w[:, 0:1] * buf[slot, 0]
    for k in range(1, TOP_K):
        acc = acc + w[:, k:k + 1] * buf[slot, k]
    o_ref[...] = x_ref[...] + g2[...] * acc


def _moe_combine(yb, slots, wts, X, mod, gate_idx, *, n_rows, B, L):
    D = X.shape[1]
    tm = MOE_TOKEN_TILE
    nlt, tps = (B * L) // tm, L // tm
    grp = functools.partial(_group_of_tile, n_lat_tiles=nlt, tiles_per_seq=tps, n_batch=B)
    return pl.pallas_call(
        functools.partial(_combine_kernel, n_tok=n_rows),
        out_shape=jax.ShapeDtypeStruct((n_rows, D), F32),
        grid_spec=pltpu.PrefetchScalarGridSpec(
            num_scalar_prefetch=1, grid=(n_rows // tm,),
            in_specs=[pl.BlockSpec(memory_space=pl.ANY),
                      pl.BlockSpec((tm, D), lambda i, sl: (i, 0)),
                      pl.BlockSpec((None, None, 1, D), lambda i, sl: (grp(i), gate_idx, 0, 0)),
                      pl.BlockSpec((tm, LANES), lambda i, sl: (i, 0))],
            out_specs=pl.BlockSpec((tm, D), lambda i, sl: (i, 0)),
            scratch_shapes=[pltpu.VMEM((2, TOP_K, tm, D), F32), pltpu.SemaphoreType.DMA((2,))]),
        compiler_params=_cparams(("arbitrary",)), name="moe_combine",
    )(slots, yb, X, mod, wts)


def _dispatch_tables(code, counts, n_tok):
    E, K = N_EXPERTS, TOP_K
    cnt = counts[0, :E].astype(jnp.int32)
    padded = (cnt + MOE_BLOCK - 1) // MOE_BLOCK * MOE_BLOCK
    pend = jnp.cumsum(padded)
    pstart = (pend - padded).astype(jnp.int32)
    P = -(-(n_tok * K + E * (MOE_BLOCK - 1)) // MOE_BLOCK) * MOE_BLOCK
    blk_row = jnp.arange(P // MOE_BLOCK, dtype=jnp.int32) * MOE_BLOCK
    blk_e = jnp.minimum(jnp.sum(pend[None, :] <= blk_row[:, None], axis=1), E - 1).astype(jnp.int32)
    return _slot_table(code[:, :K].T.reshape(-1), pstart), P, blk_e


def kernel(x, c, ctx, c_ctx, mod_w, mod_b, norm1_g, norm2_g, w_in, merge_b, m_conv_w, m_conv_b, m_gate_b, m_norm_g, a_qnorm_g, a_knorm_g, a_lambda, a_subln_g, h_conv_w, h_conv_b, h_ffn_w1, h_ffn_b1, h_ffn_w2, h_ffn_b2, h_ffn_w3, h_freq, h_decay, h_bias, w_br_m, w_br_a, w_br_h, w_out, router_w, router_b, e_w1, e_w3, e_w2):
    B, L, D = x.shape
    Lc = ctx.shape[1]
    depth = mod_w.shape[0]
    NL, NC = B * L, B * Lc
    T = NL + NC
    assert B + 1 <= 8 and Lc == ROW_TILE and L % min(L, 1024) == 0 and L % 512 == 0
    assert NL % Lc == 0 and D % LANES == 0
    width = D
    a_dh = D // (2 * A_HEADS)
    n_gates = 4 * M_HEADS
    c_mq, c_mv, c_mo = 0, 2 * width, 3 * width
    c_aq, c_ak, c_av = 4 * width, 5 * width, 6 * width
    c_hx = 7 * width
    c_gp = 10 * width

    X = jnp.concatenate([x.reshape(NL, D), ctx.reshape(NC, D)], axis=0)
    c8 = jnp.zeros((8, D), F32).at[:B].set(c).at[B].set(c_ctx)
    prep_tile = 2 * ROW_TILE if L % (2 * ROW_TILE) == 0 and NC % (2 * ROW_TILE) == 0 else ROW_TILE
    cos_tab, sin_tab = _rope_tables(L, a_dh, prep_tile)
    fft_consts = _dft_consts(L)
    g0 = 4 * width
    w_main = jnp.concatenate([w_in[:, :, :g0], w_in[:, :, g0 + n_gates:]], axis=2).astype(BF16)
    w_gate = jnp.pad(w_in[:, :, g0:g0 + n_gates], ((0, 0), (0, 0), (0, LANES - n_gates))).astype(BF16)
    merge_w = [w.astype(BF16) for w in (w_br_m, w_br_a, w_br_h, w_out)]

    for l in range(depth):
        need_ctx = l < depth - 1
        lam_init = 0.8 - 0.6 * math.exp(-0.3 * l)
        mod = _modulation(c8, mod_w, mod_b, l)[:B + 1].reshape(B + 1, 6, 1, D)

        P, G = _inproj(X, norm1_g[l], mod, w_main, w_gate, m_gate_b[l], l, L=L, B=B)

        QK = _mlstm_qk(P, m_conv_w[l], m_conv_b[l], L=L, Lc=Lc, NL=NL, width=width)
        ym, ymc = _mlstm(QK, P, G, m_norm_g[l], B=B, L=L, Lc=Lc, width=width)

        Qh = _attn_prep(P, c_aq, a_qnorm_g[l], cos_tab, sin_tab, a_dh ** -0.5 * LOG2E, L=L, NL=NL, dh=a_dh)
        Kh = _attn_prep(P, c_ak, a_knorm_g[l], cos_tab, sin_tab, 1.0, L=L, NL=NL, dh=a_dh)
        attn = functools.partial(_attention, Qh, Kh, P, c_av, a_lambda[l], a_subln_g[l], lam_init,
                                 B=B, L=L, Lc=Lc, dh=a_dh)
        ya = attn(latent=True)

        x0, z = _hyena_pre(P, c_hx, h_conv_w[l], h_conv_b[l], L=L, Lc=Lc, NL=NL, C=width)
        filt = (h_ffn_w1[l], h_ffn_b1[l], h_ffn_w2[l], h_ffn_b2[l], h_ffn_w3[l], h_freq[l], h_decay[l])
        taps, ssq = _hyena_filters(L, *filt, width)
        yh = _hyena_latent(x0, z, taps, ssq, h_bias[l], fft_consts, B=B, L=L, C=width)

        if need_ctx:
            yac = attn(latent=False)
            taps_c, ssq_c = _hyena_filters(Lc, *filt, width)
            yhc = _hyena_ctx(x0, z, taps_c, ssq_c, h_bias[l], B=B, L=L, Lc=Lc, C=width)
            branches, n_rows = [ym, ya, yh, ymc, yac, yhc], T
        else:
            branches, n_rows = [ym, ya, yh], NL
        X = _merge(X, P, c_gp, branches, merge_b[l], merge_w, l, mod, 2,
                   n_rows=n_rows, B=B, L=L, with_ctx=need_ctx)

        tok, code, wts, counts = _norm_mod(X, norm2_g[l], mod, 3, 4, n_rows=n_rows, L=L, B=B,
                                           router=(router_w, router_b))
        slots, n_slots, blk_e = _dispatch_tables(code, counts, n_rows)
        xs = _moe_dispatch(tok, slots, n_slots)
        yb = _moe_experts(xs, blk_e, e_w1, e_w3, e_w2, l)
        X = _moe_combine(yb, slots, wts, X, mod, 5, n_rows=n_rows, B=B, L=L)
    return X[:NL].reshape(B, L, D)
```

```python
import functools
import math

import numpy as np
import jax
import jax.numpy as jnp
from jax import lax
from jax.experimental import pallas as pl
from jax.experimental.pallas import tpu as pltpu

F32 = jnp.float32
BF16 = jnp.bfloat16

GRID_W = 64
EPS = 1e-6
M_HEADS = 4
M_CHUNK = 128
A_HEADS = 8
ROPE_BASE = 10000.0
H_EMB = 33
H_FFN = 64
SHORT_CONV = 3
N_EXPERTS = 32
N_GROUPS = 4
TOP_K = 2
MOE_BLOCK = 256
MOE_TOKEN_TILE = 512

LANES = 128
BF16_SUBLANES = 16
V7X_VMEM_BYTES = 64 * 1024 * 1024
VMEM_LIMIT = V7X_VMEM_BYTES * 7 // 8

ROW_TILE = 256
LOG2E = 1.4426950408889634


def _cparams(sem):
    return pltpu.CompilerParams(dimension_semantics=sem, vmem_limit_bytes=VMEM_LIMIT)


def _split(x):
    hi = x.astype(BF16)
    lo = (x - hi.astype(F32)).astype(BF16)
    return hi, lo


def _dot(a, b):
    return jnp.dot(a, b, preferred_element_type=F32)


def _dot3(a, b):
    ah, al = _split(a)
    bh, bl = _split(b)
    return _dot(ah, bh) + _dot(al, bh) + _dot(ah, bl)


def _dot3c(ch, cl, x):
    xh, xl = _split(x)
    return _dot(ch, xh) + _dot(cl, xh) + _dot(ch, xl)


def _np_split(a):
    a = jnp.asarray(np.asarray(a, np.float32))
    hi = a.astype(BF16)
    lo = (a - hi.astype(F32)).astype(BF16)
    return hi, lo


def _sigmoid(x):
    return 1.0 / (1.0 + jnp.exp(-x))


def _log_sigmoid(x):
    return jnp.minimum(x, 0.0) - jnp.log(1.0 + jnp.exp(-jnp.abs(x)))


def _mod_kernel(c_ref, w_ref, b_ref, o_ref):
    c = c_ref[...]
    o_ref[...] = _dot3(c * _sigmoid(c), w_ref[...]) + b_ref[...]


def _modulation(c8, w, b, layer):
    depth, D, N = w.shape
    tn = 1536
    return pl.pallas_call(
        _mod_kernel,
        out_shape=jax.ShapeDtypeStruct((8, N), F32),
        grid=(N // tn,),
        in_specs=[pl.BlockSpec((8, D), lambda j: (0, 0)),
                  pl.BlockSpec((None, D, tn), lambda j: (layer, 0, j)),
                  pl.BlockSpec((None, 1, tn), lambda j: (layer, 0, j))],
        out_specs=pl.BlockSpec((8, tn), lambda j: (0, j)),
        compiler_params=_cparams(("parallel",)),
        name="modulation",
    )(c8, w, b.reshape(depth, 1, N))


def _norm_mod_router_kernel(x_ref, g_ref, sh_ref, sc_ref, rw_ref, rb_ref, o_ref, code_ref, wt_ref, cnt_ref, carry):
    E, G = N_EXPERTS, N_GROUPS
    gs = E // G

    @pl.when(pl.program_id(0) == 0)
    def _():
        carry[...] = jnp.zeros_like(carry)

    x = x_ref[...]
    y = x * lax.rsqrt(jnp.mean(x * x, axis=-1, keepdims=True) + EPS) * g_ref[...]
    h = y * (1.0 + sc_ref[...]) + sh_ref[...]
    o_ref[...] = h
    tm = h.shape[0]
    s = _sigmoid(_dot3(h, rw_ref[...]))
    lane = lax.broadcasted_iota(jnp.int32, (1, LANES), 1)
    lane_f = lane.astype(F32)
    sb = jnp.where(lane < E, s + rb_ref[...], -jnp.inf)
    far = float(LANES)
    best = jnp.full((tm, 1), -jnp.inf, F32)
    e1 = jnp.zeros((tm, 1), F32)
    e2 = jnp.zeros((tm, 1), F32)
    for g in range(G):
        mg = jnp.where((lane >= g * gs) & (lane < (g + 1) * gs), sb, -jnp.inf)
        m1 = jnp.max(mg, axis=-1, keepdims=True)
        i1 = jnp.min(jnp.where(mg == m1, lane_f, far), axis=-1, keepdims=True)
        mg2 = jnp.where(lane_f == i1, -jnp.inf, mg)
        m2 = jnp.max(mg2, axis=-1, keepdims=True)
        i2 = jnp.min(jnp.where(mg2 == m2, lane_f, far), axis=-1, keepdims=True)
        score = m1 + m2
        take = score > best
        best = jnp.where(take, score, best)
        e1 = jnp.where(take, i1, e1)
        e2 = jnp.where(take, i2, e2)
    oh1 = lane_f == e1
    oh2 = lane_f == e2
    s1 = jnp.sum(jnp.where(oh1, s, 0.0), axis=-1, keepdims=True)
    s2 = jnp.sum(jnp.where(oh2, s, 0.0), axis=-1, keepdims=True)
    den = s1 + s2
    r = lax.broadcasted_iota(jnp.int32, (tm, tm), 0)
    c = lax.broadcasted_iota(jnp.int32, (tm, tm), 1)
    lower = (c < r).astype(BF16)
    o1 = oh1.astype(F32)
    o2 = oh2.astype(F32)
    cum1 = _dot(lower, o1.astype(BF16))
    cum2 = _dot(lower, o2.astype(BF16))
    tot1 = jnp.sum(o1, axis=0, keepdims=True)
    base = carry[...]
    rank1 = jnp.sum(jnp.where(oh1, base + cum1, 0.0), axis=-1, keepdims=True)
    rank2 = jnp.sum(jnp.where(oh2, base + tot1 + cum2, 0.0), axis=-1, keepdims=True)
    total = base + tot1 + jnp.sum(o2, axis=0, keepdims=True)
    carry[...] = total
    cnt_ref[...] = total
    code1 = (rank1 * E + e1).astype(jnp.int32)
    code2 = (rank2 * E + e2).astype(jnp.int32)
    code_ref[...] = jnp.where(lane == 0, code1, jnp.where(lane == 1, code2, 0))
    wt_ref[...] = jnp.where(lane == 0, s1 / den, jnp.where(lane == 1, s2 / den, 0.0))


def _group_of_tile(i, n_lat_tiles, tiles_per_seq, n_batch):
    return jnp.where(i < n_lat_tiles, i // tiles_per_seq, n_batch)


def _norm_mod(x, g, mod, shift_idx, scale_idx, *, n_rows, L, B, router):
    D = x.shape[1]
    tm = 2 * ROW_TILE if L % (2 * ROW_TILE) == 0 and (n_rows - B * L) % (2 * ROW_TILE) == 0 else ROW_TILE
    nlt, tps = (B * L) // tm, L // tm
    grp = functools.partial(_group_of_tile, n_lat_tiles=nlt, tiles_per_seq=tps, n_batch=B)
    in_specs = [pl.BlockSpec((tm, D), lambda i: (i, 0)),
                pl.BlockSpec((1, D), lambda i: (0, 0)),
                pl.BlockSpec((None, None, 1, D), lambda i: (grp(i), shift_idx, 0, 0)),
                pl.BlockSpec((None, None, 1, D), lambda i: (grp(i), scale_idx, 0, 0))]
    args = [x, g.reshape(1, D), mod, mod]
    router_w, router_b = router
    E = router_w.shape[1]
    assert E == N_EXPERTS
    rw = jnp.pad(router_w, ((0, 0), (0, LANES - E)))
    rb = jnp.pad(router_b.astype(F32).reshape(1, E), ((0, 0), (0, LANES - E)))
    lanes = pl.BlockSpec((tm, LANES), lambda i: (i, 0))
    return pl.pallas_call(
        _norm_mod_router_kernel,
        out_shape=(jax.ShapeDtypeStruct((n_rows, D), F32),
                   jax.ShapeDtypeStruct((n_rows, LANES), jnp.int32),
                   jax.ShapeDtypeStruct((n_rows, LANES), F32),
                   jax.ShapeDtypeStruct((1, LANES), F32)),
        grid=(n_rows // tm,),
        in_specs=in_specs + [pl.BlockSpec((D, LANES), lambda i: (0, 0)),
                             pl.BlockSpec((1, LANES), lambda i: (0, 0))],
        out_specs=(pl.BlockSpec((tm, D), lambda i: (i, 0)), lanes, lanes,
                   pl.BlockSpec((1, LANES), lambda i: (0, 0))),
        scratch_shapes=[pltpu.VMEM((1, LANES), F32)],
        compiler_params=_cparams(("arbitrary",)), name="norm_mod_router",
    )(*args, rw, rb)


def _largest_tile(n, cap, step):
    return max(t for t in range(step, cap + 1, step) if n % t == 0)


def _inproj_kernel(x_ref, g_ref, mod_ref, w_ref, wg_ref, gb_ref, o_ref, gate_ref, h_scr, *, L, n_batch):
    tm = x_ref.shape[0]

    @pl.when(pl.program_id(1) == 0)
    def _():
        x = x_ref[...]
        y = x * lax.rsqrt(jnp.mean(x * x, axis=-1, keepdims=True) + EPS) * g_ref[...]
        row = pl.program_id(0) * tm + lax.broadcasted_iota(jnp.int32, (tm, 1), 0)
        shift, scale = mod_ref[n_batch, 0], mod_ref[n_batch, 1]
        for b in range(n_batch - 1, -1, -1):
            in_b = row < (b + 1) * L
            shift = jnp.where(in_b, mod_ref[b, 0], shift)
            scale = jnp.where(in_b, mod_ref[b, 1], scale)
        h_scr[...] = (y * (1.0 + scale) + shift).astype(h_scr.dtype)
        gates = _dot(h_scr[...], wg_ref[...]) + gb_ref[...]
        lane = lax.broadcasted_iota(jnp.int32, (1, LANES), 1)
        forget = (lane // M_HEADS) % 2 == 1
        gate_ref[...] = jnp.where(forget, _log_sigmoid(gates), gates)

    o_ref[...] = _dot(h_scr[...], w_ref[...]).astype(o_ref.dtype)


def _inproj(X, g, mod, w_main, w_gate, gate_b, layer, *, L, B):
    T, D = X.shape
    N = w_main.shape[2]
    tm, tn = _largest_tile(T, 1536, ROW_TILE), _largest_tile(N, 1792, LANES)
    return pl.pallas_call(
        functools.partial(_inproj_kernel, L=L, n_batch=B),
        out_shape=(jax.ShapeDtypeStruct((T, N), BF16), jax.ShapeDtypeStruct((T, LANES), F32)),
        grid=(T // tm, N // tn),
        in_specs=[pl.BlockSpec((tm, D), lambda i, j: (i, 0)),
                  pl.BlockSpec((1, D), lambda i, j: (0, 0)),
                  pl.BlockSpec(mod.shape, lambda i, j: (0, 0, 0, 0)),
                  pl.BlockSpec((None, D, tn), lambda i, j: (layer, 0, j)),
                  pl.BlockSpec((None, D, LANES), lambda i, j: (layer, 0, 0)),
                  pl.BlockSpec((1, LANES), lambda i, j: (0, 0))],
        out_specs=(pl.BlockSpec((tm, tn), lambda i, j: (i, j)),
                   pl.BlockSpec((tm, LANES), lambda i, j: (i, 0))),
        scratch_shapes=[pltpu.VMEM((tm, D), BF16)],
        compiler_params=_cparams(("parallel", "arbitrary")), name="inproj",
    )(X, g.reshape(1, D), mod, w_main, w_gate,
      jnp.pad(gate_b.astype(F32).reshape(1, -1), ((0, 0), (0, LANES - gate_b.shape[0]))))


def _seq_edge_flags(tm, L, Lc, NL):
    r0 = pl.program_id(0) * tm
    lat = r0 < NL
    start = jnp.where(lat, r0 % L == 0, (r0 - NL) % Lc == 0)
    end = jnp.where(lat, (r0 + tm) % L == 0, (r0 + tm - NL) % Lc == 0)
    return jnp.where(start, 0.0, 1.0), jnp.where(end, 0.0, 1.0)


def _conv3(cur_ref, prev_ref, next_ref, w_ref, b_ref, keep_prev, keep_next):
    cur = cur_ref[...].astype(F32)
    tm = cur.shape[0]
    prev_row = prev_ref[BF16_SUBLANES - 1:BF16_SUBLANES, :].astype(F32) * keep_prev
    next_row = next_ref[0:1, :].astype(F32) * keep_next
    row = lax.broadcasted_iota(jnp.int32, (tm, 1), 0)
    up = jnp.where(row == 0, prev_row, pltpu.roll(cur, 1, 0))
    dn = jnp.where(row == tm - 1, next_row, pltpu.roll(cur, tm - 1, 0))
    return b_ref[...] + up * w_ref[0:1, :] + cur * w_ref[1:2, :] + dn * w_ref[2:3, :]


def _conv_specs(tm, tc, T, col_block):
    per = tm // BF16_SUBLANES
    last = T // BF16_SUBLANES - 1
    return [pl.BlockSpec((tm, tc), lambda i, j: (i, col_block + j)),
            pl.BlockSpec((BF16_SUBLANES, tc), lambda i, j: (jnp.maximum(i * per - 1, 0), col_block + j)),
            pl.BlockSpec((BF16_SUBLANES, tc), lambda i, j: (jnp.minimum((i + 1) * per, last), col_block + j))]


def _mconv_kernel(cur_ref, prev_ref, next_ref, w_ref, b_ref, scale_ref, o_ref, *, tm, L, Lc, NL):
    kp, kn = _seq_edge_flags(tm, L, Lc, NL)
    y = _conv3(cur_ref, prev_ref, next_ref, w_ref, b_ref, kp, kn)
    o_ref[...] = (y * _sigmoid(y) * scale_ref[...]).astype(o_ref.dtype)


def _mlstm_qk(P, conv_w, conv_b, *, L, Lc, NL, width):
    T = P.shape[0]
    C = 2 * width
    tm, tc = ROW_TILE, C
    scale = jnp.concatenate([jnp.ones((1, width), F32),
                             jnp.full((1, width), (width // M_HEADS) ** -0.5, F32)], axis=1)
    vec = lambda r: pl.BlockSpec((r, tc), lambda i, j: (0, j))
    return pl.pallas_call(
        functools.partial(_mconv_kernel, tm=tm, L=L, Lc=Lc, NL=NL),
        out_shape=jax.ShapeDtypeStruct((T, C), BF16),
        grid=(T // tm, C // tc),
        in_specs=_conv_specs(tm, tc, T, 0) + [vec(SHORT_CONV), vec(1), vec(1)],
        out_specs=pl.BlockSpec((tm, tc), lambda i, j: (i, j)),
        compiler_params=_cparams(("parallel", "parallel")), name="mlstm_qk_conv",
    )(P, P, P, conv_w, conv_b.reshape(1, C), scale)


def _hyena_pre_kernel(c0, p0, n0, c1, p1, n1, c2, p2, n2, w0, w1, w2, b0, b1, b2,
                      x0_ref, z_ref, *, tm, L, Lc, NL):
    kp, kn = _seq_edge_flags(tm, L, Lc, NL)
    x0 = _conv3(c0, p0, n0, w0, b0, kp, kn)
    x1 = _conv3(c1, p1, n1, w1, b1, kp, kn)
    v = _conv3(c2, p2, n2, w2, b2, kp, kn)
    x0_ref[...] = x0
    z_ref[...] = v * x1


def _hyena_pre(P, col0, conv_w, conv_b, *, L, Lc, NL, C):
    T = P.shape[0]
    tm, tc = ROW_TILE, C
    nb = C // tc
    specs = []
    for part in range(3):
        specs += _conv_specs(tm, tc, T, col0 // tc + part * nb)
    wspecs = [pl.BlockSpec((SHORT_CONV, tc), lambda i, j, p=part: (0, p * nb + j)) for part in range(3)]
    bspecs = [pl.BlockSpec((1, tc), lambda i, j, p=part: (0, p * nb + j)) for part in range(3)]
    b2d = conv_b.reshape(1, 3 * C)
    out = pl.BlockSpec((tm, tc), lambda i, j: (i, j))
    return pl.pallas_call(
        functools.partial(_hyena_pre_kernel, tm=tm, L=L, Lc=Lc, NL=NL),
        out_shape=(jax.ShapeDtypeStruct((T, C), F32), jax.ShapeDtypeStruct((T, C), F32)),
        grid=(T // tm, nb),
        in_specs=specs + wspecs + bspecs,
        out_specs=(out, out),
        compiler_params=_cparams(("parallel", "parallel")), name="hyena_pre",
    )(*([P] * 9), conv_w, conv_w, conv_w, b2d, b2d, b2d)


def _mlstm_chunk(q, k, v, li_r, lf_r, li_c, lf_c, C_scr, n_scr, m_scr, fwd):
    Q = M_CHUNK
    row = lax.broadcasted_iota(jnp.int32, (Q, Q), 0)
    col = lax.broadcasted_iota(jnp.int32, (Q, Q), 1)
    mask = (col <= row) if fwd else (col >= row)
    tri_c = mask.astype(BF16)
    tri_r = ((row <= col) if fwd else (row >= col)).astype(BF16)
    lfc_h, lfc_l = _split(jnp.broadcast_to(lf_c, (Q, Q)))
    lfr_h, lfr_l = _split(jnp.broadcast_to(lf_r, (Q, Q)))
    b_cols = _dot(tri_c, lfc_h) + _dot(tri_c, lfc_l)
    b_rows = _dot(lfr_h, tri_r) + _dot(lfr_l, tri_r)
    dm = jnp.where(mask, b_cols - b_rows + li_r, -jnp.inf)
    m_prev = m_scr[0:1, 0:1]
    b_col = b_cols[:, 0:1]
    inter = b_col + m_prev
    mt = jnp.maximum(inter, jnp.max(dm, axis=-1, keepdims=True))
    s = lax.dot_general(q, k, (((1,), (1,)), ((), ())), preferred_element_type=F32) * jnp.exp(dm - mt)
    wi = jnp.exp(inter - mt)
    num = _dot(s.astype(BF16), v) + wi * _dot(q, C_scr[...].astype(BF16))
    qn = jnp.sum(q.astype(F32) * n_scr[...], axis=-1, keepdims=True)
    den = jnp.sum(s, axis=-1, keepdims=True) + wi * qn
    h = num / jnp.maximum(jnp.abs(den), jnp.exp(-mt))
    b_tot = b_cols[Q - 1:Q, 0:1] if fwd else b_cols[0:1, 0:1]
    ws = b_tot - b_col + li_c
    m_new = jnp.maximum(b_tot + m_prev, jnp.max(ws, axis=0, keepdims=True))
    decay = jnp.exp(b_tot + m_prev - m_new)
    kw = k.astype(F32) * jnp.exp(ws - m_new)
    C_scr[...] = decay * C_scr[...] + lax.dot_general(
        kw.astype(BF16), v, (((0,), (0,)), ((), ())), preferred_element_type=F32)
    n_scr[...] = decay * n_scr[...] + jnp.sum(kw, axis=0, keepdims=True)
    m_scr[...] = jnp.broadcast_to(m_new, m_scr.shape)
    return h


def _mlstm_kernel(*refs, fwd, SEG, Lc, H, dh):
    if fwd:
        (gr_ref, gc_ref, grc_ref, gcc_ref, q_ref, k_ref, v_ref, qc_ref, kc_ref, vc_ref,
         h_ref, hc_ref, *scr) = refs
    else:
        (gr_ref, gc_ref, grc_ref, gcc_ref, q_ref, k_ref, v_ref, qc_ref, kc_ref, vc_ref,
         o_ref, oc_ref, hf_ref, hfc_ref, ng_ref, y_ref, yc_ref, *scr) = refs
    C_scr, n_scr, m_scr = scr[0:H], scr[H:2 * H], scr[2 * H:3 * H]
    gi = 0 if fwd else 2

    def run(nchunks, qr, kr, vr, grr, gcr, emit):
        def body(j, carry):
            c = j if fwd else nchunks - 1 - j
            rows = pl.ds(pl.multiple_of(c * M_CHUNK, M_CHUNK), M_CHUNK)
            for hh in range(H):
                cols = slice(hh * dh, (hh + 1) * dh)
                gcs = gcr[hh, rows, :]
                h = _mlstm_chunk(qr[rows, cols], kr[rows, cols], vr[rows, cols],
                                 grr[hh, gi, pl.ds(c, 1), :], grr[hh, gi + 1, pl.ds(c, 1), :],
                                 gcs[:, gi:gi + 1], gcs[:, gi + 1:gi + 2],
                                 C_scr[hh], n_scr[hh], m_scr[hh], fwd)
                emit(rows, cols, h)
            return carry

        lax.fori_loop(0, nchunks, body, 0, unroll=2)

    def emitter(dst, hf=None, o=None):
        def emit(rows, cols, h):
            if not fwd:
                h = h + hf[rows, cols]
                y = h * lax.rsqrt(jnp.mean(h * h, axis=-1, keepdims=True) + EPS) * ng_ref[:, cols]
                h = (y * _sigmoid(o[rows, cols].astype(F32))).astype(dst.dtype)
            dst[rows, cols] = h
        return emit

    @pl.when(pl.program_id(1) == 0)
    def _():
        for r in scr:
            r[...] = jnp.zeros_like(r)
        run(Lc // M_CHUNK, qc_ref, kc_ref, vc_ref, grc_ref, gcc_ref,
            emitter(hc_ref) if fwd else emitter(yc_ref, hfc_ref, oc_ref))

    run(SEG // M_CHUNK, q_ref, k_ref, v_ref, gr_ref, gc_ref,
        emitter(h_ref) if fwd else emitter(y_ref, hf_ref, o_ref))


def _mlstm(QK, P, G, norm_g, *, B, L, Lc, width):
    H = M_HEADS
    dh = width // H
    NL = B * L
    SEG = min(L, 1024)
    S = L // SEG
    g = G[:, :4 * H]

    def gate_views(rows, n):
        a = rows.reshape(B, n, 4, H)
        return (a.transpose(0, 3, 2, 1).reshape(B, H, 4, n // M_CHUNK, M_CHUNK),
                a.transpose(0, 3, 1, 2))

    gr, gc = gate_views(g[:NL], L)
    grc, gcc = gate_views(g[NL:], Lc)
    scratch = ([pltpu.VMEM((dh, dh), F32)] * H + [pltpu.VMEM((1, dh), F32)] * H
               + [pltpu.VMEM((8, LANES), F32)] * H)

    def call(fwd, extra_in, extra_specs, out_dtype):
        seg_of = (lambda s: s) if fwd else (lambda s: S - 1 - s)
        lat = lambda cb: pl.BlockSpec((SEG, width), lambda b, s: (b * S + seg_of(s), cb))
        ctx = lambda cb: pl.BlockSpec((Lc, width), lambda b, s: (NL // Lc + b, cb))
        gate_specs = [
            pl.BlockSpec((None, H, 4, SEG // M_CHUNK, M_CHUNK), lambda b, s: (b, 0, 0, seg_of(s), 0)),
            pl.BlockSpec((None, H, SEG, 4), lambda b, s: (b, 0, seg_of(s), 0)),
            pl.BlockSpec((None, H, 4, Lc // M_CHUNK, M_CHUNK), lambda b, s: (b, 0, 0, 0, 0)),
            pl.BlockSpec((None, H, Lc, 4), lambda b, s: (b, 0, 0, 0))]
        lat_out = pl.BlockSpec((SEG, width), lambda b, s: (b * S + seg_of(s), 0))
        ctx_out = pl.BlockSpec((Lc, width), lambda b, s: (b, 0))
        return pl.pallas_call(
            functools.partial(_mlstm_kernel, fwd=fwd, SEG=SEG, Lc=Lc, H=H, dh=dh),
            out_shape=(jax.ShapeDtypeStruct((NL, width), out_dtype),
                       jax.ShapeDtypeStruct((B * Lc, width), out_dtype)),
            grid=(B, S),
            in_specs=(gate_specs + [lat(0), lat(1), lat(2), ctx(0), ctx(1), ctx(2)]
                      + extra_specs(lat, ctx, lat_out, ctx_out)),
            out_specs=(lat_out, ctx_out), scratch_shapes=scratch,
            compiler_params=_cparams(("parallel", "arbitrary")),
            name="mlstm_fwd" if fwd else "mlstm_bwd",
        )(gr, gc, grc, gcc, QK, QK, P, QK, QK, P, *extra_in)

    hf, hfc = call(True, [], lambda *_: [], F32)
    return call(False, [P, P, hf, hfc, jnp.tile(norm_g.astype(F32), H).reshape(1, width)],
                lambda lat, ctx, lat_out, ctx_out: [lat(3), ctx(3), lat_out, ctx_out,
                                                    pl.BlockSpec((1, width), lambda b, s: (0, 0))], BF16)


def _attn_prep_kernel(x_ref, g_ref, cos_ref, sin_ref, o_ref, *, scale, dh):
    n_blk = x_ref.shape[1] // LANES
    r = lax.broadcasted_iota(jnp.int32, (LANES, LANES), 0)
    c = lax.broadcasted_iota(jnp.int32, (LANES, LANES), 1)
    group = (r // dh == c // dh).astype(BF16)
    lane = lax.broadcasted_iota(jnp.int32, (1, LANES), 1)
    quarter = dh // 4
    first = (lane % (2 * quarter)) < quarter
    cos = cos_ref[...]
    sin = sin_ref[...]
    for hb in range(n_blk):
        cols = slice(hb * LANES, (hb + 1) * LANES)
        x = x_ref[:, cols].astype(F32)
        hi, lo = _split(x * x)
        ms = (_dot(hi, group) + _dot(lo, group)) * (1.0 / dh)
        y = x * lax.rsqrt(ms + EPS) * g_ref[:, cols]
        rot = jnp.where(first, -pltpu.roll(y, LANES - quarter, 1), pltpu.roll(y, quarter, 1))
        o_ref[:, cols] = ((y * cos + rot * sin) * scale).astype(o_ref.dtype)


def _attn_prep(P, col0, gain, cos_tab, sin_tab, scale, *, L, NL, dh):
    T = P.shape[0]
    W = A_HEADS * 2 * dh
    tm = cos_tab.shape[0] - L
    assert L % tm == 0 and (T - NL) % tm == 0
    nlt, tps = NL // tm, L // tm
    tab = pl.BlockSpec((tm, LANES), lambda i: (jnp.where(i < nlt, i % tps, tps), 0))
    g = jnp.tile(gain.astype(F32), W // dh).reshape(1, W)
    return pl.pallas_call(
        functools.partial(_attn_prep_kernel, scale=scale, dh=dh),
        out_shape=jax.ShapeDtypeStruct((T, W), BF16),
        grid=(T // tm,),
        in_specs=[pl.BlockSpec((tm, W), lambda i: (i, col0 // W)),
                  pl.BlockSpec((1, W), lambda i: (0, 0)), tab, tab],
        out_specs=pl.BlockSpec((tm, W), lambda i: (i, 0)),
        compiler_params=_cparams(("parallel",)), name="attn_prep",
    )(P, g, cos_tab, sin_tab)


def _attn_kernel(*refs, n_lat, tk, L, Lc, dh, lam_init):
    if n_lat:
        lam_ref, q_ref, kl_ref, vl_ref, kc_ref, vc_ref, sg_ref, o_ref, vext, acc = refs
    else:
        lam_ref, q_ref, kc_ref, vc_ref, sg_ref, o_ref, vext, acc = refs
    dv = 2 * dh
    ctx0 = n_lat * tk

    @pl.when(pl.program_id(2) == 0)
    def _():
        if n_lat:
            vext[0:L, 0:dv] = vl_ref[...]
        vext[ctx0:ctx0 + Lc, 0:dv] = vc_ref[...]
        vext[:, dv:2 * dv] = jnp.ones((vext.shape[0], dv), BF16)

    q = q_ref[...]
    lane = lax.broadcasted_iota(jnp.int32, (1, dv), 1)
    qs = (jnp.where(lane < dh, q, jnp.zeros_like(q)), jnp.where(lane >= dh, q, jnp.zeros_like(q)))
    acc[...] = jnp.zeros_like(acc)
    tq = q.shape[0]

    def update(comp, kblk, vblk, m_old):
        s = lax.dot_general(qs[comp], kblk, (((1,), (1,)), ((), ())), preferred_element_type=F32)
        m_new = jnp.maximum(m_old, jnp.max(s, axis=-1, keepdims=True))
        p = jnp.exp2(s - m_new).astype(BF16)
        acc[comp] = jnp.exp2(m_old - m_new) * acc[comp] + _dot(p, vblk)
        return m_new

    m = (jnp.full((tq, 1), -jnp.inf, F32),) * 2
    for c in range(n_lat):
        kblk, vblk = kl_ref[c * tk:(c + 1) * tk, :], vext[c * tk:(c + 1) * tk, :]
        m = update(0, kblk, vblk, m[0]), update(1, kblk, vblk, m[1])
    kblk, vblk = kc_ref[...], vext[ctx0:ctx0 + Lc, :]
    update(0, kblk, vblk, m[0])
    update(1, kblk, vblk, m[1])

    lp = lam_ref[...]
    lam = (jnp.exp(jnp.sum(lp[0:1] * lp[1:2], axis=-1, keepdims=True))
           - jnp.exp(jnp.sum(lp[2:3] * lp[3:4], axis=-1, keepdims=True)) + lam_init)
    a0, a1 = acc[0], acc[1]
    o = a0[:, 0:dv] / a0[:, dv:dv + 1] - lam * (a1[:, 0:dv] / a1[:, dv:dv + 1])
    y = o * lax.rsqrt(jnp.mean(o * o, axis=-1, keepdims=True) + EPS) * sg_ref[...] * (1.0 - lam_init)
    o_ref[...] = y.astype(o_ref.dtype)


def _attention(Qh, Kh, P, vcol0, lam_p, sub_g, lam_init, *, B, L, Lc, dh, latent):
    NL = B * L
    dv = 2 * dh
    H = A_HEADS
    vb = vcol0 // dv
    if latent:
        tq, tk = 512, min(L, 1024)
        n_lat, nq, rows_out = L // tk, L // tq, NL
        q_spec = pl.BlockSpec((tq, dv), lambda b, h, i: (b * nq + i, h))
        lat_specs = [pl.BlockSpec((L, dv), lambda b, h, i: (b, h)),
                     pl.BlockSpec((L, dv), lambda b, h, i: (b, vb + h))]
        lat_args = [Kh, P]
        o_spec = pl.BlockSpec((tq, dv), lambda b, h, i: (b * nq + i, h))
        nkeys = L + Lc
    else:
        tq, tk = Lc, 512
        n_lat, nq, rows_out = 0, 1, B * Lc
        q_spec = pl.BlockSpec((Lc, dv), lambda b, h, i: (NL // Lc + b, h))
        lat_specs, lat_args = [], []
        o_spec = pl.BlockSpec((Lc, dv), lambda b, h, i: (b, h))
        nkeys = Lc
    ctx_specs = [pl.BlockSpec((Lc, dv), lambda b, h, i: (NL // Lc + b, h)),
                 pl.BlockSpec((Lc, dv), lambda b, h, i: (NL // Lc + b, vb + h))]
    return pl.pallas_call(
        functools.partial(_attn_kernel, n_lat=n_lat, tk=tk, L=L, Lc=Lc, dh=dh, lam_init=lam_init),
        out_shape=jax.ShapeDtypeStruct((rows_out, H * dv), BF16),
        grid=(B, H, nq),
        in_specs=[pl.BlockSpec((4, dh), lambda b, h, i: (0, 0)), q_spec] + lat_specs + ctx_specs
                 + [pl.BlockSpec((1, dv), lambda b, h, i: (0, 0))],
        out_specs=o_spec,
        scratch_shapes=[pltpu.VMEM((nkeys, 2 * dv), BF16), pltpu.VMEM((2, tq, 2 * dv), F32)],
        compiler_params=_cparams(("parallel", "parallel", "arbitrary")),
        name="diff_attn_latent" if latent else "diff_attn_ctx",
    )(lam_p.astype(F32), Qh, *lat_args, Kh, P, sub_g.reshape(1, dv).astype(F32))


def _rope_tables(L, dh, tm):
    rows = L // GRID_W
    row = np.repeat(np.arange(rows), GRID_W).astype(np.float64)
    col = np.tile(np.arange(GRID_W), rows).astype(np.float64)
    nf = dh // 4
    inv = (np.float32(ROPE_BASE) ** (-np.arange(nf, dtype=np.float32) / nf)).astype(np.float64)
    ang = np.concatenate([row[:, None] * inv] * 2 + [col[:, None] * inv] * 2, axis=-1)
    ang = np.tile(ang.astype(np.float32).astype(np.float64), (1, LANES // dh))
    cos = np.concatenate([np.cos(ang), np.ones((tm, LANES))], axis=0)
    sin = np.concatenate([np.sin(ang), np.zeros((tm, LANES))], axis=0)
    return jnp.asarray(cos, F32), jnp.asarray(sin, F32)


def _filter_kernel(f_ref, w1a_ref, w1b_ref, b1_ref, w2_ref, b2_ref, fr_ref, w3a_ref, w3b_ref, dl_ref,
                   taps_ref, ssq_ref):
    f = f_ref[...]
    half = f.shape[0] // 2
    freq = fr_ref[...]
    h = jnp.sin(freq * (_dot3(f[:half], w1a_ref[...]) + _dot3(f[half:], w1b_ref[...]) + b1_ref[...]))
    h = jnp.sin(freq * (_dot3(h, w2_ref[...]) + b2_ref[...]))
    h = jnp.concatenate([_dot3(h, w3a_ref[...]), _dot3(h, w3b_ref[...])], axis=0)
    h = h * jnp.exp(-f[:, 0:1] * jnp.abs(dl_ref[...]))

    @pl.when(pl.program_id(1) == 0)
    def _():
        ssq_ref[...] = jnp.zeros_like(ssq_ref)

    ssq_ref[...] += jnp.sum(h * h, axis=0, keepdims=True)
    taps_ref[...] = h * f[:, H_EMB:H_EMB + 1]


def _filter_features(L):
    t = np.linspace(0.0, 1.0, L, dtype=np.float32).astype(np.float64)[:, None]
    bands = (H_EMB - 1) // 2
    w = (np.float32(2.0 * math.pi) * np.arange(L, dtype=np.float32) / np.float32(L)).astype(np.float64)[:, None]
    f = np.linspace(1e-4, bands - 1, bands, dtype=np.float32).astype(np.float64)[None, :]
    fw = (f.astype(np.float32) * w.astype(np.float32)).astype(np.float64)
    z = np.concatenate([t, np.cos(fw), -np.sin(fw)], axis=-1)
    feat = np.zeros((2 * L, LANES), np.float64)
    feat[:L, :H_EMB] = z
    idx = (L - np.arange(L)) % L
    feat[L:, :H_EMB] = z[idx]
    feat[:, H_EMB] = 1.0
    feat[L, H_EMB] = 0.0
    return jnp.asarray(feat, F32)


def _hyena_filters(L, w1, b1, w2, b2, w3, freq, delta, C):
    assert 2 * H_FFN == LANES
    pad = LANES - H_FFN
    w1a = jnp.pad(w1, ((0, LANES - H_EMB), (0, pad)))
    w1b = jnp.pad(w1, ((0, LANES - H_EMB), (pad, 0)))
    w2d = jnp.pad(w2, ((0, pad), (0, pad))) + jnp.pad(w2, ((pad, 0), (pad, 0)))
    w3a = jnp.pad(w3, ((0, pad), (0, 0)))
    w3b = jnp.pad(w3, ((pad, 0), (0, 0)))
    row = lambda a: jnp.tile(a.reshape(1, H_FFN), (1, 2))
    tr = min(L, 1024)
    nr = L // tr
    const = lambda shape: pl.BlockSpec(shape, lambda hf, r: (0, 0))
    w3spec = pl.BlockSpec((LANES, C), lambda hf, r: (0, hf))
    return pl.pallas_call(
        _filter_kernel,
        out_shape=(jax.ShapeDtypeStruct((2 * L, C), F32), jax.ShapeDtypeStruct((2, 1, C), F32)),
        grid=(2, nr),
        in_specs=[pl.BlockSpec((tr, LANES), lambda hf, r: (hf * nr + r, 0)),
                  const((LANES, LANES)), const((LANES, LANES)), const((1, LANES)),
                  const((LANES, LANES)), const((1, LANES)), const((1, LANES)),
                  w3spec, w3spec,
                  pl.BlockSpec((1, C), lambda hf, r: (0, hf))],
        out_specs=(pl.BlockSpec((tr, C), lambda hf, r: (hf * nr + r, 0)),
                   pl.BlockSpec((None, 1, C), lambda hf, r: (hf, 0, 0))),
        compiler_params=_cparams(("parallel", "arbitrary")), name="hyena_filter",
    )(_filter_features(L), w1a, w1b, row(b1), w2d, row(b2), row(freq), w3a, w3b, delta.reshape(1, 2 * C))


def _cblock(m):
    return np.block([[m.real, -m.imag], [m.imag, m.real]])


def _dft_consts(L):
    N = 2 * L
    N2 = LANES
    N1 = N // N2
    half = N1 // 2
    n1 = np.arange(N1)
    n2 = np.arange(N2)
    F1 = np.exp(-2j * np.pi * np.outer(n1, n1) / N1)
    F2 = np.exp(-2j * np.pi * np.outer(n2, n2) / N2)
    a_data = _cblock(F1[:, :half])
    a_taps = np.concatenate([F1.real, F1.imag], axis=0)
    b_fwd = _cblock(F2)
    b_inv = _cblock(np.conj(F2))
    fin = _cblock(np.conj(F1)[:half, :]) / N
    ang = 2.0 * np.pi * np.outer(n2, n1) / N
    tw = dict(c_a=np.cos(ang)[:, :, None], s_a=np.sin(ang)[:, :, None],
              c_b=np.cos(ang.T)[:, :, None], s_b=np.sin(ang.T)[:, :, None])
    as_bf16 = lambda m: jnp.asarray(np.asarray(m, np.float32)).astype(BF16)
    return dict(N1=N1, a_data=as_bf16(a_data), a_taps=as_bf16(a_taps),
                b_fwd=as_bf16(b_fwd), b_inv=as_bf16(b_inv), fin=as_bf16(fin),
                tw={k: jnp.asarray(v, F32) for k, v in tw.items()})


def _time_slice_kernel(*refs, n_in, n_vmem, n_out, N1, compute):
    ins, vmem = refs[:n_in], refs[n_in:n_in + n_vmem]
    outs = refs[n_in + n_vmem:n_in + n_vmem + n_out]
    in_buf, out_buf, in_sem, out_sem = refs[n_in + n_vmem + n_out:]
    j = pl.program_id(0)
    n = pl.num_programs(0)
    slot = j % 2

    def in_copies(step, sl):
        return [pltpu.make_async_copy(src.at[pl.ds(0, N1), step, :], in_buf.at[sl, i], in_sem.at[sl])
                for i, src in enumerate(ins)]

    def out_copies(step, sl):
        return [pltpu.make_async_copy(out_buf.at[sl, i], dst.at[:, step, :], out_sem.at[sl])
                for i, dst in enumerate(outs)]

    @pl.when(j == 0)
    def _():
        for cp in in_copies(0, 0):
            cp.start()

    @pl.when(j + 1 < n)
    def _():
        for cp in in_copies(j + 1, 1 - slot):
            cp.start()

    for cp in in_copies(j, slot):
        cp.wait()

    @pl.when(j >= 2)
    def _():
        for cp in out_copies(j - 2, slot):
            cp.wait()

    for i, r in enumerate(compute([in_buf[slot, i] for i in range(n_in)], vmem)):
        out_buf[slot, i] = r
    for cp in out_copies(j, slot):
        cp.start()

    @pl.when(j == n - 1)
    def _():
        for cp in out_copies(j - 1, 1 - slot) + out_copies(j, slot):
            cp.wait()


def _time_slice_call(compute, hbm_inputs, vmem_inputs, vmem_specs, n_out, *, N1, C, name):
    assert LANES >= 2
    any_spec = pl.BlockSpec(memory_space=pl.ANY)
    shape = jax.ShapeDtypeStruct((N1, LANES, C), F32)
    n_in = len(hbm_inputs)
    return pl.pallas_call(
        functools.partial(_time_slice_kernel, n_in=n_in, n_vmem=len(vmem_inputs), n_out=n_out, N1=N1,
                          compute=compute),
        out_shape=(shape,) * n_out, grid=(LANES,),
        in_specs=[any_spec] * n_in + list(vmem_specs),
        out_specs=(any_spec,) * n_out,
        scratch_shapes=[pltpu.VMEM((2, n_in, N1, C), F32), pltpu.VMEM((2, n_out, N1, C), F32),
                        pltpu.SemaphoreType.DMA((2,)), pltpu.SemaphoreType.DMA((2,))],
        compiler_params=_cparams(("arbitrary",)), name=name,
    )(*hbm_inputs, *vmem_inputs)


def _pack_complex(re, im):
    hi = lax.bitcast_convert_type(re.astype(BF16).astype(F32), jnp.uint32)
    lo = lax.bitcast_convert_type(im.astype(BF16).astype(F32), jnp.uint32)
    return lax.bitcast_convert_type(hi | (lo >> 16), F32)


def _unpack_complex(p):
    w = lax.bitcast_convert_type(p, jnp.uint32)
    re = lax.bitcast_convert_type(w & jnp.uint32(0xFFFF0000), F32)
    im = lax.bitcast_convert_type(w << 16, F32)
    return jnp.concatenate([re, im], axis=0).astype(BF16)


def _fft_a_compute(xs, vmem):
    m_ref, c_ref, s_ref = vmem
    r = _dot(m_ref[...], xs[0].astype(BF16))
    n1 = r.shape[0] // 2
    re, im = r[:n1], r[n1:]
    c, s = c_ref[...], s_ref[...]
    return (_pack_complex(re * c + im * s, im * c - re * s),)


def _fft_stage_a(x3, mat, tw, *, N1, C):
    twspec = pl.BlockSpec((None, N1, 1), lambda j: (j, 0, 0))
    return _time_slice_call(_fft_a_compute, [x3], [mat, tw["c_a"], tw["s_a"]],
                            [pl.BlockSpec(mat.shape, lambda j: (0, 0)), twspec, twspec], 1,
                            N1=N1, C=C, name="hyena_fft_a")[0]


def _fft_mid_kernel(a_ref, t_ref, ssq, f_ref, i_ref, c_ref, s_ref, b_ref):
    N2 = LANES
    x = _dot(f_ref[...], _unpack_complex(a_ref[...]))
    h = _dot(f_ref[...], _unpack_complex(t_ref[...]))
    scale = lax.rsqrt(ssq[0] + ssq[1] + EPS)
    xr, xi, hr, hi = x[:N2], x[N2:], h[:N2] * scale, h[N2:] * scale
    y = jnp.concatenate([xr * hr - xi * hi, xr * hi + xi * hr], axis=0)
    r = _dot(i_ref[...], y.astype(BF16))
    re, im = r[:N2], r[N2:]
    c, s = c_ref[...], s_ref[...]
    b_ref[...] = _pack_complex(re * c - im * s, im * c + re * s)


def _fft_mid(A, Tp, ssq, consts, *, C, ct):
    N1 = consts["N1"]
    blk = pl.BlockSpec((None, LANES, ct), lambda k1, c: (k1, 0, c))
    const = pl.BlockSpec((2 * LANES, 2 * LANES), lambda k1, c: (0, 0))
    twspec = pl.BlockSpec((None, LANES, 1), lambda k1, c: (k1, 0, 0))
    return pl.pallas_call(
        _fft_mid_kernel, out_shape=jax.ShapeDtypeStruct((N1, LANES, C), F32), grid=(N1, C // ct),
        in_specs=[blk, blk, pl.BlockSpec((2, 1, ct), lambda k1, c: (0, 0, c)),
                  const, const, twspec, twspec],
        out_specs=blk,
        compiler_params=_cparams(("parallel", "parallel")), name="hyena_fft_mid",
    )(A, Tp, ssq, consts["b_fwd"], consts["b_inv"], consts["tw"]["c_b"], consts["tw"]["s_b"])


def _fft_fin_compute(xs, vmem):
    b, x0, z = xs
    m_ref, bias = vmem
    r = _dot(m_ref[...], _unpack_complex(b))
    return (x0 * (r + bias[...] * z),)


def _fft_final(Bp, x0_3d, z_3d, bias, consts, *, C):
    mat = consts["fin"]
    return _time_slice_call(_fft_fin_compute, [Bp, x0_3d, z_3d], [mat, bias],
                            [pl.BlockSpec(mat.shape, lambda j: (0, 0)), pl.BlockSpec((1, C), lambda j: (0, 0))],
                            1, N1=consts["N1"], C=C, name="hyena_fft_final")[0]


def _hyena_latent(x0, z, taps, ssq, bias, consts, *, B, L, C):
    assert B == 2
    N1 = consts["N1"]
    ct = C
    T = x0.shape[0]
    z3 = z.reshape(T // LANES, LANES, C)
    x03 = x0.reshape(T // LANES, LANES, C)
    A = _fft_stage_a(z3, consts["a_data"], consts["tw"], N1=N1, C=C)
    Tp = _fft_stage_a(taps.reshape(N1, LANES, C), consts["a_taps"], consts["tw"], N1=N1, C=C)
    Bp = _fft_mid(A, Tp, ssq, consts, C=C, ct=ct)
    y = _fft_final(Bp, x03, z3, bias.reshape(1, C).astype(F32), consts, C=C)
    return y.reshape(B * L, C)


def _hyena_ctx_kernel(z, x0, taps, ssq, bias, dh, dl, th, tl, ih, il, y, *, Lc):
    n = 2 * Lc
    x = _dot3c(dh[...], dl[...], z[...])
    h = _dot3c(th[...], tl[...], taps[...])
    scale = lax.rsqrt(ssq[0] + ssq[1] + EPS)
    xr, xi, hr, hi = x[:n], x[n:], h[:n] * scale, h[n:] * scale
    r = _dot3c(ih[...], il[...], jnp.concatenate([xr * hr - xi * hi, xr * hi + xi * hr], axis=0))
    y[...] = x0[...] * (r + bias[...] * z[...])


def _hyena_ctx(x0, z, taps, ssq, bias, *, B, L, Lc, C):
    assert B == 2 and (B * L) % (2 * Lc) == 0
    n = 2 * Lc
    idx = np.arange(n)
    F = np.exp(-2j * np.pi * np.outer(idx, idx) / n)
    d = _np_split(_cblock(F[:, :Lc]))
    t = _np_split(np.concatenate([F.real, F.imag], axis=0))
    inv = _np_split(_cblock(np.conj(F)[:Lc, :]) / n)
    ct = 256
    r0 = (B * L) // n
    both = pl.BlockSpec((n, ct), lambda c: (r0, c))
    const = lambda m: pl.BlockSpec(m.shape, lambda c: (0, 0))
    return pl.pallas_call(
        functools.partial(_hyena_ctx_kernel, Lc=Lc),
        out_shape=jax.ShapeDtypeStruct((n, C), F32),
        grid=(C // ct,),
        in_specs=[both, both,
                  pl.BlockSpec((n, ct), lambda c: (0, c)),
                  pl.BlockSpec((2, 1, ct), lambda c: (0, 0, c)),
                  pl.BlockSpec((1, ct), lambda c: (0, c)),
                  const(d[0]), const(d[1]), const(t[0]), const(t[1]), const(inv[0]), const(inv[1])],
        out_specs=pl.BlockSpec((n, ct), lambda c: (0, c)),
        compiler_params=_cparams(("parallel",)), name="hyena_ctx",
    )(z, x0, taps, ssq, bias.reshape(1, C).astype(F32), *d, *t, *inv)


def _merge_kernel(*refs, with_ctx, n_lat_tiles):
    if with_ctx:
        (ym, ya, yh, ymc, yac, yhc, gm, ga, gh, mb, wm, wa, wh, wo, x_ref, g1, o_ref) = refs
        is_ctx = pl.program_id(0) >= n_lat_tiles
        pick = lambda lat, ctx: jnp.where(is_ctx, ctx[...], lat[...])
        m, a, h = pick(ym, ymc), pick(ya, yac), pick(yh, yhc)
    else:
        (ym, ya, yh, gm, ga, gh, mb, wm, wa, wh, wo, x_ref, g1, o_ref) = refs
        m, a, h = ym[...], ya[...], yh[...]
    D = x_ref.shape[1]
    bias = mb[...]
    gate = lambda g, k: _sigmoid(g[...].astype(F32) + bias[:, k * D:(k + 1) * D])
    y = (gate(gm, 0) * _dot(m, wm[...]) + gate(ga, 1) * _dot(a, wa[...])
         + gate(gh, 2) * _dot(h.astype(BF16), wh[...]))
    o_ref[...] = x_ref[...] + g1[...] * _dot(y.astype(BF16), wo[...])


def _merge(X, P, gcol0, branches, merge_b, weights, layer, mod, gate_idx, *, n_rows, B, L, with_ctx):
    D = X.shape[1]
    ctx_rows = X.shape[0] - B * L
    tm = 2 * ROW_TILE if L % (2 * ROW_TILE) == 0 and ctx_rows % (2 * ROW_TILE) == 0 else ROW_TILE
    nlt, tps = (B * L) // tm, L // tm
    grp = functools.partial(_group_of_tile, n_lat_tiles=nlt, tiles_per_seq=tps, n_batch=B)
    lat = pl.BlockSpec((tm, D), lambda i: (jnp.minimum(i, nlt - 1), 0))
    ctx = pl.BlockSpec((tm, D), lambda i: (jnp.maximum(i - nlt, 0), 0))
    gb = gcol0 // D
    gspec = lambda k: pl.BlockSpec((tm, D), lambda i: (i, gb + k))
    wspec = pl.BlockSpec((None, D, D), lambda i: (layer, 0, 0))
    in_specs = ([lat] * 3 + ([ctx] * 3 if with_ctx else []) + [gspec(0), gspec(1), gspec(2)]
                + [pl.BlockSpec((1, 3 * D), lambda i: (0, 0))] + [wspec] * 4
                + [pl.BlockSpec((tm, D), lambda i: (i, 0)),
                   pl.BlockSpec((None, None, 1, D), lambda i: (grp(i), gate_idx, 0, 0))])
    return pl.pallas_call(
        functools.partial(_merge_kernel, with_ctx=with_ctx, n_lat_tiles=nlt),
        out_shape=jax.ShapeDtypeStruct((n_rows, D), F32),
        grid=(n_rows // tm,), in_specs=in_specs,
        out_specs=pl.BlockSpec((tm, D), lambda i: (i, 0)),
        compiler_params=_cparams(("parallel",)), name="merge",
    )(*branches, P, P, P, merge_b.reshape(1, 3 * D).astype(F32), *weights, X, mod)


EXPERT_BITS = 5
assert 1 << EXPERT_BITS == N_EXPERTS


def _slot_kernel(pstart, code_ref, o_ref):
    code = code_ref[...]
    expert = code & (N_EXPERTS - 1)
    slot = code >> EXPERT_BITS
    for e in range(N_EXPERTS):
        slot = slot + jnp.where(expert == e, pstart[e], 0)
    o_ref[...] = slot


def _slot_table(codes, pstart):
    A = codes.shape[0]
    assert A % (8 * LANES) == 0
    shape = (A // LANES, LANES)
    whole = pl.BlockSpec(shape, lambda i, ps: (0, 0))
    return pl.pallas_call(
        _slot_kernel,
        out_shape=jax.ShapeDtypeStruct(shape, jnp.int32),
        grid_spec=pltpu.PrefetchScalarGridSpec(num_scalar_prefetch=1, grid=(1,), in_specs=[whole], out_specs=whole),
        compiler_params=_cparams(("arbitrary",)), name="moe_slots",
    )(pstart, codes.reshape(shape)).reshape(A)


def _dispatch_kernel(slots, tok_ref, xs_zero, xs_out, sem, *, n_tok):
    del xs_zero
    R = tok_ref.shape[0]
    base = pl.program_id(0) * R

    def copy(pos, r):
        return pltpu.make_async_copy(tok_ref.at[pl.ds(r, 1), :], xs_out.at[pl.ds(slots[pos], 1), :], sem)

    for wait in (False, True):
        for k in range(TOP_K):
            for r in range(R):
                cp = copy(k * n_tok + base + r, r)
                cp.wait() if wait else cp.start()


def _moe_dispatch(tok, slots, P):
    N, D = tok.shape
    tm = MOE_TOKEN_TILE
    return pl.pallas_call(
        functools.partial(_dispatch_kernel, n_tok=N),
        out_shape=jax.ShapeDtypeStruct((P, D), F32),
        grid_spec=pltpu.PrefetchScalarGridSpec(
            num_scalar_prefetch=1, grid=(N // tm,),
            in_specs=[pl.BlockSpec((tm, D), lambda i, sl: (i, 0)),
                      pl.BlockSpec(memory_space=pl.ANY)],
            out_specs=pl.BlockSpec(memory_space=pl.ANY),
            scratch_shapes=[pltpu.SemaphoreType.DMA]),
        input_output_aliases={2: 0},
        compiler_params=_cparams(("arbitrary",)), name="moe_dispatch",
    )(slots, tok, jnp.zeros((P, D), F32))


def _moe_kernel(blk_e, xs_ref, w1_ref, w3_ref, w2_ref, y_ref, w1b, w3b, w2b):
    i = pl.program_id(0)

    @pl.when((i == 0) | (blk_e[i] != blk_e[jnp.maximum(i - 1, 0)]))
    def _():
        w1b[...] = w1_ref[...].astype(BF16)
        w3b[...] = w3_ref[...].astype(BF16)
        w2b[...] = w2_ref[...].astype(BF16)

    x = xs_ref[...].astype(BF16)
    a = _dot(x, w1b[...])
    g = (a * _sigmoid(a)) * _dot(x, w3b[...])
    y_ref[...] = _dot(g.astype(BF16), w2b[...])


def _moe_experts(xs, blk_e, w1, w3, w2, layer):
    P, D = xs.shape
    De = w1.shape[3]
    return pl.pallas_call(
        _moe_kernel,
        out_shape=jax.ShapeDtypeStruct((P, D), F32),
        grid_spec=pltpu.PrefetchScalarGridSpec(
            num_scalar_prefetch=1, grid=(P // MOE_BLOCK,),
            in_specs=[pl.BlockSpec((MOE_BLOCK, D), lambda i, be: (i, 0)),
                      pl.BlockSpec((None, None, D, De), lambda i, be: (layer, be[i], 0, 0)),
                      pl.BlockSpec((None, None, D, De), lambda i, be: (layer, be[i], 0, 0)),
                      pl.BlockSpec((None, None, De, D), lambda i, be: (layer, be[i], 0, 0))],
            out_specs=pl.BlockSpec((MOE_BLOCK, D), lambda i, be: (i, 0)),
            scratch_shapes=[pltpu.VMEM((D, De), BF16), pltpu.VMEM((D, De), BF16), pltpu.VMEM((De, D), BF16)]),
        compiler_params=_cparams(("arbitrary",)), name="moe_experts",
    )(blk_e, xs, w1, w3, w2)


def _combine_kernel(slots, yb_hbm, x_ref, g2, wt_ref, o_ref, buf, sems, *, n_tok):
    i = pl.program_id(0)
    n = pl.num_programs(0)
    slot = i % 2
    R = x_ref.shape[0]

    def gather(tile, slot, wait):
        dst = buf.at[slot]
        for k in range(TOP_K):
            for r in range(R):
                src = slots[k * n_tok + tile * R + r]
                cp = pltpu.make_async_copy(yb_hbm.at[pl.ds(src, 1), :], dst.at[k, pl.ds(r, 1), :], sems.at[slot])
                cp.wait() if wait else cp.start()

    @pl.when(i == 0)
    def _():
        gather(0, 0, False)

    @pl.when(i + 1 < n)
    def _():
        gather(i + 1, 1 - slot, False)

    gather(i, slot, True)
    w = wt_ref[...]
    acc = w[:, 0:1] * buf[slot, 0]
    for k in range(1, TOP_K):
        acc = acc + w[:, k:k + 1] * buf[slot, k]
    o_ref[...] = x_ref[...] + g2[...] * acc


def _moe_combine(yb, slots, wts, X, mod, gate_idx, *, n_rows, B, L):
    D = X.shape[1]
    tm = MOE_TOKEN_TILE
    nlt, tps = (B * L) // tm, L // tm
    grp = functools.partial(_group_of_tile, n_lat_tiles=nlt, tiles_per_seq=tps, n_batch=B)
    return pl.pallas_call(
        functools.partial(_combine_kernel, n_tok=n_rows),
        out_shape=jax.ShapeDtypeStruct((n_rows, D), F32),
        grid_spec=pltpu.PrefetchScalarGridSpec(
            num_scalar_prefetch=1, grid=(n_rows // tm,),
            in_specs=[pl.BlockSpec(memory_space=pl.ANY),
                      pl.BlockSpec((tm, D), lambda i, sl: (i, 0)),
                      pl.BlockSpec((None, None, 1, D), lambda i, sl: (grp(i), gate_idx, 0, 0)),
                      pl.BlockSpec((tm, LANES), lambda i, sl: (i, 0))],
            out_specs=pl.BlockSpec((tm, D), lambda i, sl: (i, 0)),
            scratch_shapes=[pltpu.VMEM((2, TOP_K, tm, D), F32), pltpu.SemaphoreType.DMA((2,))]),
        compiler_params=_cparams(("arbitrary",)), name="moe_combine",
    )(slots, yb, X, mod, wts)


def _dispatch_tables(code, counts, n_tok):
    E, K = N_EXPERTS, TOP_K
    cnt = counts[0, :E].astype(jnp.int32)
    padded = (cnt + MOE_BLOCK - 1) // MOE_BLOCK * MOE_BLOCK
    pend = jnp.cumsum(padded)
    pstart = (pend - padded).astype(jnp.int32)
    P = -(-(n_tok * K + E * (MOE_BLOCK - 1)) // MOE_BLOCK) * MOE_BLOCK
    blk_row = jnp.arange(P // MOE_BLOCK, dtype=jnp.int32) * MOE_BLOCK
    blk_e = jnp.minimum(jnp.sum(pend[None, :] <= blk_row[:, None], axis=1), E - 1).astype(jnp.int32)
    return _slot_table(code[:, :K].T.reshape(-1), pstart), P, blk_e


def kernel(x, c, ctx, c_ctx, mod_w, mod_b, norm1_g, norm2_g, w_in, merge_b, m_conv_w, m_conv_b, m_gate_b, m_norm_g, a_qnorm_g, a_knorm_g, a_lambda, a_subln_g, h_conv_w, h_conv_b, h_ffn_w1, h_ffn_b1, h_ffn_w2, h_ffn_b2, h_ffn_w3, h_freq, h_decay, h_bias, w_br_m, w_br_a, w_br_h, w_out, router_w, router_b, e_w1, e_w3, e_w2):
    B, L, D = x.shape
    Lc = ctx.shape[1]
    depth = mod_w.shape[0]
    NL, NC = B * L, B * Lc
    T = NL + NC
    assert B + 1 <= 8 and Lc == ROW_TILE and L % min(L, 1024) == 0 and L % 512 == 0
    assert NL % Lc == 0 and D % LANES == 0
    width = D
    a_dh = D // (2 * A_HEADS)
    n_gates = 4 * M_HEADS
    c_mq, c_mv, c_mo = 0, 2 * width, 3 * width
    c_aq, c_ak, c_av = 4 * width, 5 * width, 6 * width
    c_hx = 7 * width
    c_gp = 10 * width

    X = jnp.concatenate([x.reshape(NL, D), ctx.reshape(NC, D)], axis=0)
    c8 = jnp.zeros((8, D), F32).at[:B].set(c).at[B].set(c_ctx)
    prep_tile = 2 * ROW_TILE if L % (2 * ROW_TILE) == 0 and NC % (2 * ROW_TILE) == 0 else ROW_TILE
    cos_tab, sin_tab = _rope_tables(L, a_dh, prep_tile)
    fft_consts = _dft_consts(L)
    g0 = 4 * width
    w_main = jnp.concatenate([w_in[:, :, :g0], w_in[:, :, g0 + n_gates:]], axis=2).astype(BF16)
    w_gate = jnp.pad(w_in[:, :, g0:g0 + n_gates], ((0, 0), (0, 0), (0, LANES - n_gates))).astype(BF16)
    merge_w = [w.astype(BF16) for w in (w_br_m, w_br_a, w_br_h, w_out)]

    for l in range(depth):
        need_ctx = l < depth - 1
        lam_init = 0.8 - 0.6 * math.exp(-0.3 * l)
        mod = _modulation(c8, mod_w, mod_b, l)[:B + 1].reshape(B + 1, 6, 1, D)

        P, G = _inproj(X, norm1_g[l], mod, w_main, w_gate, m_gate_b[l], l, L=L, B=B)

        QK = _mlstm_qk(P, m_conv_w[l], m_conv_b[l], L=L, Lc=Lc, NL=NL, width=width)
        ym, ymc = _mlstm(QK, P, G, m_norm_g[l], B=B, L=L, Lc=Lc, width=width)

        Qh = _attn_prep(P, c_aq, a_qnorm_g[l], cos_tab, sin_tab, a_dh ** -0.5 * LOG2E, L=L, NL=NL, dh=a_dh)
        Kh = _attn_prep(P, c_ak, a_knorm_g[l], cos_tab, sin_tab, 1.0, L=L, NL=NL, dh=a_dh)
        attn = functools.partial(_attention, Qh, Kh, P, c_av, a_lambda[l], a_subln_g[l], lam_init,
                                 B=B, L=L, Lc=Lc, dh=a_dh)
        ya = attn(latent=True)

        x0, z = _hyena_pre(P, c_hx, h_conv_w[l], h_conv_b[l], L=L, Lc=Lc, NL=NL, C=width)
        filt = (h_ffn_w1[l], h_ffn_b1[l], h_ffn_w2[l], h_ffn_b2[l], h_ffn_w3[l], h_freq[l], h_decay[l])
        taps, ssq = _hyena_filters(L, *filt, width)
        yh = _hyena_latent(x0, z, taps, ssq, h_bias[l], fft_consts, B=B, L=L, C=width)

        if need_ctx:
            yac = attn(latent=False)
            taps_c, ssq_c = _hyena_filters(Lc, *filt, width)
            yhc = _hyena_ctx(x0, z, taps_c, ssq_c, h_bias[l], B=B, L=L, Lc=Lc, C=width)
            branches, n_rows = [ym, ya, yh, ymc, yac, yhc], T
        else:
            branches, n_rows = [ym, ya, yh], NL
        X = _merge(X, P, c_gp, branches, merge_b[l], merge_w, l, mod, 2,
                   n_rows=n_rows, B=B, L=L, with_ctx=need_ctx)

        tok, code, wts, counts = _norm_mod(X, norm2_g[l], mod, 3, 4, n_rows=n_rows, L=L, B=B,
                                           router=(router_w, router_b))
        slots, n_slots, blk_e = _dispatch_tables(code, counts, n_rows)
        xs = _moe_dispatch(tok, slots, n_slots)
        yb = _moe_experts(xs, blk_e, e_w1, e_w3, e_w2, l)
        X = _moe_combine(yb, slots, wts, X, mod, 5, n_rows=n_rows, B=B, L=L)
    return X[:NL].reshape(B, L, D)
```
